```python
import jax
import jax.numpy as jnp
from jax import lax
import numpy as np

D_MODEL = 2048
BATCH = 2
SEQ = 8192
DEPTH = 1
DEC_BATCH = 16
DEC_SEQ = 32
PAST_LEN = 1024

CHUNK = 64
EPS = 1e-6
GLA_HEADS = 4
GLA_DV = D_MODEL // GLA_HEADS
GLA_DK = GLA_DV // 2
GLA_LOWRANK = 16
GLA_NORMALIZER = 16.0
SWA_HD = 64
SWA_HEADS = D_MODEL // SWA_HD
SWA_KV_HEADS = SWA_HEADS // 8
SWA_GROUP = SWA_HEADS // SWA_KV_HEADS
WINDOW = 128
WINDOW_CHUNKS = WINDOW // CHUNK
ROPE_DIM = SWA_HD // 4
ROPE_THETA = 500000.0
N_MEM = 256
MEM_HEADS = 4
MEM_HD = D_MODEL // MEM_HEADS
GLA_QK_W = GLA_HEADS * GLA_DK
GLA_V_W = GLA_HEADS * GLA_DV
SWA_Q_W = SWA_HEADS * SWA_HD
SWA_KV_W = SWA_KV_HEADS * SWA_HD
MEM_W = MEM_HEADS * MEM_HD
IN_SIZES = (GLA_QK_W, GLA_QK_W, GLA_V_W, GLA_V_W, GLA_LOWRANK, SWA_Q_W, SWA_KV_W, SWA_KV_W, MEM_W)
IN_WIDTH = 2 * GLA_QK_W + 2 * GLA_V_W + GLA_LOWRANK + SWA_Q_W + 2 * SWA_KV_W + MEM_W
N_BRANCH = 3
BRANCH_W = D_MODEL
N_GROUPS = 8
EXPERTS_PER_GROUP = 8
N_EXPERTS = N_GROUPS * EXPERTS_PER_GROUP
TOP_K = 2
D_FF = D_MODEL // 4

kernel_name = 'hybrid_gla_swa_sink_memory_hmoe_stream_step'


def rms_norm(x, g):
    xf = x.astype(jnp.float32)
    y = xf * lax.rsqrt(jnp.mean(xf * xf, axis=-1, keepdims=True) + EPS)
    return (y * g.astype(jnp.float32)).astype(x.dtype)


def split_cols(a, sizes):
    idx = np.cumsum(np.array(sizes))[:-1].tolist()
    return jnp.split(a, idx, axis=-1)


def partial_rope(x, pos):
    half = ROPE_DIM // 2
    inv = ROPE_THETA ** (-jnp.arange(half, dtype=jnp.float32) / half)
    ang = pos.astype(jnp.float32)[:, None] * inv[None, :]
    cos = jnp.cos(ang)[:, None, :]
    sin = jnp.sin(ang)[:, None, :]
    xr = x[..., :ROPE_DIM].astype(jnp.float32)
    x1, x2 = xr[..., :half], xr[..., half:]
    rot = jnp.concatenate([x1 * cos - x2 * sin, x2 * cos + x1 * sin], axis=-1)
    return jnp.concatenate([rot.astype(x.dtype), x[..., ROPE_DIM:]], axis=-1)


def gla_recurrence(q, k, v, log_a, s0):
    B, T, H, DK = q.shape
    DV = v.shape[-1]
    c = min(CHUNK, T)
    n = T // c

    def to_chunks(a):
        return a.reshape(B, n, c, H, a.shape[-1]).transpose(1, 0, 3, 2, 4)

    qc, kc, vc, gc = to_chunks(q), to_chunks(k), to_chunks(v), to_chunks(log_a)
    causal = jnp.tril(jnp.ones((c, c), dtype=bool))

    def step(S, inp):
        qi, ki, vi, gi = inp
        qi = qi.astype(jnp.float32)
        ki = ki.astype(jnp.float32)
        vi = vi.astype(jnp.float32)
        b = jnp.cumsum(gi.astype(jnp.float32), axis=-2)
        b_last = b[..., -1:, :]
        q_t = qi * jnp.exp(b)
        k_t = ki * jnp.exp(-b)
        k_u = ki * jnp.exp(b_last - b)
        att = jnp.where(causal, jnp.einsum('bhtd,bhsd->bhts', q_t, k_t), 0.0)
        o = jnp.einsum('bhtd,bhdv->bhtv', q_t, S) + jnp.einsum('bhts,bhsv->bhtv', att, vi)
        S = jnp.exp(b_last)[..., 0, :, None] * S + jnp.einsum('bhsd,bhsv->bhdv', k_u, vi)
        return S, o

    S, o = lax.scan(step, s0.astype(jnp.float32), (qc, kc, vc, gc))
    o = o.transpose(1, 0, 3, 2, 4).reshape(B, T, H, DV).astype(v.dtype)
    return o, S.astype(s0.dtype)


def sink_attention(qb, kb, vb, valid, sinks):
    s = jnp.einsum('bnqhgd,bnkhd->bnhgqk', qb, kb).astype(jnp.float32) * (SWA_HD ** -0.5)
    s = jnp.where(valid[None, :, None, None, None, :], s, -jnp.inf)
    sk = jnp.broadcast_to(sinks.astype(jnp.float32).reshape(SWA_KV_HEADS, SWA_GROUP, 1, 1), s.shape[:-1] + (1,))
    pr = jax.nn.softmax(jnp.concatenate([s, sk], axis=-1), axis=-1)[..., :-1]
    o = jnp.einsum('bnhgqk,bnkhd->bnqhgd', pr.astype(vb.dtype), vb)
    B, n, Q = qb.shape[:3]
    return o.reshape(B, n * Q, SWA_Q_W)


def swa_band_prompt(q, k, v, sinks):
    B, T = q.shape[:2]
    n = T // CHUNK
    padc = WINDOW_CHUNKS * CHUNK
    pad = ((0, 0), (padc, 0), (0, 0), (0, 0))
    kc = jnp.pad(k, pad).reshape(B, n + WINDOW_CHUNKS, CHUNK, SWA_KV_HEADS, SWA_HD)
    vc = jnp.pad(v, pad).reshape(B, n + WINDOW_CHUNKS, CHUNK, SWA_KV_HEADS, SWA_HD)
    kb = jnp.concatenate([kc[:, j:j + n] for j in range(WINDOW_CHUNKS + 1)], axis=2)
    vb = jnp.concatenate([vc[:, j:j + n] for j in range(WINDOW_CHUNKS + 1)], axis=2)
    key_chunk = jnp.arange(n)[:, None] - WINDOW_CHUNKS + jnp.arange((WINDOW_CHUNKS + 1) * CHUNK)[None, :] // CHUNK
    qb = q.reshape(B, n, CHUNK, SWA_KV_HEADS, SWA_GROUP, SWA_HD)
    return sink_attention(qb, kb, vb, key_chunk >= 0, sinks)


def swa_append(q, k, v, past_k, past_v, sinks):
    B, T = q.shape[:2]
    kb = jnp.concatenate([past_k, k], axis=1)[:, None]
    vb = jnp.concatenate([past_v, v], axis=1)[:, None]
    valid = jnp.ones((1, kb.shape[2]), dtype=bool)
    qb = q.reshape(B, 1, T, SWA_KV_HEADS, SWA_GROUP, SWA_HD)
    return sink_attention(qb, kb, vb, valid, sinks)


def memory_kv(mem, norm_g, w_kv, k_norm_g):
    B, M, _ = mem.shape
    k, v = jnp.split(rms_norm(mem, norm_g) @ w_kv, 2, axis=-1)
    k = rms_norm(k.reshape(B, M, MEM_HEADS, MEM_HD), k_norm_g)
    return k, v.reshape(B, M, MEM_HEADS, MEM_HD)


def memory_attention(q, mem_k, mem_v):
    s = jnp.einsum('bthd,bmhd->bhtm', q, mem_k).astype(jnp.float32) * (MEM_HD ** -0.5)
    pr = jax.nn.softmax(s, axis=-1).astype(mem_v.dtype)
    o = jnp.einsum('bhtm,bmhd->bthd', pr, mem_v)
    return o.reshape(q.shape[0], q.shape[1], MEM_W)


def hier_moe(h, w_rg, b_rg, w_re, b_re, w_up, w_down):
    B, T, D = h.shape
    n_tok = B * T
    hf = h.reshape(n_tok, D)
    rows = jnp.arange(n_tok)
    lg = (hf @ w_rg).astype(jnp.float32) + b_rg.astype(jnp.float32)
    pg = jax.nn.softmax(lg, axis=-1)
    gsel = jnp.argmax(lg, axis=-1)
    g_w = pg[rows, gsel]
    le = ((hf @ w_re).astype(jnp.float32) + b_re.astype(jnp.float32)).reshape(n_tok, N_GROUPS, EXPERTS_PER_GROUP)
    pe = jax.nn.softmax(le[rows, gsel], axis=-1)
    top_p, top_i = lax.top_k(pe, TOP_K)
    wts = g_w[:, None] * top_p / jnp.sum(top_p, axis=-1, keepdims=True)
    eid = (gsel[:, None] * EXPERTS_PER_GROUP + top_i).reshape(-1)
    order = jnp.argsort(eid)
    tok = order // TOP_K
    sizes = jnp.bincount(eid, length=N_EXPERTS).astype(jnp.int32)
    gate, up = jnp.split(lax.ragged_dot(hf[tok], w_up, sizes), 2, axis=-1)
    ye = lax.ragged_dot(jax.nn.silu(gate) * up, w_down, sizes)
    ye = ye * wts.reshape(-1)[order][:, None].astype(ye.dtype)
    return jnp.zeros_like(hf).at[tok].add(ye).reshape(B, T, D)


def hybrid_layer(x, pos, gla_s0, swa_past_k, swa_past_v, mem_k, mem_v, p):
    B, T, _ = x.shape
    h = rms_norm(x, p['norm_mix_g'])
    gq, gk, gv, gg, ga, sq, sk, sv, mq = split_cols(h @ p['w_in'], IN_SIZES)

    q = gq.reshape(B, T, GLA_HEADS, GLA_DK) * (GLA_DK ** -0.5)
    k = gk.reshape(B, T, GLA_HEADS, GLA_DK)
    v = gv.reshape(B, T, GLA_HEADS, GLA_DV)
    log_a = jax.nn.log_sigmoid((ga @ p['w_a2'] + p['b_a2']).astype(jnp.float32)) / GLA_NORMALIZER
    o, gla_state = gla_recurrence(q, k, v, log_a.reshape(B, T, GLA_HEADS, GLA_DK), gla_s0)
    o_gla = (rms_norm(o, p['gla_norm_g']) * jax.nn.silu(gg.reshape(B, T, GLA_HEADS, GLA_DV))).reshape(B, T, GLA_V_W)

    qs = partial_rope(rms_norm(sq.reshape(B, T, SWA_HEADS, SWA_HD), p['swa_q_norm_g']), pos)
    ks = partial_rope(rms_norm(sk.reshape(B, T, SWA_KV_HEADS, SWA_HD), p['swa_k_norm_g']), pos)
    vs = sv.reshape(B, T, SWA_KV_HEADS, SWA_HD)
    if swa_past_k is None:
        o_swa = swa_band_prompt(qs, ks, vs, p['swa_sinks'])
    else:
        o_swa = swa_append(qs, ks, vs, swa_past_k, swa_past_v, p['swa_sinks'])

    qm = rms_norm(mq.reshape(B, T, MEM_HEADS, MEM_HD), p['mem_q_norm_g'])
    o_mem = memory_attention(qm, mem_k, mem_v)

    branches = jnp.stack([o_gla, o_swa, o_mem], axis=2)
    proj = jnp.einsum('btnw,nwd->btnd', branches, p['w_branch'])
    gates = jax.nn.sigmoid(h @ p['w_gate'] + p['b_gate']).reshape(B, T, N_BRANCH, D_MODEL)
    x = x + jnp.sum(gates * proj, axis=2) @ p['w_out']

    x = x + hier_moe(rms_norm(x, p['norm_ffn_g']), p['w_router_group'], p['b_router_group'],
                     p['w_router_expert'], p['b_router_expert'], p['w_up'], p['w_down'])
    return x, gla_state, ks, vs


def setup_inputs(seed: int = 0) -> dict:
    key = jax.random.key(seed)
    ks = jax.random.split(key, 40)
    f32 = jnp.float32
    L = DEPTH
    swa_cache = min(WINDOW, PAST_LEN)

    def nrm(k, shape, scale):
        return jax.random.normal(k, shape, f32) * scale

    def gain(k, n):
        return 1.0 + 0.02 * jax.random.normal(k, (L, n), f32)

    return {
        'x_prompt': nrm(ks[0], (BATCH, SEQ, D_MODEL), 1.0),
        'x_sample': nrm(ks[1], (DEC_BATCH, DEC_SEQ, D_MODEL), 1.0),
        'state_gla': nrm(ks[2], (L, DEC_BATCH, GLA_HEADS, GLA_DK, GLA_DV), 1.0),
        'cache_swa_k': nrm(ks[3], (L, DEC_BATCH, swa_cache, SWA_KV_HEADS, SWA_HD), 1.0),
        'cache_swa_v': nrm(ks[4], (L, DEC_BATCH, swa_cache, SWA_KV_HEADS, SWA_HD), 1.0),
        'cache_mem_k': nrm(ks[5], (L, DEC_BATCH, N_MEM, MEM_HEADS, MEM_HD), 1.0),
        'cache_mem_v': nrm(ks[6], (L, DEC_BATCH, N_MEM, MEM_HEADS, MEM_HD), 1.0),
        'mem_prompt': nrm(ks[7], (BATCH, N_MEM, D_MODEL), 1.0),
        'norm_mix_g': gain(ks[8], D_MODEL),
        'w_in': nrm(ks[9], (L, D_MODEL, IN_WIDTH), D_MODEL ** -0.5),
        'w_a2': nrm(ks[10], (L, GLA_LOWRANK, GLA_QK_W), GLA_LOWRANK ** -0.5),
        'b_a2': nrm(ks[11], (L, GLA_QK_W), 0.1),
        'gla_norm_g': gain(ks[12], GLA_DV),
        'swa_q_norm_g': gain(ks[13], SWA_HD),
        'swa_k_norm_g': gain(ks[14], SWA_HD),
        'swa_sinks': nrm(ks[15], (L, SWA_HEADS), 1.0),
        'norm_mem_g': gain(ks[16], D_MODEL),
        'w_mem_kv': nrm(ks[17], (L, D_MODEL, 2 * MEM_W), D_MODEL ** -0.5),
        'mem_q_norm_g': gain(ks[18], MEM_HD),
        'mem_k_norm_g': gain(ks[19], MEM_HD),
        'w_gate': nrm(ks[20], (L, D_MODEL, N_BRANCH * D_MODEL), D_MODEL ** -0.5),
        'b_gate': nrm(ks[21], (L, N_BRANCH * D_MODEL), 0.02),
        'w_branch': nrm(ks[22], (L, N_BRANCH, BRANCH_W, D_MODEL), BRANCH_W ** -0.5),
        'w_out': nrm(ks[23], (L, D_MODEL, D_MODEL), D_MODEL ** -0.5),
        'norm_ffn_g': gain(ks[24], D_MODEL),
        'w_router_group': nrm(ks[25], (L, D_MODEL, N_GROUPS), D_MODEL ** -0.5),
        'b_router_group': nrm(ks[26], (L, N_GROUPS), 0.01),
        'w_router_expert': nrm(ks[27], (L, D_MODEL, N_EXPERTS), D_MODEL ** -0.5),
        'b_router_expert': nrm(ks[28], (L, N_EXPERTS), 0.01),
        'w_up': nrm(ks[29], (L, N_EXPERTS, D_MODEL, 2 * D_FF), D_MODEL ** -0.5),
        'w_down': nrm(ks[30], (L, N_EXPERTS, D_FF, D_MODEL), D_FF ** -0.5),
    }


def reference(x_prompt, x_sample, state_gla, cache_swa_k, cache_swa_v, cache_mem_k, cache_mem_v,
              mem_prompt, norm_mix_g, w_in, w_a2, b_a2, gla_norm_g, swa_q_norm_g, swa_k_norm_g,
              swa_sinks, norm_mem_g, w_mem_kv, mem_q_norm_g, mem_k_norm_g, w_gate, b_gate,
              w_branch, w_out, norm_ffn_g, w_router_group, b_router_group, w_router_expert,
              b_router_expert, w_up, w_down):
    Bp, Tp, _ = x_prompt.shape
    Ts = x_sample.shape[1]
    keep_s = cache_swa_k.shape[2]
    keep_p = min(WINDOW, Tp)
    pos_p = jnp.arange(Tp, dtype=jnp.int32)
    pos_s = PAST_LEN + jnp.arange(Ts, dtype=jnp.int32)
    yp, ys = x_prompt, x_sample
    gla_p, swk_p, swv_p, mk_p, mv_p, gla_s, swk_s, swv_s = [], [], [], [], [], [], [], []
    for l in range(DEPTH):
        p = {
            'norm_mix_g': norm_mix_g[l], 'w_in': w_in[l], 'w_a2': w_a2[l], 'b_a2': b_a2[l],
            'gla_norm_g': gla_norm_g[l], 'swa_q_norm_g': swa_q_norm_g[l],
            'swa_k_norm_g': swa_k_norm_g[l], 'swa_sinks': swa_sinks[l],
            'mem_q_norm_g': mem_q_norm_g[l], 'w_gate': w_gate[l], 'b_gate': b_gate[l],
            'w_branch': w_branch[l], 'w_out': w_out[l], 'norm_ffn_g': norm_ffn_g[l],
            'w_router_group': w_router_group[l], 'b_router_group': b_router_group[l],
            'w_router_expert': w_router_expert[l], 'b_router_expert': b_router_expert[l],
            'w_up': w_up[l], 'w_down': w_down[l],
        }
        mk, mv = memory_kv(mem_prompt, norm_mem_g[l], w_mem_kv[l], mem_k_norm_g[l])
        s0 = jnp.zeros((Bp, GLA_HEADS, GLA_DK, GLA_DV), x_prompt.dtype)
        yp, sp, kp, vp = hybrid_layer(yp, pos_p, s0, None, None, mk, mv, p)
        gla_p.append(sp)
        swk_p.append(kp[:, Tp - keep_p:])
        swv_p.append(vp[:, Tp - keep_p:])
        mk_p.append(mk)
        mv_p.append(mv)
        ys, ss, k_s, v_s = hybrid_layer(ys, pos_s, state_gla[l], cache_swa_k[l], cache_swa_v[l],
                                        cache_mem_k[l], cache_mem_v[l], p)
        gla_s.append(ss)
        swk_s.append(jnp.concatenate([cache_swa_k[l], k_s], axis=1)[:, Ts:Ts + keep_s])
        swv_s.append(jnp.concatenate([cache_swa_v[l], v_s], axis=1)[:, Ts:Ts + keep_s])
    return (yp, ys, jnp.stack(gla_p), jnp.stack(swk_p), jnp.stack(swv_p), jnp.stack(mk_p),
            jnp.stack(mv_p), jnp.stack(gla_s), jnp.stack(swk_s), jnp.stack(swv_s))
```

```python
import functools

import jax
import jax.numpy as jnp
from jax import lax
from jax.experimental import pallas as pl
from jax.experimental.pallas import tpu as pltpu

F32 = jnp.float32
BF16 = jnp.bfloat16

D_MODEL = 2048
CHUNK = 64
EPS = 1e-6
PAST_LEN = 1024
GLA_HEADS = 4
GLA_DV = 512
GLA_DK = 256
GLA_LOWRANK = 16
GLA_NORMALIZER = 16.0
SWA_HD = 64
SWA_HEADS = 32
SWA_KV_HEADS = 4
SWA_GROUP = 8
WINDOW = 128
ROPE_DIM = 16
ROPE_THETA = 500000.0
N_MEM = 256
MEM_HEADS = 4
MEM_HD = 512
N_GROUPS = 8
EXPERTS_PER_GROUP = 8
N_EXPERTS = 64
TOP_K = 2
D_FF = 512

LANES = 128
VMEM_LIMIT = 56 * 1024 * 1024
ROW_TILE = 512
COL_TILE = 512
MOE_TILE = 256
NEG_BIG = -1e30


def _params(n_axes):
    return pltpu.CompilerParams(dimension_semantics=("arbitrary",) * n_axes,
                                vmem_limit_bytes=VMEM_LIMIT)


def _norm_kernel(x_ref, g_ref, o_ref):
    x = x_ref[...]
    y = x * lax.rsqrt(jnp.mean(x * x, axis=-1, keepdims=True) + EPS)
    o_ref[...] = (y * g_ref[...]).astype(o_ref.dtype)


def _rms_norm_rows(x, g, out_dtype, tm):
    n, d = x.shape
    return pl.pallas_call(
        _norm_kernel,
        out_shape=jax.ShapeDtypeStruct((n, d), out_dtype),
        grid=(n // tm,),
        in_specs=[pl.BlockSpec((tm, d), lambda i: (i, 0)),
                  pl.BlockSpec((1, d), lambda i: (0, 0))],
        out_specs=pl.BlockSpec((tm, d), lambda i: (i, 0)),
        compiler_params=_params(1),
        name="rms_norm_rows",
    )(x, g.reshape(1, d))


def _segment_rms(acc, seg_ref, inv_width):
    sq = acc * acc
    hi = sq.astype(BF16)
    lo = (sq - hi.astype(F32)).astype(BF16)
    ss = (jnp.dot(hi, seg_ref[...], preferred_element_type=F32)
          + jnp.dot(lo, seg_ref[...], preferred_element_type=F32))
    return ss * inv_width


def _rope(y, cos_ref, sa_ref, sb_ref):
    width = y.shape[1]
    reps = width // LANES
    c = jnp.concatenate([cos_ref[...]] * reps, axis=1)
    sa = jnp.concatenate([sa_ref[...]] * reps, axis=1)
    sb = jnp.concatenate([sb_ref[...]] * reps, axis=1)
    half = ROPE_DIM // 2
    return y * c + pltpu.roll(y, width - half, 1) * sa + pltpu.roll(y, half, 1) * sb


def _mm_plain_kernel(a_ref, w_ref, o_ref):
    o_ref[...] = jnp.dot(a_ref[...], w_ref[...], preferred_element_type=F32).astype(o_ref.dtype)


def _mm_sigmoid_kernel(a_ref, w_ref, b_ref, o_ref):
    acc = jnp.dot(a_ref[...], w_ref[...], preferred_element_type=F32) + b_ref[...]
    o_ref[...] = jax.nn.sigmoid(acc).astype(o_ref.dtype)


def _mm_residual_kernel(a_ref, w_ref, r_ref, o_ref):
    acc = jnp.dot(a_ref[...], w_ref[...], preferred_element_type=F32)
    o_ref[...] = r_ref[...] + acc


def _mm_qknorm_rope_kernel(a_ref, w_ref, g_ref, seg_ref, cos_ref, sa_ref, sb_ref, o_ref, *, keep_from):
    acc = jnp.dot(a_ref[...], w_ref[...], preferred_element_type=F32)
    ms = _segment_rms(acc, seg_ref, 1.0 / SWA_HD)
    y = acc * lax.rsqrt(ms + EPS) * g_ref[...]
    y = _rope(y, cos_ref, sa_ref, sb_ref)
    if keep_from is not None:
        col = lax.broadcasted_iota(jnp.int32, y.shape, 1)
        y = jnp.where(col < keep_from, y, acc)
    o_ref[...] = y.astype(o_ref.dtype)


def _mm_headnorm_kernel(a_ref, w_ref, g_ref, o_ref, *, norm_tiles):
    acc = jnp.dot(a_ref[...], w_ref[...], preferred_element_type=F32)
    y = acc * lax.rsqrt(jnp.mean(acc * acc, axis=-1, keepdims=True) + EPS) * g_ref[...]
    if norm_tiles is not None:
        y = jnp.where(pl.program_id(1) < norm_tiles, y, acc)
    o_ref[...] = y.astype(o_ref.dtype)


def _matmul(kernel, a, w, extras, extra_specs, out_dtype, tm, tn, name):
    m, k = a.shape
    n = w.shape[1]
    return pl.pallas_call(
        kernel,
        out_shape=jax.ShapeDtypeStruct((m, n), out_dtype),
        grid=(m // tm, n // tn),
        in_specs=[pl.BlockSpec((tm, k), lambda i, j: (i, 0)),
                  pl.BlockSpec((k, tn), lambda i, j: (0, j))] + list(extra_specs),
        out_specs=pl.BlockSpec((tm, tn), lambda i, j: (i, j)),
        compiler_params=_params(2),
        name=name,
    )(a, w, *extras)


def _gla_kernel(q_ref, k_ref, v_ref, gg_ref, ga_ref, wa_ref, ba_ref, gn_ref, s0_ref,
                o_ref, sout_ref, s_scr, *, chunk, n_chunks):
    t = pl.program_id(1)

    @pl.when(t == 0)
    def _():
        s_scr[...] = s0_ref[0]

    row = lax.broadcasted_iota(jnp.int32, (chunk, chunk), 0)
    col = lax.broadcasted_iota(jnp.int32, (chunk, chunk), 1)
    causal = row >= col
    tril = causal.astype(F32)

    def one_chunk(ci, carry):
        rows = pl.ds(pl.multiple_of(ci * chunk, chunk), chunk)
        z = jnp.dot(ga_ref[rows, :].astype(BF16), wa_ref[...], preferred_element_type=F32) + ba_ref[...]
        log_a = (jnp.minimum(z, 0.0) - jnp.log1p(jnp.exp(-jnp.abs(z)))) * (1.0 / GLA_NORMALIZER)
        b_all = jnp.dot(tril, log_a, preferred_element_type=F32, precision=lax.Precision.HIGHEST)
        for h in range(GLA_HEADS):
            ks = slice(h * GLA_DK, (h + 1) * GLA_DK)
            vs = slice(h * GLA_DV, (h + 1) * GLA_DV)
            b = b_all[:, ks]
            b_last = b[chunk - 1:chunk, :]
            q = q_ref[rows, ks] * (GLA_DK ** -0.5)
            k = k_ref[rows, ks]
            vb = v_ref[rows, vs].astype(BF16)
            q_t = (q * jnp.exp(b)).astype(BF16)
            k_t = (k * jnp.exp(-b)).astype(BF16)
            k_u = (k * jnp.exp(b_last - b)).astype(BF16)
            att = lax.dot_general(q_t, k_t, (((1,), (1,)), ((), ())), preferred_element_type=F32)
            att = jnp.where(causal, att, 0.0)
            s_old = s_scr[h]
            o = (jnp.dot(q_t, s_old.astype(BF16), preferred_element_type=F32)
                 + jnp.dot(att.astype(BF16), vb, preferred_element_type=F32))
            decay = jnp.transpose(jnp.broadcast_to(jnp.exp(b_last), (LANES, GLA_DK)))
            decay = jnp.concatenate([decay] * (GLA_DV // LANES), axis=1)
            s_scr[h] = decay * s_old + lax.dot_general(k_u, vb, (((0,), (0,)), ((), ())),
                                                       preferred_element_type=F32)
            on = o * lax.rsqrt(jnp.mean(o * o, axis=-1, keepdims=True) + EPS) * gn_ref[...]
            gg = gg_ref[rows, vs]
            o_ref[rows, vs] = (on * (gg * jax.nn.sigmoid(gg))).astype(o_ref.dtype)
        return carry

    lax.fori_loop(0, n_chunks, one_chunk, 0)

    @pl.when(t == pl.num_programs(1) - 1)
    def _():
        sout_ref[0] = s_scr[...]


def _gla(qkvg, ga, wa, ba, gn, s0, batch, seq, row0, tb, chunk):
    nt = seq // tb
    base = row0 // tb
    qk_w = GLA_HEADS * GLA_DK
    v_w = GLA_HEADS * GLA_DV
    rows = lambda b, t: base + b * nt + t
    kernel = functools.partial(_gla_kernel, chunk=chunk, n_chunks=tb // chunk)
    return pl.pallas_call(
        kernel,
        out_shape=(jax.ShapeDtypeStruct((batch * seq, v_w), BF16),
                   jax.ShapeDtypeStruct((batch, GLA_HEADS, GLA_DK, GLA_DV), F32)),
        grid=(batch, nt),
        in_specs=[pl.BlockSpec((tb, qk_w), lambda b, t: (rows(b, t), 0)),
                  pl.BlockSpec((tb, qk_w), lambda b, t: (rows(b, t), 1)),
                  pl.BlockSpec((tb, v_w), lambda b, t: (rows(b, t), 1)),
                  pl.BlockSpec((tb, v_w), lambda b, t: (rows(b, t), 2)),
                  pl.BlockSpec((tb, LANES), lambda b, t: (rows(b, t), 0)),
                  pl.BlockSpec((LANES, qk_w), lambda b, t: (0, 0)),
                  pl.BlockSpec((1, qk_w), lambda b, t: (0, 0)),
                  pl.BlockSpec((1, GLA_DV), lambda b, t: (0, 0)),
                  pl.BlockSpec((1, GLA_HEADS, GLA_DK, GLA_DV), lambda b, t: (b, 0, 0, 0))],
        out_specs=(pl.BlockSpec((tb, v_w), lambda b, t: (b * nt + t, 0)),
                   pl.BlockSpec((1, GLA_HEADS, GLA_DK, GLA_DV), lambda b, t: (b, 0, 0, 0))),
        scratch_shapes=[pltpu.VMEM((GLA_HEADS, GLA_DK, GLA_DV), F32)],
        compiler_params=_params(2),
        name="gla_chunks",
    )(qkvg, qkvg, qkvg, qkvg, ga, wa, ba, gn, s0)


def _swa_kernel(sink_ref, q_ref, kp_ref, vp_ref, ko_ref, vo_ref, o_ref, *, tq, prev_from_cache):
    i = pl.program_id(1)
    nk = WINDOW + tq
    k_all = jnp.concatenate([kp_ref[...], ko_ref[...]], axis=0)
    v_all = jnp.concatenate([vp_ref[...], vo_ref[...]], axis=0)
    qc = lax.broadcasted_iota(jnp.int32, (tq, nk), 0) // CHUNK + WINDOW // CHUNK
    kcol = lax.broadcasted_iota(jnp.int32, (tq, nk), 1)
    kc = kcol // CHUNK
    valid = (kc <= qc) & (kc >= qc - WINDOW // CHUNK)
    if not prev_from_cache:
        valid = valid & ((kcol >= WINDOW) | (i > 0))
    lane = lax.broadcasted_iota(jnp.int32, (nk, LANES), 1)
    low = lane < SWA_HD
    scale = SWA_HD ** -0.5
    for g in range(SWA_KV_HEADS):
        slab = slice((g // 2) * LANES, (g // 2 + 1) * LANES)
        k2 = k_all[:, slab]
        v2 = v_all[:, slab]
        k2r = pltpu.roll(k2, SWA_HD, 1)
        v2r = pltpu.roll(v2, SWA_HD, 1)
        if g % 2 == 0:
            k_lo, k_hi, v_lo, v_hi = k2, k2r, v2, v2r
        else:
            k_lo, k_hi, v_lo, v_hi = k2r, k2, v2r, v2
        zero = jnp.zeros_like(k2)
        km = (jnp.where(low, k_lo, zero).astype(BF16), jnp.where(low, zero, k_hi).astype(BF16))
        vm = (jnp.where(low, v_lo, zero).astype(BF16), jnp.where(low, zero, v_hi).astype(BF16))
        for j in range(SWA_GROUP // 2):
            qslab = slice((g * 4 + j) * LANES, (g * 4 + j + 1) * LANES)
            qs = q_ref[:, qslab]
            acc = jnp.zeros((tq, LANES), F32)
            for half in range(2):
                sink = sink_ref[g * SWA_GROUP + 2 * j + half]
                s = lax.dot_general(qs, km[half], (((1,), (1,)), ((), ())),
                                    preferred_element_type=F32) * scale
                s = jnp.where(valid, s, NEG_BIG)
                m = jnp.maximum(jnp.max(s, axis=-1, keepdims=True), sink)
                e = jnp.exp(s - m)
                denom = jnp.sum(e, axis=-1, keepdims=True) + jnp.exp(sink - m)
                p = (e / denom).astype(BF16)
                acc = acc + jnp.dot(p, vm[half], preferred_element_type=F32)
            o_ref[:, qslab] = acc.astype(o_ref.dtype)


def _swa(sinks, q, q_row0, k_prev, v_prev, prev_col, k_own, v_own, own_col, own_row0,
         batch, seq, tq, prev_from_cache):
    nt = seq // tq
    qb = q_row0 // tq
    ob = own_row0 // tq
    kv_w = SWA_KV_HEADS * SWA_HD
    if prev_from_cache:
        prev_map = lambda b, t, c: (b, c)
    else:
        per = seq // WINDOW
        prev_map = lambda b, t, c: (b * per + jnp.maximum(t * (tq // WINDOW) - 1, 0), c)
    kernel = functools.partial(_swa_kernel, tq=tq, prev_from_cache=prev_from_cache)
    return pl.pallas_call(
        kernel,
        out_shape=jax.ShapeDtypeStruct((batch * seq, SWA_HEADS * SWA_HD), BF16),
        grid=(batch, nt),
        in_specs=[pl.BlockSpec(memory_space=pltpu.SMEM),
                  pl.BlockSpec((tq, SWA_HEADS * SWA_HD), lambda b, t: (qb + b * nt + t, 0)),
                  pl.BlockSpec((WINDOW, kv_w), lambda b, t: prev_map(b, t, prev_col[0])),
                  pl.BlockSpec((WINDOW, kv_w), lambda b, t: prev_map(b, t, prev_col[1])),
                  pl.BlockSpec((tq, kv_w), lambda b, t: (ob + b * nt + t, own_col[0])),
                  pl.BlockSpec((tq, kv_w), lambda b, t: (ob + b * nt + t, own_col[1]))],
        out_specs=pl.BlockSpec((tq, SWA_HEADS * SWA_HD), lambda b, t: (b * nt + t, 0)),
        compiler_params=_params(2),
        name="swa_band",
    )(sinks, q, k_prev, v_prev, k_own, v_own)


def _mem_attn_kernel(q_ref, k_ref, v_ref, o_ref):
    s = lax.dot_general(q_ref[...], k_ref[...].astype(BF16), (((1,), (1,)), ((), ())),
                        preferred_element_type=F32) * (MEM_HD ** -0.5)
    m = jnp.max(s, axis=-1, keepdims=True)
    e = jnp.exp(s - m)
    p = (e / jnp.sum(e, axis=-1, keepdims=True)).astype(BF16)
    o_ref[...] = jnp.dot(p, v_ref[...].astype(BF16), preferred_element_type=F32).astype(o_ref.dtype)


def _mem_attn(q, q_row0, q_col0, mk, mk_col0, mv, mv_col0, batch, seq, tq):
    nt = seq // tq
    qb = q_row0 // tq
    return pl.pallas_call(
        _mem_attn_kernel,
        out_shape=jax.ShapeDtypeStruct((batch * seq, MEM_HEADS * MEM_HD), BF16),
        grid=(batch, MEM_HEADS, nt),
        in_specs=[pl.BlockSpec((tq, MEM_HD), lambda b, h, t: (qb + b * nt + t, q_col0 + h)),
                  pl.BlockSpec((N_MEM, MEM_HD), lambda b, h, t: (b, mk_col0 + h)),
                  pl.BlockSpec((N_MEM, MEM_HD), lambda b, h, t: (b, mv_col0 + h))],
        out_specs=pl.BlockSpec((tq, MEM_HD), lambda b, h, t: (b * nt + t, h)),
        compiler_params=_params(3),
        name="mem_attn",
    )(q, mk, mv)


def _merge_kernel(a0_ref, a1_ref, a2_ref, w_ref, g0_ref, g1_ref, g2_ref, o_ref):
    acc = g0_ref[...] * jnp.dot(a0_ref[...], w_ref[0], preferred_element_type=F32)
    acc = acc + g1_ref[...] * jnp.dot(a1_ref[...], w_ref[1], preferred_element_type=F32)
    acc = acc + g2_ref[...] * jnp.dot(a2_ref[...], w_ref[2], preferred_element_type=F32)
    o_ref[...] = acc.astype(o_ref.dtype)


def _merge(branches, w_branch, gates, tm, tn):
    n, d = branches[0].shape
    nj = d // tn
    a_spec = pl.BlockSpec((tm, d), lambda i, j: (i, 0))
    return pl.pallas_call(
        _merge_kernel,
        out_shape=jax.ShapeDtypeStruct((n, d), BF16),
        grid=(n // tm, nj),
        in_specs=[a_spec, a_spec, a_spec,
                  pl.BlockSpec((3, d, tn), lambda i, j: (0, 0, j)),
                  pl.BlockSpec((tm, tn), lambda i, j: (i, j)),
                  pl.BlockSpec((tm, tn), lambda i, j: (i, nj + j)),
                  pl.BlockSpec((tm, tn), lambda i, j: (i, 2 * nj + j))],
        out_specs=pl.BlockSpec((tm, tn), lambda i, j: (i, j)),
        compiler_params=_params(2),
        name="branch_merge",
    )(*branches, w_branch, gates, gates, gates)


def _router_kernel(x_ref, g_ref, wr_ref, br_ref, hf_ref, eid_ref, wt_ref):
    x = x_ref[...]
    hf = x * lax.rsqrt(jnp.mean(x * x, axis=-1, keepdims=True) + EPS) * g_ref[...]
    hf_ref[...] = hf
    logits = jnp.dot(hf.astype(BF16), wr_ref[...], preferred_element_type=F32) + br_ref[...]
    lane = lax.broadcasted_iota(jnp.int32, logits.shape, 1).astype(F32)
    big = 1e6
    is_g = lane < N_GROUPS
    lg = jnp.where(is_g, logits, NEG_BIG)
    mg = jnp.max(lg, axis=-1, keepdims=True)
    gsel = jnp.min(jnp.where(is_g & (lg == mg), lane, big), axis=-1, keepdims=True)
    g_w = 1.0 / jnp.sum(jnp.where(is_g, jnp.exp(lg - mg), 0.0), axis=-1, keepdims=True)
    e_lo = N_GROUPS + gsel * EXPERTS_PER_GROUP
    in_grp = (lane >= e_lo) & (lane < e_lo + EXPERTS_PER_GROUP)
    le = jnp.where(in_grp, logits, NEG_BIG)
    me = jnp.max(le, axis=-1, keepdims=True)
    ee = jnp.where(in_grp, jnp.exp(le - me), 0.0)
    pe = ee / jnp.sum(ee, axis=-1, keepdims=True)
    pe = jnp.where(in_grp, pe, -1.0)
    p1 = jnp.max(pe, axis=-1, keepdims=True)
    i1 = jnp.min(jnp.where(pe == p1, lane, big), axis=-1, keepdims=True)
    pe2 = jnp.where(lane == i1, -1.0, pe)
    p2 = jnp.max(pe2, axis=-1, keepdims=True)
    i2 = jnp.min(jnp.where(pe2 == p2, lane, big), axis=-1, keepdims=True)
    tot = p1 + p2
    w1 = g_w * p1 / tot
    w2 = g_w * p2 / tot
    eid = jnp.where(lane == 0.0, i1 - N_GROUPS, jnp.where(lane == 1.0, i2 - N_GROUPS, 0.0))
    eid_ref[...] = eid.astype(jnp.int32)
    wt_ref[...] = jnp.where(lane == 0.0, w1, jnp.where(lane == 1.0, w2, 0.0))


def _router(x2, g, wr, br, tm):
    n, d = x2.shape
    return pl.pallas_call(
        _router_kernel,
        out_shape=(jax.ShapeDtypeStruct((n, d), F32),
                   jax.ShapeDtypeStruct((n, LANES), jnp.int32),
                   jax.ShapeDtypeStruct((n, LANES), F32)),
        grid=(n // tm,),
        in_specs=[pl.BlockSpec((tm, d), lambda i: (i, 0)),
                  pl.BlockSpec((1, d), lambda i: (0, 0)),
                  pl.BlockSpec((d, LANES), lambda i: (0, 0)),
                  pl.BlockSpec((1, LANES), lambda i: (0, 0))],
        out_specs=(pl.BlockSpec((tm, d), lambda i: (i, 0)),
                   pl.BlockSpec((tm, LANES), lambda i: (i, 0)),
                   pl.BlockSpec((tm, LANES), lambda i: (i, 0))),
        compiler_params=_params(1),
        name="ffn_norm_router",
    )(x2, g.reshape(1, d), wr, br)


def _moe_kernel(te_ref, nv_ref, slot_ref, hf_hbm, wup_ref, wdn_ref, y_hbm,
                xg, yb, wup_bf, wdn_bf, sem):
    t = pl.program_id(0)
    nv = nv_ref[t]
    prev_e = te_ref[jnp.maximum(t - 1, 0)]
    new_expert = (t == 0) | (te_ref[t] != prev_e)

    @pl.when(t == 0)
    def _():
        xg[...] = jnp.zeros_like(xg)

    @pl.when(new_expert)
    def _():
        wup_bf[...] = wup_ref[0].astype(BF16)
        wdn_bf[...] = wdn_ref[0].astype(BF16)

    def row_in(r, token_row):
        return pltpu.make_async_copy(hf_hbm.at[pl.ds(token_row, 1), :], xg.at[pl.ds(r, 1), :], sem.at[0])

    def row_out(r, out_row):
        return pltpu.make_async_copy(yb.at[pl.ds(r, 1), :], y_hbm.at[pl.ds(out_row, 1), :], sem.at[1])

    @pl.when(nv > 0)
    def _():
        def start_in(r, c):
            row_in(r, slot_ref[0, 0, r] // TOP_K).start()
            return c

        def wait_in(r, c):
            row_in(r, 0).wait()
            return c

        lax.fori_loop(0, nv, start_in, 0)
        lax.fori_loop(0, nv, wait_in, 0)
        h1 = jnp.dot(xg[...].astype(BF16), wup_bf[...], preferred_element_type=F32)
        gate = h1[:, :D_FF]
        up = h1[:, D_FF:]
        act = (gate * jax.nn.sigmoid(gate)) * up
        yb[...] = jnp.dot(act.astype(BF16), wdn_bf[...], preferred_element_type=F32)

        def start_out(r, c):
            row_out(r, slot_ref[0, 0, r]).start()
            return c

        def wait_out(r, c):
            row_out(r, 0).wait()
            return c

        lax.fori_loop(0, nv, start_out, 0)
        lax.fori_loop(0, nv, wait_out, 0)


def _moe(hf, tile_expert, tile_rows, slots, w_up, w_down, n_assign):
    n_tiles = tile_expert.shape[0]
    d = hf.shape[1]
    grid_spec = pltpu.PrefetchScalarGridSpec(
        num_scalar_prefetch=2,
        grid=(n_tiles,),
        in_specs=[pl.BlockSpec((1, 1, MOE_TILE), lambda t, te, nv: (t, 0, 0), memory_space=pltpu.SMEM),
                  pl.BlockSpec(memory_space=pl.ANY),
                  pl.BlockSpec((1, d, 2 * D_FF), lambda t, te, nv: (te[t], 0, 0)),
                  pl.BlockSpec((1, D_FF, d), lambda t, te, nv: (te[t], 0, 0))],
        out_specs=pl.BlockSpec(memory_space=pl.ANY),
        scratch_shapes=[pltpu.VMEM((MOE_TILE, d), F32),
                        pltpu.VMEM((MOE_TILE, d), F32),
                        pltpu.VMEM((d, 2 * D_FF), BF16),
                        pltpu.VMEM((D_FF, d), BF16),
                        pltpu.SemaphoreType.DMA((2,))],
    )
    return pl.pallas_call(
        _moe_kernel,
        out_shape=jax.ShapeDtypeStruct((n_assign, d), F32),
        grid_spec=grid_spec,
        compiler_params=_params(1),
        name="moe_experts",
    )(tile_expert, tile_rows, slots, hf, w_up, w_down)


def _combine_kernel(x_ref, y_ref, w_ref, o_ref):
    d = x_ref.shape[1]
    w = w_ref[...]
    o_ref[...] = x_ref[...] + (y_ref[:, :d] * w[:, 0:1] + y_ref[:, d:] * w[:, 1:2])


def _combine(x2, yk, wts, tm):
    n, d = x2.shape
    return pl.pallas_call(
        _combine_kernel,
        out_shape=jax.ShapeDtypeStruct((n, d), F32),
        grid=(n // tm,),
        in_specs=[pl.BlockSpec((tm, d), lambda i: (i, 0)),
                  pl.BlockSpec((tm, TOP_K * d), lambda i: (i, 0)),
                  pl.BlockSpec((tm, LANES), lambda i: (i, 0))],
        out_specs=pl.BlockSpec((tm, d), lambda i: (i, 0)),
        compiler_params=_params(1),
        name="moe_combine",
    )(x2, yk, wts)


def _rope_tables(pos):
    half = ROPE_DIM // 2
    inv = ROPE_THETA ** (-jnp.arange(half, dtype=F32) / half)
    ang = pos.astype(F32)[:, None] * inv[None, :]
    cos, sin = jnp.cos(ang), jnp.sin(ang)
    n = pos.shape[0]
    pad = jnp.zeros((n, SWA_HD - ROPE_DIM), F32)
    cos_h = jnp.concatenate([cos, cos, pad + 1.0], axis=1)
    sa_h = jnp.concatenate([-sin, jnp.zeros_like(sin), pad], axis=1)
    sb_h = jnp.concatenate([jnp.zeros_like(sin), sin, pad], axis=1)
    reps = LANES // SWA_HD
    return tuple(jnp.tile(a, (1, reps)) for a in (cos_h, sa_h, sb_h))


def _moe_schedule(eid, n_tiles):
    a = eid.shape[0]
    order = jnp.argsort(eid, stable=True).astype(jnp.int32)
    counts = jnp.bincount(eid, length=N_EXPERTS).astype(jnp.int32)
    tiles_per = (counts + MOE_TILE - 1) // MOE_TILE
    tile_end = jnp.cumsum(tiles_per)
    tile_start = tile_end - tiles_per
    sorted_start = jnp.cumsum(counts) - counts
    tile_id = jnp.arange(n_tiles, dtype=jnp.int32)
    used = tile_id < tile_end[-1]
    te = jnp.minimum(jnp.searchsorted(tile_end, tile_id, side="right"), N_EXPERTS - 1).astype(jnp.int32)
    last_used_e = te[jnp.maximum(tile_end[-1] - 1, 0)]
    te = jnp.where(used, te, last_used_e)
    row_in_expert = (tile_id - tile_start[te]) * MOE_TILE
    rows_valid = jnp.where(used, jnp.clip(counts[te] - row_in_expert, 0, MOE_TILE), 0).astype(jnp.int32)
    r = jnp.arange(MOE_TILE, dtype=jnp.int32)[None, :]
    src = sorted_start[te][:, None] + row_in_expert[:, None] + r
    slots = jnp.where(r < rows_valid[:, None], order[jnp.clip(src, 0, a - 1)], 0)
    return te, rows_valid, slots.reshape(n_tiles, 1, MOE_TILE).astype(jnp.int32)


def kernel(x_prompt, x_sample, state_gla, cache_swa_k, cache_swa_v, cache_mem_k, cache_mem_v,
           mem_prompt, norm_mix_g, w_in, w_a2, b_a2, gla_norm_g, swa_q_norm_g, swa_k_norm_g,
           swa_sinks, norm_mem_g, w_mem_kv, mem_q_norm_g, mem_k_norm_g, w_gate, b_gate,
           w_branch, w_out, norm_ffn_g, w_router_group, b_router_group, w_router_expert,
           b_router_expert, w_up, w_down):
    bp, tp, d = x_prompt.shape
    bs, ts, _ = x_sample.shape
    n_p, n_s = bp * tp, bs * ts
    n = n_p + n_s
    tm, tn = ROW_TILE, COL_TILE
    assert d == D_MODEL and n_p % tm == 0 and n_s % tm == 0 and w_in.shape[0] == 1
    keep_s = cache_swa_k.shape[2]
    assert keep_s == WINDOW and tp % WINDOW == 0

    qk_w = GLA_HEADS * GLA_DK
    v_w = GLA_HEADS * GLA_DV
    c0 = 2 * qk_w + 2 * v_w
    c1 = c0 + GLA_LOWRANK
    sq_w = SWA_HEADS * SWA_HD
    kv_w = SWA_KV_HEADS * SWA_HD
    w_in0 = w_in[0]
    w_gla = w_in0[:, :c0].astype(BF16)
    w_ga = jnp.pad(w_in0[:, c0:c1], ((0, 0), (0, LANES - GLA_LOWRANK))).astype(BF16)
    w_sq = w_in0[:, c1:c1 + sq_w].astype(BF16)
    w_skv = w_in0[:, c1 + sq_w:c1 + sq_w + 2 * kv_w].astype(BF16)
    w_mq = w_in0[:, c1 + sq_w + 2 * kv_w:].astype(BF16)
    w_gate_b = w_gate[0].astype(BF16)
    w_branch_b = w_branch[0].astype(BF16)
    w_out_b = w_out[0].astype(BF16)
    w_memkv_b = w_mem_kv[0].astype(BF16)
    w_a2_b = jnp.pad(w_a2[0], ((0, LANES - GLA_LOWRANK), (0, 0))).astype(BF16)
    w_router = jnp.pad(jnp.concatenate([w_router_group[0], w_router_expert[0]], axis=1),
                       ((0, 0), (0, LANES - N_GROUPS - N_EXPERTS))).astype(BF16)
    b_router = jnp.pad(jnp.concatenate([b_router_group[0], b_router_expert[0]]),
                       (0, LANES - N_GROUPS - N_EXPERTS)).reshape(1, LANES)

    pos = jnp.concatenate([jnp.tile(jnp.arange(tp, dtype=jnp.int32), bp),
                           jnp.tile(PAST_LEN + jnp.arange(ts, dtype=jnp.int32), bs)])
    cos_t, sa_t, sb_t = _rope_tables(pos)
    seg_id = jnp.arange(tn, dtype=jnp.int32) // SWA_HD
    seg = (seg_id[:, None] == seg_id[None, :]).astype(BF16)
    rope_specs = [pl.BlockSpec((tm, LANES), lambda i, j: (i, 0))] * 3
    row_vec = lambda width: pl.BlockSpec((1, width), lambda i, j: (0, 0))
    seg_spec = pl.BlockSpec((tn, tn), lambda i, j: (0, 0))

    x_cat = jnp.concatenate([x_prompt.reshape(n_p, d), x_sample.reshape(n_s, d)], axis=0)
    h = _rms_norm_rows(x_cat, norm_mix_g[0], BF16, tm)

    qkvg = _matmul(_mm_plain_kernel, h, w_gla, [], [], F32, tm, tn, "proj_gla")
    ga = _matmul(_mm_plain_kernel, h, w_ga, [], [], F32, tm, LANES, "proj_gla_lowrank")
    q_gain = jnp.tile(swa_q_norm_g[0], tn // SWA_HD).reshape(1, tn)
    q_swa = _matmul(functools.partial(_mm_qknorm_rope_kernel, keep_from=None), h, w_sq,
                    [q_gain, seg, cos_t, sa_t, sb_t], [row_vec(tn), seg_spec] + rope_specs,
                    BF16, tm, tn, "proj_swa_q")
    k_gain = jnp.tile(swa_k_norm_g[0], tn // SWA_HD).reshape(1, tn)
    kv_swa = _matmul(functools.partial(_mm_qknorm_rope_kernel, keep_from=kv_w), h, w_skv,
                     [k_gain, seg, cos_t, sa_t, sb_t], [row_vec(tn), seg_spec] + rope_specs,
                     F32, tm, tn, "proj_swa_kv")
    q_mem = _matmul(functools.partial(_mm_headnorm_kernel, norm_tiles=None), h, w_mq,
                    [mem_q_norm_g[0].reshape(1, MEM_HD)], [row_vec(MEM_HD)], BF16, tm, MEM_HD, "proj_mem_q")
    gates = _matmul(_mm_sigmoid_kernel, h, w_gate_b, [b_gate[0].reshape(1, -1)],
                    [pl.BlockSpec((1, tn), lambda i, j: (0, j))], F32, tm, tn, "proj_gates")

    mem_rows = bp * N_MEM
    hm = _rms_norm_rows(mem_prompt.reshape(mem_rows, d), norm_mem_g[0], BF16, N_MEM)
    mem_kv = _matmul(functools.partial(_mm_headnorm_kernel, norm_tiles=MEM_HEADS), hm, w_memkv_b,
                     [mem_k_norm_g[0].reshape(1, MEM_HD)], [row_vec(MEM_HD)], F32, N_MEM, MEM_HD, "mem_kv")
    mem_w = MEM_HEADS * MEM_HD

    ba = b_a2[0].reshape(1, qk_w)
    gn = gla_norm_g[0].reshape(1, GLA_DV)
    s0_p = jnp.zeros((bp, GLA_HEADS, GLA_DK, GLA_DV), F32)
    o_gla_p, gla_state_p = _gla(qkvg, ga, w_a2_b, ba, gn, s0_p, bp, tp, 0, 256, CHUNK)
    o_gla_s, gla_state_s = _gla(qkvg, ga, w_a2_b, ba, gn, state_gla[0], bs, ts, n_p, ts, min(CHUNK, ts))

    sinks = swa_sinks[0]
    o_swa_p = _swa(sinks, q_swa, 0, kv_swa, kv_swa, (0, 1), kv_swa, kv_swa, (0, 1), 0,
                   bp, tp, WINDOW, False)
    ck = cache_swa_k[0].reshape(bs * keep_s, kv_w)
    cv = cache_swa_v[0].reshape(bs * keep_s, kv_w)
    o_swa_s = _swa(sinks, q_swa, n_p, ck, cv, (0, 0), kv_swa, kv_swa, (0, 1), n_p,
                   bs, ts, ts, True)

    o_mem_p = _mem_attn(q_mem, 0, 0, mem_kv, 0, mem_kv, MEM_HEADS, bp, tp, tm)
    cmk = cache_mem_k[0].reshape(bs * N_MEM, mem_w)
    cmv = cache_mem_v[0].reshape(bs * N_MEM, mem_w)
    o_mem_s = _mem_attn(q_mem, n_p, 0, cmk, 0, cmv, 0, bs, ts, ts)

    branches = [jnp.concatenate([p, s], axis=0) for p, s in
                ((o_gla_p, o_gla_s), (o_swa_p, o_swa_s), (o_mem_p, o_mem_s))]
    merged = _merge(branches, w_branch_b, gates, tm, tn)
    x2 = _matmul(_mm_residual_kernel, merged, w_out_b, [x_cat],
                 [pl.BlockSpec((tm, tn), lambda i, j: (i, j))], F32, tm, tn, "out_proj")

    hf, eid, wts = _router(x2, norm_ffn_g[0], w_router, b_router, tm)
    n_assign = TOP_K * n
    n_tiles = n_assign // MOE_TILE + N_EXPERTS
    tile_expert, tile_rows, slots = _moe_schedule(eid[:, :TOP_K].reshape(-1), n_tiles)
    yk = _moe(hf, tile_expert, tile_rows, slots, w_up[0], w_down[0], n_assign)
    y = _combine(x2, yk.reshape(n, TOP_K * d), wts, tm)

    y_p = y[:n_p].reshape(bp, tp, d)
    y_s = y[n_p:].reshape(bs, ts, d)
    kv_p = kv_swa[:n_p].reshape(bp, tp, 2, SWA_KV_HEADS, SWA_HD)[:, tp - WINDOW:]
    kv_s = kv_swa[n_p:].reshape(bs, ts, 2, SWA_KV_HEADS, SWA_HD)
    swk_s = jnp.concatenate([cache_swa_k[0], kv_s[:, :, 0]], axis=1)[:, ts:ts + keep_s]
    swv_s = jnp.concatenate([cache_swa_v[0], kv_s[:, :, 1]], axis=1)[:, ts:ts + keep_s]
    mk_p = mem_kv[:, :mem_w].reshape(bp, N_MEM, MEM_HEADS, MEM_HD)
    mv_p = mem_kv[:, mem_w:].reshape(bp, N_MEM, MEM_HEADS, MEM_HD)
    return (y_p, y_s, gla_state_p[None], kv_p[:, :, 0][None], kv_p[:, :, 1][None], mk_p[None], mv_p[None],
            gla_state_s[None], swk_s[None], swv_s[None])
```

```python
import functools

import jax
import jax.numpy as jnp
from jax import lax
from jax.experimental import pallas as pl
from jax.experimental.pallas import tpu as pltpu

F32 = jnp.float32
BF16 = jnp.bfloat16

D_MODEL = 2048
CHUNK = 64
EPS = 1e-6
PAST_LEN = 1024
GLA_HEADS = 4
GLA_DV = 512
GLA_DK = 256
GLA_LOWRANK = 16
GLA_NORMALIZER = 16.0
SWA_HD = 64
SWA_HEADS = 32
SWA_KV_HEADS = 4
SWA_GROUP = 8
WINDOW = 128
ROPE_DIM = 16
ROPE_THETA = 500000.0
N_MEM = 256
MEM_HEADS = 4
MEM_HD = 512
N_GROUPS = 8
EXPERTS_PER_GROUP = 8
N_EXPERTS = 64
TOP_K = 2
D_FF = 512

LANES = 128
VMEM_LIMIT = 56 * 1024 * 1024
ROW_TILE = 512
PROJ_ROWS = 1536
COL_TILE = 512
WIDE_COLS = 1024
MOE_TILE = 256
NEG_BIG = -1e30


def _params(n_axes):
    return pltpu.CompilerParams(dimension_semantics=("arbitrary",) * n_axes,
                                vmem_limit_bytes=VMEM_LIMIT)


def _norm_kernel(x_ref, g_ref, o_ref):
    x = x_ref[...]
    y = x * lax.rsqrt(jnp.mean(x * x, axis=-1, keepdims=True) + EPS)
    o_ref[...] = (y * g_ref[...]).astype(o_ref.dtype)


def _rms_norm_rows(x, g, out_dtype, tm):
    n, d = x.shape
    return pl.pallas_call(
        _norm_kernel,
        out_shape=jax.ShapeDtypeStruct((n, d), out_dtype),
        grid=(n // tm,),
        in_specs=[pl.BlockSpec((tm, d), lambda i: (i, 0)),
                  pl.BlockSpec((1, d), lambda i: (0, 0))],
        out_specs=pl.BlockSpec((tm, d), lambda i: (i, 0)),
        compiler_params=_params(1),
        name="rms_norm_rows",
    )(x, g.reshape(1, d))


def _norm2_kernel(xp_ref, xs_ref, g_ref, o_ref, *, p_tiles):
    def emit(x_ref):
        x = x_ref[...]
        y = x * lax.rsqrt(jnp.mean(x * x, axis=-1, keepdims=True) + EPS)
        o_ref[...] = (y * g_ref[...]).astype(o_ref.dtype)

    @pl.when(pl.program_id(0) < p_tiles)
    def _():
        emit(xp_ref)

    @pl.when(pl.program_id(0) >= p_tiles)
    def _():
        emit(xs_ref)


def _split_specs(p_tiles, tm, d):
    return (pl.BlockSpec((tm, d), lambda i: (jnp.minimum(i, p_tiles - 1), 0)),
            pl.BlockSpec((tm, d), lambda i: (jnp.maximum(i - p_tiles, 0), 0)))


def _rms_norm_two(xp, xs, g, tm):
    (n_p, d), n_s = xp.shape, xs.shape[0]
    p_tiles = n_p // tm
    return pl.pallas_call(
        functools.partial(_norm2_kernel, p_tiles=p_tiles),
        out_shape=jax.ShapeDtypeStruct((n_p + n_s, d), BF16),
        grid=((n_p + n_s) // tm,),
        in_specs=[*_split_specs(p_tiles, tm, d), pl.BlockSpec((1, d), lambda i: (0, 0))],
        out_specs=pl.BlockSpec((tm, d), lambda i: (i, 0)),
        compiler_params=_params(1),
        name="rms_norm_mix",
    )(xp, xs, g.reshape(1, d))


def _segment_rms(acc, seg_ref, inv_width):
    sq = acc * acc
    hi = sq.astype(BF16)
    lo = (sq - hi.astype(F32)).astype(BF16)
    ss = (jnp.dot(hi, seg_ref[...], preferred_element_type=F32)
          + jnp.dot(lo, seg_ref[...], preferred_element_type=F32))
    return ss * inv_width


def _rope(y, cos_ref, sa_ref, sb_ref):
    width = y.shape[1]
    reps = width // LANES
    c = jnp.concatenate([cos_ref[...]] * reps, axis=1)
    sa = jnp.concatenate([sa_ref[...]] * reps, axis=1)
    sb = jnp.concatenate([sb_ref[...]] * reps, axis=1)
    half = ROPE_DIM // 2
    return y * c + pltpu.roll(y, width - half, 1) * sa + pltpu.roll(y, half, 1) * sb


def _ep_plain(acc):
    return acc


def _ep_lowrank(acc):
    lane = lax.broadcasted_iota(jnp.int32, acc.shape, 1)
    return jnp.where(lane < GLA_LOWRANK, acc, 0.0)


def _ep_sigmoid(acc, b_ref):
    return jax.nn.sigmoid(acc + b_ref[...])


def _ep_qknorm_rope(acc, g_ref, seg_ref, cos_ref, sa_ref, sb_ref, *, keep_from):
    ms = _segment_rms(acc, seg_ref, 1.0 / SWA_HD)
    y = acc * lax.rsqrt(ms + EPS) * g_ref[...]
    y = _rope(y, cos_ref, sa_ref, sb_ref)
    if keep_from is not None:
        col = lax.broadcasted_iota(jnp.int32, y.shape, 1)
        y = jnp.where(col < keep_from, y, acc)
    return y


def _ep_headnorm(acc, g_ref, *, norm_tiles):
    y = acc * lax.rsqrt(jnp.mean(acc * acc, axis=-1, keepdims=True) + EPS) * g_ref[...]
    if norm_tiles is not None:
        y = jnp.where(pl.program_id(0) < norm_tiles, y, acc)
    return y


def _stage_weight_tile(w_refs, wbf, shift):
    if shift == 0:
        wbf[...] = w_refs[0][...].astype(BF16)
        return
    main, ext = w_refs
    n_slabs = wbf.shape[1] // LANES
    lane = lax.broadcasted_iota(jnp.int32, (wbf.shape[0], LANES), 1)
    rolled = [pltpu.roll(main[:, b * LANES:(b + 1) * LANES], LANES - shift, 1) for b in range(n_slabs)]
    rolled.append(pltpu.roll(ext[...], LANES - shift, 1))
    for b in range(n_slabs):
        wbf[:, b * LANES:(b + 1) * LANES] = jnp.where(lane < LANES - shift, rolled[b], rolled[b + 1]).astype(BF16)


def _wmm_kernel(*refs, n_w, shift, epilogue):
    a_ref, w_refs, extras, o_ref, wbf = refs[0], refs[1:1 + n_w], refs[1 + n_w:-2], refs[-2], refs[-1]

    @pl.when(pl.program_id(1) == 0)
    def _():
        _stage_weight_tile(w_refs, wbf, shift)

    acc = jnp.dot(a_ref[...], wbf[...], preferred_element_type=F32)
    o_ref[...] = epilogue(acc, *extras).astype(o_ref.dtype)


def _wmatmul(epilogue, a, w, col0, n_cols, shift, extras, extra_specs, out_dtype, tm, tn, name):
    m, k = a.shape
    assert m % tm == 0 and n_cols % tn == 0 and col0 % tn == 0 and 0 <= shift < LANES
    j0 = col0 // tn
    w_specs = [pl.BlockSpec((k, tn), lambda j, i: (0, j0 + j))]
    if shift:
        e0 = col0 // LANES
        w_specs.append(pl.BlockSpec((k, LANES), lambda j, i: (0, e0 + (j + 1) * (tn // LANES))))
    kernel = functools.partial(_wmm_kernel, n_w=len(w_specs), shift=shift, epilogue=epilogue)
    return pl.pallas_call(
        kernel,
        out_shape=jax.ShapeDtypeStruct((m, n_cols), out_dtype),
        grid=(n_cols // tn, m // tm),
        in_specs=[pl.BlockSpec((tm, k), lambda j, i: (i, 0))] + w_specs + list(extra_specs),
        out_specs=pl.BlockSpec((tm, tn), lambda j, i: (i, j)),
        scratch_shapes=[pltpu.VMEM((k, tn), BF16)],
        compiler_params=_params(2),
        name=name,
    )(a, *([w] * len(w_specs)), *extras)


def _gla_kernel(q_ref, k_ref, v_ref, gg_ref, ga_ref, wa_ref, ba_ref, gn_ref, s0_ref,
                o_ref, sout_ref, s_scr, *, chunk, n_chunks):
    t = pl.program_id(1)

    @pl.when(t == 0)
    def _():
        s_scr[...] = s0_ref[0]

    row = lax.broadcasted_iota(jnp.int32, (chunk, chunk), 0)
    col = lax.broadcasted_iota(jnp.int32, (chunk, chunk), 1)
    causal = row >= col
    tril = causal.astype(F32)

    def one_chunk(ci, carry):
        rows = pl.ds(pl.multiple_of(ci * chunk, chunk), chunk)
        z = jnp.dot(ga_ref[rows, :].astype(BF16), wa_ref[...], preferred_element_type=F32) + ba_ref[...]
        log_a = (jnp.minimum(z, 0.0) - jnp.log1p(jnp.exp(-jnp.abs(z)))) * (1.0 / GLA_NORMALIZER)
        b_all = jnp.dot(tril, log_a, preferred_element_type=F32, precision=lax.Precision.HIGHEST)
        for h in range(GLA_HEADS):
            ks = slice(h * GLA_DK, (h + 1) * GLA_DK)
            vs = slice(h * GLA_DV, (h + 1) * GLA_DV)
            b = b_all[:, ks]
            b_last = b[chunk - 1:chunk, :]
            q = q_ref[rows, ks] * (GLA_DK ** -0.5)
            k = k_ref[rows, ks]
            vb = v_ref[rows, vs].astype(BF16)
            q_t = (q * jnp.exp(b)).astype(BF16)
            k_t = (k * jnp.exp(-b)).astype(BF16)
            k_u = (k * jnp.exp(b_last - b)).astype(BF16)
            att = lax.dot_general(q_t, k_t, (((1,), (1,)), ((), ())), preferred_element_type=F32)
            att = jnp.where(causal, att, 0.0)
            s_old = s_scr[h]
            o = (jnp.dot(q_t, s_old.astype(BF16), preferred_element_type=F32)
                 + jnp.dot(att.astype(BF16), vb, preferred_element_type=F32))
            decay = jnp.transpose(jnp.broadcast_to(jnp.exp(b_last), (LANES, GLA_DK)))
            decay = jnp.concatenate([decay] * (GLA_DV // LANES), axis=1)
            s_scr[h] = decay * s_old + lax.dot_general(k_u, vb, (((0,), (0,)), ((), ())),
                                                       preferred_element_type=F32)
            on = o * lax.rsqrt(jnp.mean(o * o, axis=-1, keepdims=True) + EPS) * gn_ref[...]
            gg = gg_ref[rows, vs]
            o_ref[rows, vs] = (on * (gg * jax.nn.sigmoid(gg))).astype(o_ref.dtype)
        return carry

    lax.fori_loop(0, n_chunks, one_chunk, 0)

    @pl.when(t == pl.num_programs(1) - 1)
    def _():
        sout_ref[0] = s_scr[...]


def _gla(qkvg, ga, wa, ba, gn, s0, batch, seq, row0, tb, chunk):
    nt = seq // tb
    base = row0 // tb
    qk_w = GLA_HEADS * GLA_DK
    v_w = GLA_HEADS * GLA_DV
    rows = lambda b, t: base + b * nt + t
    kernel = functools.partial(_gla_kernel, chunk=chunk, n_chunks=tb // chunk)
    return pl.pallas_call(
        kernel,
        out_shape=(jax.ShapeDtypeStruct((batch * seq, v_w), BF16),
                   jax.ShapeDtypeStruct((batch, GLA_HEADS, GLA_DK, GLA_DV), F32)),
        grid=(batch, nt),
        in_specs=[pl.BlockSpec((tb, qk_w), lambda b, t: (rows(b, t), 0)),
                  pl.BlockSpec((tb, qk_w), lambda b, t: (rows(b, t), 1)),
                  pl.BlockSpec((tb, v_w), lambda b, t: (rows(b, t), 1)),
                  pl.BlockSpec((tb, v_w), lambda b, t: (rows(b, t), 2)),
                  pl.BlockSpec((tb, LANES), lambda b, t: (rows(b, t), 0)),
                  pl.BlockSpec((LANES, qk_w), lambda b, t: (0, 0)),
                  pl.BlockSpec((1, qk_w), lambda b, t: (0, 0)),
                  pl.BlockSpec((1, GLA_DV), lambda b, t: (0, 0)),
                  pl.BlockSpec((1, GLA_HEADS, GLA_DK, GLA_DV), lambda b, t: (b, 0, 0, 0))],
        out_specs=(pl.BlockSpec((tb, v_w), lambda b, t: (b * nt + t, 0)),
                   pl.BlockSpec((1, GLA_HEADS, GLA_DK, GLA_DV), lambda b, t: (b, 0, 0, 0))),
        scratch_shapes=[pltpu.VMEM((GLA_HEADS, GLA_DK, GLA_DV), F32)],
        compiler_params=_params(2),
        name="gla_chunks",
    )(qkvg, qkvg, qkvg, qkvg, ga, wa, ba, gn, s0)


def _swa_kernel(sink_ref, q_ref, kp_ref, vp_ref, ko_ref, vo_ref, o_ref, *, tq, prev_from_cache):
    i = pl.program_id(1)
    nk = WINDOW + tq
    k_all = jnp.concatenate([kp_ref[...], ko_ref[...]], axis=0)
    v_all = jnp.concatenate([vp_ref[...], vo_ref[...]], axis=0)
    qc = lax.broadcasted_iota(jnp.int32, (tq, nk), 0) // CHUNK + WINDOW // CHUNK
    kcol = lax.broadcasted_iota(jnp.int32, (tq, nk), 1)
    kc = kcol // CHUNK
    valid = (kc <= qc) & (kc >= qc - WINDOW // CHUNK)
    if not prev_from_cache:
        valid = valid & ((kcol >= WINDOW) | (i > 0))
    lane = lax.broadcasted_iota(jnp.int32, (nk, LANES), 1)
    low = lane < SWA_HD
    scale = SWA_HD ** -0.5
    for g in range(SWA_KV_HEADS):
        slab = slice((g // 2) * LANES, (g // 2 + 1) * LANES)
        k2 = k_all[:, slab]
        v2 = v_all[:, slab]
        k2r = pltpu.roll(k2, SWA_HD, 1)
        v2r = pltpu.roll(v2, SWA_HD, 1)
        if g % 2 == 0:
            k_lo, k_hi, v_lo, v_hi = k2, k2r, v2, v2r
        else:
            k_lo, k_hi, v_lo, v_hi = k2r, k2, v2r, v2
        zero = jnp.zeros_like(k2)
        km = (jnp.where(low, k_lo, zero).astype(BF16), jnp.where(low, zero, k_hi).astype(BF16))
        vm = (jnp.where(low, v_lo, zero).astype(BF16), jnp.where(low, zero, v_hi).astype(BF16))
        for j in range(SWA_GROUP // 2):
            qslab = slice((g * 4 + j) * LANES, (g * 4 + j + 1) * LANES)
            qs = q_ref[:, qslab]
            acc = jnp.zeros((tq, LANES), F32)
            for half in range(2):
                sink = sink_ref[g * SWA_GROUP + 2 * j + half]
                s = lax.dot_general(qs, km[half], (((1,), (1,)), ((), ())),
                                    preferred_element_type=F32) * scale
                s = jnp.where(valid, s, NEG_BIG)
                m = jnp.maximum(jnp.max(s, axis=-1, keepdims=True), sink)
                e = jnp.exp(s - m)
                denom = jnp.sum(e, axis=-1, keepdims=True) + jnp.exp(sink - m)
                p = (e / denom).astype(BF16)
                acc = acc + jnp.dot(p, vm[half], preferred_element_type=F32)
            o_ref[:, qslab] = acc.astype(o_ref.dtype)


def _swa(sinks, q, q_row0, k_prev, v_prev, prev_col, k_own, v_own, own_col, own_row0,
         batch, seq, tq, prev_from_cache):
    nt = seq // tq
    qb = q_row0 // tq
    ob = own_row0 // tq
    kv_w = SWA_KV_HEADS * SWA_HD
    if prev_from_cache:
        prev_map = lambda b, t, c: (b, c)
    else:
        per = seq // WINDOW
        prev_map = lambda b, t, c: (b * per + jnp.maximum(t * (tq // WINDOW) - 1, 0), c)
    kernel = functools.partial(_swa_kernel, tq=tq, prev_from_cache=prev_from_cache)
    return pl.pallas_call(
        kernel,
        out_shape=jax.ShapeDtypeStruct((batch * seq, SWA_HEADS * SWA_HD), BF16),
        grid=(batch, nt),
        in_specs=[pl.BlockSpec(memory_space=pltpu.SMEM),
                  pl.BlockSpec((tq, SWA_HEADS * SWA_HD), lambda b, t: (qb + b * nt + t, 0)),
                  pl.BlockSpec((WINDOW, kv_w), lambda b, t: prev_map(b, t, prev_col[0])),
                  pl.BlockSpec((WINDOW, kv_w), lambda b, t: prev_map(b, t, prev_col[1])),
                  pl.BlockSpec((tq, kv_w), lambda b, t: (ob + b * nt + t, own_col[0])),
                  pl.BlockSpec((tq, kv_w), lambda b, t: (ob + b * nt + t, own_col[1]))],
        out_specs=pl.BlockSpec((tq, SWA_HEADS * SWA_HD), lambda b, t: (b * nt + t, 0)),
        compiler_params=_params(2),
        name="swa_band",
    )(sinks, q, k_prev, v_prev, k_own, v_own)


def _mem_attn_kernel(q_ref, k_ref, v_ref, o_ref):
    s = lax.dot_general(q_ref[...], k_ref[...].astype(BF16), (((1,), (1,)), ((), ())),
                        preferred_element_type=F32) * (MEM_HD ** -0.5)
    m = jnp.max(s, axis=-1, keepdims=True)
    e = jnp.exp(s - m)
    p = (e / jnp.sum(e, axis=-1, keepdims=True)).astype(BF16)
    o_ref[...] = jnp.dot(p, v_ref[...].astype(BF16), preferred_element_type=F32).astype(o_ref.dtype)


def _mem_attn(q, q_row0, q_col0, mk, mk_col0, mv, mv_col0, batch, seq, tq):
    nt = seq // tq
    qb = q_row0 // tq
    return pl.pallas_call(
        _mem_attn_kernel,
        out_shape=jax.ShapeDtypeStruct((batch * seq, MEM_HEADS * MEM_HD), BF16),
        grid=(batch, MEM_HEADS, nt),
        in_specs=[pl.BlockSpec((tq, MEM_HD), lambda b, h, t: (qb + b * nt + t, q_col0 + h)),
                  pl.BlockSpec((N_MEM, MEM_HD), lambda b, h, t: (b, mk_col0 + h)),
                  pl.BlockSpec((N_MEM, MEM_HD), lambda b, h, t: (b, mv_col0 + h))],
        out_specs=pl.BlockSpec((tq, MEM_HD), lambda b, h, t: (b * nt + t, h)),
        compiler_params=_params(3),
        name="mem_attn",
    )(q, mk, mv)


def _merge_kernel(a0_ref, a1_ref, a2_ref, w_ref, g0_ref, g1_ref, g2_ref, o_ref):
    acc = g0_ref[...] * jnp.dot(a0_ref[...], w_ref[0], preferred_element_type=F32)
    acc = acc + g1_ref[...] * jnp.dot(a1_ref[...], w_ref[1], preferred_element_type=F32)
    acc = acc + g2_ref[...] * jnp.dot(a2_ref[...], w_ref[2], preferred_element_type=F32)
    o_ref[...] = acc.astype(o_ref.dtype)


def _merge(branches, w_branch, gates, tm, tn):
    n, d = branches[0].shape
    nj = d // tn
    a_spec = pl.BlockSpec((tm, d), lambda i, j: (i, 0))
    return pl.pallas_call(
        _merge_kernel,
        out_shape=jax.ShapeDtypeStruct((n, d), BF16),
        grid=(n // tm, nj),
        in_specs=[a_spec, a_spec, a_spec,
                  pl.BlockSpec((3, d, tn), lambda i, j: (0, 0, j)),
                  pl.BlockSpec((tm, tn), lambda i, j: (i, j)),
                  pl.BlockSpec((tm, tn), lambda i, j: (i, nj + j)),
                  pl.BlockSpec((tm, tn), lambda i, j: (i, 2 * nj + j))],
        out_specs=pl.BlockSpec((tm, tn), lambda i, j: (i, j)),
        compiler_params=_params(2),
        name="branch_merge",
    )(*branches, w_branch, gates, gates, gates)


def _outproj_router_kernel(m_ref, w_ref, xp_ref, xs_ref, g_ref, wr_ref, br_ref,
                           x2_ref, hf_ref, eid_ref, wt_ref, *, p_tiles):
    acc = jnp.dot(m_ref[...], w_ref[...], preferred_element_type=F32)

    @pl.when(pl.program_id(0) < p_tiles)
    def _():
        x2_ref[...] = xp_ref[...] + acc

    @pl.when(pl.program_id(0) >= p_tiles)
    def _():
        x2_ref[...] = xs_ref[...] + acc

    x = x2_ref[...]
    hf = x * lax.rsqrt(jnp.mean(x * x, axis=-1, keepdims=True) + EPS) * g_ref[...]
    hf_ref[...] = hf
    logits = jnp.dot(hf.astype(BF16), wr_ref[...], preferred_element_type=F32) + br_ref[...]
    lane = lax.broadcasted_iota(jnp.int32, logits.shape, 1).astype(F32)
    big = 1e6
    is_g = lane < N_GROUPS
    lg = jnp.where(is_g, logits, NEG_BIG)
    mg = jnp.max(lg, axis=-1, keepdims=True)
    gsel = jnp.min(jnp.where(is_g & (lg == mg), lane, big), axis=-1, keepdims=True)
    g_w = 1.0 / jnp.sum(jnp.where(is_g, jnp.exp(lg - mg), 0.0), axis=-1, keepdims=True)
    e_lo = N_GROUPS + gsel * EXPERTS_PER_GROUP
    in_grp = (lane >= e_lo) & (lane < e_lo + EXPERTS_PER_GROUP)
    le = jnp.where(in_grp, logits, NEG_BIG)
    me = jnp.max(le, axis=-1, keepdims=True)
    ee = jnp.where(in_grp, jnp.exp(le - me), 0.0)
    pe = ee / jnp.sum(ee, axis=-1, keepdims=True)
    pe = jnp.where(in_grp, pe, -1.0)
    p1 = jnp.max(pe, axis=-1, keepdims=True)
    i1 = jnp.min(jnp.where(pe == p1, lane, big), axis=-1, keepdims=True)
    pe2 = jnp.where(lane == i1, -1.0, pe)
    p2 = jnp.max(pe2, axis=-1, keepdims=True)
    i2 = jnp.min(jnp.where(pe2 == p2, lane, big), axis=-1, keepdims=True)
    tot = p1 + p2
    w1 = g_w * p1 / tot
    w2 = g_w * p2 / tot
    eid = jnp.where(lane == 0.0, i1 - N_GROUPS, jnp.where(lane == 1.0, i2 - N_GROUPS, 0.0))
    eid_ref[...] = eid.astype(jnp.int32)
    wt_ref[...] = jnp.where(lane == 0.0, w1, jnp.where(lane == 1.0, w2, 0.0))


def _outproj_router(merged, w_out, xp, xs, g, wr, br, tm):
    n, d = merged.shape
    p_tiles = xp.shape[0] // tm
    const = lambda shape: pl.BlockSpec(shape, lambda i: (0, 0), pipeline_mode=pl.Buffered(1))
    row = lambda width: pl.BlockSpec((tm, width), lambda i: (i, 0))
    return pl.pallas_call(
        functools.partial(_outproj_router_kernel, p_tiles=p_tiles),
        out_shape=(jax.ShapeDtypeStruct((n, d), F32),
                   jax.ShapeDtypeStruct((n, d), F32),
                   jax.ShapeDtypeStruct((n, LANES), jnp.int32),
                   jax.ShapeDtypeStruct((n, LANES), F32)),
        grid=(n // tm,),
        in_specs=[row(d), const((d, d)), *_split_specs(p_tiles, tm, d),
                  const((1, d)), const((d, LANES)), const((1, LANES))],
        out_specs=(row(d), row(d), row(LANES), row(LANES)),
        compiler_params=_params(1),
        name="outproj_router",
    )(merged, w_out, xp, xs, g.reshape(1, d), wr, br)


def _moe_kernel(te_ref, ok_ref, tok_ref, tok_next_ref, dst_ref, hf_hbm, wup_ref, wdn_ref, y_hbm,
                xg, yb, wup_bf, wdn_bf, in_sem, out_sem):
    t = pl.program_id(0)
    buf = t % 2
    valid = ok_ref[t] > 0
    valid_next = ok_ref[t + 1] > 0
    new_expert = (t == 0) | (te_ref[t] != te_ref[jnp.maximum(t - 1, 0)])

    def row_in(row, r, b):
        return pltpu.make_async_copy(hf_hbm.at[pl.ds(row, 1), :], xg.at[b, pl.ds(r, 1), :], in_sem.at[b])

    def row_out(row, r, b):
        return pltpu.make_async_copy(yb.at[b, pl.ds(r, 1), :], y_hbm.at[pl.ds(row, 1), :], out_sem.at[b])

    def tile_in(b):
        return pltpu.make_async_copy(hf_hbm.at[pl.ds(0, MOE_TILE), :], xg.at[b], in_sem.at[b])

    def tile_out(b):
        return pltpu.make_async_copy(yb.at[b], y_hbm.at[pl.ds(0, MOE_TILE), :], out_sem.at[b])

    @pl.when(t == 0)
    def _():
        def first(r, c):
            row_in(tok_ref[0, 0, r], r, 0).start()
            return c
        lax.fori_loop(0, MOE_TILE, first, 0)
        yb[...] = jnp.zeros_like(yb)
        n_real = y_hbm.shape[0] - 2 * MOE_TILE
        for b in range(2):
            spare = pltpu.make_async_copy(yb.at[b], y_hbm.at[pl.ds(n_real + b * MOE_TILE, MOE_TILE), :],
                                          out_sem.at[b])
            spare.start()
            spare.wait()

    @pl.when(valid_next)
    def _():
        for r in range(MOE_TILE):
            row_in(tok_next_ref[0, 0, r], r, 1 - buf).start()

    @pl.when(valid & new_expert)
    def _():
        wup_bf[...] = wup_ref[0].astype(BF16)
        wdn_bf[...] = wdn_ref[0].astype(BF16)

    @pl.when(valid)
    def _():
        tile_in(buf).wait()
        h1 = jnp.dot(xg[buf].astype(BF16), wup_bf[...], preferred_element_type=F32)
        gate = h1[:, :D_FF]
        up = h1[:, D_FF:]
        act = (gate * jax.nn.sigmoid(gate)) * up
        yb[buf] = jnp.dot(act.astype(BF16), wdn_bf[...], preferred_element_type=F32)
        for r in range(MOE_TILE):
            row_out(dst_ref[0, 0, r], r, buf).start()

        @pl.when(t > 0)
        def _():
            tile_out(1 - buf).wait()

        @pl.when(jnp.logical_not(valid_next))
        def _():
            tile_out(buf).wait()


def _moe(hf, tile_expert, tile_ok, tok_slots, dst_slots, w_up, w_down, out_rows):
    n_tiles = tile_expert.shape[0]
    d = hf.shape[1]
    slot_spec = lambda nxt: pl.BlockSpec((1, 1, MOE_TILE),
                                         lambda t, te, ok: (jnp.minimum(t + nxt, n_tiles - 1), 0, 0),
                                         memory_space=pltpu.SMEM)
    grid_spec = pltpu.PrefetchScalarGridSpec(
        num_scalar_prefetch=2,
        grid=(n_tiles,),
        in_specs=[slot_spec(0), slot_spec(1), slot_spec(0),
                  pl.BlockSpec(memory_space=pl.ANY),
                  pl.BlockSpec((1, d, 2 * D_FF), lambda t, te, ok: (te[t], 0, 0)),
                  pl.BlockSpec((1, D_FF, d), lambda t, te, ok: (te[t], 0, 0))],
        out_specs=pl.BlockSpec(memory_space=pl.ANY),
        scratch_shapes=[pltpu.VMEM((2, MOE_TILE, d), F32),
                        pltpu.VMEM((2, MOE_TILE, d), F32),
                        pltpu.VMEM((d, 2 * D_FF), BF16),
                        pltpu.VMEM((D_FF, d), BF16),
                        pltpu.SemaphoreType.DMA((2,)),
                        pltpu.SemaphoreType.DMA((2,))],
    )
    return pl.pallas_call(
        _moe_kernel,
        out_shape=jax.ShapeDtypeStruct((out_rows, d), F32),
        grid_spec=grid_spec,
        compiler_params=_params(1),
        name="moe_experts",
    )(tile_expert, tile_ok, tok_slots, tok_slots, dst_slots, hf, w_up, w_down)


def _combine_kernel(x_ref, y0_ref, y1_ref, w_ref, op_ref, os_ref, *, p_tiles):
    w = w_ref[...]
    y = x_ref[...] + (y0_ref[...] * w[:, 0:1] + y1_ref[...] * w[:, 1:2])

    @pl.when(pl.program_id(0) < p_tiles)
    def _():
        op_ref[...] = y

    @pl.when(pl.program_id(0) >= p_tiles)
    def _():
        os_ref[...] = y


def _combine(x2, yk, wts, n_p, tm):
    n, d = x2.shape
    p_tiles = n_p // tm
    k1 = n // tm
    return pl.pallas_call(
        functools.partial(_combine_kernel, p_tiles=p_tiles),
        out_shape=(jax.ShapeDtypeStruct((n_p, d), F32), jax.ShapeDtypeStruct((n - n_p, d), F32)),
        grid=(n // tm,),
        in_specs=[pl.BlockSpec((tm, d), lambda i: (i, 0)),
                  pl.BlockSpec((tm, d), lambda i: (i, 0)),
                  pl.BlockSpec((tm, d), lambda i: (k1 + i, 0)),
                  pl.BlockSpec((tm, LANES), lambda i: (i, 0))],
        out_specs=_split_specs(p_tiles, tm, d),
        compiler_params=_params(1),
        name="moe_combine",
    )(x2, yk, yk, wts)


def _rope_tables(pos):
    half = ROPE_DIM // 2
    inv = ROPE_THETA ** (-jnp.arange(half, dtype=F32) / half)
    ang = pos.astype(F32)[:, None] * inv[None, :]
    cos, sin = jnp.cos(ang), jnp.sin(ang)
    n = pos.shape[0]
    pad = jnp.zeros((n, SWA_HD - ROPE_DIM), F32)
    cos_h = jnp.concatenate([cos, cos, pad + 1.0], axis=1)
    sa_h = jnp.concatenate([-sin, jnp.zeros_like(sin), pad], axis=1)
    sb_h = jnp.concatenate([jnp.zeros_like(sin), sin, pad], axis=1)
    reps = LANES // SWA_HD
    return tuple(jnp.tile(a, (1, reps)) for a in (cos_h, sa_h, sb_h))


def _moe_schedule(eid, n_tok, n_tiles):
    a = eid.shape[0]
    order = jnp.argsort(eid, stable=True).astype(jnp.int32)
    counts = jnp.bincount(eid, length=N_EXPERTS).astype(jnp.int32)
    tiles_per = (counts + MOE_TILE - 1) // MOE_TILE
    tile_end = jnp.cumsum(tiles_per)
    tile_start = tile_end - tiles_per
    sorted_start = jnp.cumsum(counts) - counts
    tile_id = jnp.arange(n_tiles, dtype=jnp.int32)
    used = tile_id < tile_end[-1]
    te = jnp.minimum(jnp.searchsorted(tile_end, tile_id, side="right"), N_EXPERTS - 1).astype(jnp.int32)
    last_used_e = te[jnp.maximum(tile_end[-1] - 1, 0)]
    te = jnp.where(used, te, last_used_e)
    row_in_expert = (tile_id - tile_start[te]) * MOE_TILE
    rows_valid = jnp.where(used, jnp.clip(counts[te] - row_in_expert, 0, MOE_TILE), 0).astype(jnp.int32)
    r = jnp.arange(MOE_TILE, dtype=jnp.int32)[None, :]
    src = sorted_start[te][:, None] + row_in_expert[:, None] + r
    real = r < rows_valid[:, None]
    assign = order[jnp.clip(src, 0, a - 1)]
    tok = jnp.where(real, assign % n_tok, 0)
    spare = a + (tile_id[:, None] % 2) * MOE_TILE + r
    dst = jnp.where(real, assign, spare)
    ok = jnp.concatenate([used.astype(jnp.int32), jnp.zeros((1,), jnp.int32)])
    shape = (n_tiles, 1, MOE_TILE)
    return te, ok, tok.reshape(shape).astype(jnp.int32), dst.reshape(shape).astype(jnp.int32)


def kernel(x_prompt, x_sample, state_gla, cache_swa_k, cache_swa_v, cache_mem_k, cache_mem_v,
           mem_prompt, norm_mix_g, w_in, w_a2, b_a2, gla_norm_g, swa_q_norm_g, swa_k_norm_g,
           swa_sinks, norm_mem_g, w_mem_kv, mem_q_norm_g, mem_k_norm_g, w_gate, b_gate,
           w_branch, w_out, norm_ffn_g, w_router_group, b_router_group, w_router_expert,
           b_router_expert, w_up, w_down):
    bp, tp, d = x_prompt.shape
    bs, ts, _ = x_sample.shape
    n_p, n_s = bp * tp, bs * ts
    n = n_p + n_s
    tm, tn = ROW_TILE, COL_TILE
    tp_rows = PROJ_ROWS if n % PROJ_ROWS == 0 else tm
    assert d == D_MODEL and n_p % tm == 0 and n_s % tm == 0 and w_in.shape[0] == 1
    keep_s = cache_swa_k.shape[2]
    assert keep_s == WINDOW and tp % WINDOW == 0

    qk_w = GLA_HEADS * GLA_DK
    v_w = GLA_HEADS * GLA_DV
    c0 = 2 * qk_w + 2 * v_w
    sq_w = SWA_HEADS * SWA_HD
    kv_w = SWA_KV_HEADS * SWA_HD
    mem_w = MEM_HEADS * MEM_HD
    w_in0 = w_in[0]
    w_branch_b = w_branch[0].astype(BF16)
    w_out_b = w_out[0].astype(BF16)
    w_a2_b = jnp.pad(w_a2[0], ((0, LANES - GLA_LOWRANK), (0, 0))).astype(BF16)
    w_router = jnp.pad(jnp.concatenate([w_router_group[0], w_router_expert[0]], axis=1),
                       ((0, 0), (0, LANES - N_GROUPS - N_EXPERTS))).astype(BF16)
    b_router = jnp.pad(jnp.concatenate([b_router_group[0], b_router_expert[0]]),
                       (0, LANES - N_GROUPS - N_EXPERTS)).reshape(1, LANES)

    pos = jnp.concatenate([jnp.tile(jnp.arange(tp, dtype=jnp.int32), bp),
                           jnp.tile(PAST_LEN + jnp.arange(ts, dtype=jnp.int32), bs)])
    cos_t, sa_t, sb_t = _rope_tables(pos)
    seg_id = jnp.arange(tn, dtype=jnp.int32) // SWA_HD
    seg = (seg_id[:, None] == seg_id[None, :]).astype(BF16)
    rope_specs = [pl.BlockSpec((tp_rows, LANES), lambda j, i: (i, 0))] * 3
    row_vec = lambda width: pl.BlockSpec((1, width), lambda j, i: (0, 0))
    seg_spec = pl.BlockSpec((tn, tn), lambda j, i: (0, 0))

    xp2 = x_prompt.reshape(n_p, d)
    xs2 = x_sample.reshape(n_s, d)
    h = _rms_norm_two(xp2, xs2, norm_mix_g[0], tm)

    qkvg = _wmatmul(_ep_plain, h, w_in0, 0, c0, 0, [], [], F32, tp_rows, WIDE_COLS, "proj_gla")
    ga = _wmatmul(_ep_lowrank, h, w_in0, c0, LANES, 0, [], [], F32, tp_rows, LANES, "proj_gla_lowrank")
    q_gain = jnp.tile(swa_q_norm_g[0], tn // SWA_HD).reshape(1, tn)
    q_swa = _wmatmul(functools.partial(_ep_qknorm_rope, keep_from=None), h, w_in0, c0, sq_w, GLA_LOWRANK,
                     [q_gain, seg, cos_t, sa_t, sb_t], [row_vec(tn), seg_spec] + rope_specs,
                     BF16, tp_rows, tn, "proj_swa_q")
    k_gain = jnp.tile(swa_k_norm_g[0], tn // SWA_HD).reshape(1, tn)
    kv_swa = _wmatmul(functools.partial(_ep_qknorm_rope, keep_from=kv_w), h, w_in0, c0 + sq_w, 2 * kv_w,
                      GLA_LOWRANK, [k_gain, seg, cos_t, sa_t, sb_t], [row_vec(tn), seg_spec] + rope_specs,
                      F32, tp_rows, tn, "proj_swa_kv")
    q_mem = _wmatmul(functools.partial(_ep_headnorm, norm_tiles=None), h, w_in0, c0 + sq_w + 2 * kv_w, mem_w,
                     GLA_LOWRANK, [mem_q_norm_g[0].reshape(1, MEM_HD)], [row_vec(MEM_HD)],
                     BF16, tp_rows, MEM_HD, "proj_mem_q")
    gates = _wmatmul(_ep_sigmoid, h, w_gate[0], 0, 3 * d, 0, [b_gate[0].reshape(1, -1)],
                     [pl.BlockSpec((1, WIDE_COLS), lambda j, i: (0, j))], F32, tp_rows, WIDE_COLS, "proj_gates")

    mem_rows = bp * N_MEM
    hm = _rms_norm_rows(mem_prompt.reshape(mem_rows, d), norm_mem_g[0], BF16, N_MEM)
    mem_kv = _wmatmul(functools.partial(_ep_headnorm, norm_tiles=MEM_HEADS), hm, w_mem_kv[0], 0, 2 * mem_w, 0,
                      [mem_k_norm_g[0].reshape(1, MEM_HD)], [row_vec(MEM_HD)], F32, mem_rows, MEM_HD, "mem_kv")

    ba = b_a2[0].reshape(1, qk_w)
    gn = gla_norm_g[0].reshape(1, GLA_DV)
    s0_p = jnp.zeros((bp, GLA_HEADS, GLA_DK, GLA_DV), F32)
    o_gla_p, gla_state_p = _gla(qkvg, ga, w_a2_b, ba, gn, s0_p, bp, tp, 0, 256, CHUNK)
    o_gla_s, gla_state_s = _gla(qkvg, ga, w_a2_b, ba, gn, state_gla[0], bs, ts, n_p, ts, min(CHUNK, ts))

    sinks = swa_sinks[0]
    o_swa_p = _swa(sinks, q_swa, 0, kv_swa, kv_swa, (0, 1), kv_swa, kv_swa, (0, 1), 0,
                   bp, tp, WINDOW, False)
    ck = cache_swa_k[0].reshape(bs * keep_s, kv_w)
    cv = cache_swa_v[0].reshape(bs * keep_s, kv_w)
    o_swa_s = _swa(sinks, q_swa, n_p, ck, cv, (0, 0), kv_swa, kv_swa, (0, 1), n_p,
                   bs, ts, ts, True)

    o_mem_p = _mem_attn(q_mem, 0, 0, mem_kv, 0, mem_kv, MEM_HEADS, bp, tp, tm)
    cmk = cache_mem_k[0].reshape(bs * N_MEM, mem_w)
    cmv = cache_mem_v[0].reshape(bs * N_MEM, mem_w)
    o_mem_s = _mem_attn(q_mem, n_p, 0, cmk, 0, cmv, 0, bs, ts, ts)

    branches = [jnp.concatenate([p, s], axis=0) for p, s in
                ((o_gla_p, o_gla_s), (o_swa_p, o_swa_s), (o_mem_p, o_mem_s))]
    merged = _merge(branches, w_branch_b, gates, tm, tn)

    x2, hf, eid, wts = _outproj_router(merged, w_out_b, xp2, xs2, norm_ffn_g[0], w_router, b_router, tm)

    n_assign = TOP_K * n
    n_tiles = n_assign // MOE_TILE + N_EXPERTS
    eid_kmajor = jnp.concatenate([eid[:, k] for k in range(TOP_K)])
    tile_expert, tile_ok, tok_slots, dst_slots = _moe_schedule(eid_kmajor, n, n_tiles)
    yk = _moe(hf, tile_expert, tile_ok, tok_slots, dst_slots, w_up[0], w_down[0], n_assign + 2 * MOE_TILE)
    y_p, y_s = _combine(x2, yk, wts, n_p, tm)

    y_p = y_p.reshape(bp, tp, d)
    y_s = y_s.reshape(bs, ts, d)
    kv_p = kv_swa[:n_p].reshape(bp, tp, 2, SWA_KV_HEADS, SWA_HD)[:, tp - WINDOW:]
    kv_s = kv_swa[n_p:].reshape(bs, ts, 2, SWA_KV_HEADS, SWA_HD)
    swk_s = jnp.concatenate([cache_swa_k[0], kv_s[:, :, 0]], axis=1)[:, ts:ts + keep_s]
    swv_s = jnp.concatenate([cache_swa_v[0], kv_s[:, :, 1]], axis=1)[:, ts:ts + keep_s]
    mk_p = mem_kv[:, :mem_w].reshape(bp, N_MEM, MEM_HEADS, MEM_HD)
    mv_p = mem_kv[:, mem_w:].reshape(bp, N_MEM, MEM_HEADS, MEM_HD)
    return (y_p, y_s, gla_state_p[None], kv_p[:, :, 0][None], kv_p[:, :, 1][None], mk_p[None], mv_p[None],
            gla_state_s[None], swk_s[None], swv_s[None])
```

```python
import functools

import jax
import jax.numpy as jnp
from jax import lax
from jax.experimental import pallas as pl
from jax.experimental.pallas import tpu as pltpu

F32 = jnp.float32
BF16 = jnp.bfloat16

D_MODEL = 2048
CHUNK = 64
EPS = 1e-6
PAST_LEN = 1024
GLA_HEADS = 4
GLA_DV = 512
GLA_DK = 256
GLA_LOWRANK = 16
GLA_NORMALIZER = 16.0
SWA_HD = 64
SWA_HEADS = 32
SWA_KV_HEADS = 4
SWA_GROUP = 8
WINDOW = 128
ROPE_DIM = 16
ROPE_THETA = 500000.0
N_MEM = 256
MEM_HEADS = 4
MEM_HD = 512
N_GROUPS = 8
EXPERTS_PER_GROUP = 8
N_EXPERTS = 64
TOP_K = 2
D_FF = 512

LANES = 128
VMEM_LIMIT = 56 * 1024 * 1024
ROW_TILE = 512
PROJ_ROWS = 1536
COL_TILE = 512
WIDE_COLS = 1024
MOE_TILE = 256
TOKEN_TILE_ROWS = D_MODEL // 2 // LANES
TOKEN_TILE_PITCH = 12
TOKEN_F32_ROWS = D_MODEL // LANES
TOKEN_F32_PITCH = 20
NEG_BIG = -1e30


def _params(n_axes):
    return pltpu.CompilerParams(dimension_semantics=("arbitrary",) * n_axes,
                                vmem_limit_bytes=VMEM_LIMIT)


def _norm_kernel(x_ref, g_ref, o_ref):
    x = x_ref[...]
    y = x * lax.rsqrt(jnp.mean(x * x, axis=-1, keepdims=True) + EPS)
    o_ref[...] = (y * g_ref[...]).astype(o_ref.dtype)


def _rms_norm_rows(x, g, out_dtype, tm):
    n, d = x.shape
    return pl.pallas_call(
        _norm_kernel,
        out_shape=jax.ShapeDtypeStruct((n, d), out_dtype),
        grid=(n // tm,),
        in_specs=[pl.BlockSpec((tm, d), lambda i: (i, 0)),
                  pl.BlockSpec((1, d), lambda i: (0, 0))],
        out_specs=pl.BlockSpec((tm, d), lambda i: (i, 0)),
        compiler_params=_params(1),
        name="rms_norm_rows",
    )(x, g.reshape(1, d))


def _norm2_kernel(xp_ref, xs_ref, g_ref, o_ref, *, p_tiles):
    def emit(x_ref):
        x = x_ref[...]
        y = x * lax.rsqrt(jnp.mean(x * x, axis=-1, keepdims=True) + EPS)
        o_ref[...] = (y * g_ref[...]).astype(o_ref.dtype)

    @pl.when(pl.program_id(0) < p_tiles)
    def _():
        emit(xp_ref)

    @pl.when(pl.program_id(0) >= p_tiles)
    def _():
        emit(xs_ref)


def _split_specs(p_tiles, tm, d):
    return (pl.BlockSpec((tm, d), lambda i: (jnp.minimum(i, p_tiles - 1), 0)),
            pl.BlockSpec((tm, d), lambda i: (jnp.maximum(i - p_tiles, 0), 0)))


def _rms_norm_two(xp, xs, g, tm):
    (n_p, d), n_s = xp.shape, xs.shape[0]
    p_tiles = n_p // tm
    return pl.pallas_call(
        functools.partial(_norm2_kernel, p_tiles=p_tiles),
        out_shape=jax.ShapeDtypeStruct((n_p + n_s, d), BF16),
        grid=((n_p + n_s) // tm,),
        in_specs=[*_split_specs(p_tiles, tm, d), pl.BlockSpec((1, d), lambda i: (0, 0))],
        out_specs=pl.BlockSpec((tm, d), lambda i: (i, 0)),
        compiler_params=_params(1),
        name="rms_norm_mix",
    )(xp, xs, g.reshape(1, d))


def _segment_rms(acc, seg_ref, inv_width):
    sq = acc * acc
    hi = sq.astype(BF16)
    lo = (sq - hi.astype(F32)).astype(BF16)
    ss = (jnp.dot(hi, seg_ref[...], preferred_element_type=F32)
          + jnp.dot(lo, seg_ref[...], preferred_element_type=F32))
    return ss * inv_width


def _rope(y, cos_ref, sa_ref, sb_ref):
    width = y.shape[1]
    reps = width // LANES
    c = jnp.concatenate([cos_ref[...]] * reps, axis=1)
    sa = jnp.concatenate([sa_ref[...]] * reps, axis=1)
    sb = jnp.concatenate([sb_ref[...]] * reps, axis=1)
    half = ROPE_DIM // 2
    return y * c + pltpu.roll(y, width - half, 1) * sa + pltpu.roll(y, half, 1) * sb


def _ep_plain(acc):
    return acc


def _ep_lowrank(acc):
    lane = lax.broadcasted_iota(jnp.int32, acc.shape, 1)
    return jnp.where(lane < GLA_LOWRANK, acc, 0.0)


def _ep_sigmoid(acc, b_ref):
    return jax.nn.sigmoid(acc + b_ref[...])


def _ep_qknorm_rope(acc, g_ref, seg_ref, cos_ref, sa_ref, sb_ref, *, keep_from):
    ms = _segment_rms(acc, seg_ref, 1.0 / SWA_HD)
    y = acc * lax.rsqrt(ms + EPS) * g_ref[...]
    y = _rope(y, cos_ref, sa_ref, sb_ref)
    if keep_from is not None:
        col = lax.broadcasted_iota(jnp.int32, y.shape, 1)
        y = jnp.where(col < keep_from, y, acc)
    return y


def _ep_headnorm(acc, g_ref, *, norm_tiles):
    y = acc * lax.rsqrt(jnp.mean(acc * acc, axis=-1, keepdims=True) + EPS) * g_ref[...]
    if norm_tiles is not None:
        y = jnp.where(pl.program_id(0) < norm_tiles, y, acc)
    return y


def _stage_weight_tile(w_refs, wbf, shift):
    if shift == 0:
        wbf[...] = w_refs[0][...].astype(BF16)
        return
    main, ext = w_refs
    n_slabs = wbf.shape[1] // LANES
    lane = lax.broadcasted_iota(jnp.int32, (wbf.shape[0], LANES), 1)
    rolled = [pltpu.roll(main[:, b * LANES:(b + 1) * LANES], LANES - shift, 1) for b in range(n_slabs)]
    rolled.append(pltpu.roll(ext[...], LANES - shift, 1))
    for b in range(n_slabs):
        wbf[:, b * LANES:(b + 1) * LANES] = jnp.where(lane < LANES - shift, rolled[b], rolled[b + 1]).astype(BF16)


def _wmm_kernel(*refs, n_w, shift, epilogue):
    a_ref, w_refs, extras, o_ref, wbf = refs[0], refs[1:1 + n_w], refs[1 + n_w:-2], refs[-2], refs[-1]

    @pl.when(pl.program_id(1) == 0)
    def _():
        _stage_weight_tile(w_refs, wbf, shift)

    acc = jnp.dot(a_ref[...], wbf[...], preferred_element_type=F32)
    o_ref[...] = epilogue(acc, *extras).astype(o_ref.dtype)


def _wmatmul(epilogue, a, w, col0, n_cols, shift, extras, extra_specs, out_dtype, tm, tn, name):
    m, k = a.shape
    assert m % tm == 0 and n_cols % tn == 0 and col0 % tn == 0 and 0 <= shift < LANES
    j0 = col0 // tn
    w_specs = [pl.BlockSpec((k, tn), lambda j, i: (0, j0 + j))]
    if shift:
        e0 = col0 // LANES
        w_specs.append(pl.BlockSpec((k, LANES), lambda j, i: (0, e0 + (j + 1) * (tn // LANES))))
    kernel = functools.partial(_wmm_kernel, n_w=len(w_specs), shift=shift, epilogue=epilogue)
    return pl.pallas_call(
        kernel,
        out_shape=jax.ShapeDtypeStruct((m, n_cols), out_dtype),
        grid=(n_cols // tn, m // tm),
        in_specs=[pl.BlockSpec((tm, k), lambda j, i: (i, 0))] + w_specs + list(extra_specs),
        out_specs=pl.BlockSpec((tm, tn), lambda j, i: (i, j)),
        scratch_shapes=[pltpu.VMEM((k, tn), BF16)],
        compiler_params=_params(2),
        name=name,
    )(a, *([w] * len(w_specs)), *extras)


def _gla_kernel(q_ref, k_ref, v_ref, gg_ref, ga_ref, wa_ref, ba_ref, gn_ref, s0_ref,
                o_ref, sout_ref, s_scr, *, chunk, n_chunks):
    t = pl.program_id(1)

    @pl.when(t == 0)
    def _():
        s_scr[...] = s0_ref[0]

    row = lax.broadcasted_iota(jnp.int32, (chunk, chunk), 0)
    col = lax.broadcasted_iota(jnp.int32, (chunk, chunk), 1)
    causal = row >= col
    tril = causal.astype(F32)

    def one_chunk(ci, carry):
        rows = pl.ds(pl.multiple_of(ci * chunk, chunk), chunk)
        z = jnp.dot(ga_ref[rows, :].astype(BF16), wa_ref[...], preferred_element_type=F32) + ba_ref[...]
        log_a = (jnp.minimum(z, 0.0) - jnp.log1p(jnp.exp(-jnp.abs(z)))) * (1.0 / GLA_NORMALIZER)
        b_all = jnp.dot(tril, log_a, preferred_element_type=F32, precision=lax.Precision.HIGHEST)
        for h in range(GLA_HEADS):
            ks = slice(h * GLA_DK, (h + 1) * GLA_DK)
            vs = slice(h * GLA_DV, (h + 1) * GLA_DV)
            b = b_all[:, ks]
            b_last = b[chunk - 1:chunk, :]
            q = q_ref[rows, ks] * (GLA_DK ** -0.5)
            k = k_ref[rows, ks]
            vb = v_ref[rows, vs].astype(BF16)
            q_t = (q * jnp.exp(b)).astype(BF16)
            k_t = (k * jnp.exp(-b)).astype(BF16)
            k_u = (k * jnp.exp(b_last - b)).astype(BF16)
            att = lax.dot_general(q_t, k_t, (((1,), (1,)), ((), ())), preferred_element_type=F32)
            att = jnp.where(causal, att, 0.0)
            s_old = s_scr[h]
            o = (jnp.dot(q_t, s_old.astype(BF16), preferred_element_type=F32)
                 + jnp.dot(att.astype(BF16), vb, preferred_element_type=F32))
            decay = jnp.transpose(jnp.broadcast_to(jnp.exp(b_last), (LANES, GLA_DK)))
            decay = jnp.concatenate([decay] * (GLA_DV // LANES), axis=1)
            s_scr[h] = decay * s_old + lax.dot_general(k_u, vb, (((0,), (0,)), ((), ())),
                                                       preferred_element_type=F32)
            on = o * lax.rsqrt(jnp.mean(o * o, axis=-1, keepdims=True) + EPS) * gn_ref[...]
            gg = gg_ref[rows, vs]
            o_ref[rows, vs] = (on * (gg * jax.nn.sigmoid(gg))).astype(o_ref.dtype)
        return carry

    lax.fori_loop(0, n_chunks, one_chunk, 0)

    @pl.when(t == pl.num_programs(1) - 1)
    def _():
        sout_ref[0] = s_scr[...]


def _gla(qkvg, ga, wa, ba, gn, s0, batch, seq, row0, tb, chunk):
    nt = seq // tb
    base = row0 // tb
    qk_w = GLA_HEADS * GLA_DK
    v_w = GLA_HEADS * GLA_DV
    rows = lambda b, t: base + b * nt + t
    kernel = functools.partial(_gla_kernel, chunk=chunk, n_chunks=tb // chunk)
    return pl.pallas_call(
        kernel,
        out_shape=(jax.ShapeDtypeStruct((batch * seq, v_w), BF16),
                   jax.ShapeDtypeStruct((batch, GLA_HEADS, GLA_DK, GLA_DV), F32)),
        grid=(batch, nt),
        in_specs=[pl.BlockSpec((tb, qk_w), lambda b, t: (rows(b, t), 0)),
                  pl.BlockSpec((tb, qk_w), lambda b, t: (rows(b, t), 1)),
                  pl.BlockSpec((tb, v_w), lambda b, t: (rows(b, t), 1)),
                  pl.BlockSpec((tb, v_w), lambda b, t: (rows(b, t), 2)),
                  pl.BlockSpec((tb, LANES), lambda b, t: (rows(b, t), 0)),
                  pl.BlockSpec((LANES, qk_w), lambda b, t: (0, 0)),
                  pl.BlockSpec((1, qk_w), lambda b, t: (0, 0)),
                  pl.BlockSpec((1, GLA_DV), lambda b, t: (0, 0)),
                  pl.BlockSpec((1, GLA_HEADS, GLA_DK, GLA_DV), lambda b, t: (b, 0, 0, 0))],
        out_specs=(pl.BlockSpec((tb, v_w), lambda b, t: (b * nt + t, 0)),
                   pl.BlockSpec((1, GLA_HEADS, GLA_DK, GLA_DV), lambda b, t: (b, 0, 0, 0))),
        scratch_shapes=[pltpu.VMEM((GLA_HEADS, GLA_DK, GLA_DV), F32)],
        compiler_params=_params(2),
        name="gla_chunks",
    )(qkvg, qkvg, qkvg, qkvg, ga, wa, ba, gn, s0)


def _swa_kernel(sink_ref, q_ref, kp_ref, vp_ref, ko_ref, vo_ref, o_ref, *, tq, prev_from_cache):
    i = pl.program_id(1)
    nk = WINDOW + tq
    k_all = jnp.concatenate([kp_ref[...], ko_ref[...]], axis=0)
    v_all = jnp.concatenate([vp_ref[...], vo_ref[...]], axis=0)
    qc = lax.broadcasted_iota(jnp.int32, (tq, nk), 0) // CHUNK + WINDOW // CHUNK
    kcol = lax.broadcasted_iota(jnp.int32, (tq, nk), 1)
    kc = kcol // CHUNK
    valid = (kc <= qc) & (kc >= qc - WINDOW // CHUNK)
    if not prev_from_cache:
        valid = valid & ((kcol >= WINDOW) | (i > 0))
    lane = lax.broadcasted_iota(jnp.int32, (nk, LANES), 1)
    low = lane < SWA_HD
    low_q = lax.broadcasted_iota(jnp.int32, (tq, LANES), 1) < SWA_HD
    scale = SWA_HD ** -0.5
    for g in range(SWA_KV_HEADS):
        slab = slice((g // 2) * LANES, (g // 2 + 1) * LANES)
        k2 = k_all[:, slab]
        v2 = v_all[:, slab]
        k2r = pltpu.roll(k2, SWA_HD, 1)
        v2r = pltpu.roll(v2, SWA_HD, 1)
        if g % 2 == 0:
            k_lo, k_hi, v_lo, v_hi = k2, k2r, v2, v2r
        else:
            k_lo, k_hi, v_lo, v_hi = k2r, k2, v2r, v2
        zero = jnp.zeros_like(k2)
        one = jnp.ones_like(k2)
        km = (jnp.where(low, k_lo, zero).astype(BF16), jnp.where(low, zero, k_hi).astype(BF16))
        vm = (jnp.where(low, v_lo, one).astype(BF16), jnp.where(low, one, v_hi).astype(BF16))
        heads = [(j, half) for j in range(SWA_GROUP // 2) for half in range(2)]
        sinks = [sink_ref[g * SWA_GROUP + 2 * j + half] for j, half in heads]
        scores = []
        for j, half in heads:
            qs = q_ref[:, (g * 4 + j) * LANES:(g * 4 + j + 1) * LANES]
            s = lax.dot_general(qs, km[half], (((1,), (1,)), ((), ())), preferred_element_type=F32) * scale
            scores.append(jnp.where(valid, s, NEG_BIG))
        maxes = [jnp.maximum(jnp.max(s, axis=-1, keepdims=True), sk) for s, sk in zip(scores, sinks)]
        exps = [jnp.exp(s - m).astype(BF16) for s, m in zip(scores, maxes)]
        sink_terms = [jnp.exp(sk - m) for sk, m in zip(sinks, maxes)]
        for j in range(SWA_GROUP // 2):
            a_lo = jnp.dot(exps[2 * j], vm[0], preferred_element_type=F32)
            a_hi = jnp.dot(exps[2 * j + 1], vm[1], preferred_element_type=F32)
            num = jnp.where(low_q, a_lo, a_hi)
            den = pltpu.roll(jnp.where(low_q, a_hi, a_lo), SWA_HD, 1)
            den = den + jnp.where(low_q, sink_terms[2 * j], sink_terms[2 * j + 1])
            o_ref[:, (g * 4 + j) * LANES:(g * 4 + j + 1) * LANES] = (num / den).astype(o_ref.dtype)


def _swa(sinks, q, q_row0, k_prev, v_prev, prev_col, k_own, v_own, own_col, own_row0,
         batch, seq, tq, prev_from_cache):
    nt = seq // tq
    qb = q_row0 // tq
    ob = own_row0 // tq
    kv_w = SWA_KV_HEADS * SWA_HD
    if prev_from_cache:
        prev_map = lambda b, t, c: (b, c)
    else:
        per = seq // WINDOW
        prev_map = lambda b, t, c: (b * per + jnp.maximum(t * (tq // WINDOW) - 1, 0), c)
    kernel = functools.partial(_swa_kernel, tq=tq, prev_from_cache=prev_from_cache)
    return pl.pallas_call(
        kernel,
        out_shape=jax.ShapeDtypeStruct((batch * seq, SWA_HEADS * SWA_HD), BF16),
        grid=(batch, nt),
        in_specs=[pl.BlockSpec(memory_space=pltpu.SMEM),
                  pl.BlockSpec((tq, SWA_HEADS * SWA_HD), lambda b, t: (qb + b * nt + t, 0)),
                  pl.BlockSpec((WINDOW, kv_w), lambda b, t: prev_map(b, t, prev_col[0])),
                  pl.BlockSpec((WINDOW, kv_w), lambda b, t: prev_map(b, t, prev_col[1])),
                  pl.BlockSpec((tq, kv_w), lambda b, t: (ob + b * nt + t, own_col[0])),
                  pl.BlockSpec((tq, kv_w), lambda b, t: (ob + b * nt + t, own_col[1]))],
        out_specs=pl.BlockSpec((tq, SWA_HEADS * SWA_HD), lambda b, t: (b * nt + t, 0)),
        compiler_params=_params(2),
        name="swa_band",
    )(sinks, q, k_prev, v_prev, k_own, v_own)


def _mem_attn_kernel(q_ref, k_ref, v_ref, o_ref):
    s = lax.dot_general(q_ref[...], k_ref[...].astype(BF16), (((1,), (1,)), ((), ())),
                        preferred_element_type=F32) * (MEM_HD ** -0.5)
    m = jnp.max(s, axis=-1, keepdims=True)
    e = jnp.exp(s - m)
    p = (e / jnp.sum(e, axis=-1, keepdims=True)).astype(BF16)
    o_ref[...] = jnp.dot(p, v_ref[...].astype(BF16), preferred_element_type=F32).astype(o_ref.dtype)


def _mem_attn(q, q_row0, q_col0, mk, mk_col0, mv, mv_col0, batch, seq, tq):
    nt = seq // tq
    qb = q_row0 // tq
    return pl.pallas_call(
        _mem_attn_kernel,
        out_shape=jax.ShapeDtypeStruct((batch * seq, MEM_HEADS * MEM_HD), BF16),
        grid=(batch, MEM_HEADS, nt),
        in_specs=[pl.BlockSpec((tq, MEM_HD), lambda b, h, t: (qb + b * nt + t, q_col0 + h)),
                  pl.BlockSpec((N_MEM, MEM_HD), lambda b, h, t: (b, mk_col0 + h)),
                  pl.BlockSpec((N_MEM, MEM_HD), lambda b, h, t: (b, mv_col0 + h))],
        out_specs=pl.BlockSpec((tq, MEM_HD), lambda b, h, t: (b * nt + t, h)),
        compiler_params=_params(3),
        name="mem_attn",
    )(q, mk, mv)


def _merge_kernel(a0p, a0s, a1p, a1s, a2p, a2s, w_ref, g0_ref, g1_ref, g2_ref, o_ref, *, p_tiles):
    def emit(a0_ref, a1_ref, a2_ref):
        acc = g0_ref[...] * jnp.dot(a0_ref[...], w_ref[0], preferred_element_type=F32)
        acc = acc + g1_ref[...] * jnp.dot(a1_ref[...], w_ref[1], preferred_element_type=F32)
        acc = acc + g2_ref[...] * jnp.dot(a2_ref[...], w_ref[2], preferred_element_type=F32)
        o_ref[...] = acc.astype(o_ref.dtype)

    @pl.when(pl.program_id(0) < p_tiles)
    def _():
        emit(a0p, a1p, a2p)

    @pl.when(pl.program_id(0) >= p_tiles)
    def _():
        emit(a0s, a1s, a2s)


def _merge(branches_p, branches_s, w_branch, gates, tm, tn):
    n_p, d = branches_p[0].shape
    n = n_p + branches_s[0].shape[0]
    nj = d // tn
    p_tiles = n_p // tm
    ap = pl.BlockSpec((tm, d), lambda i, j: (jnp.minimum(i, p_tiles - 1), 0))
    asp = pl.BlockSpec((tm, d), lambda i, j: (jnp.maximum(i - p_tiles, 0), 0))
    operands = [a for pair in zip(branches_p, branches_s) for a in pair]
    return pl.pallas_call(
        functools.partial(_merge_kernel, p_tiles=p_tiles),
        out_shape=jax.ShapeDtypeStruct((n, d), BF16),
        grid=(n // tm, nj),
        in_specs=[ap, asp, ap, asp, ap, asp,
                  pl.BlockSpec((3, d, tn), lambda i, j: (0, 0, j)),
                  pl.BlockSpec((tm, tn), lambda i, j: (i, j)),
                  pl.BlockSpec((tm, tn), lambda i, j: (i, nj + j)),
                  pl.BlockSpec((tm, tn), lambda i, j: (i, 2 * nj + j))],
        out_specs=pl.BlockSpec((tm, tn), lambda i, j: (i, j)),
        compiler_params=_params(2),
        name="branch_merge",
    )(*operands, w_branch, gates, gates, gates)


def _outproj_router_kernel(m_ref, w_ref, xp_ref, xs_ref, g_ref, wr_ref, br_ref,
                           x2_ref, hf_ref, eid_ref, wt_ref, *, p_tiles):
    acc = jnp.dot(m_ref[...], w_ref[...], preferred_element_type=F32)

    @pl.when(pl.program_id(0) < p_tiles)
    def _():
        x2_ref[...] = xp_ref[...] + acc

    @pl.when(pl.program_id(0) >= p_tiles)
    def _():
        x2_ref[...] = xs_ref[...] + acc

    x = x2_ref[...]
    hf = x * lax.rsqrt(jnp.mean(x * x, axis=-1, keepdims=True) + EPS) * g_ref[...]
    hb = hf.astype(BF16)
    tm, d = hf.shape
    bits = pltpu.bitcast(hb.astype(F32), jnp.uint32)
    packed = bits[:, d // 2:] | (bits[:, :d // 2] >> 16)
    for s in range(TOKEN_TILE_ROWS):
        hf_ref[pl.ds(s, tm, stride=TOKEN_TILE_ROWS), :] = packed[:, s * LANES:(s + 1) * LANES]
    logits = jnp.dot(hb, wr_ref[...], preferred_element_type=F32) + br_ref[...]
    lane = lax.broadcasted_iota(jnp.int32, logits.shape, 1).astype(F32)
    big = 1e6
    is_g = lane < N_GROUPS
    lg = jnp.where(is_g, logits, NEG_BIG)
    mg = jnp.max(lg, axis=-1, keepdims=True)
    gsel = jnp.min(jnp.where(is_g & (lg == mg), lane, big), axis=-1, keepdims=True)
    g_w = 1.0 / jnp.sum(jnp.where(is_g, jnp.exp(lg - mg), 0.0), axis=-1, keepdims=True)
    e_lo = N_GROUPS + gsel * EXPERTS_PER_GROUP
    in_grp = (lane >= e_lo) & (lane < e_lo + EXPERTS_PER_GROUP)
    le = jnp.where(in_grp, logits, NEG_BIG)
    me = jnp.max(le, axis=-1, keepdims=True)
    ee = jnp.where(in_grp, jnp.exp(le - me), 0.0)
    pe = ee / jnp.sum(ee, axis=-1, keepdims=True)
    pe = jnp.where(in_grp, pe, -1.0)
    p1 = jnp.max(pe, axis=-1, keepdims=True)
    i1 = jnp.min(jnp.where(pe == p1, lane, big), axis=-1, keepdims=True)
    pe2 = jnp.where(lane == i1, -1.0, pe)
    p2 = jnp.max(pe2, axis=-1, keepdims=True)
    i2 = jnp.min(jnp.where(pe2 == p2, lane, big), axis=-1, keepdims=True)
    tot = p1 + p2
    w1 = g_w * p1 / tot
    w2 = g_w * p2 / tot
    eid = jnp.where(lane == 0.0, i1 - N_GROUPS, jnp.where(lane == 1.0, i2 - N_GROUPS, 0.0))
    eid_ref[...] = eid.astype(jnp.int32)
    wt_ref[...] = jnp.where(lane == 0.0, w1, jnp.where(lane == 1.0, w2, 0.0))


def _outproj_router(merged, w_out, xp, xs, g, wr, br, tm):
    n, d = merged.shape
    p_tiles = xp.shape[0] // tm
    const = lambda shape: pl.BlockSpec(shape, lambda i: (0, 0), pipeline_mode=pl.Buffered(1))
    row = lambda width: pl.BlockSpec((tm, width), lambda i: (i, 0))
    return pl.pallas_call(
        functools.partial(_outproj_router_kernel, p_tiles=p_tiles),
        out_shape=(jax.ShapeDtypeStruct((n, d), F32),
                   jax.ShapeDtypeStruct((n * TOKEN_TILE_ROWS, LANES), jnp.uint32),
                   jax.ShapeDtypeStruct((n, LANES), jnp.int32),
                   jax.ShapeDtypeStruct((n, LANES), F32)),
        grid=(n // tm,),
        in_specs=[row(d), const((d, d)), *_split_specs(p_tiles, tm, d),
                  const((1, d)), const((d, LANES)), const((1, LANES))],
        out_specs=(row(d), pl.BlockSpec((tm * TOKEN_TILE_ROWS, LANES), lambda i: (i, 0)),
                   row(LANES), row(LANES)),
        compiler_params=_params(1),
        name="outproj_router",
    )(merged, w_out, xp, xs, g.reshape(1, d), wr, br)


def _moe_kernel(te_ref, ok_ref, tok_ref, tok_next_ref, dst_ref, hf_hbm, wup_ref, wdn_ref, y_hbm,
                xg, yb, wup_bf, wdn_bf, in_sem, out_sem):
    t = pl.program_id(0)
    buf = t % 2
    valid = ok_ref[t] > 0
    valid_next = ok_ref[t + 1] > 0
    new_expert = (t == 0) | (te_ref[t] != te_ref[jnp.maximum(t - 1, 0)])

    in_rows, in_pitch = TOKEN_TILE_ROWS, TOKEN_TILE_PITCH
    out_rows, out_pitch = TOKEN_F32_ROWS, TOKEN_F32_PITCH

    def row_in(row0, r, b):
        return pltpu.make_async_copy(hf_hbm.at[pl.ds(pl.multiple_of(row0, in_rows), in_rows), :],
                                     xg.at[b, pl.ds(r * in_pitch, in_rows), :], in_sem.at[b])

    def row_out(row0, r, b):
        return pltpu.make_async_copy(yb.at[b, pl.ds(r * out_pitch, out_rows), :],
                                     y_hbm.at[pl.ds(pl.multiple_of(row0, out_rows), out_rows), :], out_sem.at[b])

    def tile_in(b):
        return pltpu.make_async_copy(hf_hbm.at[pl.ds(0, MOE_TILE * in_rows), :],
                                     xg.at[b, pl.ds(0, MOE_TILE * in_rows), :], in_sem.at[b])

    def tile_out(b):
        return pltpu.make_async_copy(yb.at[b, pl.ds(0, MOE_TILE * out_rows), :],
                                     y_hbm.at[pl.ds(0, MOE_TILE * out_rows), :], out_sem.at[b])

    @pl.when(t == 0)
    def _():
        def first(r, c):
            row_in(tok_ref[0, 0, r], r, 0).start()
            return c
        lax.fori_loop(0, MOE_TILE, first, 0)
        yb[...] = jnp.zeros_like(yb)
        n_real = y_hbm.shape[0] - 2 * MOE_TILE * out_rows
        for b in range(2):
            spare = pltpu.make_async_copy(
                yb.at[b, pl.ds(0, MOE_TILE * out_rows), :],
                y_hbm.at[pl.ds(n_real + b * MOE_TILE * out_rows, MOE_TILE * out_rows), :], out_sem.at[b])
            spare.start()
            spare.wait()

    @pl.when(valid_next)
    def _():
        for r in range(MOE_TILE):
            row_in(tok_next_ref[0, 0, r], r, 1 - buf).start()

    @pl.when(valid & new_expert)
    def _():
        wup_bf[...] = wup_ref[0].astype(BF16)
        wdn_bf[...] = wdn_ref[0].astype(BF16)

    @pl.when(valid)
    def _():
        tile_in(buf).wait()
        lo, hi = [], []
        for s in range(in_rows):
            word = xg[buf, pl.ds(s, MOE_TILE, stride=in_pitch), :]
            lo.append(pltpu.bitcast(word << 16, F32).astype(BF16))
            hi.append(pltpu.bitcast(word & jnp.uint32(0xFFFF0000), F32).astype(BF16))
        x = jnp.concatenate(lo + hi, axis=1)
        h1 = jnp.dot(x, wup_bf[...], preferred_element_type=F32)
        gate = h1[:, :D_FF]
        up = h1[:, D_FF:]
        act = (gate * jax.nn.sigmoid(gate)) * up
        ye = jnp.dot(act.astype(BF16), wdn_bf[...], preferred_element_type=F32)
        for s in range(out_rows):
            yb[buf, pl.ds(s, MOE_TILE, stride=out_pitch), :] = ye[:, s * LANES:(s + 1) * LANES]
        for r in range(MOE_TILE):
            row_out(dst_ref[0, 0, r], r, buf).start()

        @pl.when(t > 0)
        def _():
            tile_out(1 - buf).wait()

        @pl.when(jnp.logical_not(valid_next))
        def _():
            tile_out(buf).wait()


def _moe(hf, tile_expert, tile_ok, tok_slots, dst_slots, w_up, w_down, out_tokens):
    n_tiles = tile_expert.shape[0]
    d = w_up.shape[1]
    slot_spec = lambda nxt: pl.BlockSpec((1, 1, MOE_TILE),
                                         lambda t, te, ok: (jnp.minimum(t + nxt, n_tiles - 1), 0, 0),
                                         memory_space=pltpu.SMEM)
    grid_spec = pltpu.PrefetchScalarGridSpec(
        num_scalar_prefetch=2,
        grid=(n_tiles,),
        in_specs=[slot_spec(0), slot_spec(1), slot_spec(0),
                  pl.BlockSpec(memory_space=pl.ANY),
                  pl.BlockSpec((1, d, 2 * D_FF), lambda t, te, ok: (te[t], 0, 0)),
                  pl.BlockSpec((1, D_FF, d), lambda t, te, ok: (te[t], 0, 0))],
        out_specs=pl.BlockSpec(memory_space=pl.ANY),
        scratch_shapes=[pltpu.VMEM((2, MOE_TILE * TOKEN_TILE_PITCH, LANES), jnp.uint32),
                        pltpu.VMEM((2, MOE_TILE * TOKEN_F32_PITCH, LANES), F32),
                        pltpu.VMEM((d, 2 * D_FF), BF16),
                        pltpu.VMEM((D_FF, d), BF16),
                        pltpu.SemaphoreType.DMA((2,)),
                        pltpu.SemaphoreType.DMA((2,))],
    )
    return pl.pallas_call(
        _moe_kernel,
        out_shape=jax.ShapeDtypeStruct((out_tokens * TOKEN_F32_ROWS, LANES), F32),
        grid_spec=grid_spec,
        compiler_params=_params(1),
        name="moe_experts",
    )(tile_expert, tile_ok, tok_slots, tok_slots, dst_slots, hf, w_up, w_down)


def _combine_kernel(x_ref, y0_ref, y1_ref, w_ref, op_ref, os_ref, *, p_tiles):
    tm = x_ref.shape[0]
    w = w_ref[...]
    w0 = w[:, 0:1]
    w1 = w[:, 1:2]

    def emit(o_ref):
        for s in range(TOKEN_F32_ROWS):
            cols = slice(s * LANES, (s + 1) * LANES)
            rows = pl.ds(s, tm, stride=TOKEN_F32_ROWS)
            o_ref[:, cols] = x_ref[:, cols] + (y0_ref[rows, :] * w0 + y1_ref[rows, :] * w1)

    @pl.when(pl.program_id(0) < p_tiles)
    def _():
        emit(op_ref)

    @pl.when(pl.program_id(0) >= p_tiles)
    def _():
        emit(os_ref)


def _combine(x2, yk, wts, n_p, tm):
    n, d = x2.shape
    p_tiles = n_p // tm
    k1 = n // tm
    y_rows = tm * TOKEN_F32_ROWS
    return pl.pallas_call(
        functools.partial(_combine_kernel, p_tiles=p_tiles),
        out_shape=(jax.ShapeDtypeStruct((n_p, d), F32), jax.ShapeDtypeStruct((n - n_p, d), F32)),
        grid=(n // tm,),
        in_specs=[pl.BlockSpec((tm, d), lambda i: (i, 0)),
                  pl.BlockSpec((y_rows, LANES), lambda i: (i, 0)),
                  pl.BlockSpec((y_rows, LANES), lambda i: (k1 + i, 0)),
                  pl.BlockSpec((tm, LANES), lambda i: (i, 0))],
        out_specs=_split_specs(p_tiles, tm, d),
        compiler_params=_params(1),
        name="moe_combine",
    )(x2, yk, yk, wts)


def _rope_tables(pos):
    half = ROPE_DIM // 2
    inv = ROPE_THETA ** (-jnp.arange(half, dtype=F32) / half)
    ang = pos.astype(F32)[:, None] * inv[None, :]
    cos, sin = jnp.cos(ang), jnp.sin(ang)
    n = pos.shape[0]
    pad = jnp.zeros((n, SWA_HD - ROPE_DIM), F32)
    cos_h = jnp.concatenate([cos, cos, pad + 1.0], axis=1)
    sa_h = jnp.concatenate([-sin, jnp.zeros_like(sin), pad], axis=1)
    sb_h = jnp.concatenate([jnp.zeros_like(sin), sin, pad], axis=1)
    reps = LANES // SWA_HD
    return tuple(jnp.tile(a, (1, reps)) for a in (cos_h, sa_h, sb_h))


def _moe_schedule(eid, n_tok, n_tiles):
    a = eid.shape[0]
    order = jnp.argsort(eid, stable=True).astype(jnp.int32)
    counts = jnp.bincount(eid, length=N_EXPERTS).astype(jnp.int32)
    tiles_per = (counts + MOE_TILE - 1) // MOE_TILE
    tile_end = jnp.cumsum(tiles_per)
    tile_start = tile_end - tiles_per
    sorted_start = jnp.cumsum(counts) - counts
    tile_id = jnp.arange(n_tiles, dtype=jnp.int32)
    used = tile_id < tile_end[-1]
    te = jnp.minimum(jnp.searchsorted(tile_end, tile_id, side="right"), N_EXPERTS - 1).astype(jnp.int32)
    last_used_e = te[jnp.maximum(tile_end[-1] - 1, 0)]
    te = jnp.where(used, te, last_used_e)
    row_in_expert = (tile_id - tile_start[te]) * MOE_TILE
    rows_valid = jnp.where(used, jnp.clip(counts[te] - row_in_expert, 0, MOE_TILE), 0).astype(jnp.int32)
    r = jnp.arange(MOE_TILE, dtype=jnp.int32)[None, :]
    src = sorted_start[te][:, None] + row_in_expert[:, None] + r
    real = r < rows_valid[:, None]
    assign = order[jnp.clip(src, 0, a - 1)]
    tok = jnp.where(real, assign % n_tok, 0) * TOKEN_TILE_ROWS
    spare = a + (tile_id[:, None] % 2) * MOE_TILE + r
    dst = jnp.where(real, assign, spare) * TOKEN_F32_ROWS
    ok = jnp.concatenate([used.astype(jnp.int32), jnp.zeros((1,), jnp.int32)])
    shape = (n_tiles, 1, MOE_TILE)
    return te, ok, tok.reshape(shape).astype(jnp.int32), dst.reshape(shape).astype(jnp.int32)


def kernel(x_prompt, x_sample, state_gla, cache_swa_k, cache_swa_v, cache_mem_k, cache_mem_v,
           mem_prompt, norm_mix_g, w_in, w_a2, b_a2, gla_norm_g, swa_q_norm_g, swa_k_norm_g,
           swa_sinks, norm_mem_g, w_mem_kv, mem_q_norm_g, mem_k_norm_g, w_gate, b_gate,
           w_branch, w_out, norm_ffn_g, w_router_group, b_router_group, w_router_expert,
           b_router_expert, w_up, w_down):
    bp, tp, d = x_prompt.shape
    bs, ts, _ = x_sample.shape
    n_p, n_s = bp * tp, bs * ts
    n = n_p + n_s
    tm, tn = ROW_TILE, COL_TILE
    tp_rows = PROJ_ROWS if n % PROJ_ROWS == 0 else tm
    assert d == D_MODEL and n_p % tm == 0 and n_s % tm == 0 and w_in.shape[0] == 1
    keep_s = cache_swa_k.shape[2]
    assert keep_s == WINDOW and tp % WINDOW == 0

    qk_w = GLA_HEADS * GLA_DK
    v_w = GLA_HEADS * GLA_DV
    c0 = 2 * qk_w + 2 * v_w
    sq_w = SWA_HEADS * SWA_HD
    kv_w = SWA_KV_HEADS * SWA_HD
    mem_w = MEM_HEADS * MEM_HD
    w_in0 = w_in[0]
    w_branch_b = w_branch[0].astype(BF16)
    w_out_b = w_out[0].astype(BF16)
    w_a2_b = jnp.pad(w_a2[0], ((0, LANES - GLA_LOWRANK), (0, 0))).astype(BF16)
    w_router = jnp.pad(jnp.concatenate([w_router_group[0], w_router_expert[0]], axis=1),
                       ((0, 0), (0, LANES - N_GROUPS - N_EXPERTS))).astype(BF16)
    b_router = jnp.pad(jnp.concatenate([b_router_group[0], b_router_expert[0]]),
                       (0, LANES - N_GROUPS - N_EXPERTS)).reshape(1, LANES)

    pos = jnp.concatenate([jnp.tile(jnp.arange(tp, dtype=jnp.int32), bp),
                           jnp.tile(PAST_LEN + jnp.arange(ts, dtype=jnp.int32), bs)])
    cos_t, sa_t, sb_t = _rope_tables(pos)
    seg_id = jnp.arange(tn, dtype=jnp.int32) // SWA_HD
    seg = (seg_id[:, None] == seg_id[None, :]).astype(BF16)
    rope_specs = [pl.BlockSpec((tp_rows, LANES), lambda j, i: (i, 0))] * 3
    row_vec = lambda width: pl.BlockSpec((1, width), lambda j, i: (0, 0))
    seg_spec = pl.BlockSpec((tn, tn), lambda j, i: (0, 0))

    xp2 = x_prompt.reshape(n_p, d)
    xs2 = x_sample.reshape(n_s, d)
    h = _rms_norm_two(xp2, xs2, norm_mix_g[0], tm)

    qkvg = _wmatmul(_ep_plain, h, w_in0, 0, c0, 0, [], [], F32, tp_rows, WIDE_COLS, "proj_gla")
    ga = _wmatmul(_ep_lowrank, h, w_in0, c0, LANES, 0, [], [], F32, tp_rows, LANES, "proj_gla_lowrank")
    q_gain = jnp.tile(swa_q_norm_g[0], tn // SWA_HD).reshape(1, tn)
    q_swa = _wmatmul(functools.partial(_ep_qknorm_rope, keep_from=None), h, w_in0, c0, sq_w, GLA_LOWRANK,
                     [q_gain, seg, cos_t, sa_t, sb_t], [row_vec(tn), seg_spec] + rope_specs,
                     BF16, tp_rows, tn, "proj_swa_q")
    k_gain = jnp.tile(swa_k_norm_g[0], tn // SWA_HD).reshape(1, tn)
    kv_swa = _wmatmul(functools.partial(_ep_qknorm_rope, keep_from=kv_w), h, w_in0, c0 + sq_w, 2 * kv_w,
                      GLA_LOWRANK, [k_gain, seg, cos_t, sa_t, sb_t], [row_vec(tn), seg_spec] + rope_specs,
                      F32, tp_rows, tn, "proj_swa_kv")
    q_mem = _wmatmul(functools.partial(_ep_headnorm, norm_tiles=None), h, w_in0, c0 + sq_w + 2 * kv_w, mem_w,
                     GLA_LOWRANK, [mem_q_norm_g[0].reshape(1, MEM_HD)], [row_vec(MEM_HD)],
                     BF16, tp_rows, MEM_HD, "proj_mem_q")
    gates = _wmatmul(_ep_sigmoid, h, w_gate[0], 0, 3 * d, 0, [b_gate[0].reshape(1, -1)],
                     [pl.BlockSpec((1, WIDE_COLS), lambda j, i: (0, j))], F32, tp_rows, WIDE_COLS, "proj_gates")

    mem_rows = bp * N_MEM
    hm = _rms_norm_rows(mem_prompt.reshape(mem_rows, d), norm_mem_g[0], BF16, N_MEM)
    mem_kv = _wmatmul(functools.partial(_ep_headnorm, norm_tiles=MEM_HEADS), hm, w_mem_kv[0], 0, 2 * mem_w, 0,
                      [mem_k_norm_g[0].reshape(1, MEM_HD)], [row_vec(MEM_HD)], F32, mem_rows, MEM_HD, "mem_kv")

    ba = b_a2[0].reshape(1, qk_w)
    gn = gla_norm_g[0].reshape(1, GLA_DV)
    s0_p = jnp.zeros((bp, GLA_HEADS, GLA_DK, GLA_DV), F32)
    o_gla_p, gla_state_p = _gla(qkvg, ga, w_a2_b, ba, gn, s0_p, bp, tp, 0, 256, CHUNK)
    o_gla_s, gla_state_s = _gla(qkvg, ga, w_a2_b, ba, gn, state_gla[0], bs, ts, n_p, ts, min(CHUNK, ts))

    sinks = swa_sinks[0]
    o_swa_p = _swa(sinks, q_swa, 0, kv_swa, kv_swa, (0, 1), kv_swa, kv_swa, (0, 1), 0,
                   bp, tp, WINDOW, False)
    ck = cache_swa_k[0].reshape(bs * keep_s, kv_w)
    cv = cache_swa_v[0].reshape(bs * keep_s, kv_w)
    o_swa_s = _swa(sinks, q_swa, n_p, ck, cv, (0, 0), kv_swa, kv_swa, (0, 1), n_p,
                   bs, ts, ts, True)

    o_mem_p = _mem_attn(q_mem, 0, 0, mem_kv, 0, mem_kv, MEM_HEADS, bp, tp, tm)
    cmk = cache_mem_k[0].reshape(bs * N_MEM, mem_w)
    cmv = cache_mem_v[0].reshape(bs * N_MEM, mem_w)
    o_mem_s = _mem_attn(q_mem, n_p, 0, cmk, 0, cmv, 0, bs, ts, ts)

    merged = _merge((o_gla_p, o_swa_p, o_mem_p), (o_gla_s, o_swa_s, o_mem_s), w_branch_b, gates, tm, tn)

    x2, hf, eid, wts = _outproj_router(merged, w_out_b, xp2, xs2, norm_ffn_g[0], w_router, b_router, tm)

    n_assign = TOP_K * n
    n_tiles = n_assign // MOE_TILE + N_EXPERTS
    eid_kmajor = jnp.concatenate([eid[:, k] for k in range(TOP_K)])
    tile_expert, tile_ok, tok_slots, dst_slots = _moe_schedule(eid_kmajor, n, n_tiles)
    yk = _moe(hf, tile_expert, tile_ok, tok_slots, dst_slots, w_up[0], w_down[0], n_assign + 2 * MOE_TILE)
    y_p, y_s = _combine(x2, yk, wts, n_p, tm)

    y_p = y_p.reshape(bp, tp, d)
    y_s = y_s.reshape(bs, ts, d)
    kv_p = kv_swa[:n_p].reshape(bp, tp, 2, SWA_KV_HEADS, SWA_HD)[:, tp - WINDOW:]
    kv_s = kv_swa[n_p:].reshape(bs, ts, 2, SWA_KV_HEADS, SWA_HD)
    swk_s = jnp.concatenate([cache_swa_k[0], kv_s[:, :, 0]], axis=1)[:, ts:ts + keep_s]
    swv_s = jnp.concatenate([cache_swa_v[0], kv_s[:, :, 1]], axis=1)[:, ts:ts + keep_s]
    mk_p = mem_kv[:, :mem_w].reshape(bp, N_MEM, MEM_HEADS, MEM_HD)
    mv_p = mem_kv[:, mem_w:].reshape(bp, N_MEM, MEM_HEADS, MEM_HD)
    return (y_p, y_s, gla_state_p[None], kv_p[:, :, 0][None], kv_p[:, :, 1][None], mk_p[None], mv_p[None],
            gla_state_s[None], swk_s[None], swv_s[None])
```

```python
import functools

import jax
import jax.numpy as jnp
from jax import lax
from jax.experimental import pallas as pl
from jax.experimental.pallas import tpu as pltpu

F32 = jnp.float32
BF16 = jnp.bfloat16

D_MODEL = 2048
CHUNK = 64
EPS = 1e-6
PAST_LEN = 1024
GLA_HEADS = 4
GLA_DV = 512
GLA_DK = 256
GLA_LOWRANK = 16
GLA_NORMALIZER = 16.0
SWA_HD = 64
SWA_HEADS = 32
SWA_KV_HEADS = 4
SWA_GROUP = 8
WINDOW = 128
ROPE_DIM = 16
ROPE_THETA = 500000.0
N_MEM = 256
MEM_HEADS = 4
MEM_HD = 512
N_GROUPS = 8
EXPERTS_PER_GROUP = 8
N_EXPERTS = 64
TOP_K = 2
D_FF = 512

LANES = 128
VMEM_LIMIT = 56 * 1024 * 1024
ROW_TILE = 512
PROJ_ROWS = 1536
COL_TILE = 512
WIDE_COLS = 1024
ROUTER_SUB_ROWS = 256
MOE_TILE = 256
TOKEN_TILE_ROWS = D_MODEL // 2 // LANES
TOKEN_TILE_PITCH = 12
TOKEN_F32_ROWS = D_MODEL // LANES
TOKEN_F32_PITCH = 20
NEG_BIG = -1e30


def _params(n_axes):
    return pltpu.CompilerParams(dimension_semantics=("arbitrary",) * n_axes,
                                vmem_limit_bytes=VMEM_LIMIT)


def _norm_kernel(x_ref, g_ref, o_ref):
    x = x_ref[...]
    y = x * lax.rsqrt(jnp.mean(x * x, axis=-1, keepdims=True) + EPS)
    o_ref[...] = (y * g_ref[...]).astype(o_ref.dtype)


def _rms_norm_rows(x, g, out_dtype, tm):
    n, d = x.shape
    return pl.pallas_call(
        _norm_kernel,
        out_shape=jax.ShapeDtypeStruct((n, d), out_dtype),
        grid=(n // tm,),
        in_specs=[pl.BlockSpec((tm, d), lambda i: (i, 0)),
                  pl.BlockSpec((1, d), lambda i: (0, 0))],
        out_specs=pl.BlockSpec((tm, d), lambda i: (i, 0)),
        compiler_params=_params(1),
        name="rms_norm_rows",
    )(x, g.reshape(1, d))


def _norm2_kernel(xp_ref, xs_ref, g_ref, o_ref, *, p_tiles):
    def emit(x_ref):
        x = x_ref[...]
        y = x * lax.rsqrt(jnp.mean(x * x, axis=-1, keepdims=True) + EPS)
        o_ref[...] = (y * g_ref[...]).astype(o_ref.dtype)

    @pl.when(pl.program_id(0) < p_tiles)
    def _():
        emit(xp_ref)

    @pl.when(pl.program_id(0) >= p_tiles)
    def _():
        emit(xs_ref)


def _split_specs(p_tiles, tm, d):
    return (pl.BlockSpec((tm, d), lambda i: (jnp.minimum(i, p_tiles - 1), 0)),
            pl.BlockSpec((tm, d), lambda i: (jnp.maximum(i - p_tiles, 0), 0)))


def _rms_norm_two(xp, xs, g, tm):
    (n_p, d), n_s = xp.shape, xs.shape[0]
    p_tiles = n_p // tm
    return pl.pallas_call(
        functools.partial(_norm2_kernel, p_tiles=p_tiles),
        out_shape=jax.ShapeDtypeStruct((n_p + n_s, d), BF16),
        grid=((n_p + n_s) // tm,),
        in_specs=[*_split_specs(p_tiles, tm, d), pl.BlockSpec((1, d), lambda i: (0, 0))],
        out_specs=pl.BlockSpec((tm, d), lambda i: (i, 0)),
        compiler_params=_params(1),
        name="rms_norm_mix",
    )(xp, xs, g.reshape(1, d))


def _segment_rms(acc, seg_ref, inv_width):
    sq = acc * acc
    hi = sq.astype(BF16)
    lo = (sq - hi.astype(F32)).astype(BF16)
    ss = (jnp.dot(hi, seg_ref[...], preferred_element_type=F32)
          + jnp.dot(lo, seg_ref[...], preferred_element_type=F32))
    return ss * inv_width


def _rope(y, cos_ref, sa_ref, sb_ref):
    width = y.shape[1]
    reps = width // LANES
    c = jnp.concatenate([cos_ref[...]] * reps, axis=1)
    sa = jnp.concatenate([sa_ref[...]] * reps, axis=1)
    sb = jnp.concatenate([sb_ref[...]] * reps, axis=1)
    half = ROPE_DIM // 2
    return y * c + pltpu.roll(y, width - half, 1) * sa + pltpu.roll(y, half, 1) * sb


def _ep_plain(acc):
    return acc


def _ep_lowrank(acc):
    lane = lax.broadcasted_iota(jnp.int32, acc.shape, 1)
    return jnp.where(lane < GLA_LOWRANK, acc, 0.0)


def _ep_sigmoid(acc, b_ref):
    return jax.nn.sigmoid(acc + b_ref[...])


def _ep_qknorm_rope(acc, g_ref, seg_ref, cos_ref, sa_ref, sb_ref, *, keep_from):
    ms = _segment_rms(acc, seg_ref, 1.0 / SWA_HD)
    y = acc * lax.rsqrt(ms + EPS) * g_ref[...]
    y = _rope(y, cos_ref, sa_ref, sb_ref)
    if keep_from is not None:
        col = lax.broadcasted_iota(jnp.int32, y.shape, 1)
        y = jnp.where(col < keep_from, y, acc)
    return y


def _ep_headnorm(acc, g_ref, *, norm_tiles):
    y = acc * lax.rsqrt(jnp.mean(acc * acc, axis=-1, keepdims=True) + EPS) * g_ref[...]
    if norm_tiles is not None:
        y = jnp.where(pl.program_id(0) < norm_tiles, y, acc)
    return y


def _stage_weight_tile(w_refs, wbf, shift):
    if shift == 0:
        wbf[...] = w_refs[0][...].astype(BF16)
        return
    main, ext = w_refs
    n_slabs = wbf.shape[1] // LANES
    lane = lax.broadcasted_iota(jnp.int32, (wbf.shape[0], LANES), 1)
    rolled = [pltpu.roll(main[:, b * LANES:(b + 1) * LANES], LANES - shift, 1) for b in range(n_slabs)]
    rolled.append(pltpu.roll(ext[...], LANES - shift, 1))
    for b in range(n_slabs):
        wbf[:, b * LANES:(b + 1) * LANES] = jnp.where(lane < LANES - shift, rolled[b], rolled[b + 1]).astype(BF16)


def _wmm_kernel(*refs, n_w, shift, epilogue):
    a_ref, w_refs, extras, o_ref, wbf = refs[0], refs[1:1 + n_w], refs[1 + n_w:-2], refs[-2], refs[-1]

    @pl.when(pl.program_id(1) == 0)
    def _():
        _stage_weight_tile(w_refs, wbf, shift)

    acc = jnp.dot(a_ref[...], wbf[...], preferred_element_type=F32)
    o_ref[...] = epilogue(acc, *extras).astype(o_ref.dtype)


def _wmatmul(epilogue, a, w, col0, n_cols, shift, extras, extra_specs, out_dtype, tm, tn, name):
    m, k = a.shape
    assert m % tm == 0 and n_cols % tn == 0 and col0 % tn == 0 and 0 <= shift < LANES
    j0 = col0 // tn
    w_specs = [pl.BlockSpec((k, tn), lambda j, i: (0, j0 + j))]
    if shift:
        e0 = col0 // LANES
        w_specs.append(pl.BlockSpec((k, LANES), lambda j, i: (0, e0 + (j + 1) * (tn // LANES))))
    kernel = functools.partial(_wmm_kernel, n_w=len(w_specs), shift=shift, epilogue=epilogue)
    return pl.pallas_call(
        kernel,
        out_shape=jax.ShapeDtypeStruct((m, n_cols), out_dtype),
        grid=(n_cols // tn, m // tm),
        in_specs=[pl.BlockSpec((tm, k), lambda j, i: (i, 0))] + w_specs + list(extra_specs),
        out_specs=pl.BlockSpec((tm, tn), lambda j, i: (i, j)),
        scratch_shapes=[pltpu.VMEM((k, tn), BF16)],
        compiler_params=_params(2),
        name=name,
    )(a, *([w] * len(w_specs)), *extras)


def _gla_kernel(q_ref, k_ref, v_ref, gg_ref, ga_ref, wa_ref, ba_ref, gn_ref, s0_ref,
                o_ref, sout_ref, s_scr, *, chunk, n_chunks):
    t = pl.program_id(1)

    @pl.when(t == 0)
    def _():
        s_scr[...] = s0_ref[0]

    row = lax.broadcasted_iota(jnp.int32, (chunk, chunk), 0)
    col = lax.broadcasted_iota(jnp.int32, (chunk, chunk), 1)
    causal = row >= col
    tril = causal.astype(F32)

    def one_chunk(ci, carry):
        rows = pl.ds(pl.multiple_of(ci * chunk, chunk), chunk)
        z = jnp.dot(ga_ref[rows, :].astype(BF16), wa_ref[...], preferred_element_type=F32) + ba_ref[...]
        log_a = (jnp.minimum(z, 0.0) - jnp.log1p(jnp.exp(-jnp.abs(z)))) * (1.0 / GLA_NORMALIZER)
        b_all = jnp.dot(tril, log_a, preferred_element_type=F32, precision=lax.Precision.HIGHEST)
        for h in range(GLA_HEADS):
            ks = slice(h * GLA_DK, (h + 1) * GLA_DK)
            vs = slice(h * GLA_DV, (h + 1) * GLA_DV)
            b = b_all[:, ks]
            b_last = b[chunk - 1:chunk, :]
            q = q_ref[rows, ks] * (GLA_DK ** -0.5)
            k = k_ref[rows, ks]
            vb = v_ref[rows, vs].astype(BF16)
            q_t = (q * jnp.exp(b)).astype(BF16)
            k_t = (k * jnp.exp(-b)).astype(BF16)
            k_u = (k * jnp.exp(b_last - b)).astype(BF16)
            att = lax.dot_general(q_t, k_t, (((1,), (1,)), ((), ())), preferred_element_type=F32)
            att = jnp.where(causal, att, 0.0)
            s_old = s_scr[h]
            o = (jnp.dot(q_t, s_old.astype(BF16), preferred_element_type=F32)
                 + jnp.dot(att.astype(BF16), vb, preferred_element_type=F32))
            decay = jnp.transpose(jnp.broadcast_to(jnp.exp(b_last), (LANES, GLA_DK)))
            decay = jnp.concatenate([decay] * (GLA_DV // LANES), axis=1)
            s_scr[h] = decay * s_old + lax.dot_general(k_u, vb, (((0,), (0,)), ((), ())),
                                                       preferred_element_type=F32)
            on = o * lax.rsqrt(jnp.mean(o * o, axis=-1, keepdims=True) + EPS) * gn_ref[...]
            gg = gg_ref[rows, vs]
            o_ref[rows, vs] = (on * (gg * jax.nn.sigmoid(gg))).astype(o_ref.dtype)
        return carry

    lax.fori_loop(0, n_chunks, one_chunk, 0)

    @pl.when(t == pl.num_programs(1) - 1)
    def _():
        sout_ref[0] = s_scr[...]


def _gla(qkvg, ga, wa, ba, gn, s0, batch, seq, row0, tb, chunk):
    nt = seq // tb
    base = row0 // tb
    qk_w = GLA_HEADS * GLA_DK
    v_w = GLA_HEADS * GLA_DV
    rows = lambda b, t: base + b * nt + t
    kernel = functools.partial(_gla_kernel, chunk=chunk, n_chunks=tb // chunk)
    return pl.pallas_call(
        kernel,
        out_shape=(jax.ShapeDtypeStruct((batch * seq, v_w), BF16),
                   jax.ShapeDtypeStruct((batch, GLA_HEADS, GLA_DK, GLA_DV), F32)),
        grid=(batch, nt),
        in_specs=[pl.BlockSpec((tb, qk_w), lambda b, t: (rows(b, t), 0)),
                  pl.BlockSpec((tb, qk_w), lambda b, t: (rows(b, t), 1)),
                  pl.BlockSpec((tb, v_w), lambda b, t: (rows(b, t), 1)),
                  pl.BlockSpec((tb, v_w), lambda b, t: (rows(b, t), 2)),
                  pl.BlockSpec((tb, LANES), lambda b, t: (rows(b, t), 0)),
                  pl.BlockSpec((LANES, qk_w), lambda b, t: (0, 0)),
                  pl.BlockSpec((1, qk_w), lambda b, t: (0, 0)),
                  pl.BlockSpec((1, GLA_DV), lambda b, t: (0, 0)),
                  pl.BlockSpec((1, GLA_HEADS, GLA_DK, GLA_DV), lambda b, t: (b, 0, 0, 0))],
        out_specs=(pl.BlockSpec((tb, v_w), lambda b, t: (b * nt + t, 0)),
                   pl.BlockSpec((1, GLA_HEADS, GLA_DK, GLA_DV), lambda b, t: (b, 0, 0, 0))),
        scratch_shapes=[pltpu.VMEM((GLA_HEADS, GLA_DK, GLA_DV), F32)],
        compiler_params=_params(2),
        name="gla_chunks",
    )(qkvg, qkvg, qkvg, qkvg, ga, wa, ba, gn, s0)


def _swa_kernel(sink_ref, q_ref, kp_ref, vp_ref, ko_ref, vo_ref, o_ref, *, tq, prev_from_cache):
    i = pl.program_id(1)
    nk = WINDOW + tq
    k_all = jnp.concatenate([kp_ref[...], ko_ref[...]], axis=0)
    v_all = jnp.concatenate([vp_ref[...], vo_ref[...]], axis=0)
    qc = lax.broadcasted_iota(jnp.int32, (tq, nk), 0) // CHUNK + WINDOW // CHUNK
    kcol = lax.broadcasted_iota(jnp.int32, (tq, nk), 1)
    kc = kcol // CHUNK
    valid = (kc <= qc) & (kc >= qc - WINDOW // CHUNK)
    if not prev_from_cache:
        valid = valid & ((kcol >= WINDOW) | (i > 0))
    lane = lax.broadcasted_iota(jnp.int32, (nk, LANES), 1)
    low = lane < SWA_HD
    low_q = lax.broadcasted_iota(jnp.int32, (tq, LANES), 1) < SWA_HD
    scale = SWA_HD ** -0.5
    for g in range(SWA_KV_HEADS):
        slab = slice((g // 2) * LANES, (g // 2 + 1) * LANES)
        k2 = k_all[:, slab]
        v2 = v_all[:, slab]
        k2r = pltpu.roll(k2, SWA_HD, 1)
        v2r = pltpu.roll(v2, SWA_HD, 1)
        if g % 2 == 0:
            k_lo, k_hi, v_lo, v_hi = k2, k2r, v2, v2r
        else:
            k_lo, k_hi, v_lo, v_hi = k2r, k2, v2r, v2
        zero = jnp.zeros_like(k2)
        one = jnp.ones_like(k2)
        km = (jnp.where(low, k_lo, zero).astype(BF16), jnp.where(low, zero, k_hi).astype(BF16))
        vm = (jnp.where(low, v_lo, one).astype(BF16), jnp.where(low, one, v_hi).astype(BF16))
        heads = [(j, half) for j in range(SWA_GROUP // 2) for half in range(2)]
        sinks = [sink_ref[g * SWA_GROUP + 2 * j + half] for j, half in heads]
        scores = []
        for j, half in heads:
            qs = q_ref[:, (g * 4 + j) * LANES:(g * 4 + j + 1) * LANES]
            s = lax.dot_general(qs, km[half], (((1,), (1,)), ((), ())), preferred_element_type=F32) * scale
            scores.append(jnp.where(valid, s, NEG_BIG))
        maxes = [jnp.maximum(jnp.max(s, axis=-1, keepdims=True), sk) for s, sk in zip(scores, sinks)]
        exps = [jnp.exp(s - m).astype(BF16) for s, m in zip(scores, maxes)]
        sink_terms = [jnp.exp(sk - m) for sk, m in zip(sinks, maxes)]
        for j in range(SWA_GROUP // 2):
            a_lo = jnp.dot(exps[2 * j], vm[0], preferred_element_type=F32)
            a_hi = jnp.dot(exps[2 * j + 1], vm[1], preferred_element_type=F32)
            num = jnp.where(low_q, a_lo, a_hi)
            den = pltpu.roll(jnp.where(low_q, a_hi, a_lo), SWA_HD, 1)
            den = den + jnp.where(low_q, sink_terms[2 * j], sink_terms[2 * j + 1])
            o_ref[:, (g * 4 + j) * LANES:(g * 4 + j + 1) * LANES] = (num / den).astype(o_ref.dtype)


def _swa(sinks, q, q_row0, k_prev, v_prev, prev_col, k_own, v_own, own_col, own_row0,
         batch, seq, tq, prev_from_cache):
    nt = seq // tq
    qb = q_row0 // tq
    ob = own_row0 // tq
    kv_w = SWA_KV_HEADS * SWA_HD
    if prev_from_cache:
        prev_map = lambda b, t, c: (b, c)
    else:
        per = seq // WINDOW
        prev_map = lambda b, t, c: (b * per + jnp.maximum(t * (tq // WINDOW) - 1, 0), c)
    kernel = functools.partial(_swa_kernel, tq=tq, prev_from_cache=prev_from_cache)
    return pl.pallas_call(
        kernel,
        out_shape=jax.ShapeDtypeStruct((batch * seq, SWA_HEADS * SWA_HD), BF16),
        grid=(batch, nt),
        in_specs=[pl.BlockSpec(memory_space=pltpu.SMEM),
                  pl.BlockSpec((tq, SWA_HEADS * SWA_HD), lambda b, t: (qb + b * nt + t, 0)),
                  pl.BlockSpec((WINDOW, kv_w), lambda b, t: prev_map(b, t, prev_col[0])),
                  pl.BlockSpec((WINDOW, kv_w), lambda b, t: prev_map(b, t, prev_col[1])),
                  pl.BlockSpec((tq, kv_w), lambda b, t: (ob + b * nt + t, own_col[0])),
                  pl.BlockSpec((tq, kv_w), lambda b, t: (ob + b * nt + t, own_col[1]))],
        out_specs=pl.BlockSpec((tq, SWA_HEADS * SWA_HD), lambda b, t: (b * nt + t, 0)),
        compiler_params=_params(2),
        name="swa_band",
    )(sinks, q, k_prev, v_prev, k_own, v_own)


def _mem_attn_head(q, k, v):
    s = lax.dot_general(q, k.astype(BF16), (((1,), (1,)), ((), ())),
                        preferred_element_type=F32) * (MEM_HD ** -0.5)
    m = jnp.max(s, axis=-1, keepdims=True)
    e = jnp.exp(s - m)
    p = (e / jnp.sum(e, axis=-1, keepdims=True)).astype(BF16)
    return jnp.dot(p, v.astype(BF16), preferred_element_type=F32)


def _mem_attn_kernel(q_ref, k_ref, v_ref, o_ref):
    o_ref[...] = _mem_attn_head(q_ref[...], k_ref[...], v_ref[...]).astype(o_ref.dtype)


def _mem_attn_cache_kernel(q_ref, k_ref, v_ref, o_ref):
    for h in range(MEM_HEADS):
        cols = slice(h * MEM_HD, (h + 1) * MEM_HD)
        o_ref[:, cols] = _mem_attn_head(q_ref[:, cols], k_ref[0, 0, :, h, :], v_ref[0, 0, :, h, :]).astype(o_ref.dtype)


def _mem_attn_cache(q, q_row0, cache_k, cache_v, batch, seq):
    qb = q_row0 // seq
    width = MEM_HEADS * MEM_HD
    cache_spec = pl.BlockSpec((1, 1, N_MEM, MEM_HEADS, MEM_HD), lambda b: (0, b, 0, 0, 0))
    return pl.pallas_call(
        _mem_attn_cache_kernel,
        out_shape=jax.ShapeDtypeStruct((batch * seq, width), BF16),
        grid=(batch,),
        in_specs=[pl.BlockSpec((seq, width), lambda b: (qb + b, 0)), cache_spec, cache_spec],
        out_specs=pl.BlockSpec((seq, width), lambda b: (b, 0)),
        compiler_params=_params(1),
        name="mem_attn_cache",
    )(q, cache_k, cache_v)


def _mem_attn(q, q_row0, q_col0, mk, mk_col0, mv, mv_col0, batch, seq, tq):
    nt = seq // tq
    qb = q_row0 // tq
    return pl.pallas_call(
        _mem_attn_kernel,
        out_shape=jax.ShapeDtypeStruct((batch * seq, MEM_HEADS * MEM_HD), BF16),
        grid=(batch, MEM_HEADS, nt),
        in_specs=[pl.BlockSpec((tq, MEM_HD), lambda b, h, t: (qb + b * nt + t, q_col0 + h)),
                  pl.BlockSpec((N_MEM, MEM_HD), lambda b, h, t: (b, mk_col0 + h)),
                  pl.BlockSpec((N_MEM, MEM_HD), lambda b, h, t: (b, mv_col0 + h))],
        out_specs=pl.BlockSpec((tq, MEM_HD), lambda b, h, t: (b * nt + t, h)),
        compiler_params=_params(3),
        name="mem_attn",
    )(q, mk, mv)


def _merge_kernel(a0p, a0s, a1p, a1s, a2p, a2s, w_ref, g0_ref, g1_ref, g2_ref, o_ref, *, p_tiles):
    def emit(a0_ref, a1_ref, a2_ref):
        acc = g0_ref[...] * jnp.dot(a0_ref[...], w_ref[0], preferred_element_type=F32)
        acc = acc + g1_ref[...] * jnp.dot(a1_ref[...], w_ref[1], preferred_element_type=F32)
        acc = acc + g2_ref[...] * jnp.dot(a2_ref[...], w_ref[2], preferred_element_type=F32)
        o_ref[...] = acc.astype(o_ref.dtype)

    @pl.when(pl.program_id(1) < p_tiles)
    def _():
        emit(a0p, a1p, a2p)

    @pl.when(pl.program_id(1) >= p_tiles)
    def _():
        emit(a0s, a1s, a2s)


def _merge(branches_p, branches_s, w_branch, gates, tm, tn):
    n_p, d = branches_p[0].shape
    n = n_p + branches_s[0].shape[0]
    nj = d // tn
    p_tiles = n_p // tm
    ap = pl.BlockSpec((tm, d), lambda j, i: (jnp.minimum(i, p_tiles - 1), 0))
    asp = pl.BlockSpec((tm, d), lambda j, i: (jnp.maximum(i - p_tiles, 0), 0))
    operands = [a for pair in zip(branches_p, branches_s) for a in pair]
    return pl.pallas_call(
        functools.partial(_merge_kernel, p_tiles=p_tiles),
        out_shape=jax.ShapeDtypeStruct((n, d), BF16),
        grid=(nj, n // tm),
        in_specs=[ap, asp, ap, asp, ap, asp,
                  pl.BlockSpec((3, d, tn), lambda j, i: (0, 0, j), pipeline_mode=pl.Buffered(1)),
                  pl.BlockSpec((tm, tn), lambda j, i: (i, j)),
                  pl.BlockSpec((tm, tn), lambda j, i: (i, nj + j)),
                  pl.BlockSpec((tm, tn), lambda j, i: (i, 2 * nj + j))],
        out_specs=pl.BlockSpec((tm, tn), lambda j, i: (i, j)),
        compiler_params=_params(2),
        name="branch_merge",
    )(*operands, w_branch, gates, gates, gates)


def _outproj_router_kernel(m_ref, w_ref, xp_ref, xs_ref, g_ref, wr_ref, br_ref,
                           x2_ref, hf_ref, eid_ref, wt_ref, *, p_tiles):
    is_prompt = pl.program_id(0) < p_tiles
    tm, d = x2_ref.shape
    for r0 in range(0, tm, ROUTER_SUB_ROWS):
        rows = slice(r0, r0 + ROUTER_SUB_ROWS)
        acc = jnp.dot(m_ref[rows, :], w_ref[...], preferred_element_type=F32)
        x = jnp.where(is_prompt, xp_ref[rows, :], xs_ref[rows, :]) + acc
        x2_ref[rows, :] = x
        hf = x * lax.rsqrt(jnp.mean(x * x, axis=-1, keepdims=True) + EPS) * g_ref[...]
        hb = hf.astype(BF16)
        bits = pltpu.bitcast(hb.astype(F32), jnp.uint32)
        packed = bits[:, d // 2:] | (bits[:, :d // 2] >> 16)
        for s in range(TOKEN_TILE_ROWS):
            hf_ref[pl.ds(r0 * TOKEN_TILE_ROWS + s, ROUTER_SUB_ROWS, stride=TOKEN_TILE_ROWS), :] = (
                packed[:, s * LANES:(s + 1) * LANES])
        logits = jnp.dot(hb, wr_ref[...], preferred_element_type=F32) + br_ref[...]
        eid, wts = _route(logits)
        eid_ref[rows, :] = eid
        wt_ref[rows, :] = wts


def _route(logits):
    lane = lax.broadcasted_iota(jnp.int32, logits.shape, 1).astype(F32)
    big = 1e6
    is_g = lane < N_GROUPS
    lg = jnp.where(is_g, logits, NEG_BIG)
    mg = jnp.max(lg, axis=-1, keepdims=True)
    gsel = jnp.min(jnp.where(is_g & (lg == mg), lane, big), axis=-1, keepdims=True)
    g_w = 1.0 / jnp.sum(jnp.where(is_g, jnp.exp(lg - mg), 0.0), axis=-1, keepdims=True)
    e_lo = N_GROUPS + gsel * EXPERTS_PER_GROUP
    in_grp = (lane >= e_lo) & (lane < e_lo + EXPERTS_PER_GROUP)
    le = jnp.where(in_grp, logits, NEG_BIG)
    me = jnp.max(le, axis=-1, keepdims=True)
    ee = jnp.where(in_grp, jnp.exp(le - me), 0.0)
    pe = ee / jnp.sum(ee, axis=-1, keepdims=True)
    pe = jnp.where(in_grp, pe, -1.0)
    p1 = jnp.max(pe, axis=-1, keepdims=True)
    i1 = jnp.min(jnp.where(pe == p1, lane, big), axis=-1, keepdims=True)
    pe2 = jnp.where(lane == i1, -1.0, pe)
    p2 = jnp.max(pe2, axis=-1, keepdims=True)
    i2 = jnp.min(jnp.where(pe2 == p2, lane, big), axis=-1, keepdims=True)
    tot = p1 + p2
    w1 = g_w * p1 / tot
    w2 = g_w * p2 / tot
    eid = jnp.where(lane == 0.0, i1 - N_GROUPS, jnp.where(lane == 1.0, i2 - N_GROUPS, 0.0))
    return eid.astype(jnp.int32), jnp.where(lane == 0.0, w1, jnp.where(lane == 1.0, w2, 0.0))


def _outproj_router(merged, w_out, xp, xs, g, wr, br, tm):
    n, d = merged.shape
    p_tiles = xp.shape[0] // tm
    const = lambda shape: pl.BlockSpec(shape, lambda i: (0, 0), pipeline_mode=pl.Buffered(1))
    row = lambda width: pl.BlockSpec((tm, width), lambda i: (i, 0))
    return pl.pallas_call(
        functools.partial(_outproj_router_kernel, p_tiles=p_tiles),
        out_shape=(jax.ShapeDtypeStruct((n, d), F32),
                   jax.ShapeDtypeStruct((n * TOKEN_TILE_ROWS, LANES), jnp.uint32),
                   jax.ShapeDtypeStruct((n, LANES), jnp.int32),
                   jax.ShapeDtypeStruct((n, LANES), F32)),
        grid=(n // tm,),
        in_specs=[row(d), const((d, d)), *_split_specs(p_tiles, tm, d),
                  const((1, d)), const((d, LANES)), const((1, LANES))],
        out_specs=(row(d), pl.BlockSpec((tm * TOKEN_TILE_ROWS, LANES), lambda i: (i, 0)),
                   row(LANES), row(LANES)),
        compiler_params=_params(1),
        name="outproj_router",
    )(merged, w_out, xp, xs, g.reshape(1, d), wr, br)


def _moe_kernel(te_ref, ok_ref, nxt_ref, tok_ref, tok_next_ref, dst_ref, hf_hbm, wup_hbm, wdn_hbm, y_hbm,
                xg, yb, wup_f32, wdn_f32, wup_bf, wdn_bf, in_sem, out_sem, w_sem):
    t = pl.program_id(0)
    buf = t % 2
    valid = ok_ref[t] > 0
    valid_next = ok_ref[t + 1] > 0
    new_expert = (t == 0) | (te_ref[t] != te_ref[jnp.maximum(t - 1, 0)])

    def weight_copies(e):
        return (pltpu.make_async_copy(wup_hbm.at[e], wup_f32, w_sem.at[0]),
                pltpu.make_async_copy(wdn_hbm.at[e], wdn_f32, w_sem.at[1]))

    in_rows, in_pitch = TOKEN_TILE_ROWS, TOKEN_TILE_PITCH
    out_rows, out_pitch = TOKEN_F32_ROWS, TOKEN_F32_PITCH

    def row_in(row0, r, b):
        return pltpu.make_async_copy(hf_hbm.at[pl.ds(pl.multiple_of(row0, in_rows), in_rows), :],
                                     xg.at[b, pl.ds(r * in_pitch, in_rows), :], in_sem.at[b])

    def row_out(row0, r, b):
        return pltpu.make_async_copy(yb.at[b, pl.ds(r * out_pitch, out_rows), :],
                                     y_hbm.at[pl.ds(pl.multiple_of(row0, out_rows), out_rows), :], out_sem.at[b])

    def tile_in(b):
        return pltpu.make_async_copy(hf_hbm.at[pl.ds(0, MOE_TILE * in_rows), :],
                                     xg.at[b, pl.ds(0, MOE_TILE * in_rows), :], in_sem.at[b])

    def tile_out(b):
        return pltpu.make_async_copy(yb.at[b, pl.ds(0, MOE_TILE * out_rows), :],
                                     y_hbm.at[pl.ds(0, MOE_TILE * out_rows), :], out_sem.at[b])

    @pl.when(t == 0)
    def _():
        for c in weight_copies(te_ref[0]):
            c.start(priority=1)

        def first(r, c):
            row_in(tok_ref[0, 0, r], r, 0).start()
            return c
        lax.fori_loop(0, MOE_TILE, first, 0)
        yb[...] = jnp.zeros_like(yb)
        n_real = y_hbm.shape[0] - 2 * MOE_TILE * out_rows
        for b in range(2):
            spare = pltpu.make_async_copy(
                yb.at[b, pl.ds(0, MOE_TILE * out_rows), :],
                y_hbm.at[pl.ds(n_real + b * MOE_TILE * out_rows, MOE_TILE * out_rows), :], out_sem.at[b])
            spare.start()
            spare.wait()

    @pl.when(valid_next)
    def _():
        for r in range(MOE_TILE):
            row_in(tok_next_ref[0, 0, r], r, 1 - buf).start()

    @pl.when(valid & new_expert)
    def _():
        for c in weight_copies(te_ref[t]):
            c.wait()
        wup_bf[...] = wup_f32[...].astype(BF16)
        wdn_bf[...] = wdn_f32[...].astype(BF16)

        @pl.when(nxt_ref[t] >= 0)
        def _():
            for c in weight_copies(nxt_ref[t]):
                c.start(priority=1)

    @pl.when(valid)
    def _():
        tile_in(buf).wait()
        lo, hi = [], []
        for s in range(in_rows):
            word = xg[buf, pl.ds(s, MOE_TILE, stride=in_pitch), :]
            lo.append(pltpu.bitcast(word << 16, F32).astype(BF16))
            hi.append(pltpu.bitcast(word & jnp.uint32(0xFFFF0000), F32).astype(BF16))
        x = jnp.concatenate(lo + hi, axis=1)
        h1 = jnp.dot(x, wup_bf[...], preferred_element_type=F32)
        gate = h1[:, :D_FF]
        up = h1[:, D_FF:]
        act = (gate * jax.nn.sigmoid(gate)) * up
        ye = jnp.dot(act.astype(BF16), wdn_bf[...], preferred_element_type=F32)
        for s in range(out_rows):
            yb[buf, pl.ds(s, MOE_TILE, stride=out_pitch), :] = ye[:, s * LANES:(s + 1) * LANES]
        for r in range(MOE_TILE):
            row_out(dst_ref[0, 0, r], r, buf).start()

        @pl.when(t > 0)
        def _():
            tile_out(1 - buf).wait()

        @pl.when(jnp.logical_not(valid_next))
        def _():
            tile_out(buf).wait()


def _moe(hf, tile_expert, tile_ok, next_expert, tok_slots, dst_slots, w_up, w_down, out_tokens):
    n_tiles = tile_expert.shape[0]
    d = w_up.shape[1]
    slot_spec = lambda nxt: pl.BlockSpec((1, 1, MOE_TILE),
                                         lambda t, te, ok, ne: (jnp.minimum(t + nxt, n_tiles - 1), 0, 0),
                                         memory_space=pltpu.SMEM)
    hbm = pl.BlockSpec(memory_space=pl.ANY)
    grid_spec = pltpu.PrefetchScalarGridSpec(
        num_scalar_prefetch=3,
        grid=(n_tiles,),
        in_specs=[slot_spec(0), slot_spec(1), slot_spec(0), hbm, hbm, hbm],
        out_specs=hbm,
        scratch_shapes=[pltpu.VMEM((2, MOE_TILE * TOKEN_TILE_PITCH, LANES), jnp.uint32),
                        pltpu.VMEM((2, MOE_TILE * TOKEN_F32_PITCH, LANES), F32),
                        pltpu.VMEM((d, 2 * D_FF), F32),
                        pltpu.VMEM((D_FF, d), F32),
                        pltpu.VMEM((d, 2 * D_FF), BF16),
                        pltpu.VMEM((D_FF, d), BF16),
                        pltpu.SemaphoreType.DMA((2,)),
                        pltpu.SemaphoreType.DMA((2,)),
                        pltpu.SemaphoreType.DMA((2,))],
    )
    return pl.pallas_call(
        _moe_kernel,
        out_shape=jax.ShapeDtypeStruct((out_tokens * TOKEN_F32_ROWS, LANES), F32),
        grid_spec=grid_spec,
        compiler_params=_params(1),
        name="moe_experts",
    )(tile_expert, tile_ok, next_expert, tok_slots, tok_slots, dst_slots, hf, w_up, w_down)


def _combine_kernel(x_ref, y0_ref, y1_ref, w_ref, op_ref, os_ref, *, p_tiles):
    tm = x_ref.shape[0]
    w = w_ref[...]
    w0 = w[:, 0:1]
    w1 = w[:, 1:2]

    def emit(o_ref):
        for s in range(TOKEN_F32_ROWS):
            cols = slice(s * LANES, (s + 1) * LANES)
            rows = pl.ds(s, tm, stride=TOKEN_F32_ROWS)
            o_ref[:, cols] = x_ref[:, cols] + (y0_ref[rows, :] * w0 + y1_ref[rows, :] * w1)

    @pl.when(pl.program_id(0) < p_tiles)
    def _():
        emit(op_ref)

    @pl.when(pl.program_id(0) >= p_tiles)
    def _():
        emit(os_ref)


def _combine(x2, yk, wts, n_p, tm):
    n, d = x2.shape
    p_tiles = n_p // tm
    k1 = n // tm
    y_rows = tm * TOKEN_F32_ROWS
    return pl.pallas_call(
        functools.partial(_combine_kernel, p_tiles=p_tiles),
        out_shape=(jax.ShapeDtypeStruct((n_p, d), F32), jax.ShapeDtypeStruct((n - n_p, d), F32)),
        grid=(n // tm,),
        in_specs=[pl.BlockSpec((tm, d), lambda i: (i, 0)),
                  pl.BlockSpec((y_rows, LANES), lambda i: (i, 0)),
                  pl.BlockSpec((y_rows, LANES), lambda i: (k1 + i, 0)),
                  pl.BlockSpec((tm, LANES), lambda i: (i, 0))],
        out_specs=_split_specs(p_tiles, tm, d),
        compiler_params=_params(1),
        name="moe_combine",
    )(x2, yk, yk, wts)


def _rope_tables(pos):
    half = ROPE_DIM // 2
    inv = ROPE_THETA ** (-jnp.arange(half, dtype=F32) / half)
    ang = pos.astype(F32)[:, None] * inv[None, :]
    cos, sin = jnp.cos(ang), jnp.sin(ang)
    n = pos.shape[0]
    pad = jnp.zeros((n, SWA_HD - ROPE_DIM), F32)
    cos_h = jnp.concatenate([cos, cos, pad + 1.0], axis=1)
    sa_h = jnp.concatenate([-sin, jnp.zeros_like(sin), pad], axis=1)
    sb_h = jnp.concatenate([jnp.zeros_like(sin), sin, pad], axis=1)
    reps = LANES // SWA_HD
    return tuple(jnp.tile(a, (1, reps)) for a in (cos_h, sa_h, sb_h))


def _moe_schedule(eid, n_tok, n_tiles):
    a = eid.shape[0]
    order = jnp.argsort(eid, stable=True).astype(jnp.int32)
    counts = jnp.bincount(eid, length=N_EXPERTS).astype(jnp.int32)
    tiles_per = (counts + MOE_TILE - 1) // MOE_TILE
    tile_end = jnp.cumsum(tiles_per)
    tile_start = tile_end - tiles_per
    sorted_start = jnp.cumsum(counts) - counts
    tile_id = jnp.arange(n_tiles, dtype=jnp.int32)
    used = tile_id < tile_end[-1]
    te = jnp.minimum(jnp.searchsorted(tile_end, tile_id, side="right"), N_EXPERTS - 1).astype(jnp.int32)
    last_used_e = te[jnp.maximum(tile_end[-1] - 1, 0)]
    te = jnp.where(used, te, last_used_e)
    row_in_expert = (tile_id - tile_start[te]) * MOE_TILE
    rows_valid = jnp.where(used, jnp.clip(counts[te] - row_in_expert, 0, MOE_TILE), 0).astype(jnp.int32)
    r = jnp.arange(MOE_TILE, dtype=jnp.int32)[None, :]
    src = sorted_start[te][:, None] + row_in_expert[:, None] + r
    real = r < rows_valid[:, None]
    assign = order[jnp.clip(src, 0, a - 1)]
    tok = jnp.where(real, assign % n_tok, 0) * TOKEN_TILE_ROWS
    spare = a + (tile_id[:, None] % 2) * MOE_TILE + r
    dst = jnp.where(real, assign, spare) * TOKEN_F32_ROWS
    ok = jnp.concatenate([used.astype(jnp.int32), jnp.zeros((1,), jnp.int32)])
    first_other = jnp.searchsorted(jnp.where(used, te, N_EXPERTS), te, side="right")
    nxt = jnp.where(first_other < tile_end[-1], te[jnp.minimum(first_other, n_tiles - 1)], -1).astype(jnp.int32)
    shape = (n_tiles, 1, MOE_TILE)
    return te, ok, nxt, tok.reshape(shape).astype(jnp.int32), dst.reshape(shape).astype(jnp.int32)


def kernel(x_prompt, x_sample, state_gla, cache_swa_k, cache_swa_v, cache_mem_k, cache_mem_v,
           mem_prompt, norm_mix_g, w_in, w_a2, b_a2, gla_norm_g, swa_q_norm_g, swa_k_norm_g,
           swa_sinks, norm_mem_g, w_mem_kv, mem_q_norm_g, mem_k_norm_g, w_gate, b_gate,
           w_branch, w_out, norm_ffn_g, w_router_group, b_router_group, w_router_expert,
           b_router_expert, w_up, w_down):
    bp, tp, d = x_prompt.shape
    bs, ts, _ = x_sample.shape
    n_p, n_s = bp * tp, bs * ts
    n = n_p + n_s
    tm, tn = ROW_TILE, COL_TILE
    tp_rows = PROJ_ROWS if n % PROJ_ROWS == 0 else tm
    assert d == D_MODEL and n_p % tm == 0 and n_s % tm == 0 and w_in.shape[0] == 1
    keep_s = cache_swa_k.shape[2]
    assert keep_s == WINDOW and tp % WINDOW == 0

    qk_w = GLA_HEADS * GLA_DK
    v_w = GLA_HEADS * GLA_DV
    c0 = 2 * qk_w + 2 * v_w
    sq_w = SWA_HEADS * SWA_HD
    kv_w = SWA_KV_HEADS * SWA_HD
    mem_w = MEM_HEADS * MEM_HD
    w_in0 = w_in[0]
    w_branch_b = w_branch[0].astype(BF16)
    w_out_b = w_out[0].astype(BF16)
    w_a2_b = jnp.pad(w_a2[0], ((0, LANES - GLA_LOWRANK), (0, 0))).astype(BF16)
    w_router = jnp.pad(jnp.concatenate([w_router_group[0], w_router_expert[0]], axis=1),
                       ((0, 0), (0, LANES - N_GROUPS - N_EXPERTS))).astype(BF16)
    b_router = jnp.pad(jnp.concatenate([b_router_group[0], b_router_expert[0]]),
                       (0, LANES - N_GROUPS - N_EXPERTS)).reshape(1, LANES)

    pos = jnp.concatenate([jnp.tile(jnp.arange(tp, dtype=jnp.int32), bp),
                           jnp.tile(PAST_LEN + jnp.arange(ts, dtype=jnp.int32), bs)])
    cos_t, sa_t, sb_t = _rope_tables(pos)
    seg_id = jnp.arange(tn, dtype=jnp.int32) // SWA_HD
    seg = (seg_id[:, None] == seg_id[None, :]).astype(BF16)
    rope_specs = [pl.BlockSpec((tp_rows, LANES), lambda j, i: (i, 0))] * 3
    row_vec = lambda width: pl.BlockSpec((1, width), lambda j, i: (0, 0))
    seg_spec = pl.BlockSpec((tn, tn), lambda j, i: (0, 0))

    xp2 = x_prompt.reshape(n_p, d)
    xs2 = x_sample.reshape(n_s, d)
    h = _rms_norm_two(xp2, xs2, norm_mix_g[0], tm)

    qkvg = _wmatmul(_ep_plain, h, w_in0, 0, c0, 0, [], [], F32, tp_rows, WIDE_COLS, "proj_gla")
    ga = _wmatmul(_ep_lowrank, h, w_in0, c0, LANES, 0, [], [], F32, tp_rows, LANES, "proj_gla_lowrank")
    q_gain = jnp.tile(swa_q_norm_g[0], tn // SWA_HD).reshape(1, tn)
    q_swa = _wmatmul(functools.partial(_ep_qknorm_rope, keep_from=None), h, w_in0, c0, sq_w, GLA_LOWRANK,
                     [q_gain, seg, cos_t, sa_t, sb_t], [row_vec(tn), seg_spec] + rope_specs,
                     BF16, tp_rows, tn, "proj_swa_q")
    k_gain = jnp.tile(swa_k_norm_g[0], tn // SWA_HD).reshape(1, tn)
    kv_swa = _wmatmul(functools.partial(_ep_qknorm_rope, keep_from=kv_w), h, w_in0, c0 + sq_w, 2 * kv_w,
                      GLA_LOWRANK, [k_gain, seg, cos_t, sa_t, sb_t], [row_vec(tn), seg_spec] + rope_specs,
                      F32, tp_rows, tn, "proj_swa_kv")
    q_mem = _wmatmul(functools.partial(_ep_headnorm, norm_tiles=None), h, w_in0, c0 + sq_w + 2 * kv_w, mem_w,
                     GLA_LOWRANK, [mem_q_norm_g[0].reshape(1, MEM_HD)], [row_vec(MEM_HD)],
                     BF16, tp_rows, MEM_HD, "proj_mem_q")
    gates = _wmatmul(_ep_sigmoid, h, w_gate[0], 0, 3 * d, 0, [b_gate[0].reshape(1, -1)],
                     [pl.BlockSpec((1, WIDE_COLS), lambda j, i: (0, j))], F32, tp_rows, WIDE_COLS, "proj_gates")

    mem_rows = bp * N_MEM
    hm = _rms_norm_rows(mem_prompt.reshape(mem_rows, d), norm_mem_g[0], BF16, N_MEM)
    mem_kv = _wmatmul(functools.partial(_ep_headnorm, norm_tiles=MEM_HEADS), hm, w_mem_kv[0], 0, 2 * mem_w, 0,
                      [mem_k_norm_g[0].reshape(1, MEM_HD)], [row_vec(MEM_HD)], F32, mem_rows, MEM_HD, "mem_kv")

    ba = b_a2[0].reshape(1, qk_w)
    gn = gla_norm_g[0].reshape(1, GLA_DV)
    s0_p = jnp.zeros((bp, GLA_HEADS, GLA_DK, GLA_DV), F32)
    o_gla_p, gla_state_p = _gla(qkvg, ga, w_a2_b, ba, gn, s0_p, bp, tp, 0, 256, CHUNK)
    o_gla_s, gla_state_s = _gla(qkvg, ga, w_a2_b, ba, gn, state_gla[0], bs, ts, n_p, ts, min(CHUNK, ts))

    sinks = swa_sinks[0]
    o_swa_p = _swa(sinks, q_swa, 0, kv_swa, kv_swa, (0, 1), kv_swa, kv_swa, (0, 1), 0,
                   bp, tp, WINDOW, False)
    ck = cache_swa_k[0].reshape(bs * keep_s, kv_w)
    cv = cache_swa_v[0].reshape(bs * keep_s, kv_w)
    o_swa_s = _swa(sinks, q_swa, n_p, ck, cv, (0, 0), kv_swa, kv_swa, (0, 1), n_p,
                   bs, ts, ts, True)

    o_mem_p = _mem_attn(q_mem, 0, 0, mem_kv, 0, mem_kv, MEM_HEADS, bp, tp, tm)
    o_mem_s = _mem_attn_cache(q_mem, n_p, cache_mem_k, cache_mem_v, bs, ts)

    merged = _merge((o_gla_p, o_swa_p, o_mem_p), (o_gla_s, o_swa_s, o_mem_s), w_branch_b, gates, tm, WIDE_COLS)

    x2, hf, eid, wts = _outproj_router(merged, w_out_b, xp2, xs2, norm_ffn_g[0], w_router, b_router, tm)

    n_assign = TOP_K * n
    n_tiles = n_assign // MOE_TILE + N_EXPERTS
    eid_kmajor = jnp.concatenate([eid[:, k] for k in range(TOP_K)])
    tile_expert, tile_ok, next_expert, tok_slots, dst_slots = _moe_schedule(eid_kmajor, n, n_tiles)
    yk = _moe(hf, tile_expert, tile_ok, next_expert, tok_slots, dst_slots, w_up[0], w_down[0],
              n_assign + 2 * MOE_TILE)
    y_p, y_s = _combine(x2, yk, wts, n_p, tm)

    y_p = y_p.reshape(bp, tp, d)
    y_s = y_s.reshape(bs, ts, d)
    kv_p = kv_swa[:n_p].reshape(bp, tp, 2, SWA_KV_HEADS, SWA_HD)[:, tp - WINDOW:]
    kv_s = kv_swa[n_p:].reshape(bs, ts, 2, SWA_KV_HEADS, SWA_HD)
    swk_s = jnp.concatenate([cache_swa_k[0], kv_s[:, :, 0]], axis=1)[:, ts:ts + keep_s]
    swv_s = jnp.concatenate([cache_swa_v[0], kv_s[:, :, 1]], axis=1)[:, ts:ts + keep_s]
    mk_p = mem_kv[:, :mem_w].reshape(bp, N_MEM, MEM_HEADS, MEM_HD)
    mv_p = mem_kv[:, mem_w:].reshape(bp, N_MEM, MEM_HEADS, MEM_HD)
    return (y_p, y_s, gla_state_p[None], kv_p[:, :, 0][None], kv_p[:, :, 1][None], mk_p[None], mv_p[None],
            gla_state_s[None], swk_s[None], swv_s[None])
```

```python
import functools

import jax
import jax.numpy as jnp
from jax import lax
from jax.experimental import pallas as pl
from jax.experimental.pallas import tpu as pltpu

F32 = jnp.float32
BF16 = jnp.bfloat16

D_MODEL = 2048
CHUNK = 64
EPS = 1e-6
PAST_LEN = 1024
GLA_HEADS = 4
GLA_DV = 512
GLA_DK = 256
GLA_LOWRANK = 16
GLA_NORMALIZER = 16.0
SWA_HD = 64
SWA_HEADS = 32
SWA_KV_HEADS = 4
SWA_GROUP = 8
WINDOW = 128
ROPE_DIM = 16
ROPE_THETA = 500000.0
N_MEM = 256
MEM_HEADS = 4
MEM_HD = 512
N_GROUPS = 8
EXPERTS_PER_GROUP = 8
N_EXPERTS = 64
TOP_K = 2
D_FF = 512

LANES = 128
VMEM_LIMIT = 56 * 1024 * 1024
ROW_TILE = 512
PROJ_ROWS = 1536
COL_TILE = 512
WIDE_COLS = 1024
ROUTER_SUB_ROWS = 256
MOE_TILE = 256
MOE_PHASES = 4
TOKEN_TILE_ROWS = D_MODEL // 2 // LANES
TOKEN_TILE_PITCH = 12
TOKEN_F32_ROWS = D_MODEL // LANES
TOKEN_F32_PITCH = 20
NEG_BIG = -1e30


def _params(n_axes):
    return pltpu.CompilerParams(dimension_semantics=("arbitrary",) * n_axes,
                                vmem_limit_bytes=VMEM_LIMIT)


def _norm_kernel(x_ref, g_ref, o_ref):
    x = x_ref[...]
    y = x * lax.rsqrt(jnp.mean(x * x, axis=-1, keepdims=True) + EPS)
    o_ref[...] = (y * g_ref[...]).astype(o_ref.dtype)


def _rms_norm_rows(x, g, out_dtype, tm):
    n, d = x.shape
    return pl.pallas_call(
        _norm_kernel,
        out_shape=jax.ShapeDtypeStruct((n, d), out_dtype),
        grid=(n // tm,),
        in_specs=[pl.BlockSpec((tm, d), lambda i: (i, 0)),
                  pl.BlockSpec((1, d), lambda i: (0, 0))],
        out_specs=pl.BlockSpec((tm, d), lambda i: (i, 0)),
        compiler_params=_params(1),
        name="rms_norm_rows",
    )(x, g.reshape(1, d))


def _norm2_kernel(xp_ref, xs_ref, g_ref, o_ref, *, p_tiles):
    def emit(x_ref):
        x = x_ref[...]
        y = x * lax.rsqrt(jnp.mean(x * x, axis=-1, keepdims=True) + EPS)
        o_ref[...] = (y * g_ref[...]).astype(o_ref.dtype)

    @pl.when(pl.program_id(0) < p_tiles)
    def _():
        emit(xp_ref)

    @pl.when(pl.program_id(0) >= p_tiles)
    def _():
        emit(xs_ref)


def _split_specs(p_tiles, tm, d):
    return (pl.BlockSpec((tm, d), lambda i: (jnp.minimum(i, p_tiles - 1), 0)),
            pl.BlockSpec((tm, d), lambda i: (jnp.maximum(i - p_tiles, 0), 0)))


def _rms_norm_two(xp, xs, g, tm):
    (n_p, d), n_s = xp.shape, xs.shape[0]
    p_tiles = n_p // tm
    return pl.pallas_call(
        functools.partial(_norm2_kernel, p_tiles=p_tiles),
        out_shape=jax.ShapeDtypeStruct((n_p + n_s, d), BF16),
        grid=((n_p + n_s) // tm,),
        in_specs=[*_split_specs(p_tiles, tm, d), pl.BlockSpec((1, d), lambda i: (0, 0))],
        out_specs=pl.BlockSpec((tm, d), lambda i: (i, 0)),
        compiler_params=_params(1),
        name="rms_norm_mix",
    )(xp, xs, g.reshape(1, d))


def _segment_rms(acc, seg_ref, inv_width):
    sq = acc * acc
    hi = sq.astype(BF16)
    lo = (sq - hi.astype(F32)).astype(BF16)
    ss = (jnp.dot(hi, seg_ref[...], preferred_element_type=F32)
          + jnp.dot(lo, seg_ref[...], preferred_element_type=F32))
    return ss * inv_width


def _rope(y, cos_ref, sa_ref, sb_ref):
    width = y.shape[1]
    reps = width // LANES
    c = jnp.concatenate([cos_ref[...]] * reps, axis=1)
    sa = jnp.concatenate([sa_ref[...]] * reps, axis=1)
    sb = jnp.concatenate([sb_ref[...]] * reps, axis=1)
    half = ROPE_DIM // 2
    return y * c + pltpu.roll(y, width - half, 1) * sa + pltpu.roll(y, half, 1) * sb


def _ep_plain(acc):
    return acc


def _ep_lowrank(acc):
    lane = lax.broadcasted_iota(jnp.int32, acc.shape, 1)
    return jnp.where(lane < GLA_LOWRANK, acc, 0.0)


def _ep_sigmoid(acc, b_ref):
    return jax.nn.sigmoid(acc + b_ref[...])


def _ep_qknorm_rope(acc, g_ref, seg_ref, cos_ref, sa_ref, sb_ref, *, keep_from):
    ms = _segment_rms(acc, seg_ref, 1.0 / SWA_HD)
    y = acc * lax.rsqrt(ms + EPS) * g_ref[...]
    y = _rope(y, cos_ref, sa_ref, sb_ref)
    if keep_from is not None:
        col = lax.broadcasted_iota(jnp.int32, y.shape, 1)
        y = jnp.where(col < keep_from, y, acc)
    return y


def _ep_headnorm(acc, g_ref, *, norm_tiles):
    y = acc * lax.rsqrt(jnp.mean(acc * acc, axis=-1, keepdims=True) + EPS) * g_ref[...]
    if norm_tiles is not None:
        y = jnp.where(pl.program_id(0) < norm_tiles, y, acc)
    return y


def _wmm_kernel(a_ref, w_ref, *rest, w_is_transposed, epilogue):
    extras, o_ref, wbf = rest[:-2], rest[-2], rest[-1]

    @pl.when(pl.program_id(1) == 0)
    def _():
        wbf[...] = w_ref[...].astype(BF16)

    if w_is_transposed:
        acc = lax.dot_general(a_ref[...], wbf[...], (((1,), (1,)), ((), ())), preferred_element_type=F32)
    else:
        acc = jnp.dot(a_ref[...], wbf[...], preferred_element_type=F32)
    o_ref[...] = epilogue(acc, *extras).astype(o_ref.dtype)


def _wmatmul(epilogue, a, w, col0, n_cols, w_is_transposed, extras, extra_specs, out_dtype, tm, tn, name):
    m, k = a.shape
    assert m % tm == 0 and n_cols % tn == 0
    if w_is_transposed:
        assert col0 % 8 == 0
        w_spec = pl.BlockSpec((pl.Element(tn), pl.Element(k)), lambda j, i: (pl.multiple_of(col0 + j * tn, 8), 0))
        w_tile = (tn, k)
    else:
        assert col0 % tn == 0
        w_spec = pl.BlockSpec((k, tn), lambda j, i: (0, col0 // tn + j))
        w_tile = (k, tn)
    kernel = functools.partial(_wmm_kernel, w_is_transposed=w_is_transposed, epilogue=epilogue)
    return pl.pallas_call(
        kernel,
        out_shape=jax.ShapeDtypeStruct((m, n_cols), out_dtype),
        grid=(n_cols // tn, m // tm),
        in_specs=[pl.BlockSpec((tm, k), lambda j, i: (i, 0)), w_spec] + list(extra_specs),
        out_specs=pl.BlockSpec((tm, tn), lambda j, i: (i, j)),
        scratch_shapes=[pltpu.VMEM(w_tile, BF16)],
        compiler_params=_params(2),
        name=name,
    )(a, w, *extras)


def _gla_kernel(q_ref, k_ref, v_ref, gg_ref, ga_ref, wa_ref, ba_ref, gn_ref, s0_ref,
                o_ref, sout_ref, s_scr, *, chunk, n_chunks):
    t = pl.program_id(1)

    @pl.when(t == 0)
    def _():
        s_scr[...] = s0_ref[0]

    row = lax.broadcasted_iota(jnp.int32, (chunk, chunk), 0)
    col = lax.broadcasted_iota(jnp.int32, (chunk, chunk), 1)
    causal = row >= col
    tril = causal.astype(F32)

    def one_chunk(ci, carry):
        rows = pl.ds(pl.multiple_of(ci * chunk, chunk), chunk)
        z = jnp.dot(ga_ref[rows, :].astype(BF16), wa_ref[...], preferred_element_type=F32) + ba_ref[...]
        log_a = (jnp.minimum(z, 0.0) - jnp.log1p(jnp.exp(-jnp.abs(z)))) * (1.0 / GLA_NORMALIZER)
        b_all = jnp.dot(tril, log_a, preferred_element_type=F32, precision=lax.Precision.HIGHEST)
        for h in range(GLA_HEADS):
            ks = slice(h * GLA_DK, (h + 1) * GLA_DK)
            vs = slice(h * GLA_DV, (h + 1) * GLA_DV)
            b = b_all[:, ks]
            b_last = b[chunk - 1:chunk, :]
            q = q_ref[rows, ks] * (GLA_DK ** -0.5)
            k = k_ref[rows, ks]
            vb = v_ref[rows, vs].astype(BF16)
            q_t = (q * jnp.exp(b)).astype(BF16)
            k_t = (k * jnp.exp(-b)).astype(BF16)
            k_u = (k * jnp.exp(b_last - b)).astype(BF16)
            att = lax.dot_general(q_t, k_t, (((1,), (1,)), ((), ())), preferred_element_type=F32)
            att = jnp.where(causal, att, 0.0)
            s_old = s_scr[h]
            o = (jnp.dot(q_t, s_old.astype(BF16), preferred_element_type=F32)
                 + jnp.dot(att.astype(BF16), vb, preferred_element_type=F32))
            decay = jnp.transpose(jnp.broadcast_to(jnp.exp(b_last), (LANES, GLA_DK)))
            decay = jnp.concatenate([decay] * (GLA_DV // LANES), axis=1)
            s_scr[h] = decay * s_old + lax.dot_general(k_u, vb, (((0,), (0,)), ((), ())),
                                                       preferred_element_type=F32)
            on = o * lax.rsqrt(jnp.mean(o * o, axis=-1, keepdims=True) + EPS) * gn_ref[...]
            gg = gg_ref[rows, vs]
            o_ref[rows, vs] = (on * (gg * jax.nn.sigmoid(gg))).astype(o_ref.dtype)
        return carry

    lax.fori_loop(0, n_chunks, one_chunk, 0)

    @pl.when(t == pl.num_programs(1) - 1)
    def _():
        sout_ref[0] = s_scr[...]


def _gla(qkvg, ga, wa, ba, gn, s0, batch, seq, row0, tb, chunk):
    nt = seq // tb
    base = row0 // tb
    qk_w = GLA_HEADS * GLA_DK
    v_w = GLA_HEADS * GLA_DV
    rows = lambda b, t: base + b * nt + t
    kernel = functools.partial(_gla_kernel, chunk=chunk, n_chunks=tb // chunk)
    return pl.pallas_call(
        kernel,
        out_shape=(jax.ShapeDtypeStruct((batch * seq, v_w), BF16),
                   jax.ShapeDtypeStruct((batch, GLA_HEADS, GLA_DK, GLA_DV), F32)),
        grid=(batch, nt),
        in_specs=[pl.BlockSpec((tb, qk_w), lambda b, t: (rows(b, t), 0)),
                  pl.BlockSpec((tb, qk_w), lambda b, t: (rows(b, t), 1)),
                  pl.BlockSpec((tb, v_w), lambda b, t: (rows(b, t), 1)),
                  pl.BlockSpec((tb, v_w), lambda b, t: (rows(b, t), 2)),
                  pl.BlockSpec((tb, LANES), lambda b, t: (rows(b, t), 0)),
                  pl.BlockSpec((LANES, qk_w), lambda b, t: (0, 0)),
                  pl.BlockSpec((1, qk_w), lambda b, t: (0, 0)),
                  pl.BlockSpec((1, GLA_DV), lambda b, t: (0, 0)),
                  pl.BlockSpec((1, GLA_HEADS, GLA_DK, GLA_DV), lambda b, t: (b, 0, 0, 0))],
        out_specs=(pl.BlockSpec((tb, v_w), lambda b, t: (b * nt + t, 0)),
                   pl.BlockSpec((1, GLA_HEADS, GLA_DK, GLA_DV), lambda b, t: (b, 0, 0, 0))),
        scratch_shapes=[pltpu.VMEM((GLA_HEADS, GLA_DK, GLA_DV), F32)],
        compiler_params=_params(2),
        name="gla_chunks",
    )(qkvg, qkvg, qkvg, qkvg, ga, wa, ba, gn, s0)


def _swa_kernel(sink_ref, q_ref, kp_ref, vp_ref, ko_ref, vo_ref, o_ref, *, tq, prev_from_cache):
    i = pl.program_id(1)
    nk = WINDOW + tq
    k_all = jnp.concatenate([kp_ref[...], ko_ref[...]], axis=0)
    v_all = jnp.concatenate([vp_ref[...], vo_ref[...]], axis=0)
    qc = lax.broadcasted_iota(jnp.int32, (tq, nk), 0) // CHUNK + WINDOW // CHUNK
    kcol = lax.broadcasted_iota(jnp.int32, (tq, nk), 1)
    kc = kcol // CHUNK
    valid = (kc <= qc) & (kc >= qc - WINDOW // CHUNK)
    if not prev_from_cache:
        valid = valid & ((kcol >= WINDOW) | (i > 0))
    lane = lax.broadcasted_iota(jnp.int32, (nk, LANES), 1)
    low = lane < SWA_HD
    low_q = lax.broadcasted_iota(jnp.int32, (tq, LANES), 1) < SWA_HD
    scale = SWA_HD ** -0.5
    for g in range(SWA_KV_HEADS):
        slab = slice((g // 2) * LANES, (g // 2 + 1) * LANES)
        k2 = k_all[:, slab]
        v2 = v_all[:, slab]
        k2r = pltpu.roll(k2, SWA_HD, 1)
        v2r = pltpu.roll(v2, SWA_HD, 1)
        if g % 2 == 0:
            k_lo, k_hi, v_lo, v_hi = k2, k2r, v2, v2r
        else:
            k_lo, k_hi, v_lo, v_hi = k2r, k2, v2r, v2
        zero = jnp.zeros_like(k2)
        one = jnp.ones_like(k2)
        km = (jnp.where(low, k_lo, zero).astype(BF16), jnp.where(low, zero, k_hi).astype(BF16))
        vm = (jnp.where(low, v_lo, one).astype(BF16), jnp.where(low, one, v_hi).astype(BF16))
        heads = [(j, half) for j in range(SWA_GROUP // 2) for half in range(2)]
        sinks = [sink_ref[g * SWA_GROUP + 2 * j + half] for j, half in heads]
        scores = []
        for j, half in heads:
            qs = q_ref[:, (g * 4 + j) * LANES:(g * 4 + j + 1) * LANES]
            s = lax.dot_general(qs, km[half], (((1,), (1,)), ((), ())), preferred_element_type=F32) * scale
            scores.append(jnp.where(valid, s, NEG_BIG))
        maxes = [jnp.maximum(jnp.max(s, axis=-1, keepdims=True), sk) for s, sk in zip(scores, sinks)]
        exps = [jnp.exp(s - m).astype(BF16) for s, m in zip(scores, maxes)]
        sink_terms = [jnp.exp(sk - m) for sk, m in zip(sinks, maxes)]
        for j in range(SWA_GROUP // 2):
            a_lo = jnp.dot(exps[2 * j], vm[0], preferred_element_type=F32)
            a_hi = jnp.dot(exps[2 * j + 1], vm[1], preferred_element_type=F32)
            num = jnp.where(low_q, a_lo, a_hi)
            den = pltpu.roll(jnp.where(low_q, a_hi, a_lo), SWA_HD, 1)
            den = den + jnp.where(low_q, sink_terms[2 * j], sink_terms[2 * j + 1])
            o_ref[:, (g * 4 + j) * LANES:(g * 4 + j + 1) * LANES] = (num / den).astype(o_ref.dtype)


def _swa(sinks, q, q_row0, k_prev, v_prev, prev_col, k_own, v_own, own_col, own_row0,
         batch, seq, tq, prev_from_cache):
    nt = seq // tq
    qb = q_row0 // tq
    ob = own_row0 // tq
    kv_w = SWA_KV_HEADS * SWA_HD
    if prev_from_cache:
        prev_map = lambda b, t, c: (b, c)
    else:
        per = seq // WINDOW
        prev_map = lambda b, t, c: (b * per + jnp.maximum(t * (tq // WINDOW) - 1, 0), c)
    kernel = functools.partial(_swa_kernel, tq=tq, prev_from_cache=prev_from_cache)
    return pl.pallas_call(
        kernel,
        out_shape=jax.ShapeDtypeStruct((batch * seq, SWA_HEADS * SWA_HD), BF16),
        grid=(batch, nt),
        in_specs=[pl.BlockSpec(memory_space=pltpu.SMEM),
                  pl.BlockSpec((tq, SWA_HEADS * SWA_HD), lambda b, t: (qb + b * nt + t, 0)),
                  pl.BlockSpec((WINDOW, kv_w), lambda b, t: prev_map(b, t, prev_col[0])),
                  pl.BlockSpec((WINDOW, kv_w), lambda b, t: prev_map(b, t, prev_col[1])),
                  pl.BlockSpec((tq, kv_w), lambda b, t: (ob + b * nt + t, own_col[0])),
                  pl.BlockSpec((tq, kv_w), lambda b, t: (ob + b * nt + t, own_col[1]))],
        out_specs=pl.BlockSpec((tq, SWA_HEADS * SWA_HD), lambda b, t: (b * nt + t, 0)),
        compiler_params=_params(2),
        name="swa_band",
    )(sinks, q, k_prev, v_prev, k_own, v_own)


def _mem_attn_head(q, k, v):
    s = lax.dot_general(q, k.astype(BF16), (((1,), (1,)), ((), ())),
                        preferred_element_type=F32) * (MEM_HD ** -0.5)
    m = jnp.max(s, axis=-1, keepdims=True)
    e = jnp.exp(s - m)
    p = (e / jnp.sum(e, axis=-1, keepdims=True)).astype(BF16)
    return jnp.dot(p, v.astype(BF16), preferred_element_type=F32)


def _mem_attn_kernel(q_ref, k_ref, v_ref, o_ref):
    o_ref[...] = _mem_attn_head(q_ref[...], k_ref[...], v_ref[...]).astype(o_ref.dtype)


def _mem_attn_cache_kernel(q_ref, k_ref, v_ref, o_ref):
    for h in range(MEM_HEADS):
        cols = slice(h * MEM_HD, (h + 1) * MEM_HD)
        o_ref[:, cols] = _mem_attn_head(q_ref[:, cols], k_ref[0, 0, :, h, :], v_ref[0, 0, :, h, :]).astype(o_ref.dtype)


def _mem_attn_cache(q, q_row0, cache_k, cache_v, batch, seq):
    qb = q_row0 // seq
    width = MEM_HEADS * MEM_HD
    cache_spec = pl.BlockSpec((1, 1, N_MEM, MEM_HEADS, MEM_HD), lambda b: (0, b, 0, 0, 0))
    return pl.pallas_call(
        _mem_attn_cache_kernel,
        out_shape=jax.ShapeDtypeStruct((batch * seq, width), BF16),
        grid=(batch,),
        in_specs=[pl.BlockSpec((seq, width), lambda b: (qb + b, 0)), cache_spec, cache_spec],
        out_specs=pl.BlockSpec((seq, width), lambda b: (b, 0)),
        compiler_params=_params(1),
        name="mem_attn_cache",
    )(q, cache_k, cache_v)


def _mem_attn(q, q_row0, q_col0, mk, mk_col0, mv, mv_col0, batch, seq, tq):
    nt = seq // tq
    qb = q_row0 // tq
    return pl.pallas_call(
        _mem_attn_kernel,
        out_shape=jax.ShapeDtypeStruct((batch * seq, MEM_HEADS * MEM_HD), BF16),
        grid=(batch, MEM_HEADS, nt),
        in_specs=[pl.BlockSpec((tq, MEM_HD), lambda b, h, t: (qb + b * nt + t, q_col0 + h)),
                  pl.BlockSpec((N_MEM, MEM_HD), lambda b, h, t: (b, mk_col0 + h)),
                  pl.BlockSpec((N_MEM, MEM_HD), lambda b, h, t: (b, mv_col0 + h))],
        out_specs=pl.BlockSpec((tq, MEM_HD), lambda b, h, t: (b * nt + t, h)),
        compiler_params=_params(3),
        name="mem_attn",
    )(q, mk, mv)


def _merge_kernel(a0p, a0s, a1p, a1s, a2p, a2s, w_ref, g0_ref, g1_ref, g2_ref, o_ref, *, p_tiles):
    def emit(a0_ref, a1_ref, a2_ref):
        acc = g0_ref[...] * jnp.dot(a0_ref[...], w_ref[0], preferred_element_type=F32)
        acc = acc + g1_ref[...] * jnp.dot(a1_ref[...], w_ref[1], preferred_element_type=F32)
        acc = acc + g2_ref[...] * jnp.dot(a2_ref[...], w_ref[2], preferred_element_type=F32)
        o_ref[...] = acc.astype(o_ref.dtype)

    @pl.when(pl.program_id(1) < p_tiles)
    def _():
        emit(a0p, a1p, a2p)

    @pl.when(pl.program_id(1) >= p_tiles)
    def _():
        emit(a0s, a1s, a2s)


def _merge(branches_p, branches_s, w_branch, gates, tm, tn):
    n_p, d = branches_p[0].shape
    n = n_p + branches_s[0].shape[0]
    nj = d // tn
    p_tiles = n_p // tm
    ap = pl.BlockSpec((tm, d), lambda j, i: (jnp.minimum(i, p_tiles - 1), 0))
    asp = pl.BlockSpec((tm, d), lambda j, i: (jnp.maximum(i - p_tiles, 0), 0))
    operands = [a for pair in zip(branches_p, branches_s) for a in pair]
    return pl.pallas_call(
        functools.partial(_merge_kernel, p_tiles=p_tiles),
        out_shape=jax.ShapeDtypeStruct((n, d), BF16),
        grid=(nj, n // tm),
        in_specs=[ap, asp, ap, asp, ap, asp,
                  pl.BlockSpec((3, d, tn), lambda j, i: (0, 0, j), pipeline_mode=pl.Buffered(1)),
                  pl.BlockSpec((tm, tn), lambda j, i: (i, j)),
                  pl.BlockSpec((tm, tn), lambda j, i: (i, nj + j)),
                  pl.BlockSpec((tm, tn), lambda j, i: (i, 2 * nj + j))],
        out_specs=pl.BlockSpec((tm, tn), lambda j, i: (i, j)),
        compiler_params=_params(2),
        name="branch_merge",
    )(*operands, w_branch, gates, gates, gates)


def _outproj_router_kernel(m_ref, w_ref, xp_ref, xs_ref, g_ref, wr_ref, br_ref,
                           x2_ref, hf_ref, eid_ref, wt_ref, *, p_tiles):
    is_prompt = pl.program_id(0) < p_tiles
    tm, d = x2_ref.shape
    for r0 in range(0, tm, ROUTER_SUB_ROWS):
        rows = slice(r0, r0 + ROUTER_SUB_ROWS)
        acc = jnp.dot(m_ref[rows, :], w_ref[...], preferred_element_type=F32)
        x = jnp.where(is_prompt, xp_ref[rows, :], xs_ref[rows, :]) + acc
        x2_ref[rows, :] = x
        hf = x * lax.rsqrt(jnp.mean(x * x, axis=-1, keepdims=True) + EPS) * g_ref[...]
        hb = hf.astype(BF16)
        bits = pltpu.bitcast(hb.astype(F32), jnp.uint32)
        packed = bits[:, d // 2:] | (bits[:, :d // 2] >> 16)
        for s in range(TOKEN_TILE_ROWS):
            hf_ref[pl.ds(r0 * TOKEN_TILE_ROWS + s, ROUTER_SUB_ROWS, stride=TOKEN_TILE_ROWS), :] = (
                packed[:, s * LANES:(s + 1) * LANES])
        logits = jnp.dot(hb, wr_ref[...], preferred_element_type=F32) + br_ref[...]
        eid, wts = _route(logits)
        eid_ref[rows, :] = eid
        wt_ref[rows, :] = wts


def _route(logits):
    lane = lax.broadcasted_iota(jnp.int32, logits.shape, 1).astype(F32)
    big = 1e6
    is_g = lane < N_GROUPS
    lg = jnp.where(is_g, logits, NEG_BIG)
    mg = jnp.max(lg, axis=-1, keepdims=True)
    gsel = jnp.min(jnp.where(is_g & (lg == mg), lane, big), axis=-1, keepdims=True)
    g_w = 1.0 / jnp.sum(jnp.where(is_g, jnp.exp(lg - mg), 0.0), axis=-1, keepdims=True)
    e_lo = N_GROUPS + gsel * EXPERTS_PER_GROUP
    in_grp = (lane >= e_lo) & (lane < e_lo + EXPERTS_PER_GROUP)
    le = jnp.where(in_grp, logits, NEG_BIG)
    me = jnp.max(le, axis=-1, keepdims=True)
    ee = jnp.where(in_grp, jnp.exp(le - me), 0.0)
    pe = ee / jnp.sum(ee, axis=-1, keepdims=True)
    pe = jnp.where(in_grp, pe, -1.0)
    p1 = jnp.max(pe, axis=-1, keepdims=True)
    i1 = jnp.min(jnp.where(pe == p1, lane, big), axis=-1, keepdims=True)
    pe2 = jnp.where(lane == i1, -1.0, pe)
    p2 = jnp.max(pe2, axis=-1, keepdims=True)
    i2 = jnp.min(jnp.where(pe2 == p2, lane, big), axis=-1, keepdims=True)
    tot = p1 + p2
    w1 = g_w * p1 / tot
    w2 = g_w * p2 / tot
    eid = jnp.where(lane == 0.0, i1 - N_GROUPS, jnp.where(lane == 1.0, i2 - N_GROUPS, 0.0))
    return eid.astype(jnp.int32), jnp.where(lane == 0.0, w1, jnp.where(lane == 1.0, w2, 0.0))


def _outproj_router(merged, w_out, xp, xs, g, wr, br, tm):
    n, d = merged.shape
    p_tiles = xp.shape[0] // tm
    const = lambda shape: pl.BlockSpec(shape, lambda i: (0, 0), pipeline_mode=pl.Buffered(1))
    row = lambda width: pl.BlockSpec((tm, width), lambda i: (i, 0))
    return pl.pallas_call(
        functools.partial(_outproj_router_kernel, p_tiles=p_tiles),
        out_shape=(jax.ShapeDtypeStruct((n, d), F32),
                   jax.ShapeDtypeStruct((n * TOKEN_TILE_ROWS, LANES), jnp.uint32),
                   jax.ShapeDtypeStruct((n, LANES), jnp.int32),
                   jax.ShapeDtypeStruct((n, LANES), F32)),
        grid=(n // tm,),
        in_specs=[row(d), const((d, d)), *_split_specs(p_tiles, tm, d),
                  const((1, d)), const((d, LANES)), const((1, LANES))],
        out_specs=(row(d), pl.BlockSpec((tm * TOKEN_TILE_ROWS, LANES), lambda i: (i, 0)),
                   row(LANES), row(LANES)),
        compiler_params=_params(1),
        name="outproj_router",
    )(merged, w_out, xp, xs, g.reshape(1, d), wr, br)


def _moe_kernel(te_ref, ok_ref, nxt_ref, tok_ref, tok_next_ref, dst_prev_ref, hf_hbm, wup_hbm, wdn_hbm, y_hbm,
                xg, yb, x_scr, h1_scr, act_scr, wup_f32, wdn_f32, wup_bf, wdn_bf, in_sem, out_sem, w_sem):
    t = pl.program_id(0)
    buf = t % 2
    valid = ok_ref[t] > 0
    valid_next = ok_ref[t + 1] > 0
    valid_prev = (t >= 1) & (ok_ref[jnp.maximum(t - 1, 0)] > 0)
    valid_prev2 = (t >= 2) & (ok_ref[jnp.maximum(t - 2, 0)] > 0)
    new_expert = (t == 0) | (te_ref[t] != te_ref[jnp.maximum(t - 1, 0)])

    def weight_copies(e):
        return (pltpu.make_async_copy(wup_hbm.at[e], wup_f32, w_sem.at[0]),
                pltpu.make_async_copy(wdn_hbm.at[e], wdn_f32, w_sem.at[1]))

    in_rows, in_pitch = TOKEN_TILE_ROWS, TOKEN_TILE_PITCH
    out_rows, out_pitch = TOKEN_F32_ROWS, TOKEN_F32_PITCH

    def row_in(row0, r, b):
        return pltpu.make_async_copy(hf_hbm.at[pl.ds(pl.multiple_of(row0, in_rows), in_rows), :],
                                     xg.at[b, pl.ds(r * in_pitch, in_rows), :], in_sem.at[b])

    def row_out(row0, r, b):
        return pltpu.make_async_copy(yb.at[b, pl.ds(r * out_pitch, out_rows), :],
                                     y_hbm.at[pl.ds(pl.multiple_of(row0, out_rows), out_rows), :], out_sem.at[b])

    def tile_in(b):
        return pltpu.make_async_copy(hf_hbm.at[pl.ds(0, MOE_TILE * in_rows), :],
                                     xg.at[b, pl.ds(0, MOE_TILE * in_rows), :], in_sem.at[b])

    def tile_out(b):
        return pltpu.make_async_copy(yb.at[b, pl.ds(0, MOE_TILE * out_rows), :],
                                     y_hbm.at[pl.ds(0, MOE_TILE * out_rows), :], out_sem.at[b])

    @pl.when(t == 0)
    def _():
        for c in weight_copies(te_ref[0]):
            c.start(priority=1)

        def first(r, c):
            row_in(tok_ref[0, 0, r], r, 0).start()
            return c
        lax.fori_loop(0, MOE_TILE, first, 0)
        yb[...] = jnp.zeros_like(yb)
        n_real = y_hbm.shape[0] - 2 * MOE_TILE * out_rows
        for b in range(2):
            spare = pltpu.make_async_copy(
                yb.at[b, pl.ds(0, MOE_TILE * out_rows), :],
                y_hbm.at[pl.ds(n_real + b * MOE_TILE * out_rows, MOE_TILE * out_rows), :], out_sem.at[b])
            spare.start()
            spare.wait()

    @pl.when(valid & new_expert)
    def _():
        for c in weight_copies(te_ref[t]):
            c.wait()
        wup_bf[...] = wup_f32[...].astype(BF16)
        wdn_bf[...] = wdn_f32[...].astype(BF16)

        @pl.when(nxt_ref[t] >= 0)
        def _():
            for c in weight_copies(nxt_ref[t]):
                c.start(priority=1)

    @pl.when(valid)
    def _():
        tile_in(buf).wait()
        half = x_scr.shape[1] // 2
        for s in range(in_rows):
            word = xg[buf, pl.ds(s, MOE_TILE, stride=in_pitch), :]
            x_scr[:, s * LANES:(s + 1) * LANES] = pltpu.bitcast(word << 16, F32).astype(BF16)
            x_scr[:, half + s * LANES:half + (s + 1) * LANES] = (
                pltpu.bitcast(word & jnp.uint32(0xFFFF0000), F32).astype(BF16))

    @pl.when(valid_prev2)
    def _():
        tile_out(buf).wait()

    burst = MOE_TILE // MOE_PHASES
    up_cols = (2 * D_FF) // MOE_PHASES
    dn_cols = x_scr.shape[1] // MOE_PHASES
    for c in range(MOE_PHASES):
        @pl.when(valid)
        def _():
            cols = slice(c * up_cols, (c + 1) * up_cols)
            h1_scr[:, cols] = jnp.dot(x_scr[...], wup_bf[:, cols], preferred_element_type=F32)

        @pl.when(valid_next)
        def _():
            for r in range(c * burst, (c + 1) * burst):
                row_in(tok_next_ref[0, 0, r], r, 1 - buf).start()

    @pl.when(valid)
    def _():
        gate = h1_scr[:, :D_FF]
        up = h1_scr[:, D_FF:]
        act_scr[...] = ((gate * jax.nn.sigmoid(gate)) * up).astype(BF16)

    for c in range(MOE_PHASES):
        @pl.when(valid)
        def _():
            ye = jnp.dot(act_scr[...], wdn_bf[:, c * dn_cols:(c + 1) * dn_cols], preferred_element_type=F32)
            for s in range(dn_cols // LANES):
                yb[buf, pl.ds(c * (dn_cols // LANES) + s, MOE_TILE, stride=out_pitch), :] = (
                    ye[:, s * LANES:(s + 1) * LANES])

        @pl.when(valid_prev)
        def _():
            for r in range(c * burst, (c + 1) * burst):
                row_out(dst_prev_ref[0, 0, r], r, 1 - buf).start(priority=r % 2)


def _moe(hf, tile_expert, tile_ok, next_expert, tok_slots, dst_slots, w_up, w_down, out_tokens):
    n_tiles = tile_expert.shape[0]
    d = w_up.shape[1]
    slot_spec = lambda off: pl.BlockSpec((1, 1, MOE_TILE),
                                         lambda t, te, ok, ne: (jnp.clip(t + off, 0, n_tiles - 1), 0, 0),
                                         memory_space=pltpu.SMEM)
    hbm = pl.BlockSpec(memory_space=pl.ANY)
    grid_spec = pltpu.PrefetchScalarGridSpec(
        num_scalar_prefetch=3,
        grid=(n_tiles,),
        in_specs=[slot_spec(0), slot_spec(1), slot_spec(-1), hbm, hbm, hbm],
        out_specs=hbm,
        scratch_shapes=[pltpu.VMEM((2, MOE_TILE * TOKEN_TILE_PITCH, LANES), jnp.uint32),
                        pltpu.VMEM((2, MOE_TILE * TOKEN_F32_PITCH, LANES), F32),
                        pltpu.VMEM((MOE_TILE, d), BF16),
                        pltpu.VMEM((MOE_TILE, 2 * D_FF), F32),
                        pltpu.VMEM((MOE_TILE, D_FF), BF16),
                        pltpu.VMEM((d, 2 * D_FF), F32),
                        pltpu.VMEM((D_FF, d), F32),
                        pltpu.VMEM((d, 2 * D_FF), BF16),
                        pltpu.VMEM((D_FF, d), BF16),
                        pltpu.SemaphoreType.DMA((2,)),
                        pltpu.SemaphoreType.DMA((2,)),
                        pltpu.SemaphoreType.DMA((2,))],
    )
    return pl.pallas_call(
        _moe_kernel,
        out_shape=jax.ShapeDtypeStruct((out_tokens * TOKEN_F32_ROWS, LANES), F32),
        grid_spec=grid_spec,
        compiler_params=_params(1),
        name="moe_experts",
    )(tile_expert, tile_ok, next_expert, tok_slots, tok_slots, dst_slots, hf, w_up, w_down)


def _combine_kernel(x_ref, y0_ref, y1_ref, w_ref, op_ref, os_ref, *, p_tiles):
    tm = x_ref.shape[0]
    w = w_ref[...]
    w0 = w[:, 0:1]
    w1 = w[:, 1:2]

    def emit(o_ref):
        for s in range(TOKEN_F32_ROWS):
            cols = slice(s * LANES, (s + 1) * LANES)
            rows = pl.ds(s, tm, stride=TOKEN_F32_ROWS)
            o_ref[:, cols] = x_ref[:, cols] + (y0_ref[rows, :] * w0 + y1_ref[rows, :] * w1)

    @pl.when(pl.program_id(0) < p_tiles)
    def _():
        emit(op_ref)

    @pl.when(pl.program_id(0) >= p_tiles)
    def _():
        emit(os_ref)


def _combine(x2, yk, wts, n_p, tm):
    n, d = x2.shape
    p_tiles = n_p // tm
    k1 = n // tm
    y_rows = tm * TOKEN_F32_ROWS
    return pl.pallas_call(
        functools.partial(_combine_kernel, p_tiles=p_tiles),
        out_shape=(jax.ShapeDtypeStruct((n_p, d), F32), jax.ShapeDtypeStruct((n - n_p, d), F32)),
        grid=(n // tm,),
        in_specs=[pl.BlockSpec((tm, d), lambda i: (i, 0)),
                  pl.BlockSpec((y_rows, LANES), lambda i: (i, 0)),
                  pl.BlockSpec((y_rows, LANES), lambda i: (k1 + i, 0)),
                  pl.BlockSpec((tm, LANES), lambda i: (i, 0))],
        out_specs=_split_specs(p_tiles, tm, d),
        compiler_params=_params(1),
        name="moe_combine",
    )(x2, yk, yk, wts)


def _rope_tables(pos):
    half = ROPE_DIM // 2
    inv = ROPE_THETA ** (-jnp.arange(half, dtype=F32) / half)
    ang = pos.astype(F32)[:, None] * inv[None, :]
    cos, sin = jnp.cos(ang), jnp.sin(ang)
    n = pos.shape[0]
    pad = jnp.zeros((n, SWA_HD - ROPE_DIM), F32)
    cos_h = jnp.concatenate([cos, cos, pad + 1.0], axis=1)
    sa_h = jnp.concatenate([-sin, jnp.zeros_like(sin), pad], axis=1)
    sb_h = jnp.concatenate([jnp.zeros_like(sin), sin, pad], axis=1)
    reps = LANES // SWA_HD
    return tuple(jnp.tile(a, (1, reps)) for a in (cos_h, sa_h, sb_h))


def _moe_schedule(eid, n_tok, n_tiles):
    a = eid.shape[0]
    order = jnp.argsort(eid, stable=True).astype(jnp.int32)
    counts = jnp.bincount(eid, length=N_EXPERTS).astype(jnp.int32)
    tiles_per = (counts + MOE_TILE - 1) // MOE_TILE
    tile_end = jnp.cumsum(tiles_per)
    tile_start = tile_end - tiles_per
    sorted_start = jnp.cumsum(counts) - counts
    tile_id = jnp.arange(n_tiles, dtype=jnp.int32)
    used = tile_id < tile_end[-1]
    te = jnp.minimum(jnp.searchsorted(tile_end, tile_id, side="right"), N_EXPERTS - 1).astype(jnp.int32)
    last_used_e = te[jnp.maximum(tile_end[-1] - 1, 0)]
    te = jnp.where(used, te, last_used_e)
    row_in_expert = (tile_id - tile_start[te]) * MOE_TILE
    rows_valid = jnp.where(used, jnp.clip(counts[te] - row_in_expert, 0, MOE_TILE), 0).astype(jnp.int32)
    r = jnp.arange(MOE_TILE, dtype=jnp.int32)[None, :]
    src = sorted_start[te][:, None] + row_in_expert[:, None] + r
    real = r < rows_valid[:, None]
    assign = order[jnp.clip(src, 0, a - 1)]
    tok = jnp.where(real, assign % n_tok, 0) * TOKEN_TILE_ROWS
    spare = a + (tile_id[:, None] % 2) * MOE_TILE + r
    dst = jnp.where(real, assign, spare) * TOKEN_F32_ROWS
    ok = jnp.concatenate([used.astype(jnp.int32), jnp.zeros((1,), jnp.int32)])
    first_other = jnp.searchsorted(jnp.where(used, te, N_EXPERTS), te, side="right")
    nxt = jnp.where(first_other < tile_end[-1], te[jnp.minimum(first_other, n_tiles - 1)], -1).astype(jnp.int32)
    shape = (n_tiles, 1, MOE_TILE)
    return te, ok, nxt, tok.reshape(shape).astype(jnp.int32), dst.reshape(shape).astype(jnp.int32)


def kernel(x_prompt, x_sample, state_gla, cache_swa_k, cache_swa_v, cache_mem_k, cache_mem_v,
           mem_prompt, norm_mix_g, w_in, w_a2, b_a2, gla_norm_g, swa_q_norm_g, swa_k_norm_g,
           swa_sinks, norm_mem_g, w_mem_kv, mem_q_norm_g, mem_k_norm_g, w_gate, b_gate,
           w_branch, w_out, norm_ffn_g, w_router_group, b_router_group, w_router_expert,
           b_router_expert, w_up, w_down):
    bp, tp, d = x_prompt.shape
    bs, ts, _ = x_sample.shape
    n_p, n_s = bp * tp, bs * ts
    n = n_p + n_s
    tm, tn = ROW_TILE, COL_TILE
    tp_rows = PROJ_ROWS if n % PROJ_ROWS == 0 else tm
    assert d == D_MODEL and n_p % tm == 0 and n_s % tm == 0 and w_in.shape[0] == 1
    keep_s = cache_swa_k.shape[2]
    assert keep_s == WINDOW and tp % WINDOW == 0

    qk_w = GLA_HEADS * GLA_DK
    v_w = GLA_HEADS * GLA_DV
    c0 = 2 * qk_w + 2 * v_w
    sq_w = SWA_HEADS * SWA_HD
    kv_w = SWA_KV_HEADS * SWA_HD
    mem_w = MEM_HEADS * MEM_HD
    w_in_t = jnp.transpose(w_in[0])
    w_branch_b = w_branch[0].astype(BF16)
    w_out_b = w_out[0].astype(BF16)
    w_a2_b = jnp.pad(w_a2[0], ((0, LANES - GLA_LOWRANK), (0, 0))).astype(BF16)
    w_router = jnp.pad(jnp.concatenate([w_router_group[0], w_router_expert[0]], axis=1),
                       ((0, 0), (0, LANES - N_GROUPS - N_EXPERTS))).astype(BF16)
    b_router = jnp.pad(jnp.concatenate([b_router_group[0], b_router_expert[0]]),
                       (0, LANES - N_GROUPS - N_EXPERTS)).reshape(1, LANES)

    pos = jnp.concatenate([jnp.tile(jnp.arange(tp, dtype=jnp.int32), bp),
                           jnp.tile(PAST_LEN + jnp.arange(ts, dtype=jnp.int32), bs)])
    cos_t, sa_t, sb_t = _rope_tables(pos)
    seg_id = jnp.arange(tn, dtype=jnp.int32) // SWA_HD
    seg = (seg_id[:, None] == seg_id[None, :]).astype(BF16)
    rope_specs = [pl.BlockSpec((tp_rows, LANES), lambda j, i: (i, 0))] * 3
    row_vec = lambda width: pl.BlockSpec((1, width), lambda j, i: (0, 0))
    seg_spec = pl.BlockSpec((tn, tn), lambda j, i: (0, 0))

    xp2 = x_prompt.reshape(n_p, d)
    xs2 = x_sample.reshape(n_s, d)
    h = _rms_norm_two(xp2, xs2, norm_mix_g[0], tm)

    c1 = c0 + GLA_LOWRANK
    c2 = c1 + sq_w
    c3 = c2 + 2 * kv_w
    qkvg = _wmatmul(_ep_plain, h, w_in_t, 0, c0, True, [], [], F32, tp_rows, WIDE_COLS, "proj_gla")
    ga = _wmatmul(_ep_lowrank, h, w_in_t, c0, LANES, True, [], [], F32, tp_rows, LANES, "proj_gla_lowrank")
    q_gain = jnp.tile(swa_q_norm_g[0], tn // SWA_HD).reshape(1, tn)
    q_swa = _wmatmul(functools.partial(_ep_qknorm_rope, keep_from=None), h, w_in_t, c1, sq_w, True,
                     [q_gain, seg, cos_t, sa_t, sb_t], [row_vec(tn), seg_spec] + rope_specs,
                     BF16, tp_rows, tn, "proj_swa_q")
    k_gain = jnp.tile(swa_k_norm_g[0], tn // SWA_HD).reshape(1, tn)
    kv_swa = _wmatmul(functools.partial(_ep_qknorm_rope, keep_from=kv_w), h, w_in_t, c2, 2 * kv_w, True,
                      [k_gain, seg, cos_t, sa_t, sb_t], [row_vec(tn), seg_spec] + rope_specs,
                      F32, tp_rows, tn, "proj_swa_kv")
    q_mem = _wmatmul(functools.partial(_ep_headnorm, norm_tiles=None), h, w_in_t, c3, mem_w, True,
                     [mem_q_norm_g[0].reshape(1, MEM_HD)], [row_vec(MEM_HD)],
                     BF16, tp_rows, MEM_HD, "proj_mem_q")
    gates = _wmatmul(_ep_sigmoid, h, w_gate[0], 0, 3 * d, False, [b_gate[0].reshape(1, -1)],
                     [pl.BlockSpec((1, WIDE_COLS), lambda j, i: (0, j))], F32, tp_rows, WIDE_COLS, "proj_gates")

    mem_rows = bp * N_MEM
    hm = _rms_norm_rows(mem_prompt.reshape(mem_rows, d), norm_mem_g[0], BF16, N_MEM)
    mem_kv = _wmatmul(functools.partial(_ep_headnorm, norm_tiles=MEM_HEADS), hm, w_mem_kv[0], 0, 2 * mem_w, False,
                      [mem_k_norm_g[0].reshape(1, MEM_HD)], [row_vec(MEM_HD)], F32, mem_rows, MEM_HD, "mem_kv")

    ba = b_a2[0].reshape(1, qk_w)
    gn = gla_norm_g[0].reshape(1, GLA_DV)
    s0_p = jnp.zeros((bp, GLA_HEADS, GLA_DK, GLA_DV), F32)
    o_gla_p, gla_state_p = _gla(qkvg, ga, w_a2_b, ba, gn, s0_p, bp, tp, 0, 256, CHUNK)
    o_gla_s, gla_state_s = _gla(qkvg, ga, w_a2_b, ba, gn, state_gla[0], bs, ts, n_p, ts, min(CHUNK, ts))

    sinks = swa_sinks[0]
    o_swa_p = _swa(sinks, q_swa, 0, kv_swa, kv_swa, (0, 1), kv_swa, kv_swa, (0, 1), 0,
                   bp, tp, WINDOW, False)
    ck = cache_swa_k[0].reshape(bs * keep_s, kv_w)
    cv = cache_swa_v[0].reshape(bs * keep_s, kv_w)
    o_swa_s = _swa(sinks, q_swa, n_p, ck, cv, (0, 0), kv_swa, kv_swa, (0, 1), n_p,
                   bs, ts, ts, True)

    o_mem_p = _mem_attn(q_mem, 0, 0, mem_kv, 0, mem_kv, MEM_HEADS, bp, tp, tm)
    o_mem_s = _mem_attn_cache(q_mem, n_p, cache_mem_k, cache_mem_v, bs, ts)

    merged = _merge((o_gla_p, o_swa_p, o_mem_p), (o_gla_s, o_swa_s, o_mem_s), w_branch_b, gates, tm, WIDE_COLS)

    x2, hf, eid, wts = _outproj_router(merged, w_out_b, xp2, xs2, norm_ffn_g[0], w_router, b_router, tm)

    n_assign = TOP_K * n
    n_tiles = n_assign // MOE_TILE + N_EXPERTS + 1
    eid_kmajor = jnp.concatenate([eid[:, k] for k in range(TOP_K)])
    tile_expert, tile_ok, next_expert, tok_slots, dst_slots = _moe_schedule(eid_kmajor, n, n_tiles)
    yk = _moe(hf, tile_expert, tile_ok, next_expert, tok_slots, dst_slots, w_up[0], w_down[0],
              n_assign + 2 * MOE_TILE)
    y_p, y_s = _combine(x2, yk, wts, n_p, tm)

    y_p = y_p.reshape(bp, tp, d)
    y_s = y_s.reshape(bs, ts, d)
    kv_p = kv_swa[:n_p].reshape(bp, tp, 2 * kv_w)[:, tp - WINDOW:].reshape(bp, WINDOW, 2, SWA_KV_HEADS, SWA_HD)
    kv_s = kv_swa[n_p:].reshape(bs, ts, 2, SWA_KV_HEADS, SWA_HD)
    swk_s = jnp.concatenate([cache_swa_k[0], kv_s[:, :, 0]], axis=1)[:, ts:ts + keep_s]
    swv_s = jnp.concatenate([cache_swa_v[0], kv_s[:, :, 1]], axis=1)[:, ts:ts + keep_s]
    mk_p = mem_kv[:, :mem_w].reshape(bp, N_MEM, MEM_HEADS, MEM_HD)
    mv_p = mem_kv[:, mem_w:].reshape(bp, N_MEM, MEM_HEADS, MEM_HD)
    return (y_p, y_s, gla_state_p[None], kv_p[:, :, 0][None], kv_p[:, :, 1][None], mk_p[None], mv_p[None],
            gla_state_s[None], swk_s[None], swv_s[None])
```

```python
import functools

import jax
import jax.numpy as jnp
from jax import lax
from jax.experimental import pallas as pl
from jax.experimental.pallas import tpu as pltpu

F32 = jnp.float32
BF16 = jnp.bfloat16

D_MODEL = 2048
CHUNK = 64
EPS = 1e-6
PAST_LEN = 1024
GLA_HEADS = 4
GLA_DV = 512
GLA_DK = 256
GLA_LOWRANK = 16
GLA_NORMALIZER = 16.0
SWA_HD = 64
SWA_HEADS = 32
SWA_KV_HEADS = 4
SWA_GROUP = 8
WINDOW = 128
ROPE_DIM = 16
ROPE_THETA = 500000.0
N_MEM = 256
MEM_HEADS = 4
MEM_HD = 512
N_GROUPS = 8
EXPERTS_PER_GROUP = 8
N_EXPERTS = 64
TOP_K = 2
D_FF = 512

LANES = 128
VMEM_LIMIT = 56 * 1024 * 1024
ROW_TILE = 512
PROJ_ROWS = 1536
COL_TILE = 512
WIDE_COLS = 1024
ROUTER_SUB_ROWS = 256
MOE_TILE = 256
MOE_ROW_GROUP = 32
TOKEN_TILE_ROWS = D_MODEL // 2 // LANES
TOKEN_TILE_PITCH = 12
TOKEN_F32_ROWS = D_MODEL // LANES
TOKEN_F32_PITCH = 20
NEG_BIG = -1e30


def _params(n_axes):
    return pltpu.CompilerParams(dimension_semantics=("arbitrary",) * n_axes,
                                vmem_limit_bytes=VMEM_LIMIT)


def _norm_kernel(x_ref, g_ref, o_ref):
    x = x_ref[...]
    y = x * lax.rsqrt(jnp.mean(x * x, axis=-1, keepdims=True) + EPS)
    o_ref[...] = (y * g_ref[...]).astype(o_ref.dtype)


def _rms_norm_rows(x, g, out_dtype, tm):
    n, d = x.shape
    return pl.pallas_call(
        _norm_kernel,
        out_shape=jax.ShapeDtypeStruct((n, d), out_dtype),
        grid=(n // tm,),
        in_specs=[pl.BlockSpec((tm, d), lambda i: (i, 0)),
                  pl.BlockSpec((1, d), lambda i: (0, 0))],
        out_specs=pl.BlockSpec((tm, d), lambda i: (i, 0)),
        compiler_params=_params(1),
        name="rms_norm_rows",
    )(x, g.reshape(1, d))


def _norm2_kernel(xp_ref, xs_ref, g_ref, o_ref, *, p_tiles):
    def emit(x_ref):
        x = x_ref[...]
        y = x * lax.rsqrt(jnp.mean(x * x, axis=-1, keepdims=True) + EPS)
        o_ref[...] = (y * g_ref[...]).astype(o_ref.dtype)

    @pl.when(pl.program_id(0) < p_tiles)
    def _():
        emit(xp_ref)

    @pl.when(pl.program_id(0) >= p_tiles)
    def _():
        emit(xs_ref)


def _split_specs(p_tiles, tm, d):
    return (pl.BlockSpec((tm, d), lambda i: (jnp.minimum(i, p_tiles - 1), 0)),
            pl.BlockSpec((tm, d), lambda i: (jnp.maximum(i - p_tiles, 0), 0)))


def _rms_norm_two(xp, xs, g, tm):
    (n_p, d), n_s = xp.shape, xs.shape[0]
    p_tiles = n_p // tm
    return pl.pallas_call(
        functools.partial(_norm2_kernel, p_tiles=p_tiles),
        out_shape=jax.ShapeDtypeStruct((n_p + n_s, d), BF16),
        grid=((n_p + n_s) // tm,),
        in_specs=[*_split_specs(p_tiles, tm, d), pl.BlockSpec((1, d), lambda i: (0, 0))],
        out_specs=pl.BlockSpec((tm, d), lambda i: (i, 0)),
        compiler_params=_params(1),
        name="rms_norm_mix",
    )(xp, xs, g.reshape(1, d))


def _segment_rms(acc, seg_ref, inv_width):
    ss = jnp.dot((acc * acc).astype(BF16), seg_ref[...], preferred_element_type=F32)
    return ss * inv_width


def _rope(y, cos_ref, sa_ref, sb_ref):
    width = y.shape[1]
    reps = width // LANES
    c = jnp.concatenate([cos_ref[...]] * reps, axis=1)
    sa = jnp.concatenate([sa_ref[...]] * reps, axis=1)
    sb = jnp.concatenate([sb_ref[...]] * reps, axis=1)
    half = ROPE_DIM // 2
    return y * c + pltpu.roll(y, width - half, 1) * sa + pltpu.roll(y, half, 1) * sb


def _ep_plain(acc):
    return acc


def _ep_lowrank(acc):
    lane = lax.broadcasted_iota(jnp.int32, acc.shape, 1)
    return jnp.where(lane < GLA_LOWRANK, acc, 0.0)


def _ep_sigmoid(acc, b_ref):
    return jax.nn.sigmoid(acc + b_ref[...])


def _ep_qknorm_rope(acc, g_ref, seg_ref, cos_ref, sa_ref, sb_ref, *, keep_from):
    ms = _segment_rms(acc, seg_ref, 1.0 / SWA_HD)
    y = acc * lax.rsqrt(ms + EPS) * g_ref[...]
    y = _rope(y, cos_ref, sa_ref, sb_ref)
    if keep_from is not None:
        col = lax.broadcasted_iota(jnp.int32, y.shape, 1)
        y = jnp.where(col < keep_from, y, acc)
    return y


def _ep_headnorm(acc, g_ref, *, norm_tiles):
    y = acc * lax.rsqrt(jnp.mean(acc * acc, axis=-1, keepdims=True) + EPS) * g_ref[...]
    if norm_tiles is not None:
        y = jnp.where(pl.program_id(0) < norm_tiles, y, acc)
    return y


def _wmm_kernel(a_ref, w_ref, *rest, w_is_transposed, epilogue):
    extras, o_ref, wbf = rest[:-2], rest[-2], rest[-1]

    @pl.when(pl.program_id(1) == 0)
    def _():
        wbf[...] = w_ref[...].astype(BF16)

    if w_is_transposed:
        acc = lax.dot_general(a_ref[...], wbf[...], (((1,), (1,)), ((), ())), preferred_element_type=F32)
    else:
        acc = jnp.dot(a_ref[...], wbf[...], preferred_element_type=F32)
    o_ref[...] = epilogue(acc, *extras).astype(o_ref.dtype)


def _wmatmul(epilogue, a, w, col0, n_cols, w_is_transposed, extras, extra_specs, out_dtype, tm, tn, name):
    m, k = a.shape
    assert m % tm == 0 and n_cols % tn == 0
    if w_is_transposed:
        assert col0 % 8 == 0
        w_spec = pl.BlockSpec((pl.Element(tn), pl.Element(k)), lambda j, i: (pl.multiple_of(col0 + j * tn, 8), 0))
        w_tile = (tn, k)
    else:
        assert col0 % tn == 0
        w_spec = pl.BlockSpec((k, tn), lambda j, i: (0, col0 // tn + j))
        w_tile = (k, tn)
    kernel = functools.partial(_wmm_kernel, w_is_transposed=w_is_transposed, epilogue=epilogue)
    return pl.pallas_call(
        kernel,
        out_shape=jax.ShapeDtypeStruct((m, n_cols), out_dtype),
        grid=(n_cols // tn, m // tm),
        in_specs=[pl.BlockSpec((tm, k), lambda j, i: (i, 0)), w_spec] + list(extra_specs),
        out_specs=pl.BlockSpec((tm, tn), lambda j, i: (i, j)),
        scratch_shapes=[pltpu.VMEM(w_tile, BF16)],
        compiler_params=_params(2),
        name=name,
    )(a, w, *extras)


def _gla_kernel(q_ref, k_ref, v_ref, gg_ref, ga_ref, wa_ref, ba_ref, gn_ref, s0_ref,
                o_ref, sout_ref, s_scr, *, chunk, n_chunks):
    t = pl.program_id(1)

    @pl.when(t == 0)
    def _():
        s_scr[...] = s0_ref[0]

    row = lax.broadcasted_iota(jnp.int32, (chunk, chunk), 0)
    col = lax.broadcasted_iota(jnp.int32, (chunk, chunk), 1)
    causal = row >= col
    tril = causal.astype(F32)

    def one_chunk(ci, carry):
        rows = pl.ds(pl.multiple_of(ci * chunk, chunk), chunk)
        z = jnp.dot(ga_ref[rows, :].astype(BF16), wa_ref[...], preferred_element_type=F32) + ba_ref[...]
        log_a = (jnp.minimum(z, 0.0) - jnp.log1p(jnp.exp(-jnp.abs(z)))) * (1.0 / GLA_NORMALIZER)
        b_all = jnp.dot(tril, log_a, preferred_element_type=F32, precision=lax.Precision.HIGHEST)
        for h in range(GLA_HEADS):
            ks = slice(h * GLA_DK, (h + 1) * GLA_DK)
            vs = slice(h * GLA_DV, (h + 1) * GLA_DV)
            b = b_all[:, ks]
            b_last = b[chunk - 1:chunk, :]
            q = q_ref[rows, ks] * (GLA_DK ** -0.5)
            k = k_ref[rows, ks]
            vb = v_ref[rows, vs].astype(BF16)
            q_t = (q * jnp.exp(b)).astype(BF16)
            k_t = (k * jnp.exp(-b)).astype(BF16)
            k_u = (k * jnp.exp(b_last - b)).astype(BF16)
            att = lax.dot_general(q_t, k_t, (((1,), (1,)), ((), ())), preferred_element_type=F32)
            att = jnp.where(causal, att, 0.0)
            s_old = s_scr[h]
            o = (jnp.dot(q_t, s_old.astype(BF16), preferred_element_type=F32)
                 + jnp.dot(att.astype(BF16), vb, preferred_element_type=F32))
            decay = jnp.transpose(jnp.broadcast_to(jnp.exp(b_last), (LANES, GLA_DK)))
            decay = jnp.concatenate([decay] * (GLA_DV // LANES), axis=1)
            s_scr[h] = decay * s_old + lax.dot_general(k_u, vb, (((0,), (0,)), ((), ())),
                                                       preferred_element_type=F32)
            on = o * lax.rsqrt(jnp.mean(o * o, axis=-1, keepdims=True) + EPS) * gn_ref[...]
            gg = gg_ref[rows, vs]
            o_ref[rows, vs] = (on * (gg * jax.nn.sigmoid(gg))).astype(o_ref.dtype)
        return carry

    lax.fori_loop(0, n_chunks, one_chunk, 0)

    @pl.when(t == pl.num_programs(1) - 1)
    def _():
        sout_ref[0] = s_scr[...]


def _gla(qkvg, ga, wa, ba, gn, s0, batch, seq, row0, tb, chunk):
    nt = seq // tb
    base = row0 // tb
    qk_w = GLA_HEADS * GLA_DK
    v_w = GLA_HEADS * GLA_DV
    rows = lambda b, t: base + b * nt + t
    kernel = functools.partial(_gla_kernel, chunk=chunk, n_chunks=tb // chunk)
    return pl.pallas_call(
        kernel,
        out_shape=(jax.ShapeDtypeStruct((batch * seq, v_w), BF16),
                   jax.ShapeDtypeStruct((batch, GLA_HEADS, GLA_DK, GLA_DV), F32)),
        grid=(batch, nt),
        in_specs=[pl.BlockSpec((tb, qk_w), lambda b, t: (rows(b, t), 0)),
                  pl.BlockSpec((tb, qk_w), lambda b, t: (rows(b, t), 1)),
                  pl.BlockSpec((tb, v_w), lambda b, t: (rows(b, t), 1)),
                  pl.BlockSpec((tb, v_w), lambda b, t: (rows(b, t), 2)),
                  pl.BlockSpec((tb, LANES), lambda b, t: (rows(b, t), 0)),
                  pl.BlockSpec((LANES, qk_w), lambda b, t: (0, 0)),
                  pl.BlockSpec((1, qk_w), lambda b, t: (0, 0)),
                  pl.BlockSpec((1, GLA_DV), lambda b, t: (0, 0)),
                  pl.BlockSpec((1, GLA_HEADS, GLA_DK, GLA_DV), lambda b, t: (b, 0, 0, 0))],
        out_specs=(pl.BlockSpec((tb, v_w), lambda b, t: (b * nt + t, 0)),
                   pl.BlockSpec((1, GLA_HEADS, GLA_DK, GLA_DV), lambda b, t: (b, 0, 0, 0))),
        scratch_shapes=[pltpu.VMEM((GLA_HEADS, GLA_DK, GLA_DV), F32)],
        compiler_params=_params(2),
        name="gla_chunks",
    )(qkvg, qkvg, qkvg, qkvg, ga, wa, ba, gn, s0)


def _swa_kernel(sink_ref, q_ref, kp_ref, vp_ref, ko_ref, vo_ref, o_ref, *, tq, prev_from_cache):
    i = pl.program_id(1)
    nk = WINDOW + tq
    k_all = jnp.concatenate([kp_ref[...], ko_ref[...]], axis=0)
    v_all = jnp.concatenate([vp_ref[...], vo_ref[...]], axis=0)
    qc = lax.broadcasted_iota(jnp.int32, (tq, nk), 0) // CHUNK + WINDOW // CHUNK
    kcol = lax.broadcasted_iota(jnp.int32, (tq, nk), 1)
    kc = kcol // CHUNK
    valid = (kc <= qc) & (kc >= qc - WINDOW // CHUNK)
    if not prev_from_cache:
        valid = valid & ((kcol >= WINDOW) | (i > 0))
    lane = lax.broadcasted_iota(jnp.int32, (nk, LANES), 1)
    low = lane < SWA_HD
    low_q = lax.broadcasted_iota(jnp.int32, (tq, LANES), 1) < SWA_HD
    scale = SWA_HD ** -0.5
    for g in range(SWA_KV_HEADS):
        slab = slice((g // 2) * LANES, (g // 2 + 1) * LANES)
        k2 = k_all[:, slab]
        v2 = v_all[:, slab]
        k2r = pltpu.roll(k2, SWA_HD, 1)
        v2r = pltpu.roll(v2, SWA_HD, 1)
        if g % 2 == 0:
            k_lo, k_hi, v_lo, v_hi = k2, k2r, v2, v2r
        else:
            k_lo, k_hi, v_lo, v_hi = k2r, k2, v2r, v2
        zero = jnp.zeros_like(k2)
        one = jnp.ones_like(k2)
        km = (jnp.where(low, k_lo, zero).astype(BF16), jnp.where(low, zero, k_hi).astype(BF16))
        vm = (jnp.where(low, v_lo, one).astype(BF16), jnp.where(low, one, v_hi).astype(BF16))
        heads = [(j, half) for j in range(SWA_GROUP // 2) for half in range(2)]
        sinks = [sink_ref[g * SWA_GROUP + 2 * j + half] for j, half in heads]
        scores = []
        for j, half in heads:
            qs = q_ref[:, (g * 4 + j) * LANES:(g * 4 + j + 1) * LANES]
            s = lax.dot_general(qs, km[half], (((1,), (1,)), ((), ())), preferred_element_type=F32) * scale
            scores.append(jnp.where(valid, s, NEG_BIG))
        maxes = [jnp.maximum(jnp.max(s, axis=-1, keepdims=True), sk) for s, sk in zip(scores, sinks)]
        exps = [jnp.exp(s - m).astype(BF16) for s, m in zip(scores, maxes)]
        sink_terms = [jnp.exp(sk - m) for sk, m in zip(sinks, maxes)]
        for j in range(SWA_GROUP // 2):
            a_lo = jnp.dot(exps[2 * j], vm[0], preferred_element_type=F32)
            a_hi = jnp.dot(exps[2 * j + 1], vm[1], preferred_element_type=F32)
            num = jnp.where(low_q, a_lo, a_hi)
            den = pltpu.roll(jnp.where(low_q, a_hi, a_lo), SWA_HD, 1)
            den = den + jnp.where(low_q, sink_terms[2 * j], sink_terms[2 * j + 1])
            o_ref[:, (g * 4 + j) * LANES:(g * 4 + j + 1) * LANES] = (num / den).astype(o_ref.dtype)


def _swa(sinks, q, q_row0, k_prev, v_prev, prev_col, k_own, v_own, own_col, own_row0,
         batch, seq, tq, prev_from_cache):
    nt = seq // tq
    qb = q_row0 // tq
    ob = own_row0 // tq
    kv_w = SWA_KV_HEADS * SWA_HD
    if prev_from_cache:
        prev_map = lambda b, t, c: (b, c)
    else:
        per = seq // WINDOW
        prev_map = lambda b, t, c: (b * per + jnp.maximum(t * (tq // WINDOW) - 1, 0), c)
    kernel = functools.partial(_swa_kernel, tq=tq, prev_from_cache=prev_from_cache)
    return pl.pallas_call(
        kernel,
        out_shape=jax.ShapeDtypeStruct((batch * seq, SWA_HEADS * SWA_HD), BF16),
        grid=(batch, nt),
        in_specs=[pl.BlockSpec(memory_space=pltpu.SMEM),
                  pl.BlockSpec((tq, SWA_HEADS * SWA_HD), lambda b, t: (qb + b * nt + t, 0)),
                  pl.BlockSpec((WINDOW, kv_w), lambda b, t: prev_map(b, t, prev_col[0])),
                  pl.BlockSpec((WINDOW, kv_w), lambda b, t: prev_map(b, t, prev_col[1])),
                  pl.BlockSpec((tq, kv_w), lambda b, t: (ob + b * nt + t, own_col[0])),
                  pl.BlockSpec((tq, kv_w), lambda b, t: (ob + b * nt + t, own_col[1]))],
        out_specs=pl.BlockSpec((tq, SWA_HEADS * SWA_HD), lambda b, t: (b * nt + t, 0)),
        compiler_params=_params(2),
        name="swa_band",
    )(sinks, q, k_prev, v_prev, k_own, v_own)


def _mem_attn_head(q, k, v):
    s = lax.dot_general(q, k.astype(BF16), (((1,), (1,)), ((), ())),
                        preferred_element_type=F32) * (MEM_HD ** -0.5)
    m = jnp.max(s, axis=-1, keepdims=True)
    e = jnp.exp(s - m)
    p = (e / jnp.sum(e, axis=-1, keepdims=True)).astype(BF16)
    return jnp.dot(p, v.astype(BF16), preferred_element_type=F32)


def _mem_attn_kernel(q_ref, kv_ref, o_ref, kv_bf):
    @pl.when(pl.program_id(1) == 0)
    def _():
        kv_bf[...] = kv_ref[...].astype(BF16)

    width = MEM_HEADS * MEM_HD
    cols = [slice(h * MEM_HD, (h + 1) * MEM_HD) for h in range(MEM_HEADS)]
    scores = [lax.dot_general(q_ref[:, c], kv_bf[:, c], (((1,), (1,)), ((), ())),
                              preferred_element_type=F32) * (MEM_HD ** -0.5) for c in cols]
    exps = [jnp.exp(s - jnp.max(s, axis=-1, keepdims=True)) for s in scores]
    probs = [(e / jnp.sum(e, axis=-1, keepdims=True)).astype(BF16) for e in exps]
    for c, p in zip(cols, probs):
        v = kv_bf[:, width + c.start:width + c.stop]
        o_ref[:, c] = jnp.dot(p, v, preferred_element_type=F32).astype(o_ref.dtype)


def _mem_attn_cache_kernel(q_ref, k_ref, v_ref, o_ref):
    for h in range(MEM_HEADS):
        cols = slice(h * MEM_HD, (h + 1) * MEM_HD)
        o_ref[:, cols] = _mem_attn_head(q_ref[:, cols], k_ref[0, 0, :, h, :], v_ref[0, 0, :, h, :]).astype(o_ref.dtype)


def _mem_attn_cache(q, q_row0, cache_k, cache_v, batch, seq):
    qb = q_row0 // seq
    width = MEM_HEADS * MEM_HD
    cache_spec = pl.BlockSpec((1, 1, N_MEM, MEM_HEADS, MEM_HD), lambda b: (0, b, 0, 0, 0))
    return pl.pallas_call(
        _mem_attn_cache_kernel,
        out_shape=jax.ShapeDtypeStruct((batch * seq, width), BF16),
        grid=(batch,),
        in_specs=[pl.BlockSpec((seq, width), lambda b: (qb + b, 0)), cache_spec, cache_spec],
        out_specs=pl.BlockSpec((seq, width), lambda b: (b, 0)),
        compiler_params=_params(1),
        name="mem_attn_cache",
    )(q, cache_k, cache_v)


def _mem_attn(q, mem_kv, batch, seq, tq):
    nt = seq // tq
    width = MEM_HEADS * MEM_HD
    return pl.pallas_call(
        _mem_attn_kernel,
        out_shape=jax.ShapeDtypeStruct((batch * seq, width), BF16),
        grid=(batch, nt),
        in_specs=[pl.BlockSpec((tq, width), lambda b, t: (b * nt + t, 0)),
                  pl.BlockSpec((N_MEM, 2 * width), lambda b, t: (b, 0))],
        out_specs=pl.BlockSpec((tq, width), lambda b, t: (b * nt + t, 0)),
        scratch_shapes=[pltpu.VMEM((N_MEM, 2 * width), BF16)],
        compiler_params=_params(2),
        name="mem_attn",
    )(q, mem_kv)


def _merge_kernel(a0p, a0s, a1p, a1s, a2p, a2s, w_ref, g0_ref, g1_ref, g2_ref, o_ref, *, p_tiles):
    def emit(a0_ref, a1_ref, a2_ref):
        acc = g0_ref[...] * jnp.dot(a0_ref[...], w_ref[0], preferred_element_type=F32)
        acc = acc + g1_ref[...] * jnp.dot(a1_ref[...], w_ref[1], preferred_element_type=F32)
        acc = acc + g2_ref[...] * jnp.dot(a2_ref[...], w_ref[2], preferred_element_type=F32)
        o_ref[...] = acc.astype(o_ref.dtype)

    @pl.when(pl.program_id(1) < p_tiles)
    def _():
        emit(a0p, a1p, a2p)

    @pl.when(pl.program_id(1) >= p_tiles)
    def _():
        emit(a0s, a1s, a2s)


def _merge(branches_p, branches_s, w_branch, gates, tm, tn):
    n_p, d = branches_p[0].shape
    n = n_p + branches_s[0].shape[0]
    nj = d // tn
    p_tiles = n_p // tm
    ap = pl.BlockSpec((tm, d), lambda j, i: (jnp.minimum(i, p_tiles - 1), 0))
    asp = pl.BlockSpec((tm, d), lambda j, i: (jnp.maximum(i - p_tiles, 0), 0))
    operands = [a for pair in zip(branches_p, branches_s) for a in pair]
    return pl.pallas_call(
        functools.partial(_merge_kernel, p_tiles=p_tiles),
        out_shape=jax.ShapeDtypeStruct((n, d), BF16),
        grid=(nj, n // tm),
        in_specs=[ap, asp, ap, asp, ap, asp,
                  pl.BlockSpec((3, d, tn), lambda j, i: (0, 0, j), pipeline_mode=pl.Buffered(1)),
                  pl.BlockSpec((tm, tn), lambda j, i: (i, j)),
                  pl.BlockSpec((tm, tn), lambda j, i: (i, nj + j)),
                  pl.BlockSpec((tm, tn), lambda j, i: (i, 2 * nj + j))],
        out_specs=pl.BlockSpec((tm, tn), lambda j, i: (i, j)),
        compiler_params=_params(2),
        name="branch_merge",
    )(*operands, w_branch, gates, gates, gates)


def _outproj_router_kernel(m_ref, w_ref, xp_ref, xs_ref, g_ref, wr_ref, br_ref,
                           x2_ref, hf_ref, eid_ref, wt_ref, *, p_tiles):
    is_prompt = pl.program_id(0) < p_tiles
    tm, d = x2_ref.shape
    for r0 in range(0, tm, ROUTER_SUB_ROWS):
        rows = slice(r0, r0 + ROUTER_SUB_ROWS)
        acc = jnp.dot(m_ref[rows, :], w_ref[...], preferred_element_type=F32)
        x = jnp.where(is_prompt, xp_ref[rows, :], xs_ref[rows, :]) + acc
        x2_ref[rows, :] = x
        hf = x * lax.rsqrt(jnp.mean(x * x, axis=-1, keepdims=True) + EPS) * g_ref[...]
        hb = hf.astype(BF16)
        bits = pltpu.bitcast(hb.astype(F32), jnp.uint32)
        packed = bits[:, d // 2:] | (bits[:, :d // 2] >> 16)
        for s in range(TOKEN_TILE_ROWS):
            hf_ref[pl.ds(r0 * TOKEN_TILE_ROWS + s, ROUTER_SUB_ROWS, stride=TOKEN_TILE_ROWS), :] = (
                packed[:, s * LANES:(s + 1) * LANES])
        logits = jnp.dot(hb, wr_ref[...], preferred_element_type=F32) + br_ref[...]
        eid, wts = _route(logits)
        eid_ref[rows, :] = eid
        wt_ref[rows, :] = wts


def _route(logits):
    lane = lax.broadcasted_iota(jnp.int32, logits.shape, 1).astype(F32)
    big = 1e6
    is_g = lane < N_GROUPS
    lg = jnp.where(is_g, logits, NEG_BIG)
    mg = jnp.max(lg, axis=-1, keepdims=True)
    gsel = jnp.min(jnp.where(is_g & (lg == mg), lane, big), axis=-1, keepdims=True)
    g_w = 1.0 / jnp.sum(jnp.where(is_g, jnp.exp(lg - mg), 0.0), axis=-1, keepdims=True)
    e_lo = N_GROUPS + gsel * EXPERTS_PER_GROUP
    in_grp = (lane >= e_lo) & (lane < e_lo + EXPERTS_PER_GROUP)
    le = jnp.where(in_grp, logits, NEG_BIG)
    me = jnp.max(le, axis=-1, keepdims=True)
    ee = jnp.where(in_grp, jnp.exp(le - me), 0.0)
    pe = ee / jnp.sum(ee, axis=-1, keepdims=True)
    pe = jnp.where(in_grp, pe, -1.0)
    p1 = jnp.max(pe, axis=-1, keepdims=True)
    i1 = jnp.min(jnp.where(pe == p1, lane, big), axis=-1, keepdims=True)
    pe2 = jnp.where(lane == i1, -1.0, pe)
    p2 = jnp.max(pe2, axis=-1, keepdims=True)
    i2 = jnp.min(jnp.where(pe2 == p2, lane, big), axis=-1, keepdims=True)
    tot = p1 + p2
    w1 = g_w * p1 / tot
    w2 = g_w * p2 / tot
    eid = jnp.where(lane == 0.0, i1 - N_GROUPS, jnp.where(lane == 1.0, i2 - N_GROUPS, 0.0))
    return eid.astype(jnp.int32), jnp.where(lane == 0.0, w1, jnp.where(lane == 1.0, w2, 0.0))


def _outproj_router(merged, w_out, xp, xs, g, wr, br, tm):
    n, d = merged.shape
    p_tiles = xp.shape[0] // tm
    const = lambda shape: pl.BlockSpec(shape, lambda i: (0, 0), pipeline_mode=pl.Buffered(1))
    row = lambda width: pl.BlockSpec((tm, width), lambda i: (i, 0))
    return pl.pallas_call(
        functools.partial(_outproj_router_kernel, p_tiles=p_tiles),
        out_shape=(jax.ShapeDtypeStruct((n, d), F32),
                   jax.ShapeDtypeStruct((n * TOKEN_TILE_ROWS, LANES), jnp.uint32),
                   jax.ShapeDtypeStruct((n, LANES), jnp.int32),
                   jax.ShapeDtypeStruct((n, LANES), F32)),
        grid=(n // tm,),
        in_specs=[row(d), const((d, d)), *_split_specs(p_tiles, tm, d),
                  const((1, d)), const((d, LANES)), const((1, LANES))],
        out_specs=(row(d), pl.BlockSpec((tm * TOKEN_TILE_ROWS, LANES), lambda i: (i, 0)),
                   row(LANES), row(LANES)),
        compiler_params=_params(1),
        name="outproj_router",
    )(merged, w_out, xp, xs, g.reshape(1, d), wr, br)


def _moe_kernel(te_ref, nv_ref, nxt_ref, tok_ref, tok_next_ref, dst_ref, hf_hbm, wup_hbm, wdn_hbm, y_hbm,
                xg, yb, wup_f32, wdn_f32, wup_bf, wdn_bf, in_sem, out_sem, w_sem):
    t = pl.program_id(0)
    buf = t % 2
    rows_now = nv_ref[t]
    rows_next = nv_ref[t + 1]
    rows_prev = nv_ref[jnp.maximum(t - 1, 0)]
    valid = rows_now > 0
    valid_next = rows_next > 0
    new_expert = (t == 0) | (te_ref[t] != te_ref[jnp.maximum(t - 1, 0)])

    def weight_copies(e):
        return (pltpu.make_async_copy(wup_hbm.at[e], wup_f32, w_sem.at[0]),
                pltpu.make_async_copy(wdn_hbm.at[e], wdn_f32, w_sem.at[1]))

    in_rows, in_pitch = TOKEN_TILE_ROWS, TOKEN_TILE_PITCH
    out_rows, out_pitch = TOKEN_F32_ROWS, TOKEN_F32_PITCH

    def row_in(row0, r, b):
        return pltpu.make_async_copy(hf_hbm.at[pl.ds(pl.multiple_of(row0, in_rows), in_rows), :],
                                     xg.at[b, pl.ds(r * in_pitch, in_rows), :], in_sem.at[b])

    def row_out(row0, r, b):
        return pltpu.make_async_copy(yb.at[b, pl.ds(r * out_pitch, out_rows), :],
                                     y_hbm.at[pl.ds(pl.multiple_of(row0, out_rows), out_rows), :], out_sem.at[b])

    group = MOE_ROW_GROUP
    n_groups = MOE_TILE // group

    def group_in(b):
        return pltpu.make_async_copy(hf_hbm.at[pl.ds(0, group * in_rows), :],
                                     xg.at[b, pl.ds(0, group * in_rows), :], in_sem.at[b])

    def group_out(b):
        return pltpu.make_async_copy(yb.at[b, pl.ds(0, group * out_rows), :],
                                     y_hbm.at[pl.ds(0, group * out_rows), :], out_sem.at[b])

    def per_started_group(rows, fn):
        for g in range(n_groups):
            pl.when(rows > g * group)(functools.partial(fn, g))

    @pl.when(t == 0)
    def _():
        for c in weight_copies(te_ref[0]):
            c.start(priority=1)
        xg[...] = jnp.zeros_like(xg)

        def first(r, c):
            row_in(tok_ref[0, 0, r], r, 0).start()
            return c
        lax.fori_loop(0, ((rows_now + group - 1) // group) * group, first, 0)
        yb[...] = jnp.zeros_like(yb)
        n_real = y_hbm.shape[0] - 2 * MOE_TILE * out_rows
        for b in range(2):
            spare = pltpu.make_async_copy(
                yb.at[b, pl.ds(0, MOE_TILE * out_rows), :],
                y_hbm.at[pl.ds(n_real + b * MOE_TILE * out_rows, MOE_TILE * out_rows), :], out_sem.at[b])
            spare.start()
            spare.wait()

    @pl.when(valid & new_expert)
    def _():
        for c in weight_copies(te_ref[t]):
            c.wait()
        wup_bf[...] = wup_f32[...].astype(BF16)
        wdn_bf[...] = wdn_f32[...].astype(BF16)

        @pl.when(nxt_ref[t] >= 0)
        def _():
            for c in weight_copies(nxt_ref[t]):
                c.start(priority=1)

    def gather_next(g):
        for r in range(g * group, (g + 1) * group):
            row_in(tok_next_ref[0, 0, r], r, 1 - buf).start()

    def scatter_now(g):
        for r in range(g * group, (g + 1) * group):
            row_out(dst_ref[0, 0, r], r, buf).start(priority=r % 2)

    per_started_group(rows_next, gather_next)

    @pl.when(valid)
    def _():
        per_started_group(rows_now, lambda g: group_in(buf).wait())
        lo, hi = [], []
        for s in range(in_rows):
            word = xg[buf, pl.ds(s, MOE_TILE, stride=in_pitch), :]
            lo.append(pltpu.bitcast(word << 16, F32).astype(BF16))
            hi.append(pltpu.bitcast(word & jnp.uint32(0xFFFF0000), F32).astype(BF16))
        x = jnp.concatenate(lo + hi, axis=1)
        h1 = jnp.dot(x, wup_bf[...], preferred_element_type=F32)
        gate = h1[:, :D_FF]
        up = h1[:, D_FF:]
        act = (gate * jax.nn.sigmoid(gate)) * up
        ye = jnp.dot(act.astype(BF16), wdn_bf[...], preferred_element_type=F32)
        for s in range(out_rows):
            yb[buf, pl.ds(s, MOE_TILE, stride=out_pitch), :] = ye[:, s * LANES:(s + 1) * LANES]
        per_started_group(rows_now, scatter_now)

        @pl.when(t > 0)
        def _():
            per_started_group(rows_prev, lambda g: group_out(1 - buf).wait())

        @pl.when(jnp.logical_not(valid_next))
        def _():
            per_started_group(rows_now, lambda g: group_out(buf).wait())


def _moe(hf, tile_expert, tile_rows, next_expert, tok_slots, dst_slots, w_up, w_down, out_tokens):
    n_tiles = tile_expert.shape[0]
    d = w_up.shape[1]
    slot_spec = lambda off: pl.BlockSpec((1, 1, MOE_TILE),
                                         lambda t, te, nv, ne: (jnp.minimum(t + off, n_tiles - 1), 0, 0),
                                         memory_space=pltpu.SMEM)
    hbm = pl.BlockSpec(memory_space=pl.ANY)
    grid_spec = pltpu.PrefetchScalarGridSpec(
        num_scalar_prefetch=3,
        grid=(n_tiles,),
        in_specs=[slot_spec(0), slot_spec(1), slot_spec(0), hbm, hbm, hbm],
        out_specs=hbm,
        scratch_shapes=[pltpu.VMEM((2, MOE_TILE * TOKEN_TILE_PITCH, LANES), jnp.uint32),
                        pltpu.VMEM((2, MOE_TILE * TOKEN_F32_PITCH, LANES), F32),
                        pltpu.VMEM((d, 2 * D_FF), F32),
                        pltpu.VMEM((D_FF, d), F32),
                        pltpu.VMEM((d, 2 * D_FF), BF16),
                        pltpu.VMEM((D_FF, d), BF16),
                        pltpu.SemaphoreType.DMA((2,)),
                        pltpu.SemaphoreType.DMA((2,)),
                        pltpu.SemaphoreType.DMA((2,))],
    )
    return pl.pallas_call(
        _moe_kernel,
        out_shape=jax.ShapeDtypeStruct((out_tokens * TOKEN_F32_ROWS, LANES), F32),
        grid_spec=grid_spec,
        compiler_params=_params(1),
        name="moe_experts",
    )(tile_expert, tile_rows, next_expert, tok_slots, tok_slots, dst_slots, hf, w_up, w_down)


def _combine_kernel(x_ref, y0_ref, y1_ref, w_ref, op_ref, os_ref, *, p_tiles):
    tm = x_ref.shape[0]
    w = w_ref[...]
    w0 = w[:, 0:1]
    w1 = w[:, 1:2]

    def emit(o_ref):
        for s in range(TOKEN_F32_ROWS):
            cols = slice(s * LANES, (s + 1) * LANES)
            rows = pl.ds(s, tm, stride=TOKEN_F32_ROWS)
            o_ref[:, cols] = x_ref[:, cols] + (y0_ref[rows, :] * w0 + y1_ref[rows, :] * w1)

    @pl.when(pl.program_id(0) < p_tiles)
    def _():
        emit(op_ref)

    @pl.when(pl.program_id(0) >= p_tiles)
    def _():
        emit(os_ref)


def _combine(x2, yk, wts, n_p, tm):
    n, d = x2.shape
    p_tiles = n_p // tm
    k1 = n // tm
    y_rows = tm * TOKEN_F32_ROWS
    return pl.pallas_call(
        functools.partial(_combine_kernel, p_tiles=p_tiles),
        out_shape=(jax.ShapeDtypeStruct((n_p, d), F32), jax.ShapeDtypeStruct((n - n_p, d), F32)),
        grid=(n // tm,),
        in_specs=[pl.BlockSpec((tm, d), lambda i: (i, 0)),
                  pl.BlockSpec((y_rows, LANES), lambda i: (i, 0)),
                  pl.BlockSpec((y_rows, LANES), lambda i: (k1 + i, 0)),
                  pl.BlockSpec((tm, LANES), lambda i: (i, 0))],
        out_specs=_split_specs(p_tiles, tm, d),
        compiler_params=_params(1),
        name="moe_combine",
    )(x2, yk, yk, wts)


def _rope_tables(pos):
    half = ROPE_DIM // 2
    inv = ROPE_THETA ** (-jnp.arange(half, dtype=F32) / half)
    ang = pos.astype(F32)[:, None] * inv[None, :]
    cos, sin = jnp.cos(ang), jnp.sin(ang)
    n = pos.shape[0]
    pad = jnp.zeros((n, SWA_HD - ROPE_DIM), F32)
    cos_h = jnp.concatenate([cos, cos, pad + 1.0], axis=1)
    sa_h = jnp.concatenate([-sin, jnp.zeros_like(sin), pad], axis=1)
    sb_h = jnp.concatenate([jnp.zeros_like(sin), sin, pad], axis=1)
    reps = LANES // SWA_HD
    return tuple(jnp.tile(a, (1, reps)) for a in (cos_h, sa_h, sb_h))


def _moe_schedule(eid, n_tok, n_tiles):
    a = eid.shape[0]
    order = jnp.argsort(eid, stable=True).astype(jnp.int32)
    counts = jnp.bincount(eid, length=N_EXPERTS).astype(jnp.int32)
    tiles_per = (counts + MOE_TILE - 1) // MOE_TILE
    tile_end = jnp.cumsum(tiles_per)
    tile_start = tile_end - tiles_per
    sorted_start = jnp.cumsum(counts) - counts
    tile_id = jnp.arange(n_tiles, dtype=jnp.int32)
    used = tile_id < tile_end[-1]
    te = jnp.minimum(jnp.sum(tile_end[None, :] <= tile_id[:, None], axis=1), N_EXPERTS - 1).astype(jnp.int32)
    last_used_e = te[jnp.maximum(tile_end[-1] - 1, 0)]
    te = jnp.where(used, te, last_used_e)
    row_in_expert = (tile_id - tile_start[te]) * MOE_TILE
    rows_valid = jnp.where(used, jnp.clip(counts[te] - row_in_expert, 0, MOE_TILE), 0).astype(jnp.int32)
    r = jnp.arange(MOE_TILE, dtype=jnp.int32)[None, :]
    src = sorted_start[te][:, None] + row_in_expert[:, None] + r
    real = r < rows_valid[:, None]
    assign = order[jnp.clip(src, 0, a - 1)]
    tok = jnp.where(real, assign % n_tok, 0) * TOKEN_TILE_ROWS
    spare = a + (tile_id[:, None] % 2) * MOE_TILE + r
    dst = jnp.where(real, assign, spare) * TOKEN_F32_ROWS
    tile_rows = jnp.concatenate([rows_valid, jnp.zeros((1,), jnp.int32)])
    e_id = jnp.arange(N_EXPERTS, dtype=jnp.int32)[None, :]
    later = (e_id > te[:, None]) & (counts[None, :] > 0)
    nxt = jnp.min(jnp.where(later, e_id, N_EXPERTS), axis=1)
    nxt = jnp.where(nxt < N_EXPERTS, nxt, -1).astype(jnp.int32)
    shape = (n_tiles, 1, MOE_TILE)
    return te, tile_rows, nxt, tok.reshape(shape).astype(jnp.int32), dst.reshape(shape).astype(jnp.int32)


def kernel(x_prompt, x_sample, state_gla, cache_swa_k, cache_swa_v, cache_mem_k, cache_mem_v,
           mem_prompt, norm_mix_g, w_in, w_a2, b_a2, gla_norm_g, swa_q_norm_g, swa_k_norm_g,
           swa_sinks, norm_mem_g, w_mem_kv, mem_q_norm_g, mem_k_norm_g, w_gate, b_gate,
           w_branch, w_out, norm_ffn_g, w_router_group, b_router_group, w_router_expert,
           b_router_expert, w_up, w_down):
    bp, tp, d = x_prompt.shape
    bs, ts, _ = x_sample.shape
    n_p, n_s = bp * tp, bs * ts
    n = n_p + n_s
    tm, tn = ROW_TILE, COL_TILE
    tp_rows = PROJ_ROWS if n % PROJ_ROWS == 0 else tm
    assert d == D_MODEL and n_p % tm == 0 and n_s % tm == 0 and w_in.shape[0] == 1
    keep_s = cache_swa_k.shape[2]
    assert keep_s == WINDOW and tp % WINDOW == 0

    qk_w = GLA_HEADS * GLA_DK
    v_w = GLA_HEADS * GLA_DV
    c0 = 2 * qk_w + 2 * v_w
    sq_w = SWA_HEADS * SWA_HD
    kv_w = SWA_KV_HEADS * SWA_HD
    mem_w = MEM_HEADS * MEM_HD
    w_in_t = jnp.transpose(w_in[0])
    w_branch_b = w_branch[0].astype(BF16)
    w_out_b = w_out[0].astype(BF16)
    w_a2_b = jnp.pad(w_a2[0], ((0, LANES - GLA_LOWRANK), (0, 0))).astype(BF16)
    w_router = jnp.pad(jnp.concatenate([w_router_group[0], w_router_expert[0]], axis=1),
                       ((0, 0), (0, LANES - N_GROUPS - N_EXPERTS))).astype(BF16)
    b_router = jnp.pad(jnp.concatenate([b_router_group[0], b_router_expert[0]]),
                       (0, LANES - N_GROUPS - N_EXPERTS)).reshape(1, LANES)

    pos = jnp.concatenate([jnp.tile(jnp.arange(tp, dtype=jnp.int32), bp),
                           jnp.tile(PAST_LEN + jnp.arange(ts, dtype=jnp.int32), bs)])
    cos_t, sa_t, sb_t = _rope_tables(pos)
    seg_id = jnp.arange(tn, dtype=jnp.int32) // SWA_HD
    seg = (seg_id[:, None] == seg_id[None, :]).astype(BF16)
    rope_specs = [pl.BlockSpec((tp_rows, LANES), lambda j, i: (i, 0))] * 3
    row_vec = lambda width: pl.BlockSpec((1, width), lambda j, i: (0, 0))
    seg_spec = pl.BlockSpec((tn, tn), lambda j, i: (0, 0))

    xp2 = x_prompt.reshape(n_p, d)
    xs2 = x_sample.reshape(n_s, d)
    h = _rms_norm_two(xp2, xs2, norm_mix_g[0], tm)

    c1 = c0 + GLA_LOWRANK
    c2 = c1 + sq_w
    c3 = c2 + 2 * kv_w
    qkvg = _wmatmul(_ep_plain, h, w_in_t, 0, c0, True, [], [], F32, tp_rows, WIDE_COLS, "proj_gla")
    ga = _wmatmul(_ep_lowrank, h, w_in_t, c0, LANES, True, [], [], F32, tp_rows, LANES, "proj_gla_lowrank")
    q_gain = jnp.tile(swa_q_norm_g[0], tn // SWA_HD).reshape(1, tn)
    q_swa = _wmatmul(functools.partial(_ep_qknorm_rope, keep_from=None), h, w_in_t, c1, sq_w, True,
                     [q_gain, seg, cos_t, sa_t, sb_t], [row_vec(tn), seg_spec] + rope_specs,
                     BF16, tp_rows, tn, "proj_swa_q")
    k_gain = jnp.tile(swa_k_norm_g[0], tn // SWA_HD).reshape(1, tn)
    kv_swa = _wmatmul(functools.partial(_ep_qknorm_rope, keep_from=kv_w), h, w_in_t, c2, 2 * kv_w, True,
                      [k_gain, seg, cos_t, sa_t, sb_t], [row_vec(tn), seg_spec] + rope_specs,
                      F32, tp_rows, tn, "proj_swa_kv")
    q_mem = _wmatmul(functools.partial(_ep_headnorm, norm_tiles=None), h, w_in_t, c3, mem_w, True,
                     [mem_q_norm_g[0].reshape(1, MEM_HD)], [row_vec(MEM_HD)],
                     BF16, tp_rows, MEM_HD, "proj_mem_q")
    gates = _wmatmul(_ep_sigmoid, h, w_gate[0], 0, 3 * d, False, [b_gate[0].reshape(1, -1)],
                     [pl.BlockSpec((1, WIDE_COLS), lambda j, i: (0, j))], F32, tp_rows, WIDE_COLS, "proj_gates")

    mem_rows = bp * N_MEM
    hm = _rms_norm_rows(mem_prompt.reshape(mem_rows, d), norm_mem_g[0], BF16, N_MEM)
    mem_kv = _wmatmul(functools.partial(_ep_headnorm, norm_tiles=MEM_HEADS), hm, w_mem_kv[0], 0, 2 * mem_w, False,
                      [mem_k_norm_g[0].reshape(1, MEM_HD)], [row_vec(MEM_HD)], F32, mem_rows, MEM_HD, "mem_kv")

    ba = b_a2[0].reshape(1, qk_w)
    gn = gla_norm_g[0].reshape(1, GLA_DV)
    s0_p = jnp.zeros((bp, GLA_HEADS, GLA_DK, GLA_DV), F32)
    o_gla_p, gla_state_p = _gla(qkvg, ga, w_a2_b, ba, gn, s0_p, bp, tp, 0, 256, CHUNK)
    o_gla_s, gla_state_s = _gla(qkvg, ga, w_a2_b, ba, gn, state_gla[0], bs, ts, n_p, ts, min(CHUNK, ts))

    sinks = swa_sinks[0]
    o_swa_p = _swa(sinks, q_swa, 0, kv_swa, kv_swa, (0, 1), kv_swa, kv_swa, (0, 1), 0,
                   bp, tp, WINDOW, False)
    ck = cache_swa_k[0].reshape(bs * keep_s, kv_w)
    cv = cache_swa_v[0].reshape(bs * keep_s, kv_w)
    o_swa_s = _swa(sinks, q_swa, n_p, ck, cv, (0, 0), kv_swa, kv_swa, (0, 1), n_p,
                   bs, ts, ts, True)

    o_mem_p = _mem_attn(q_mem, mem_kv, bp, tp, tm)
    o_mem_s = _mem_attn_cache(q_mem, n_p, cache_mem_k, cache_mem_v, bs, ts)

    merged = _merge((o_gla_p, o_swa_p, o_mem_p), (o_gla_s, o_swa_s, o_mem_s), w_branch_b, gates, tm, WIDE_COLS)

    x2, hf, eid, wts = _outproj_router(merged, w_out_b, xp2, xs2, norm_ffn_g[0], w_router, b_router, tm)

    n_assign = TOP_K * n
    n_tiles = n_assign // MOE_TILE + N_EXPERTS
    eid_kmajor = jnp.concatenate([eid[:, k] for k in range(TOP_K)])
    tile_expert, tile_rows, next_expert, tok_slots, dst_slots = _moe_schedule(eid_kmajor, n, n_tiles)
    yk = _moe(hf, tile_expert, tile_rows, next_expert, tok_slots, dst_slots, w_up[0], w_down[0],
              n_assign + 2 * MOE_TILE)
    y_p, y_s = _combine(x2, yk, wts, n_p, tm)

    y_p = y_p.reshape(bp, tp, d)
    y_s = y_s.reshape(bs, ts, d)
    kv_p = jnp.stack([kv_swa[(b + 1) * tp - WINDOW:(b + 1) * tp] for b in range(bp)])
    kv_p = kv_p.reshape(bp, WINDOW, 2, SWA_KV_HEADS, SWA_HD)
    kv_s = kv_swa[n_p:].reshape(bs, ts, 2, SWA_KV_HEADS, SWA_HD)
    swk_s = jnp.concatenate([cache_swa_k[0], kv_s[:, :, 0]], axis=1)[:, ts:ts + keep_s]
    swv_s = jnp.concatenate([cache_swa_v[0], kv_s[:, :, 1]], axis=1)[:, ts:ts + keep_s]
    mk_p = mem_kv[:, :mem_w].reshape(bp, N_MEM, MEM_HEADS, MEM_HD)
    mv_p = mem_kv[:, mem_w:].reshape(bp, N_MEM, MEM_HEADS, MEM_HD)
    return (y_p, y_s, gla_state_p[None], kv_p[:, :, 0][None], kv_p[:, :, 1][None], mk_p[None], mv_p[None],
            gla_state_s[None], swk_s[None], swv_s[None])
```

```python
import functools

import jax
import jax.numpy as jnp
from jax import lax
from jax.experimental import pallas as pl
from jax.experimental.pallas import tpu as pltpu

F32 = jnp.float32
BF16 = jnp.bfloat16

D_MODEL = 2048
CHUNK = 64
EPS = 1e-6
PAST_LEN = 1024
GLA_HEADS = 4
GLA_DV = 512
GLA_DK = 256
GLA_LOWRANK = 16
GLA_NORMALIZER = 16.0
SWA_HD = 64
SWA_HEADS = 32
SWA_KV_HEADS = 4
SWA_GROUP = 8
WINDOW = 128
ROPE_DIM = 16
ROPE_THETA = 500000.0
N_MEM = 256
MEM_HEADS = 4
MEM_HD = 512
N_GROUPS = 8
EXPERTS_PER_GROUP = 8
N_EXPERTS = 64
TOP_K = 2
D_FF = 512

LANES = 128
VMEM_LIMIT = 56 * 1024 * 1024
ROW_TILE = 512
PROJ_ROWS = 1536
COL_TILE = 512
WIDE_COLS = 1024
ROUTER_SUB_ROWS = 256
MOE_TILE = 256
MOE_ROW_GROUP = 32
TOKEN_TILE_ROWS = D_MODEL // 2 // LANES
TOKEN_TILE_PITCH = 12
TOKEN_F32_ROWS = D_MODEL // LANES
TOKEN_F32_PITCH = 20
NEG_BIG = -1e30


def _params(n_axes):
    return pltpu.CompilerParams(dimension_semantics=("arbitrary",) * n_axes,
                                vmem_limit_bytes=VMEM_LIMIT)


def _norm_kernel(x_ref, g_ref, o_ref):
    x = x_ref[...]
    y = x * lax.rsqrt(jnp.mean(x * x, axis=-1, keepdims=True) + EPS)
    o_ref[...] = (y * g_ref[...]).astype(o_ref.dtype)


def _rms_norm_rows(x, g, out_dtype, tm):
    n, d = x.shape
    return pl.pallas_call(
        _norm_kernel,
        out_shape=jax.ShapeDtypeStruct((n, d), out_dtype),
        grid=(n // tm,),
        in_specs=[pl.BlockSpec((tm, d), lambda i: (i, 0)),
                  pl.BlockSpec((1, d), lambda i: (0, 0))],
        out_specs=pl.BlockSpec((tm, d), lambda i: (i, 0)),
        compiler_params=_params(1),
        name="rms_norm_rows",
    )(x, g.reshape(1, d))


def _norm2_kernel(xp_ref, xs_ref, g_ref, o_ref, *, p_tiles):
    def emit(x_ref):
        x = x_ref[...]
        y = x * lax.rsqrt(jnp.mean(x * x, axis=-1, keepdims=True) + EPS)
        o_ref[...] = (y * g_ref[...]).astype(o_ref.dtype)

    @pl.when(pl.program_id(0) < p_tiles)
    def _():
        emit(xp_ref)

    @pl.when(pl.program_id(0) >= p_tiles)
    def _():
        emit(xs_ref)


def _split_specs(p_tiles, tm, d):
    return (pl.BlockSpec((tm, d), lambda i: (jnp.minimum(i, p_tiles - 1), 0)),
            pl.BlockSpec((tm, d), lambda i: (jnp.maximum(i - p_tiles, 0), 0)))


def _rms_norm_two(xp, xs, g, tm):
    (n_p, d), n_s = xp.shape, xs.shape[0]
    p_tiles = n_p // tm
    return pl.pallas_call(
        functools.partial(_norm2_kernel, p_tiles=p_tiles),
        out_shape=jax.ShapeDtypeStruct((n_p + n_s, d), BF16),
        grid=((n_p + n_s) // tm,),
        in_specs=[*_split_specs(p_tiles, tm, d), pl.BlockSpec((1, d), lambda i: (0, 0))],
        out_specs=pl.BlockSpec((tm, d), lambda i: (i, 0)),
        compiler_params=_params(1),
        name="rms_norm_mix",
    )(xp, xs, g.reshape(1, d))


def _segment_rms(acc, seg_ref, inv_width):
    ss = jnp.dot((acc * acc).astype(BF16), seg_ref[...], preferred_element_type=F32)
    return ss * inv_width


def _rope(y, rows, cos_ref, sa_ref, sb_ref):
    width = y.shape[1]
    reps = width // LANES
    c = jnp.concatenate([cos_ref[rows, :]] * reps, axis=1)
    sa = jnp.concatenate([sa_ref[rows, :]] * reps, axis=1)
    sb = jnp.concatenate([sb_ref[rows, :]] * reps, axis=1)
    half = ROPE_DIM // 2
    return y * c + pltpu.roll(y, width - half, 1) * sa + pltpu.roll(y, half, 1) * sb


def _ep_plain(acc, rows):
    return acc


def _ep_lowrank(acc, rows):
    lane = lax.broadcasted_iota(jnp.int32, acc.shape, 1)
    return jnp.where(lane < GLA_LOWRANK, acc, 0.0)


def _ep_sigmoid(acc, rows, b_ref):
    return jax.nn.sigmoid(acc + b_ref[...])


def _ep_qknorm_rope(acc, rows, g_ref, seg_ref, cos_ref, sa_ref, sb_ref, *, keep_from):
    ms = _segment_rms(acc, seg_ref, 1.0 / SWA_HD)
    y = acc * lax.rsqrt(ms + EPS) * g_ref[...]
    y = _rope(y, rows, cos_ref, sa_ref, sb_ref)
    if keep_from is not None:
        col = lax.broadcasted_iota(jnp.int32, y.shape, 1)
        y = jnp.where(col < keep_from, y, acc)
    return y


def _ep_headnorm(acc, rows, g_ref, *, norm_tiles):
    y = acc * lax.rsqrt(jnp.mean(acc * acc, axis=-1, keepdims=True) + EPS) * g_ref[...]
    if norm_tiles is not None:
        y = jnp.where(pl.program_id(0) < norm_tiles, y, acc)
    return y


def _wmm_kernel(a_ref, w_ref, *rest, w_is_transposed, sub_rows, epilogue):
    extras, o_ref, wbf = rest[:-2], rest[-2], rest[-1]

    @pl.when(pl.program_id(1) == 0)
    def _():
        wbf[...] = w_ref[...].astype(BF16)

    for r0 in range(0, a_ref.shape[0], sub_rows):
        rows = slice(r0, r0 + sub_rows)
        if w_is_transposed:
            acc = lax.dot_general(a_ref[rows, :], wbf[...], (((1,), (1,)), ((), ())), preferred_element_type=F32)
        else:
            acc = jnp.dot(a_ref[rows, :], wbf[...], preferred_element_type=F32)
        o_ref[rows, :] = epilogue(acc, rows, *extras).astype(o_ref.dtype)


def _wmatmul(epilogue, a, w, col0, n_cols, w_is_transposed, extras, extra_specs, out_dtype, tm, tn, name,
             sub_rows=None):
    m, k = a.shape
    sub_rows = tm if sub_rows is None else sub_rows
    assert m % tm == 0 and n_cols % tn == 0 and tm % sub_rows == 0
    if w_is_transposed:
        assert col0 % 8 == 0
        w_spec = pl.BlockSpec((pl.Element(tn), pl.Element(k)), lambda j, i: (pl.multiple_of(col0 + j * tn, 8), 0))
        w_tile = (tn, k)
    else:
        assert col0 % tn == 0
        w_spec = pl.BlockSpec((k, tn), lambda j, i: (0, col0 // tn + j))
        w_tile = (k, tn)
    kernel = functools.partial(_wmm_kernel, w_is_transposed=w_is_transposed, sub_rows=sub_rows, epilogue=epilogue)
    return pl.pallas_call(
        kernel,
        out_shape=jax.ShapeDtypeStruct((m, n_cols), out_dtype),
        grid=(n_cols // tn, m // tm),
        in_specs=[pl.BlockSpec((tm, k), lambda j, i: (i, 0)), w_spec] + list(extra_specs),
        out_specs=pl.BlockSpec((tm, tn), lambda j, i: (i, j)),
        scratch_shapes=[pltpu.VMEM(w_tile, BF16)],
        compiler_params=_params(2),
        name=name,
    )(a, w, *extras)


def _gla_kernel(q_ref, k_ref, v_ref, gg_ref, ga_ref, wa_ref, ba_ref, gn_ref, s0_ref,
                o_ref, sout_ref, s_scr, *, chunk, n_chunks):
    t = pl.program_id(1)

    @pl.when(t == 0)
    def _():
        s_scr[...] = s0_ref[0]

    row = lax.broadcasted_iota(jnp.int32, (chunk, chunk), 0)
    col = lax.broadcasted_iota(jnp.int32, (chunk, chunk), 1)
    causal = row >= col
    tril = causal.astype(F32)

    def one_chunk(ci, carry):
        rows = pl.ds(pl.multiple_of(ci * chunk, chunk), chunk)
        z = jnp.dot(ga_ref[rows, :].astype(BF16), wa_ref[...], preferred_element_type=F32) + ba_ref[...]
        log_a = (jnp.minimum(z, 0.0) - jnp.log1p(jnp.exp(-jnp.abs(z)))) * (1.0 / GLA_NORMALIZER)
        b_all = jnp.dot(tril, log_a, preferred_element_type=F32, precision=lax.Precision.HIGHEST)
        for h in range(GLA_HEADS):
            ks = slice(h * GLA_DK, (h + 1) * GLA_DK)
            vs = slice(h * GLA_DV, (h + 1) * GLA_DV)
            b = b_all[:, ks]
            b_last = b[chunk - 1:chunk, :]
            q = q_ref[rows, ks] * (GLA_DK ** -0.5)
            k = k_ref[rows, ks]
            vb = v_ref[rows, vs].astype(BF16)
            q_t = (q * jnp.exp(b)).astype(BF16)
            k_t = (k * jnp.exp(-b)).astype(BF16)
            k_u = (k * jnp.exp(b_last - b)).astype(BF16)
            att = lax.dot_general(q_t, k_t, (((1,), (1,)), ((), ())), preferred_element_type=F32)
            att = jnp.where(causal, att, 0.0)
            s_old = s_scr[h]
            o = (jnp.dot(q_t, s_old.astype(BF16), preferred_element_type=F32)
                 + jnp.dot(att.astype(BF16), vb, preferred_element_type=F32))
            decay = jnp.transpose(jnp.broadcast_to(jnp.exp(b_last), (LANES, GLA_DK)))
            decay = jnp.concatenate([decay] * (GLA_DV // LANES), axis=1)
            s_scr[h] = decay * s_old + lax.dot_general(k_u, vb, (((0,), (0,)), ((), ())),
                                                       preferred_element_type=F32)
            on = o * lax.rsqrt(jnp.mean(o * o, axis=-1, keepdims=True) + EPS) * gn_ref[...]
            gg = gg_ref[rows, vs]
            o_ref[rows, vs] = (on * (gg * jax.nn.sigmoid(gg))).astype(o_ref.dtype)
        return carry

    lax.fori_loop(0, n_chunks, one_chunk, 0)

    @pl.when(t == pl.num_programs(1) - 1)
    def _():
        sout_ref[0] = s_scr[...]


def _gla(qkvg, ga, wa, ba, gn, s0, batch, seq, row0, tb, chunk):
    nt = seq // tb
    base = row0 // tb
    qk_w = GLA_HEADS * GLA_DK
    v_w = GLA_HEADS * GLA_DV
    rows = lambda b, t: base + b * nt + t
    kernel = functools.partial(_gla_kernel, chunk=chunk, n_chunks=tb // chunk)
    return pl.pallas_call(
        kernel,
        out_shape=(jax.ShapeDtypeStruct((batch * seq, v_w), BF16),
                   jax.ShapeDtypeStruct((batch, GLA_HEADS, GLA_DK, GLA_DV), F32)),
        grid=(batch, nt),
        in_specs=[pl.BlockSpec((tb, qk_w), lambda b, t: (rows(b, t), 0)),
                  pl.BlockSpec((tb, qk_w), lambda b, t: (rows(b, t), 1)),
                  pl.BlockSpec((tb, v_w), lambda b, t: (rows(b, t), 1)),
                  pl.BlockSpec((tb, v_w), lambda b, t: (rows(b, t), 2)),
                  pl.BlockSpec((tb, LANES), lambda b, t: (rows(b, t), 0)),
                  pl.BlockSpec((LANES, qk_w), lambda b, t: (0, 0)),
                  pl.BlockSpec((1, qk_w), lambda b, t: (0, 0)),
                  pl.BlockSpec((1, GLA_DV), lambda b, t: (0, 0)),
                  pl.BlockSpec((1, GLA_HEADS, GLA_DK, GLA_DV), lambda b, t: (b, 0, 0, 0))],
        out_specs=(pl.BlockSpec((tb, v_w), lambda b, t: (b * nt + t, 0)),
                   pl.BlockSpec((1, GLA_HEADS, GLA_DK, GLA_DV), lambda b, t: (b, 0, 0, 0))),
        scratch_shapes=[pltpu.VMEM((GLA_HEADS, GLA_DK, GLA_DV), F32)],
        compiler_params=_params(2),
        name="gla_chunks",
    )(qkvg, qkvg, qkvg, qkvg, ga, wa, ba, gn, s0)


def _swa_kernel(sink_ref, q_ref, kp_ref, vp_ref, ko_ref, vo_ref, o_ref, *, tq, prev_from_cache):
    i = pl.program_id(1)
    nk = WINDOW + tq
    k_all = jnp.concatenate([kp_ref[...], ko_ref[...]], axis=0)
    v_all = jnp.concatenate([vp_ref[...], vo_ref[...]], axis=0)
    qc = lax.broadcasted_iota(jnp.int32, (tq, nk), 0) // CHUNK + WINDOW // CHUNK
    kcol = lax.broadcasted_iota(jnp.int32, (tq, nk), 1)
    kc = kcol // CHUNK
    valid = (kc <= qc) & (kc >= qc - WINDOW // CHUNK)
    if not prev_from_cache:
        valid = valid & ((kcol >= WINDOW) | (i > 0))
    lane = lax.broadcasted_iota(jnp.int32, (nk, LANES), 1)
    low = lane < SWA_HD
    low_q = lax.broadcasted_iota(jnp.int32, (tq, LANES), 1) < SWA_HD
    scale = SWA_HD ** -0.5
    for g in range(SWA_KV_HEADS):
        slab = slice((g // 2) * LANES, (g // 2 + 1) * LANES)
        k2 = k_all[:, slab]
        v2 = v_all[:, slab]
        k2r = pltpu.roll(k2, SWA_HD, 1)
        v2r = pltpu.roll(v2, SWA_HD, 1)
        if g % 2 == 0:
            k_lo, k_hi, v_lo, v_hi = k2, k2r, v2, v2r
        else:
            k_lo, k_hi, v_lo, v_hi = k2r, k2, v2r, v2
        zero = jnp.zeros_like(k2)
        one = jnp.ones_like(k2)
        km = (jnp.where(low, k_lo, zero).astype(BF16), jnp.where(low, zero, k_hi).astype(BF16))
        vm = (jnp.where(low, v_lo, one).astype(BF16), jnp.where(low, one, v_hi).astype(BF16))
        heads = [(j, half) for j in range(SWA_GROUP // 2) for half in range(2)]
        sinks = [sink_ref[g * SWA_GROUP + 2 * j + half] for j, half in heads]
        scores = []
        for j, half in heads:
            qs = q_ref[:, (g * 4 + j) * LANES:(g * 4 + j + 1) * LANES]
            s = lax.dot_general(qs, km[half], (((1,), (1,)), ((), ())), preferred_element_type=F32) * scale
            scores.append(jnp.where(valid, s, NEG_BIG))
        maxes = [jnp.maximum(jnp.max(s, axis=-1, keepdims=True), sk) for s, sk in zip(scores, sinks)]
        exps = [jnp.exp(s - m).astype(BF16) for s, m in zip(scores, maxes)]
        sink_terms = [jnp.exp(sk - m) for sk, m in zip(sinks, maxes)]
        for j in range(SWA_GROUP // 2):
            a_lo = jnp.dot(exps[2 * j], vm[0], preferred_element_type=F32)
            a_hi = jnp.dot(exps[2 * j + 1], vm[1], preferred_element_type=F32)
            num = jnp.where(low_q, a_lo, a_hi)
            den = pltpu.roll(jnp.where(low_q, a_hi, a_lo), SWA_HD, 1)
            den = den + jnp.where(low_q, sink_terms[2 * j], sink_terms[2 * j + 1])
            o_ref[:, (g * 4 + j) * LANES:(g * 4 + j + 1) * LANES] = (num / den).astype(o_ref.dtype)


def _swa(sinks, q, q_row0, k_prev, v_prev, prev_col, k_own, v_own, own_col, own_row0,
         batch, seq, tq, prev_from_cache):
    nt = seq // tq
    qb = q_row0 // tq
    ob = own_row0 // tq
    kv_w = SWA_KV_HEADS * SWA_HD
    if prev_from_cache:
        prev_map = lambda b, t, c: (b, c)
    else:
        per = seq // WINDOW
        prev_map = lambda b, t, c: (b * per + jnp.maximum(t * (tq // WINDOW) - 1, 0), c)
    kernel = functools.partial(_swa_kernel, tq=tq, prev_from_cache=prev_from_cache)
    return pl.pallas_call(
        kernel,
        out_shape=jax.ShapeDtypeStruct((batch * seq, SWA_HEADS * SWA_HD), BF16),
        grid=(batch, nt),
        in_specs=[pl.BlockSpec(memory_space=pltpu.SMEM),
                  pl.BlockSpec((tq, SWA_HEADS * SWA_HD), lambda b, t: (qb + b * nt + t, 0)),
                  pl.BlockSpec((WINDOW, kv_w), lambda b, t: prev_map(b, t, prev_col[0])),
                  pl.BlockSpec((WINDOW, kv_w), lambda b, t: prev_map(b, t, prev_col[1])),
                  pl.BlockSpec((tq, kv_w), lambda b, t: (ob + b * nt + t, own_col[0])),
                  pl.BlockSpec((tq, kv_w), lambda b, t: (ob + b * nt + t, own_col[1]))],
        out_specs=pl.BlockSpec((tq, SWA_HEADS * SWA_HD), lambda b, t: (b * nt + t, 0)),
        compiler_params=_params(2),
        name="swa_band",
    )(sinks, q, k_prev, v_prev, k_own, v_own)


def _mem_attn_head(q, k, v):
    s = lax.dot_general(q, k.astype(BF16), (((1,), (1,)), ((), ())),
                        preferred_element_type=F32) * (MEM_HD ** -0.5)
    m = jnp.max(s, axis=-1, keepdims=True)
    e = jnp.exp(s - m)
    p = (e / jnp.sum(e, axis=-1, keepdims=True)).astype(BF16)
    return jnp.dot(p, v.astype(BF16), preferred_element_type=F32)


def _mem_attn_kernel(q_ref, kv_ref, o_ref, kv_bf):
    @pl.when(pl.program_id(1) == 0)
    def _():
        kv_bf[...] = kv_ref[...].astype(BF16)

    width = MEM_HEADS * MEM_HD
    cols = [slice(h * MEM_HD, (h + 1) * MEM_HD) for h in range(MEM_HEADS)]
    scores = [lax.dot_general(q_ref[:, c], kv_bf[:, c], (((1,), (1,)), ((), ())),
                              preferred_element_type=F32) * (MEM_HD ** -0.5) for c in cols]
    exps = [jnp.exp(s - jnp.max(s, axis=-1, keepdims=True)) for s in scores]
    probs = [(e / jnp.sum(e, axis=-1, keepdims=True)).astype(BF16) for e in exps]
    for c, p in zip(cols, probs):
        v = kv_bf[:, width + c.start:width + c.stop]
        o_ref[:, c] = jnp.dot(p, v, preferred_element_type=F32).astype(o_ref.dtype)


def _mem_attn_cache_kernel(q_ref, k_ref, v_ref, o_ref):
    for h in range(MEM_HEADS):
        cols = slice(h * MEM_HD, (h + 1) * MEM_HD)
        o_ref[:, cols] = _mem_attn_head(q_ref[:, cols], k_ref[0, 0, :, h, :], v_ref[0, 0, :, h, :]).astype(o_ref.dtype)


def _mem_attn_cache(q, q_row0, cache_k, cache_v, batch, seq):
    qb = q_row0 // seq
    width = MEM_HEADS * MEM_HD
    cache_spec = pl.BlockSpec((1, 1, N_MEM, MEM_HEADS, MEM_HD), lambda b: (0, b, 0, 0, 0))
    return pl.pallas_call(
        _mem_attn_cache_kernel,
        out_shape=jax.ShapeDtypeStruct((batch * seq, width), BF16),
        grid=(batch,),
        in_specs=[pl.BlockSpec((seq, width), lambda b: (qb + b, 0)), cache_spec, cache_spec],
        out_specs=pl.BlockSpec((seq, width), lambda b: (b, 0)),
        compiler_params=_params(1),
        name="mem_attn_cache",
    )(q, cache_k, cache_v)


def _mem_attn(q, mem_kv, batch, seq, tq):
    nt = seq // tq
    width = MEM_HEADS * MEM_HD
    return pl.pallas_call(
        _mem_attn_kernel,
        out_shape=jax.ShapeDtypeStruct((batch * seq, width), BF16),
        grid=(batch, nt),
        in_specs=[pl.BlockSpec((tq, width), lambda b, t: (b * nt + t, 0)),
                  pl.BlockSpec((N_MEM, 2 * width), lambda b, t: (b, 0))],
        out_specs=pl.BlockSpec((tq, width), lambda b, t: (b * nt + t, 0)),
        scratch_shapes=[pltpu.VMEM((N_MEM, 2 * width), BF16)],
        compiler_params=_params(2),
        name="mem_attn",
    )(q, mem_kv)


def _merge_kernel(a0p, a0s, a1p, a1s, a2p, a2s, w_ref, g0_ref, g1_ref, g2_ref, o_ref, *, p_tiles):
    def emit(a0_ref, a1_ref, a2_ref):
        acc = g0_ref[...] * jnp.dot(a0_ref[...], w_ref[0], preferred_element_type=F32)
        acc = acc + g1_ref[...] * jnp.dot(a1_ref[...], w_ref[1], preferred_element_type=F32)
        acc = acc + g2_ref[...] * jnp.dot(a2_ref[...], w_ref[2], preferred_element_type=F32)
        o_ref[...] = acc.astype(o_ref.dtype)

    @pl.when(pl.program_id(1) < p_tiles)
    def _():
        emit(a0p, a1p, a2p)

    @pl.when(pl.program_id(1) >= p_tiles)
    def _():
        emit(a0s, a1s, a2s)


def _merge(branches_p, branches_s, w_branch, gates, tm, tn):
    n_p, d = branches_p[0].shape
    n = n_p + branches_s[0].shape[0]
    nj = d // tn
    p_tiles = n_p // tm
    ap = pl.BlockSpec((tm, d), lambda j, i: (jnp.minimum(i, p_tiles - 1), 0))
    asp = pl.BlockSpec((tm, d), lambda j, i: (jnp.maximum(i - p_tiles, 0), 0))
    operands = [a for pair in zip(branches_p, branches_s) for a in pair]
    return pl.pallas_call(
        functools.partial(_merge_kernel, p_tiles=p_tiles),
        out_shape=jax.ShapeDtypeStruct((n, d), BF16),
        grid=(nj, n // tm),
        in_specs=[ap, asp, ap, asp, ap, asp,
                  pl.BlockSpec((3, d, tn), lambda j, i: (0, 0, j), pipeline_mode=pl.Buffered(1)),
                  pl.BlockSpec((tm, tn), lambda j, i: (i, j)),
                  pl.BlockSpec((tm, tn), lambda j, i: (i, nj + j)),
                  pl.BlockSpec((tm, tn), lambda j, i: (i, 2 * nj + j))],
        out_specs=pl.BlockSpec((tm, tn), lambda j, i: (i, j)),
        compiler_params=_params(2),
        name="branch_merge",
    )(*operands, w_branch, gates, gates, gates)


def _outproj_router_kernel(m_ref, w_ref, xp_ref, xs_ref, g_ref, wr_ref, br_ref,
                           x2_ref, hf_ref, eid_ref, wt_ref, *, p_tiles):
    is_prompt = pl.program_id(0) < p_tiles
    tm, d = x2_ref.shape
    for r0 in range(0, tm, ROUTER_SUB_ROWS):
        rows = slice(r0, r0 + ROUTER_SUB_ROWS)
        acc = jnp.dot(m_ref[rows, :], w_ref[...], preferred_element_type=F32)
        x = jnp.where(is_prompt, xp_ref[rows, :], xs_ref[rows, :]) + acc
        x2_ref[rows, :] = x
        hf = x * lax.rsqrt(jnp.mean(x * x, axis=-1, keepdims=True) + EPS) * g_ref[...]
        hb = hf.astype(BF16)
        bits = pltpu.bitcast(hb.astype(F32), jnp.uint32)
        packed = bits[:, d // 2:] | (bits[:, :d // 2] >> 16)
        for s in range(TOKEN_TILE_ROWS):
            hf_ref[pl.ds(r0 * TOKEN_TILE_ROWS + s, ROUTER_SUB_ROWS, stride=TOKEN_TILE_ROWS), :] = (
                packed[:, s * LANES:(s + 1) * LANES])
        logits = jnp.dot(hb, wr_ref[...], preferred_element_type=F32) + br_ref[...]
        eid, wts = _route(logits)
        eid_ref[rows, :] = eid
        wt_ref[rows, :] = wts


def _route(logits):
    lane = lax.broadcasted_iota(jnp.int32, logits.shape, 1).astype(F32)
    big = 1e6
    is_g = lane < N_GROUPS
    lg = jnp.where(is_g, logits, NEG_BIG)
    mg = jnp.max(lg, axis=-1, keepdims=True)
    gsel = jnp.min(jnp.where(is_g & (lg == mg), lane, big), axis=-1, keepdims=True)
    g_w = 1.0 / jnp.sum(jnp.where(is_g, jnp.exp(lg - mg), 0.0), axis=-1, keepdims=True)
    e_lo = N_GROUPS + gsel * EXPERTS_PER_GROUP
    in_grp = (lane >= e_lo) & (lane < e_lo + EXPERTS_PER_GROUP)
    le = jnp.where(in_grp, logits, NEG_BIG)
    me = jnp.max(le, axis=-1, keepdims=True)
    ee = jnp.where(in_grp, jnp.exp(le - me), 0.0)
    pe = ee / jnp.sum(ee, axis=-1, keepdims=True)
    pe = jnp.where(in_grp, pe, -1.0)
    p1 = jnp.max(pe, axis=-1, keepdims=True)
    i1 = jnp.min(jnp.where(pe == p1, lane, big), axis=-1, keepdims=True)
    pe2 = jnp.where(lane == i1, -1.0, pe)
    p2 = jnp.max(pe2, axis=-1, keepdims=True)
    i2 = jnp.min(jnp.where(pe2 == p2, lane, big), axis=-1, keepdims=True)
    tot = p1 + p2
    w1 = g_w * p1 / tot
    w2 = g_w * p2 / tot
    eid = jnp.where(lane == 0.0, i1 - N_GROUPS, jnp.where(lane == 1.0, i2 - N_GROUPS, 0.0))
    return eid.astype(jnp.int32), jnp.where(lane == 0.0, w1, jnp.where(lane == 1.0, w2, 0.0))


def _outproj_router(merged, w_out, xp, xs, g, wr, br, tm):
    n, d = merged.shape
    p_tiles = xp.shape[0] // tm
    const = lambda shape: pl.BlockSpec(shape, lambda i: (0, 0), pipeline_mode=pl.Buffered(1))
    row = lambda width: pl.BlockSpec((tm, width), lambda i: (i, 0))
    return pl.pallas_call(
        functools.partial(_outproj_router_kernel, p_tiles=p_tiles),
        out_shape=(jax.ShapeDtypeStruct((n, d), F32),
                   jax.ShapeDtypeStruct((n * TOKEN_TILE_ROWS, LANES), jnp.uint32),
                   jax.ShapeDtypeStruct((n, LANES), jnp.int32),
                   jax.ShapeDtypeStruct((n, LANES), F32)),
        grid=(n // tm,),
        in_specs=[row(d), const((d, d)), *_split_specs(p_tiles, tm, d),
                  const((1, d)), const((d, LANES)), const((1, LANES))],
        out_specs=(row(d), pl.BlockSpec((tm * TOKEN_TILE_ROWS, LANES), lambda i: (i, 0)),
                   row(LANES), row(LANES)),
        compiler_params=_params(1),
        name="outproj_router",
    )(merged, w_out, xp, xs, g.reshape(1, d), wr, br)


def _moe_kernel(te_ref, nv_ref, nxt_ref, tok_ref, tok_next_ref, dst_ref, hf_hbm, wup_hbm, wdn_hbm, y_hbm,
                xg, yb, wup_f32, wdn_f32, wup_bf, wdn_bf, in_sem, out_sem, w_sem):
    t = pl.program_id(0)
    buf = t % 2
    rows_now = nv_ref[t]
    rows_next = nv_ref[t + 1]
    rows_prev = nv_ref[jnp.maximum(t - 1, 0)]
    valid = rows_now > 0
    valid_next = rows_next > 0
    new_expert = (t == 0) | (te_ref[t] != te_ref[jnp.maximum(t - 1, 0)])

    def weight_copies(e):
        return (pltpu.make_async_copy(wup_hbm.at[e], wup_f32, w_sem.at[0]),
                pltpu.make_async_copy(wdn_hbm.at[e], wdn_f32, w_sem.at[1]))

    in_rows, in_pitch = TOKEN_TILE_ROWS, TOKEN_TILE_PITCH
    out_rows, out_pitch = TOKEN_F32_ROWS, TOKEN_F32_PITCH

    def row_in(row0, r, b):
        return pltpu.make_async_copy(hf_hbm.at[pl.ds(pl.multiple_of(row0, in_rows), in_rows), :],
                                     xg.at[b, pl.ds(r * in_pitch, in_rows), :], in_sem.at[b])

    def row_out(row0, r, b):
        return pltpu.make_async_copy(yb.at[b, pl.ds(r * out_pitch, out_rows), :],
                                     y_hbm.at[pl.ds(pl.multiple_of(row0, out_rows), out_rows), :], out_sem.at[b])

    group = MOE_ROW_GROUP
    n_groups = MOE_TILE // group

    def group_in(b):
        return pltpu.make_async_copy(hf_hbm.at[pl.ds(0, group * in_rows), :],
                                     xg.at[b, pl.ds(0, group * in_rows), :], in_sem.at[b])

    def group_out(b):
        return pltpu.make_async_copy(yb.at[b, pl.ds(0, group * out_rows), :],
                                     y_hbm.at[pl.ds(0, group * out_rows), :], out_sem.at[b])

    def per_started_group(rows, fn):
        for g in range(n_groups):
            pl.when(rows > g * group)(functools.partial(fn, g))

    @pl.when(t == 0)
    def _():
        for c in weight_copies(te_ref[0]):
            c.start(priority=1)
        xg[...] = jnp.zeros_like(xg)

        def first(r, c):
            row_in(tok_ref[0, 0, r], r, 0).start()
            return c
        lax.fori_loop(0, ((rows_now + group - 1) // group) * group, first, 0)
        yb[...] = jnp.zeros_like(yb)
        n_real = y_hbm.shape[0] - 2 * MOE_TILE * out_rows
        for b in range(2):
            spare = pltpu.make_async_copy(
                yb.at[b, pl.ds(0, MOE_TILE * out_rows), :],
                y_hbm.at[pl.ds(n_real + b * MOE_TILE * out_rows, MOE_TILE * out_rows), :], out_sem.at[b])
            spare.start()
            spare.wait()

    @pl.when(valid & new_expert)
    def _():
        for c in weight_copies(te_ref[t]):
            c.wait()
        wup_bf[...] = wup_f32[...].astype(BF16)
        wdn_bf[...] = wdn_f32[...].astype(BF16)

        @pl.when(nxt_ref[t] >= 0)
        def _():
            for c in weight_copies(nxt_ref[t]):
                c.start(priority=1)

    def gather_next(g):
        for r in range(g * group, (g + 1) * group):
            row_in(tok_next_ref[0, 0, r], r, 1 - buf).start(priority=r % 2)

    def scatter_now(g):
        for r in range(g * group, (g + 1) * group):
            row_out(dst_ref[0, 0, r], r, buf).start(priority=r % 2)

    per_started_group(rows_next, gather_next)

    @pl.when(valid)
    def _():
        per_started_group(rows_now, lambda g: group_in(buf).wait())
        lo, hi = [], []
        for s in range(in_rows):
            word = xg[buf, pl.ds(s, MOE_TILE, stride=in_pitch), :]
            lo.append(pltpu.bitcast(word << 16, F32).astype(BF16))
            hi.append(pltpu.bitcast(word & jnp.uint32(0xFFFF0000), F32).astype(BF16))
        x = jnp.concatenate(lo + hi, axis=1)
        h1 = jnp.dot(x, wup_bf[...], preferred_element_type=F32)
        gate = h1[:, :D_FF]
        up = h1[:, D_FF:]
        act = (gate * jax.nn.sigmoid(gate)) * up
        ye = jnp.dot(act.astype(BF16), wdn_bf[...], preferred_element_type=F32)
        for s in range(out_rows):
            yb[buf, pl.ds(s, MOE_TILE, stride=out_pitch), :] = ye[:, s * LANES:(s + 1) * LANES]
        per_started_group(rows_now, scatter_now)

        @pl.when(t > 0)
        def _():
            per_started_group(rows_prev, lambda g: group_out(1 - buf).wait())

        @pl.when(jnp.logical_not(valid_next))
        def _():
            per_started_group(rows_now, lambda g: group_out(buf).wait())


def _moe(hf, tile_expert, tile_rows, next_expert, tok_slots, dst_slots, w_up, w_down, out_tokens):
    n_tiles = tile_expert.shape[0]
    d = w_up.shape[1]
    slot_spec = lambda off: pl.BlockSpec((1, 1, MOE_TILE),
                                         lambda t, te, nv, ne: (jnp.minimum(t + off, n_tiles - 1), 0, 0),
                                         memory_space=pltpu.SMEM)
    hbm = pl.BlockSpec(memory_space=pl.ANY)
    grid_spec = pltpu.PrefetchScalarGridSpec(
        num_scalar_prefetch=3,
        grid=(n_tiles,),
        in_specs=[slot_spec(0), slot_spec(1), slot_spec(0), hbm, hbm, hbm],
        out_specs=hbm,
        scratch_shapes=[pltpu.VMEM((2, MOE_TILE * TOKEN_TILE_PITCH, LANES), jnp.uint32),
                        pltpu.VMEM((2, MOE_TILE * TOKEN_F32_PITCH, LANES), F32),
                        pltpu.VMEM((d, 2 * D_FF), F32),
                        pltpu.VMEM((D_FF, d), F32),
                        pltpu.VMEM((d, 2 * D_FF), BF16),
                        pltpu.VMEM((D_FF, d), BF16),
                        pltpu.SemaphoreType.DMA((2,)),
                        pltpu.SemaphoreType.DMA((2,)),
                        pltpu.SemaphoreType.DMA((2,))],
    )
    return pl.pallas_call(
        _moe_kernel,
        out_shape=jax.ShapeDtypeStruct((out_tokens * TOKEN_F32_ROWS, LANES), F32),
        grid_spec=grid_spec,
        compiler_params=_params(1),
        name="moe_experts",
    )(tile_expert, tile_rows, next_expert, tok_slots, tok_slots, dst_slots, hf, w_up, w_down)


def _combine_kernel(x_ref, y0_ref, y1_ref, w_ref, op_ref, os_ref, *, p_tiles):
    tm = x_ref.shape[0]
    w = w_ref[...]
    w0 = w[:, 0:1]
    w1 = w[:, 1:2]

    def emit(o_ref):
        for s in range(TOKEN_F32_ROWS):
            cols = slice(s * LANES, (s + 1) * LANES)
            rows = pl.ds(s, tm, stride=TOKEN_F32_ROWS)
            o_ref[:, cols] = x_ref[:, cols] + (y0_ref[rows, :] * w0 + y1_ref[rows, :] * w1)

    @pl.when(pl.program_id(0) < p_tiles)
    def _():
        emit(op_ref)

    @pl.when(pl.program_id(0) >= p_tiles)
    def _():
        emit(os_ref)


def _combine(x2, yk, wts, n_p, tm):
    n, d = x2.shape
    p_tiles = n_p // tm
    k1 = n // tm
    y_rows = tm * TOKEN_F32_ROWS
    return pl.pallas_call(
        functools.partial(_combine_kernel, p_tiles=p_tiles),
        out_shape=(jax.ShapeDtypeStruct((n_p, d), F32), jax.ShapeDtypeStruct((n - n_p, d), F32)),
        grid=(n // tm,),
        in_specs=[pl.BlockSpec((tm, d), lambda i: (i, 0)),
                  pl.BlockSpec((y_rows, LANES), lambda i: (i, 0)),
                  pl.BlockSpec((y_rows, LANES), lambda i: (k1 + i, 0)),
                  pl.BlockSpec((tm, LANES), lambda i: (i, 0))],
        out_specs=_split_specs(p_tiles, tm, d),
        compiler_params=_params(1),
        name="moe_combine",
    )(x2, yk, yk, wts)


def _rope_tables(pos):
    half = ROPE_DIM // 2
    inv = ROPE_THETA ** (-jnp.arange(half, dtype=F32) / half)
    ang = pos.astype(F32)[:, None] * inv[None, :]
    cos, sin = jnp.cos(ang), jnp.sin(ang)
    n = pos.shape[0]
    pad = jnp.zeros((n, SWA_HD - ROPE_DIM), F32)
    cos_h = jnp.concatenate([cos, cos, pad + 1.0], axis=1)
    sa_h = jnp.concatenate([-sin, jnp.zeros_like(sin), pad], axis=1)
    sb_h = jnp.concatenate([jnp.zeros_like(sin), sin, pad], axis=1)
    reps = LANES // SWA_HD
    return tuple(jnp.tile(a, (1, reps)) for a in (cos_h, sa_h, sb_h))


def _moe_schedule(eid, n_tok, n_tiles):
    a = eid.shape[0]
    order = jnp.argsort(eid, stable=True).astype(jnp.int32)
    counts = jnp.bincount(eid, length=N_EXPERTS).astype(jnp.int32)
    tiles_per = (counts + MOE_TILE - 1) // MOE_TILE
    tile_end = jnp.cumsum(tiles_per)
    tile_start = tile_end - tiles_per
    sorted_start = jnp.cumsum(counts) - counts
    tile_id = jnp.arange(n_tiles, dtype=jnp.int32)
    used = tile_id < tile_end[-1]
    te = jnp.minimum(jnp.sum(tile_end[None, :] <= tile_id[:, None], axis=1), N_EXPERTS - 1).astype(jnp.int32)
    last_used_e = te[jnp.maximum(tile_end[-1] - 1, 0)]
    te = jnp.where(used, te, last_used_e)
    row_in_expert = (tile_id - tile_start[te]) * MOE_TILE
    rows_valid = jnp.where(used, jnp.clip(counts[te] - row_in_expert, 0, MOE_TILE), 0).astype(jnp.int32)
    r = jnp.arange(MOE_TILE, dtype=jnp.int32)[None, :]
    src = sorted_start[te][:, None] + row_in_expert[:, None] + r
    real = r < rows_valid[:, None]
    assign = order[jnp.clip(src, 0, a - 1)]
    tok = jnp.where(real, assign % n_tok, 0) * TOKEN_TILE_ROWS
    spare = a + (tile_id[:, None] % 2) * MOE_TILE + r
    dst = jnp.where(real, assign, spare) * TOKEN_F32_ROWS
    tile_rows = jnp.concatenate([rows_valid, jnp.zeros((1,), jnp.int32)])
    e_id = jnp.arange(N_EXPERTS, dtype=jnp.int32)[None, :]
    later = (e_id > te[:, None]) & (counts[None, :] > 0)
    nxt = jnp.min(jnp.where(later, e_id, N_EXPERTS), axis=1)
    nxt = jnp.where(nxt < N_EXPERTS, nxt, -1).astype(jnp.int32)
    shape = (n_tiles, 1, MOE_TILE)
    return te, tile_rows, nxt, tok.reshape(shape).astype(jnp.int32), dst.reshape(shape).astype(jnp.int32)


def kernel(x_prompt, x_sample, state_gla, cache_swa_k, cache_swa_v, cache_mem_k, cache_mem_v,
           mem_prompt, norm_mix_g, w_in, w_a2, b_a2, gla_norm_g, swa_q_norm_g, swa_k_norm_g,
           swa_sinks, norm_mem_g, w_mem_kv, mem_q_norm_g, mem_k_norm_g, w_gate, b_gate,
           w_branch, w_out, norm_ffn_g, w_router_group, b_router_group, w_router_expert,
           b_router_expert, w_up, w_down):
    bp, tp, d = x_prompt.shape
    bs, ts, _ = x_sample.shape
    n_p, n_s = bp * tp, bs * ts
    n = n_p + n_s
    tm, tn = ROW_TILE, COL_TILE
    tp_rows = PROJ_ROWS if n % PROJ_ROWS == 0 else tm
    heavy_sub = tp_rows // 2
    assert d == D_MODEL and n_p % tm == 0 and n_s % tm == 0 and w_in.shape[0] == 1
    keep_s = cache_swa_k.shape[2]
    assert keep_s == WINDOW and tp % WINDOW == 0

    qk_w = GLA_HEADS * GLA_DK
    v_w = GLA_HEADS * GLA_DV
    c0 = 2 * qk_w + 2 * v_w
    sq_w = SWA_HEADS * SWA_HD
    kv_w = SWA_KV_HEADS * SWA_HD
    mem_w = MEM_HEADS * MEM_HD
    w_in_t = jnp.transpose(w_in[0])
    w_branch_b = w_branch[0].astype(BF16)
    w_out_b = w_out[0].astype(BF16)
    w_a2_b = jnp.pad(w_a2[0], ((0, LANES - GLA_LOWRANK), (0, 0))).astype(BF16)
    w_router = jnp.pad(jnp.concatenate([w_router_group[0], w_router_expert[0]], axis=1),
                       ((0, 0), (0, LANES - N_GROUPS - N_EXPERTS))).astype(BF16)
    b_router = jnp.pad(jnp.concatenate([b_router_group[0], b_router_expert[0]]),
                       (0, LANES - N_GROUPS - N_EXPERTS)).reshape(1, LANES)

    pos = jnp.concatenate([jnp.tile(jnp.arange(tp, dtype=jnp.int32), bp),
                           jnp.tile(PAST_LEN + jnp.arange(ts, dtype=jnp.int32), bs)])
    cos_t, sa_t, sb_t = _rope_tables(pos)
    seg_id = jnp.arange(tn, dtype=jnp.int32) // SWA_HD
    seg = (seg_id[:, None] == seg_id[None, :]).astype(BF16)
    rope_specs = [pl.BlockSpec((tp_rows, LANES), lambda j, i: (i, 0))] * 3
    row_vec = lambda width: pl.BlockSpec((1, width), lambda j, i: (0, 0))
    seg_spec = pl.BlockSpec((tn, tn), lambda j, i: (0, 0))

    xp2 = x_prompt.reshape(n_p, d)
    xs2 = x_sample.reshape(n_s, d)
    h = _rms_norm_two(xp2, xs2, norm_mix_g[0], tm)

    c1 = c0 + GLA_LOWRANK
    c2 = c1 + sq_w
    c3 = c2 + 2 * kv_w
    qkvg = _wmatmul(_ep_plain, h, w_in_t, 0, c0, True, [], [], F32, tp_rows, WIDE_COLS, "proj_gla")
    ga = _wmatmul(_ep_lowrank, h, w_in_t, c0, LANES, True, [], [], F32, tp_rows, LANES, "proj_gla_lowrank")
    q_gain = jnp.tile(swa_q_norm_g[0], tn // SWA_HD).reshape(1, tn)
    q_swa = _wmatmul(functools.partial(_ep_qknorm_rope, keep_from=None), h, w_in_t, c1, sq_w, True,
                     [q_gain, seg, cos_t, sa_t, sb_t], [row_vec(tn), seg_spec] + rope_specs,
                     BF16, tp_rows, tn, "proj_swa_q", sub_rows=heavy_sub)
    k_gain = jnp.tile(swa_k_norm_g[0], tn // SWA_HD).reshape(1, tn)
    kv_swa = _wmatmul(functools.partial(_ep_qknorm_rope, keep_from=kv_w), h, w_in_t, c2, 2 * kv_w, True,
                      [k_gain, seg, cos_t, sa_t, sb_t], [row_vec(tn), seg_spec] + rope_specs,
                      F32, tp_rows, tn, "proj_swa_kv", sub_rows=heavy_sub)
    q_mem = _wmatmul(functools.partial(_ep_headnorm, norm_tiles=None), h, w_in_t, c3, mem_w, True,
                     [mem_q_norm_g[0].reshape(1, MEM_HD)], [row_vec(MEM_HD)],
                     BF16, tp_rows, MEM_HD, "proj_mem_q", sub_rows=heavy_sub)
    gates = _wmatmul(_ep_sigmoid, h, w_gate[0], 0, 3 * d, False, [b_gate[0].reshape(1, -1)],
                     [pl.BlockSpec((1, WIDE_COLS), lambda j, i: (0, j))], F32, tp_rows, WIDE_COLS, "proj_gates")

    mem_rows = bp * N_MEM
    hm = _rms_norm_rows(mem_prompt.reshape(mem_rows, d), norm_mem_g[0], BF16, N_MEM)
    mem_kv = _wmatmul(functools.partial(_ep_headnorm, norm_tiles=MEM_HEADS), hm, w_mem_kv[0], 0, 2 * mem_w, False,
                      [mem_k_norm_g[0].reshape(1, MEM_HD)], [row_vec(MEM_HD)], F32, mem_rows, MEM_HD, "mem_kv")

    ba = b_a2[0].reshape(1, qk_w)
    gn = gla_norm_g[0].reshape(1, GLA_DV)
    s0_p = jnp.zeros((bp, GLA_HEADS, GLA_DK, GLA_DV), F32)
    o_gla_p, gla_state_p = _gla(qkvg, ga, w_a2_b, ba, gn, s0_p, bp, tp, 0, 256, CHUNK)
    o_gla_s, gla_state_s = _gla(qkvg, ga, w_a2_b, ba, gn, state_gla[0], bs, ts, n_p, ts, min(CHUNK, ts))

    sinks = swa_sinks[0]
    o_swa_p = _swa(sinks, q_swa, 0, kv_swa, kv_swa, (0, 1), kv_swa, kv_swa, (0, 1), 0,
                   bp, tp, WINDOW, False)
    ck = cache_swa_k[0].reshape(bs * keep_s, kv_w)
    cv = cache_swa_v[0].reshape(bs * keep_s, kv_w)
    o_swa_s = _swa(sinks, q_swa, n_p, ck, cv, (0, 0), kv_swa, kv_swa, (0, 1), n_p,
                   bs, ts, ts, True)

    o_mem_p = _mem_attn(q_mem, mem_kv, bp, tp, tm)
    o_mem_s = _mem_attn_cache(q_mem, n_p, cache_mem_k, cache_mem_v, bs, ts)

    merged = _merge((o_gla_p, o_swa_p, o_mem_p), (o_gla_s, o_swa_s, o_mem_s), w_branch_b, gates, tm, WIDE_COLS)

    x2, hf, eid, wts = _outproj_router(merged, w_out_b, xp2, xs2, norm_ffn_g[0], w_router, b_router, tm)

    n_assign = TOP_K * n
    n_tiles = n_assign // MOE_TILE + N_EXPERTS
    eid_kmajor = jnp.concatenate([eid[:, k] for k in range(TOP_K)])
    tile_expert, tile_rows, next_expert, tok_slots, dst_slots = _moe_schedule(eid_kmajor, n, n_tiles)
    yk = _moe(hf, tile_expert, tile_rows, next_expert, tok_slots, dst_slots, w_up[0], w_down[0],
              n_assign + 2 * MOE_TILE)
    y_p, y_s = _combine(x2, yk, wts, n_p, tm)

    y_p = y_p.reshape(bp, tp, d)
    y_s = y_s.reshape(bs, ts, d)
    kv_p = jnp.stack([kv_swa[(b + 1) * tp - WINDOW:(b + 1) * tp] for b in range(bp)])
    kv_p = kv_p.reshape(bp, WINDOW, 2, SWA_KV_HEADS, SWA_HD)
    kv_s = kv_swa[n_p:].reshape(bs, ts, 2, SWA_KV_HEADS, SWA_HD)
    swk_s = jnp.concatenate([cache_swa_k[0], kv_s[:, :, 0]], axis=1)[:, ts:ts + keep_s]
    swv_s = jnp.concatenate([cache_swa_v[0], kv_s[:, :, 1]], axis=1)[:, ts:ts + keep_s]
    mk_p = mem_kv[:, :mem_w].reshape(bp, N_MEM, MEM_HEADS, MEM_HD)
    mv_p = mem_kv[:, mem_w:].reshape(bp, N_MEM, MEM_HEADS, MEM_HD)
    return (y_p, y_s, gla_state_p[None], kv_p[:, :, 0][None], kv_p[:, :, 1][None], mk_p[None], mv_p[None],
            gla_state_s[None], swk_s[None], swv_s[None])
```

```python
import functools

import jax
import jax.numpy as jnp
from jax import lax
from jax.experimental import pallas as pl
from jax.experimental.pallas import tpu as pltpu

F32 = jnp.float32
BF16 = jnp.bfloat16

D_MODEL = 2048
CHUNK = 64
EPS = 1e-6
PAST_LEN = 1024
GLA_HEADS = 4
GLA_DV = 512
GLA_DK = 256
GLA_LOWRANK = 16
GLA_NORMALIZER = 16.0
SWA_HD = 64
SWA_HEADS = 32
SWA_KV_HEADS = 4
SWA_GROUP = 8
WINDOW = 128
ROPE_DIM = 16
ROPE_THETA = 500000.0
N_MEM = 256
MEM_HEADS = 4
MEM_HD = 512
N_GROUPS = 8
EXPERTS_PER_GROUP = 8
N_EXPERTS = 64
TOP_K = 2
D_FF = 512

LANES = 128
VMEM_LIMIT = 56 * 1024 * 1024
ROW_TILE = 512
PROJ_ROWS = 1536
COL_TILE = 512
WIDE_COLS = 1024
ROUTER_SUB_ROWS = 256
MOE_TILE = 256
MOE_ROW_GROUP = 32
TOKEN_TILE_ROWS = D_MODEL // 2 // LANES
TOKEN_TILE_PITCH = 12
TOKEN_F32_ROWS = D_MODEL // LANES
TOKEN_F32_PITCH = 20
NEG_BIG = -1e30


def _params(n_axes):
    return pltpu.CompilerParams(dimension_semantics=("arbitrary",) * n_axes,
                                vmem_limit_bytes=VMEM_LIMIT)


def _norm_kernel(x_ref, g_ref, o_ref):
    x = x_ref[...]
    y = x * lax.rsqrt(jnp.mean(x * x, axis=-1, keepdims=True) + EPS)
    o_ref[...] = (y * g_ref[...]).astype(o_ref.dtype)


def _rms_norm_rows(x, g, out_dtype, tm):
    n, d = x.shape
    return pl.pallas_call(
        _norm_kernel,
        out_shape=jax.ShapeDtypeStruct((n, d), out_dtype),
        grid=(n // tm,),
        in_specs=[pl.BlockSpec((tm, d), lambda i: (i, 0)),
                  pl.BlockSpec((1, d), lambda i: (0, 0))],
        out_specs=pl.BlockSpec((tm, d), lambda i: (i, 0)),
        compiler_params=_params(1),
        name="rms_norm_rows",
    )(x, g.reshape(1, d))


def _norm2_kernel(xp_ref, xs_ref, g_ref, o_ref, *, p_tiles):
    def emit(x_ref):
        x = x_ref[...]
        y = x * lax.rsqrt(jnp.mean(x * x, axis=-1, keepdims=True) + EPS)
        o_ref[...] = (y * g_ref[...]).astype(o_ref.dtype)

    @pl.when(pl.program_id(0) < p_tiles)
    def _():
        emit(xp_ref)

    @pl.when(pl.program_id(0) >= p_tiles)
    def _():
        emit(xs_ref)


def _split_specs(p_tiles, tm, d):
    return (pl.BlockSpec((tm, d), lambda i: (jnp.minimum(i, p_tiles - 1), 0)),
            pl.BlockSpec((tm, d), lambda i: (jnp.maximum(i - p_tiles, 0), 0)))


def _rms_norm_two(xp, xs, g, tm):
    (n_p, d), n_s = xp.shape, xs.shape[0]
    p_tiles = n_p // tm
    return pl.pallas_call(
        functools.partial(_norm2_kernel, p_tiles=p_tiles),
        out_shape=jax.ShapeDtypeStruct((n_p + n_s, d), BF16),
        grid=((n_p + n_s) // tm,),
        in_specs=[*_split_specs(p_tiles, tm, d), pl.BlockSpec((1, d), lambda i: (0, 0))],
        out_specs=pl.BlockSpec((tm, d), lambda i: (i, 0)),
        compiler_params=_params(1),
        name="rms_norm_mix",
    )(xp, xs, g.reshape(1, d))


def _segment_rms(acc, seg_ref, inv_width):
    ss = jnp.dot((acc * acc).astype(BF16), seg_ref[...], preferred_element_type=F32)
    return ss * inv_width


def _rope(y, rows, cos_ref, sa_ref, sb_ref):
    width = y.shape[1]
    reps = width // LANES
    c = jnp.concatenate([cos_ref[rows, :]] * reps, axis=1)
    sa = jnp.concatenate([sa_ref[rows, :]] * reps, axis=1)
    sb = jnp.concatenate([sb_ref[rows, :]] * reps, axis=1)
    half = ROPE_DIM // 2
    return y * c + pltpu.roll(y, width - half, 1) * sa + pltpu.roll(y, half, 1) * sb


def _ep_plain(acc, rows):
    return acc


def _ep_lowrank(acc, rows):
    lane = lax.broadcasted_iota(jnp.int32, acc.shape, 1)
    return jnp.where(lane < GLA_LOWRANK, acc, 0.0)


def _ep_sigmoid(acc, rows, b_ref):
    return jax.nn.sigmoid(acc + b_ref[...])


def _ep_qknorm_rope(acc, rows, g_ref, seg_ref, cos_ref, sa_ref, sb_ref, *, keep_from):
    ms = _segment_rms(acc, seg_ref, 1.0 / SWA_HD)
    y = acc * lax.rsqrt(ms + EPS) * g_ref[...]
    y = _rope(y, rows, cos_ref, sa_ref, sb_ref)
    if keep_from is not None:
        col = lax.broadcasted_iota(jnp.int32, y.shape, 1)
        y = jnp.where(col < keep_from, y, acc)
    return y


def _ep_headnorm(acc, rows, g_ref, *, norm_tiles):
    y = acc * lax.rsqrt(jnp.mean(acc * acc, axis=-1, keepdims=True) + EPS) * g_ref[...]
    if norm_tiles is not None:
        y = jnp.where(pl.program_id(0) < norm_tiles, y, acc)
    return y


def _wmm_kernel(a_ref, w_ref, *rest, w_is_transposed, sub_rows, epilogue):
    extras, o_ref, wbf = rest[:-2], rest[-2], rest[-1]

    @pl.when(pl.program_id(1) == 0)
    def _():
        wbf[...] = w_ref[...].astype(BF16)

    for r0 in range(0, a_ref.shape[0], sub_rows):
        rows = slice(r0, r0 + sub_rows)
        if w_is_transposed:
            acc = lax.dot_general(a_ref[rows, :], wbf[...], (((1,), (1,)), ((), ())), preferred_element_type=F32)
        else:
            acc = jnp.dot(a_ref[rows, :], wbf[...], preferred_element_type=F32)
        o_ref[rows, :] = epilogue(acc, rows, *extras).astype(o_ref.dtype)


def _wmatmul(epilogue, a, w, col0, n_cols, w_is_transposed, extras, extra_specs, out_dtype, tm, tn, name,
             sub_rows=None):
    m, k = a.shape
    sub_rows = tm if sub_rows is None else sub_rows
    assert m % tm == 0 and n_cols % tn == 0 and tm % sub_rows == 0
    if w_is_transposed:
        assert col0 % 8 == 0
        w_spec = pl.BlockSpec((pl.Element(tn), pl.Element(k)), lambda j, i: (pl.multiple_of(col0 + j * tn, 8), 0))
        w_tile = (tn, k)
    else:
        assert col0 % tn == 0
        w_spec = pl.BlockSpec((k, tn), lambda j, i: (0, col0 // tn + j))
        w_tile = (k, tn)
    kernel = functools.partial(_wmm_kernel, w_is_transposed=w_is_transposed, sub_rows=sub_rows, epilogue=epilogue)
    return pl.pallas_call(
        kernel,
        out_shape=jax.ShapeDtypeStruct((m, n_cols), out_dtype),
        grid=(n_cols // tn, m // tm),
        in_specs=[pl.BlockSpec((tm, k), lambda j, i: (i, 0)), w_spec] + list(extra_specs),
        out_specs=pl.BlockSpec((tm, tn), lambda j, i: (i, j)),
        scratch_shapes=[pltpu.VMEM(w_tile, BF16)],
        compiler_params=_params(2),
        name=name,
    )(a, w, *extras)


def _gla_kernel(q_ref, k_ref, v_ref, gg_ref, ga_ref, wa_ref, ba_ref, gn_ref, s0_ref,
                o_ref, sout_ref, s_scr, qt_scr, ku_scr, o_scr, dec_scr, *, chunk, n_chunks):
    t = pl.program_id(1)
    tb = chunk * n_chunks
    heads = [(slice(h * GLA_DK, (h + 1) * GLA_DK), slice(h * GLA_DV, (h + 1) * GLA_DV)) for h in range(GLA_HEADS)]

    @pl.when(t == 0)
    def _():
        s_scr[...] = s0_ref[0]

    row = lax.broadcasted_iota(jnp.int32, (chunk, chunk), 0)
    col = lax.broadcasted_iota(jnp.int32, (chunk, chunk), 1)
    tril = (row >= col).astype(BF16)
    z = jnp.dot(ga_ref[...].astype(BF16), wa_ref[...], preferred_element_type=F32) + ba_ref[...]
    log_a = (jnp.minimum(z, 0.0) - jnp.log(1.0 + jnp.exp(-jnp.abs(z)))) * (1.0 / GLA_NORMALIZER)
    hi = log_a.astype(BF16)
    rest = log_a - hi.astype(F32)
    mid = rest.astype(BF16)
    lo = (rest - mid.astype(F32)).astype(BF16)
    b_parts, last_parts = [], []
    for ci in range(n_chunks):
        crows = slice(ci * chunk, (ci + 1) * chunk)
        b_c = (jnp.dot(tril, hi[crows], preferred_element_type=F32)
               + jnp.dot(tril, mid[crows], preferred_element_type=F32)
               + jnp.dot(tril, lo[crows], preferred_element_type=F32))
        b_last = b_c[chunk - 1:chunk, :]
        b_parts.append(b_c)
        last_parts.append(jnp.broadcast_to(b_last, b_c.shape))
        for h, (ks, _) in enumerate(heads):
            dec_scr[ci, h] = jnp.transpose(jnp.broadcast_to(jnp.exp(b_last[:, ks]), (LANES, GLA_DK)))
    b = jnp.concatenate(b_parts, axis=0)
    b_last_rows = jnp.concatenate(last_parts, axis=0)

    q = q_ref[...] * (GLA_DK ** -0.5)
    k = k_ref[...]
    q_t = (q * jnp.exp(b)).astype(BF16)
    k_t = (k * jnp.exp(-b)).astype(BF16)
    qt_scr[...] = q_t
    ku_scr[...] = (k * jnp.exp(b_last_rows - b)).astype(BF16)

    brow = lax.broadcasted_iota(jnp.int32, (tb, tb), 0)
    bcol = lax.broadcasted_iota(jnp.int32, (tb, tb), 1)
    mask = (bcol >= (brow & -chunk)) & (brow >= bcol)
    for ks, vs in heads:
        att = lax.dot_general(q_t[:, ks], k_t[:, ks], (((1,), (1,)), ((), ())), preferred_element_type=F32)
        att = jnp.where(mask, att, 0.0).astype(BF16)
        o_scr[:, vs] = jnp.dot(att, v_ref[:, vs].astype(BF16), preferred_element_type=F32)

    def one_chunk(ci, carry):
        rows = pl.ds(pl.multiple_of(ci * chunk, chunk), chunk)
        for h, (ks, vs) in enumerate(heads):
            s_old = s_scr[h]
            o_scr[rows, vs] += jnp.dot(qt_scr[rows, ks], s_old.astype(BF16), preferred_element_type=F32)
            decay = jnp.concatenate([dec_scr[ci, h]] * (GLA_DV // LANES), axis=1)
            s_scr[h] = decay * s_old + lax.dot_general(ku_scr[rows, ks], v_ref[rows, vs].astype(BF16),
                                                       (((0,), (0,)), ((), ())), preferred_element_type=F32)
        return carry

    lax.fori_loop(0, n_chunks, one_chunk, 0)

    for _, vs in heads:
        o = o_scr[:, vs]
        on = o * lax.rsqrt(jnp.mean(o * o, axis=-1, keepdims=True) + EPS) * gn_ref[...]
        gg = gg_ref[:, vs]
        o_ref[:, vs] = (on * (gg * jax.nn.sigmoid(gg))).astype(o_ref.dtype)

    @pl.when(t == pl.num_programs(1) - 1)
    def _():
        sout_ref[0] = s_scr[...]


def _gla(qkvg, ga, wa, ba, gn, s0, batch, seq, row0, tb, chunk):
    nt = seq // tb
    base = row0 // tb
    qk_w = GLA_HEADS * GLA_DK
    v_w = GLA_HEADS * GLA_DV
    rows = lambda b, t: base + b * nt + t
    kernel = functools.partial(_gla_kernel, chunk=chunk, n_chunks=tb // chunk)
    return pl.pallas_call(
        kernel,
        out_shape=(jax.ShapeDtypeStruct((batch * seq, v_w), BF16),
                   jax.ShapeDtypeStruct((batch, GLA_HEADS, GLA_DK, GLA_DV), F32)),
        grid=(batch, nt),
        in_specs=[pl.BlockSpec((tb, qk_w), lambda b, t: (rows(b, t), 0)),
                  pl.BlockSpec((tb, qk_w), lambda b, t: (rows(b, t), 1)),
                  pl.BlockSpec((tb, v_w), lambda b, t: (rows(b, t), 1)),
                  pl.BlockSpec((tb, v_w), lambda b, t: (rows(b, t), 2)),
                  pl.BlockSpec((tb, LANES), lambda b, t: (rows(b, t), 0)),
                  pl.BlockSpec((LANES, qk_w), lambda b, t: (0, 0)),
                  pl.BlockSpec((1, qk_w), lambda b, t: (0, 0)),
                  pl.BlockSpec((1, GLA_DV), lambda b, t: (0, 0)),
                  pl.BlockSpec((1, GLA_HEADS, GLA_DK, GLA_DV), lambda b, t: (b, 0, 0, 0))],
        out_specs=(pl.BlockSpec((tb, v_w), lambda b, t: (b * nt + t, 0)),
                   pl.BlockSpec((1, GLA_HEADS, GLA_DK, GLA_DV), lambda b, t: (b, 0, 0, 0))),
        scratch_shapes=[pltpu.VMEM((GLA_HEADS, GLA_DK, GLA_DV), F32),
                        pltpu.VMEM((tb, qk_w), BF16),
                        pltpu.VMEM((tb, qk_w), BF16),
                        pltpu.VMEM((tb, v_w), F32),
                        pltpu.VMEM((tb // chunk, GLA_HEADS, GLA_DK, LANES), F32)],
        compiler_params=_params(2),
        name="gla_chunks",
    )(qkvg, qkvg, qkvg, qkvg, ga, wa, ba, gn, s0)


def _swa_kernel(sink_ref, q_ref, kp_ref, vp_ref, ko_ref, vo_ref, o_ref, *, tq, prev_from_cache):
    i = pl.program_id(1)
    nk = WINDOW + tq
    k_all = jnp.concatenate([kp_ref[...], ko_ref[...]], axis=0)
    v_all = jnp.concatenate([vp_ref[...], vo_ref[...]], axis=0)
    qc = lax.broadcasted_iota(jnp.int32, (tq, nk), 0) // CHUNK + WINDOW // CHUNK
    kcol = lax.broadcasted_iota(jnp.int32, (tq, nk), 1)
    kc = kcol // CHUNK
    valid = (kc <= qc) & (kc >= qc - WINDOW // CHUNK)
    if not prev_from_cache:
        valid = valid & ((kcol >= WINDOW) | (i > 0))
    lane = lax.broadcasted_iota(jnp.int32, (nk, LANES), 1)
    low = lane < SWA_HD
    low_q = lax.broadcasted_iota(jnp.int32, (tq, LANES), 1) < SWA_HD
    scale = SWA_HD ** -0.5
    for g in range(SWA_KV_HEADS):
        slab = slice((g // 2) * LANES, (g // 2 + 1) * LANES)
        k2 = k_all[:, slab]
        v2 = v_all[:, slab]
        k2r = pltpu.roll(k2, SWA_HD, 1)
        v2r = pltpu.roll(v2, SWA_HD, 1)
        if g % 2 == 0:
            k_lo, k_hi, v_lo, v_hi = k2, k2r, v2, v2r
        else:
            k_lo, k_hi, v_lo, v_hi = k2r, k2, v2r, v2
        zero = jnp.zeros_like(k2)
        one = jnp.ones_like(k2)
        km = (jnp.where(low, k_lo, zero).astype(BF16), jnp.where(low, zero, k_hi).astype(BF16))
        vm = (jnp.where(low, v_lo, one).astype(BF16), jnp.where(low, one, v_hi).astype(BF16))
        heads = [(j, half) for j in range(SWA_GROUP // 2) for half in range(2)]
        sinks = [sink_ref[g * SWA_GROUP + 2 * j + half] for j, half in heads]
        scores = []
        for j, half in heads:
            qs = q_ref[:, (g * 4 + j) * LANES:(g * 4 + j + 1) * LANES]
            s = lax.dot_general(qs, km[half], (((1,), (1,)), ((), ())), preferred_element_type=F32) * scale
            scores.append(jnp.where(valid, s, NEG_BIG))
        maxes = [jnp.maximum(jnp.max(s, axis=-1, keepdims=True), sk) for s, sk in zip(scores, sinks)]
        exps = [jnp.exp(s - m).astype(BF16) for s, m in zip(scores, maxes)]
        sink_terms = [jnp.exp(sk - m) for sk, m in zip(sinks, maxes)]
        for j in range(SWA_GROUP // 2):
            a_lo = jnp.dot(exps[2 * j], vm[0], preferred_element_type=F32)
            a_hi = jnp.dot(exps[2 * j + 1], vm[1], preferred_element_type=F32)
            num = jnp.where(low_q, a_lo, a_hi)
            den = pltpu.roll(jnp.where(low_q, a_hi, a_lo), SWA_HD, 1)
            den = den + jnp.where(low_q, sink_terms[2 * j], sink_terms[2 * j + 1])
            o_ref[:, (g * 4 + j) * LANES:(g * 4 + j + 1) * LANES] = (num / den).astype(o_ref.dtype)


def _swa(sinks, q, q_row0, k_prev, v_prev, prev_col, k_own, v_own, own_col, own_row0,
         batch, seq, tq, prev_from_cache):
    nt = seq // tq
    qb = q_row0 // tq
    ob = own_row0 // tq
    kv_w = SWA_KV_HEADS * SWA_HD
    if prev_from_cache:
        prev_map = lambda b, t, c: (b, c)
    else:
        per = seq // WINDOW
        prev_map = lambda b, t, c: (b * per + jnp.maximum(t * (tq // WINDOW) - 1, 0), c)
    kernel = functools.partial(_swa_kernel, tq=tq, prev_from_cache=prev_from_cache)
    return pl.pallas_call(
        kernel,
        out_shape=jax.ShapeDtypeStruct((batch * seq, SWA_HEADS * SWA_HD), BF16),
        grid=(batch, nt),
        in_specs=[pl.BlockSpec(memory_space=pltpu.SMEM),
                  pl.BlockSpec((tq, SWA_HEADS * SWA_HD), lambda b, t: (qb + b * nt + t, 0)),
                  pl.BlockSpec((WINDOW, kv_w), lambda b, t: prev_map(b, t, prev_col[0])),
                  pl.BlockSpec((WINDOW, kv_w), lambda b, t: prev_map(b, t, prev_col[1])),
                  pl.BlockSpec((tq, kv_w), lambda b, t: (ob + b * nt + t, own_col[0])),
                  pl.BlockSpec((tq, kv_w), lambda b, t: (ob + b * nt + t, own_col[1]))],
        out_specs=pl.BlockSpec((tq, SWA_HEADS * SWA_HD), lambda b, t: (b * nt + t, 0)),
        compiler_params=_params(2),
        name="swa_band",
    )(sinks, q, k_prev, v_prev, k_own, v_own)


def _mem_attn_head(q, k, v):
    s = lax.dot_general(q, k.astype(BF16), (((1,), (1,)), ((), ())),
                        preferred_element_type=F32) * (MEM_HD ** -0.5)
    m = jnp.max(s, axis=-1, keepdims=True)
    e = jnp.exp(s - m)
    p = (e / jnp.sum(e, axis=-1, keepdims=True)).astype(BF16)
    return jnp.dot(p, v.astype(BF16), preferred_element_type=F32)


def _mem_attn_kernel(q_ref, kv_ref, o_ref, kv_bf):
    @pl.when(pl.program_id(1) == 0)
    def _():
        kv_bf[...] = kv_ref[...].astype(BF16)

    width = MEM_HEADS * MEM_HD
    cols = [slice(h * MEM_HD, (h + 1) * MEM_HD) for h in range(MEM_HEADS)]
    scores = [lax.dot_general(q_ref[:, c], kv_bf[:, c], (((1,), (1,)), ((), ())),
                              preferred_element_type=F32) * (MEM_HD ** -0.5) for c in cols]
    exps = [jnp.exp(s - jnp.max(s, axis=-1, keepdims=True)) for s in scores]
    probs = [(e / jnp.sum(e, axis=-1, keepdims=True)).astype(BF16) for e in exps]
    for c, p in zip(cols, probs):
        v = kv_bf[:, width + c.start:width + c.stop]
        o_ref[:, c] = jnp.dot(p, v, preferred_element_type=F32).astype(o_ref.dtype)


def _mem_attn_cache_kernel(q_ref, k_ref, v_ref, o_ref):
    for h in range(MEM_HEADS):
        cols = slice(h * MEM_HD, (h + 1) * MEM_HD)
        o_ref[:, cols] = _mem_attn_head(q_ref[:, cols], k_ref[0, 0, :, h, :], v_ref[0, 0, :, h, :]).astype(o_ref.dtype)


def _mem_attn_cache(q, q_row0, cache_k, cache_v, batch, seq):
    qb = q_row0 // seq
    width = MEM_HEADS * MEM_HD
    cache_spec = pl.BlockSpec((1, 1, N_MEM, MEM_HEADS, MEM_HD), lambda b: (0, b, 0, 0, 0))
    return pl.pallas_call(
        _mem_attn_cache_kernel,
        out_shape=jax.ShapeDtypeStruct((batch * seq, width), BF16),
        grid=(batch,),
        in_specs=[pl.BlockSpec((seq, width), lambda b: (qb + b, 0)), cache_spec, cache_spec],
        out_specs=pl.BlockSpec((seq, width), lambda b: (b, 0)),
        compiler_params=_params(1),
        name="mem_attn_cache",
    )(q, cache_k, cache_v)


def _mem_attn(q, mem_kv, batch, seq, tq):
    nt = seq // tq
    width = MEM_HEADS * MEM_HD
    return pl.pallas_call(
        _mem_attn_kernel,
        out_shape=jax.ShapeDtypeStruct((batch * seq, width), BF16),
        grid=(batch, nt),
        in_specs=[pl.BlockSpec((tq, width), lambda b, t: (b * nt + t, 0)),
                  pl.BlockSpec((N_MEM, 2 * width), lambda b, t: (b, 0))],
        out_specs=pl.BlockSpec((tq, width), lambda b, t: (b * nt + t, 0)),
        scratch_shapes=[pltpu.VMEM((N_MEM, 2 * width), BF16)],
        compiler_params=_params(2),
        name="mem_attn",
    )(q, mem_kv)


def _merge_kernel(a0p, a0s, a1p, a1s, a2p, a2s, w_ref, g0_ref, g1_ref, g2_ref, o_ref, *, p_tiles):
    def emit(a0_ref, a1_ref, a2_ref):
        acc = g0_ref[...] * jnp.dot(a0_ref[...], w_ref[0], preferred_element_type=F32)
        acc = acc + g1_ref[...] * jnp.dot(a1_ref[...], w_ref[1], preferred_element_type=F32)
        acc = acc + g2_ref[...] * jnp.dot(a2_ref[...], w_ref[2], preferred_element_type=F32)
        o_ref[...] = acc.astype(o_ref.dtype)

    @pl.when(pl.program_id(1) < p_tiles)
    def _():
        emit(a0p, a1p, a2p)

    @pl.when(pl.program_id(1) >= p_tiles)
    def _():
        emit(a0s, a1s, a2s)


def _merge(branches_p, branches_s, w_branch, gates, tm, tn):
    n_p, d = branches_p[0].shape
    n = n_p + branches_s[0].shape[0]
    nj = d // tn
    p_tiles = n_p // tm
    ap = pl.BlockSpec((tm, d), lambda j, i: (jnp.minimum(i, p_tiles - 1), 0))
    asp = pl.BlockSpec((tm, d), lambda j, i: (jnp.maximum(i - p_tiles, 0), 0))
    operands = [a for pair in zip(branches_p, branches_s) for a in pair]
    return pl.pallas_call(
        functools.partial(_merge_kernel, p_tiles=p_tiles),
        out_shape=jax.ShapeDtypeStruct((n, d), BF16),
        grid=(nj, n // tm),
        in_specs=[ap, asp, ap, asp, ap, asp,
                  pl.BlockSpec((3, d, tn), lambda j, i: (0, 0, j), pipeline_mode=pl.Buffered(1)),
                  pl.BlockSpec((tm, tn), lambda j, i: (i, j)),
                  pl.BlockSpec((tm, tn), lambda j, i: (i, nj + j)),
                  pl.BlockSpec((tm, tn), lambda j, i: (i, 2 * nj + j))],
        out_specs=pl.BlockSpec((tm, tn), lambda j, i: (i, j)),
        compiler_params=_params(2),
        name="branch_merge",
    )(*operands, w_branch, gates, gates, gates)


def _outproj_router_kernel(m_ref, w_ref, xp_ref, xs_ref, g_ref, wr_ref, br_ref,
                           x2_ref, hf_ref, eid_ref, wt_ref, *, p_tiles):
    is_prompt = pl.program_id(0) < p_tiles
    tm, d = x2_ref.shape
    for r0 in range(0, tm, ROUTER_SUB_ROWS):
        rows = slice(r0, r0 + ROUTER_SUB_ROWS)
        acc = jnp.dot(m_ref[rows, :], w_ref[...], preferred_element_type=F32)
        x = jnp.where(is_prompt, xp_ref[rows, :], xs_ref[rows, :]) + acc
        x2_ref[rows, :] = x
        hf = x * lax.rsqrt(jnp.mean(x * x, axis=-1, keepdims=True) + EPS) * g_ref[...]
        hb = hf.astype(BF16)
        bits = pltpu.bitcast(hb.astype(F32), jnp.uint32)
        packed = bits[:, d // 2:] | (bits[:, :d // 2] >> 16)
        for s in range(TOKEN_TILE_ROWS):
            hf_ref[pl.ds(r0 * TOKEN_TILE_ROWS + s, ROUTER_SUB_ROWS, stride=TOKEN_TILE_ROWS), :] = (
                packed[:, s * LANES:(s + 1) * LANES])
        logits = jnp.dot(hb, wr_ref[...], preferred_element_type=F32) + br_ref[...]
        eid, wts = _route(logits)
        eid_ref[rows, :] = eid
        wt_ref[rows, :] = wts


def _route(logits):
    lane = lax.broadcasted_iota(jnp.int32, logits.shape, 1).astype(F32)
    big = 1e6
    is_g = lane < N_GROUPS
    lg = jnp.where(is_g, logits, NEG_BIG)
    mg = jnp.max(lg, axis=-1, keepdims=True)
    gsel = jnp.min(jnp.where(is_g & (lg == mg), lane, big), axis=-1, keepdims=True)
    g_w = 1.0 / jnp.sum(jnp.where(is_g, jnp.exp(lg - mg), 0.0), axis=-1, keepdims=True)
    e_lo = N_GROUPS + gsel * EXPERTS_PER_GROUP
    in_grp = (lane >= e_lo) & (lane < e_lo + EXPERTS_PER_GROUP)
    le = jnp.where(in_grp, logits, NEG_BIG)
    me = jnp.max(le, axis=-1, keepdims=True)
    ee = jnp.where(in_grp, jnp.exp(le - me), 0.0)
    pe = ee / jnp.sum(ee, axis=-1, keepdims=True)
    pe = jnp.where(in_grp, pe, -1.0)
    p1 = jnp.max(pe, axis=-1, keepdims=True)
    i1 = jnp.min(jnp.where(pe == p1, lane, big), axis=-1, keepdims=True)
    pe2 = jnp.where(lane == i1, -1.0, pe)
    p2 = jnp.max(pe2, axis=-1, keepdims=True)
    i2 = jnp.min(jnp.where(pe2 == p2, lane, big), axis=-1, keepdims=True)
    tot = p1 + p2
    w1 = g_w * p1 / tot
    w2 = g_w * p2 / tot
    eid = jnp.where(lane == 0.0, i1 - N_GROUPS, jnp.where(lane == 1.0, i2 - N_GROUPS, 0.0))
    return eid.astype(jnp.int32), jnp.where(lane == 0.0, w1, jnp.where(lane == 1.0, w2, 0.0))


def _outproj_router(merged, w_out, xp, xs, g, wr, br, tm):
    n, d = merged.shape
    p_tiles = xp.shape[0] // tm
    const = lambda shape: pl.BlockSpec(shape, lambda i: (0, 0), pipeline_mode=pl.Buffered(1))
    row = lambda width: pl.BlockSpec((tm, width), lambda i: (i, 0))
    return pl.pallas_call(
        functools.partial(_outproj_router_kernel, p_tiles=p_tiles),
        out_shape=(jax.ShapeDtypeStruct((n, d), F32),
                   jax.ShapeDtypeStruct((n * TOKEN_TILE_ROWS, LANES), jnp.uint32),
                   jax.ShapeDtypeStruct((n, LANES), jnp.int32),
                   jax.ShapeDtypeStruct((n, LANES), F32)),
        grid=(n // tm,),
        in_specs=[row(d), const((d, d)), *_split_specs(p_tiles, tm, d),
                  const((1, d)), const((d, LANES)), const((1, LANES))],
        out_specs=(row(d), pl.BlockSpec((tm * TOKEN_TILE_ROWS, LANES), lambda i: (i, 0)),
                   row(LANES), row(LANES)),
        compiler_params=_params(1),
        name="outproj_router",
    )(merged, w_out, xp, xs, g.reshape(1, d), wr, br)


def _moe_kernel(te_ref, nv_ref, nxt_ref, tok_ref, tok_next_ref, dst_ref, hf_hbm, wup_hbm, wdn_hbm, y_hbm,
                xg, yb, wup_f32, wdn_f32, wup_bf, wdn_bf, in_sem, out_sem, w_sem):
    t = pl.program_id(0)
    buf = t % 2
    rows_now = nv_ref[t]
    rows_next = nv_ref[t + 1]
    rows_prev = nv_ref[jnp.maximum(t - 1, 0)]
    valid = rows_now > 0
    valid_next = rows_next > 0
    new_expert = (t == 0) | (te_ref[t] != te_ref[jnp.maximum(t - 1, 0)])

    def weight_copies(e):
        return (pltpu.make_async_copy(wup_hbm.at[e], wup_f32, w_sem.at[0]),
                pltpu.make_async_copy(wdn_hbm.at[e], wdn_f32, w_sem.at[1]))

    in_rows, in_pitch = TOKEN_TILE_ROWS, TOKEN_TILE_PITCH
    out_rows, out_pitch = TOKEN_F32_ROWS, TOKEN_F32_PITCH

    def row_in(row0, r, b):
        return pltpu.make_async_copy(hf_hbm.at[pl.ds(pl.multiple_of(row0, in_rows), in_rows), :],
                                     xg.at[b, pl.ds(r * in_pitch, in_rows), :], in_sem.at[b])

    def row_out(row0, r, b):
        return pltpu.make_async_copy(yb.at[b, pl.ds(r * out_pitch, out_rows), :],
                                     y_hbm.at[pl.ds(pl.multiple_of(row0, out_rows), out_rows), :], out_sem.at[b])

    group = MOE_ROW_GROUP
    n_groups = MOE_TILE // group

    def group_in(b):
        return pltpu.make_async_copy(hf_hbm.at[pl.ds(0, group * in_rows), :],
                                     xg.at[b, pl.ds(0, group * in_rows), :], in_sem.at[b])

    def group_out(b):
        return pltpu.make_async_copy(yb.at[b, pl.ds(0, group * out_rows), :],
                                     y_hbm.at[pl.ds(0, group * out_rows), :], out_sem.at[b])

    def per_started_group(rows, fn):
        for g in range(n_groups):
            pl.when(rows > g * group)(functools.partial(fn, g))

    @pl.when(t == 0)
    def _():
        for c in weight_copies(te_ref[0]):
            c.start(priority=1)
        xg[...] = jnp.zeros_like(xg)

        def first(r, c):
            row_in(tok_ref[0, 0, r], r, 0).start()
            return c
        lax.fori_loop(0, ((rows_now + group - 1) // group) * group, first, 0)
        yb[...] = jnp.zeros_like(yb)
        n_real = y_hbm.shape[0] - 2 * MOE_TILE * out_rows
        for b in range(2):
            spare = pltpu.make_async_copy(
                yb.at[b, pl.ds(0, MOE_TILE * out_rows), :],
                y_hbm.at[pl.ds(n_real + b * MOE_TILE * out_rows, MOE_TILE * out_rows), :], out_sem.at[b])
            spare.start()
            spare.wait()

    @pl.when(valid & new_expert)
    def _():
        for c in weight_copies(te_ref[t]):
            c.wait()
        wup_bf[...] = wup_f32[...].astype(BF16)
        wdn_bf[...] = wdn_f32[...].astype(BF16)

        @pl.when(nxt_ref[t] >= 0)
        def _():
            for c in weight_copies(nxt_ref[t]):
                c.start(priority=1)

    def gather_next(g):
        for r in range(g * group, (g + 1) * group):
            row_in(tok_next_ref[0, 0, r], r, 1 - buf).start()

    def scatter_now(g):
        for r in range(g * group, (g + 1) * group):
            row_out(dst_ref[0, 0, r], r, buf).start(priority=r % 2)

    per_started_group(rows_next, gather_next)

    @pl.when(valid)
    def _():
        per_started_group(rows_now, lambda g: group_in(buf).wait())
        lo, hi = [], []
        for s in range(in_rows):
            word = xg[buf, pl.ds(s, MOE_TILE, stride=in_pitch), :]
            lo.append(pltpu.bitcast(word << 16, F32).astype(BF16))
            hi.append(pltpu.bitcast(word & jnp.uint32(0xFFFF0000), F32).astype(BF16))
        x = jnp.concatenate(lo + hi, axis=1)
        h1 = jnp.dot(x, wup_bf[...], preferred_element_type=F32)
        gate = h1[:, :D_FF]
        up = h1[:, D_FF:]
        act = (gate * jax.nn.sigmoid(gate)) * up
        ye = jnp.dot(act.astype(BF16), wdn_bf[...], preferred_element_type=F32)
        for s in range(out_rows):
            yb[buf, pl.ds(s, MOE_TILE, stride=out_pitch), :] = ye[:, s * LANES:(s + 1) * LANES]
        per_started_group(rows_now, scatter_now)

        @pl.when(t > 0)
        def _():
            per_started_group(rows_prev, lambda g: group_out(1 - buf).wait())

        @pl.when(jnp.logical_not(valid_next))
        def _():
            per_started_group(rows_now, lambda g: group_out(buf).wait())


def _moe(hf, tile_expert, tile_rows, next_expert, tok_slots, dst_slots, w_up, w_down, out_tokens):
    n_tiles = tile_expert.shape[0]
    d = w_up.shape[1]
    slot_spec = lambda off: pl.BlockSpec((1, 1, MOE_TILE),
                                         lambda t, te, nv, ne: (jnp.minimum(t + off, n_tiles - 1), 0, 0),
                                         memory_space=pltpu.SMEM)
    hbm = pl.BlockSpec(memory_space=pl.ANY)
    grid_spec = pltpu.PrefetchScalarGridSpec(
        num_scalar_prefetch=3,
        grid=(n_tiles,),
        in_specs=[slot_spec(0), slot_spec(1), slot_spec(0), hbm, hbm, hbm],
        out_specs=hbm,
        scratch_shapes=[pltpu.VMEM((2, MOE_TILE * TOKEN_TILE_PITCH, LANES), jnp.uint32),
                        pltpu.VMEM((2, MOE_TILE * TOKEN_F32_PITCH, LANES), F32),
                        pltpu.VMEM((d, 2 * D_FF), F32),
                        pltpu.VMEM((D_FF, d), F32),
                        pltpu.VMEM((d, 2 * D_FF), BF16),
                        pltpu.VMEM((D_FF, d), BF16),
                        pltpu.SemaphoreType.DMA((2,)),
                        pltpu.SemaphoreType.DMA((2,)),
                        pltpu.SemaphoreType.DMA((2,))],
    )
    return pl.pallas_call(
        _moe_kernel,
        out_shape=jax.ShapeDtypeStruct((out_tokens * TOKEN_F32_ROWS, LANES), F32),
        grid_spec=grid_spec,
        compiler_params=_params(1),
        name="moe_experts",
    )(tile_expert, tile_rows, next_expert, tok_slots, tok_slots, dst_slots, hf, w_up, w_down)


def _combine_kernel(x_ref, y0_ref, y1_ref, w_ref, op_ref, os_ref, *, p_tiles):
    tm = x_ref.shape[0]
    w = w_ref[...]
    w0 = w[:, 0:1]
    w1 = w[:, 1:2]

    def emit(o_ref):
        for s in range(TOKEN_F32_ROWS):
            cols = slice(s * LANES, (s + 1) * LANES)
            rows = pl.ds(s, tm, stride=TOKEN_F32_ROWS)
            o_ref[:, cols] = x_ref[:, cols] + (y0_ref[rows, :] * w0 + y1_ref[rows, :] * w1)

    @pl.when(pl.program_id(0) < p_tiles)
    def _():
        emit(op_ref)

    @pl.when(pl.program_id(0) >= p_tiles)
    def _():
        emit(os_ref)


def _combine(x2, yk, wts, n_p, tm):
    n, d = x2.shape
    p_tiles = n_p // tm
    k1 = n // tm
    y_rows = tm * TOKEN_F32_ROWS
    return pl.pallas_call(
        functools.partial(_combine_kernel, p_tiles=p_tiles),
        out_shape=(jax.ShapeDtypeStruct((n_p, d), F32), jax.ShapeDtypeStruct((n - n_p, d), F32)),
        grid=(n // tm,),
        in_specs=[pl.BlockSpec((tm, d), lambda i: (i, 0)),
                  pl.BlockSpec((y_rows, LANES), lambda i: (i, 0)),
                  pl.BlockSpec((y_rows, LANES), lambda i: (k1 + i, 0)),
                  pl.BlockSpec((tm, LANES), lambda i: (i, 0))],
        out_specs=_split_specs(p_tiles, tm, d),
        compiler_params=_params(1),
        name="moe_combine",
    )(x2, yk, yk, wts)


def _rope_tables(pos):
    half = ROPE_DIM // 2
    inv = ROPE_THETA ** (-jnp.arange(half, dtype=F32) / half)
    ang = pos.astype(F32)[:, None] * inv[None, :]
    cos, sin = jnp.cos(ang), jnp.sin(ang)
    n = pos.shape[0]
    pad = jnp.zeros((n, SWA_HD - ROPE_DIM), F32)
    cos_h = jnp.concatenate([cos, cos, pad + 1.0], axis=1)
    sa_h = jnp.concatenate([-sin, jnp.zeros_like(sin), pad], axis=1)
    sb_h = jnp.concatenate([jnp.zeros_like(sin), sin, pad], axis=1)
    reps = LANES // SWA_HD
    return tuple(jnp.tile(a, (1, reps)) for a in (cos_h, sa_h, sb_h))


def _moe_schedule(eid, n_tok, n_tiles):
    a = eid.shape[0]
    order = jnp.argsort(eid, stable=True).astype(jnp.int32)
    counts = jnp.bincount(eid, length=N_EXPERTS).astype(jnp.int32)
    tiles_per = (counts + MOE_TILE - 1) // MOE_TILE
    tile_end = jnp.cumsum(tiles_per)
    tile_start = tile_end - tiles_per
    sorted_start = jnp.cumsum(counts) - counts
    tile_id = jnp.arange(n_tiles, dtype=jnp.int32)
    used = tile_id < tile_end[-1]
    te = jnp.minimum(jnp.sum(tile_end[None, :] <= tile_id[:, None], axis=1), N_EXPERTS - 1).astype(jnp.int32)
    last_used_e = te[jnp.maximum(tile_end[-1] - 1, 0)]
    te = jnp.where(used, te, last_used_e)
    row_in_expert = (tile_id - tile_start[te]) * MOE_TILE
    rows_valid = jnp.where(used, jnp.clip(counts[te] - row_in_expert, 0, MOE_TILE), 0).astype(jnp.int32)
    r = jnp.arange(MOE_TILE, dtype=jnp.int32)[None, :]
    src = sorted_start[te][:, None] + row_in_expert[:, None] + r
    real = r < rows_valid[:, None]
    assign = order[jnp.clip(src, 0, a - 1)]
    tok = jnp.where(real, assign % n_tok, 0) * TOKEN_TILE_ROWS
    spare = a + (tile_id[:, None] % 2) * MOE_TILE + r
    dst = jnp.where(real, assign, spare) * TOKEN_F32_ROWS
    tile_rows = jnp.concatenate([rows_valid, jnp.zeros((1,), jnp.int32)])
    e_id = jnp.arange(N_EXPERTS, dtype=jnp.int32)[None, :]
    later = (e_id > te[:, None]) & (counts[None, :] > 0)
    nxt = jnp.min(jnp.where(later, e_id, N_EXPERTS), axis=1)
    nxt = jnp.where(nxt < N_EXPERTS, nxt, -1).astype(jnp.int32)
    shape = (n_tiles, 1, MOE_TILE)
    return te, tile_rows, nxt, tok.reshape(shape).astype(jnp.int32), dst.reshape(shape).astype(jnp.int32)


def kernel(x_prompt, x_sample, state_gla, cache_swa_k, cache_swa_v, cache_mem_k, cache_mem_v,
           mem_prompt, norm_mix_g, w_in, w_a2, b_a2, gla_norm_g, swa_q_norm_g, swa_k_norm_g,
           swa_sinks, norm_mem_g, w_mem_kv, mem_q_norm_g, mem_k_norm_g, w_gate, b_gate,
           w_branch, w_out, norm_ffn_g, w_router_group, b_router_group, w_router_expert,
           b_router_expert, w_up, w_down):
    bp, tp, d = x_prompt.shape
    bs, ts, _ = x_sample.shape
    n_p, n_s = bp * tp, bs * ts
    n = n_p + n_s
    tm, tn = ROW_TILE, COL_TILE
    tp_rows = PROJ_ROWS if n % PROJ_ROWS == 0 else tm
    heavy_sub = tp_rows // 2
    assert d == D_MODEL and n_p % tm == 0 and n_s % tm == 0 and w_in.shape[0] == 1
    keep_s = cache_swa_k.shape[2]
    assert keep_s == WINDOW and tp % WINDOW == 0

    qk_w = GLA_HEADS * GLA_DK
    v_w = GLA_HEADS * GLA_DV
    c0 = 2 * qk_w + 2 * v_w
    sq_w = SWA_HEADS * SWA_HD
    kv_w = SWA_KV_HEADS * SWA_HD
    mem_w = MEM_HEADS * MEM_HD
    w_in_t = jnp.transpose(w_in[0])
    w_branch_b = w_branch[0].astype(BF16)
    w_out_b = w_out[0].astype(BF16)
    w_a2_b = jnp.pad(w_a2[0], ((0, LANES - GLA_LOWRANK), (0, 0))).astype(BF16)
    w_router = jnp.pad(jnp.concatenate([w_router_group[0], w_router_expert[0]], axis=1),
                       ((0, 0), (0, LANES - N_GROUPS - N_EXPERTS))).astype(BF16)
    b_router = jnp.pad(jnp.concatenate([b_router_group[0], b_router_expert[0]]),
                       (0, LANES - N_GROUPS - N_EXPERTS)).reshape(1, LANES)

    pos = jnp.concatenate([jnp.tile(jnp.arange(tp, dtype=jnp.int32), bp),
                           jnp.tile(PAST_LEN + jnp.arange(ts, dtype=jnp.int32), bs)])
    cos_t, sa_t, sb_t = _rope_tables(pos)
    seg_id = jnp.arange(tn, dtype=jnp.int32) // SWA_HD
    seg = (seg_id[:, None] == seg_id[None, :]).astype(BF16)
    rope_specs = [pl.BlockSpec((tp_rows, LANES), lambda j, i: (i, 0))] * 3
    row_vec = lambda width: pl.BlockSpec((1, width), lambda j, i: (0, 0))
    seg_spec = pl.BlockSpec((tn, tn), lambda j, i: (0, 0))

    xp2 = x_prompt.reshape(n_p, d)
    xs2 = x_sample.reshape(n_s, d)
    h = _rms_norm_two(xp2, xs2, norm_mix_g[0], tm)

    c1 = c0 + GLA_LOWRANK
    c2 = c1 + sq_w
    c3 = c2 + 2 * kv_w
    qkvg = _wmatmul(_ep_plain, h, w_in_t, 0, c0, True, [], [], F32, tp_rows, WIDE_COLS, "proj_gla")
    ga = _wmatmul(_ep_lowrank, h, w_in_t, c0, LANES, True, [], [], F32, tp_rows, LANES, "proj_gla_lowrank")
    q_gain = jnp.tile(swa_q_norm_g[0], tn // SWA_HD).reshape(1, tn)
    q_swa = _wmatmul(functools.partial(_ep_qknorm_rope, keep_from=None), h, w_in_t, c1, sq_w, True,
                     [q_gain, seg, cos_t, sa_t, sb_t], [row_vec(tn), seg_spec] + rope_specs,
                     BF16, tp_rows, tn, "proj_swa_q", sub_rows=heavy_sub)
    k_gain = jnp.tile(swa_k_norm_g[0], tn // SWA_HD).reshape(1, tn)
    kv_swa = _wmatmul(functools.partial(_ep_qknorm_rope, keep_from=kv_w), h, w_in_t, c2, 2 * kv_w, True,
                      [k_gain, seg, cos_t, sa_t, sb_t], [row_vec(tn), seg_spec] + rope_specs,
                      F32, tp_rows, tn, "proj_swa_kv", sub_rows=heavy_sub)
    q_mem = _wmatmul(functools.partial(_ep_headnorm, norm_tiles=None), h, w_in_t, c3, mem_w, True,
                     [mem_q_norm_g[0].reshape(1, MEM_HD)], [row_vec(MEM_HD)],
                     BF16, tp_rows, MEM_HD, "proj_mem_q", sub_rows=heavy_sub)
    gates = _wmatmul(_ep_sigmoid, h, w_gate[0], 0, 3 * d, False, [b_gate[0].reshape(1, -1)],
                     [pl.BlockSpec((1, WIDE_COLS), lambda j, i: (0, j))], F32, tp_rows, WIDE_COLS, "proj_gates")

    mem_rows = bp * N_MEM
    hm = _rms_norm_rows(mem_prompt.reshape(mem_rows, d), norm_mem_g[0], BF16, N_MEM)
    mem_kv = _wmatmul(functools.partial(_ep_headnorm, norm_tiles=MEM_HEADS), hm, w_mem_kv[0], 0, 2 * mem_w, False,
                      [mem_k_norm_g[0].reshape(1, MEM_HD)], [row_vec(MEM_HD)], F32, mem_rows, MEM_HD, "mem_kv")

    ba = b_a2[0].reshape(1, qk_w)
    gn = gla_norm_g[0].reshape(1, GLA_DV)
    s0_p = jnp.zeros((bp, GLA_HEADS, GLA_DK, GLA_DV), F32)
    o_gla_p, gla_state_p = _gla(qkvg, ga, w_a2_b, ba, gn, s0_p, bp, tp, 0, 256, CHUNK)
    o_gla_s, gla_state_s = _gla(qkvg, ga, w_a2_b, ba, gn, state_gla[0], bs, ts, n_p, ts, min(CHUNK, ts))

    sinks = swa_sinks[0]
    o_swa_p = _swa(sinks, q_swa, 0, kv_swa, kv_swa, (0, 1), kv_swa, kv_swa, (0, 1), 0,
                   bp, tp, WINDOW, False)
    ck = cache_swa_k[0].reshape(bs * keep_s, kv_w)
    cv = cache_swa_v[0].reshape(bs * keep_s, kv_w)
    o_swa_s = _swa(sinks, q_swa, n_p, ck, cv, (0, 0), kv_swa, kv_swa, (0, 1), n_p,
                   bs, ts, ts, True)

    o_mem_p = _mem_attn(q_mem, mem_kv, bp, tp, tm)
    o_mem_s = _mem_attn_cache(q_mem, n_p, cache_mem_k, cache_mem_v, bs, ts)

    merged = _merge((o_gla_p, o_swa_p, o_mem_p), (o_gla_s, o_swa_s, o_mem_s), w_branch_b, gates, tm, WIDE_COLS)

    x2, hf, eid, wts = _outproj_router(merged, w_out_b, xp2, xs2, norm_ffn_g[0], w_router, b_router, tm)

    n_assign = TOP_K * n
    n_tiles = n_assign // MOE_TILE + N_EXPERTS
    eid_kmajor = jnp.concatenate([eid[:, k] for k in range(TOP_K)])
    tile_expert, tile_rows, next_expert, tok_slots, dst_slots = _moe_schedule(eid_kmajor, n, n_tiles)
    yk = _moe(hf, tile_expert, tile_rows, next_expert, tok_slots, dst_slots, w_up[0], w_down[0],
              n_assign + 2 * MOE_TILE)
    y_p, y_s = _combine(x2, yk, wts, n_p, tm)

    y_p = y_p.reshape(bp, tp, d)
    y_s = y_s.reshape(bs, ts, d)
    kv_p = jnp.stack([kv_swa[(b + 1) * tp - WINDOW:(b + 1) * tp] for b in range(bp)])
    kv_p = kv_p.reshape(bp, WINDOW, 2, SWA_KV_HEADS, SWA_HD)
    kv_s = kv_swa[n_p:].reshape(bs, ts, 2, SWA_KV_HEADS, SWA_HD)
    swk_s = jnp.concatenate([cache_swa_k[0], kv_s[:, :, 0]], axis=1)[:, ts:ts + keep_s]
    swv_s = jnp.concatenate([cache_swa_v[0], kv_s[:, :, 1]], axis=1)[:, ts:ts + keep_s]
    mk_p = mem_kv[:, :mem_w].reshape(bp, N_MEM, MEM_HEADS, MEM_HD)
    mv_p = mem_kv[:, mem_w:].reshape(bp, N_MEM, MEM_HEADS, MEM_HD)
    return (y_p, y_s, gla_state_p[None], kv_p[:, :, 0][None], kv_p[:, :, 1][None], mk_p[None], mv_p[None],
            gla_state_s[None], swk_s[None], swv_s[None])
```

```python
import functools

import jax
import jax.numpy as jnp
from jax import lax
from jax.experimental import pallas as pl
from jax.experimental.pallas import tpu as pltpu

F32 = jnp.float32
BF16 = jnp.bfloat16

D_MODEL = 2048
CHUNK = 64
EPS = 1e-6
PAST_LEN = 1024
GLA_HEADS = 4
GLA_DV = 512
GLA_DK = 256
GLA_LOWRANK = 16
GLA_NORMALIZER = 16.0
SWA_HD = 64
SWA_HEADS = 32
SWA_KV_HEADS = 4
SWA_GROUP = 8
WINDOW = 128
ROPE_DIM = 16
ROPE_THETA = 500000.0
N_MEM = 256
MEM_HEADS = 4
MEM_HD = 512
N_GROUPS = 8
EXPERTS_PER_GROUP = 8
N_EXPERTS = 64
TOP_K = 2
D_FF = 512

LANES = 128
VMEM_LIMIT = 56 * 1024 * 1024
ROW_TILE = 512
PROJ_ROWS = 1536
COL_TILE = 512
WIDE_COLS = 1024
ROUTER_SUB_ROWS = 256
MOE_TILE = 256
MOE_ROW_GROUP = 32
TOKEN_TILE_ROWS = D_MODEL // 2 // LANES
TOKEN_TILE_PITCH = 12
TOKEN_F32_ROWS = D_MODEL // LANES
TOKEN_F32_PITCH = 20
NEG_BIG = -1e30


def _params(n_axes):
    return pltpu.CompilerParams(dimension_semantics=("arbitrary",) * n_axes,
                                vmem_limit_bytes=VMEM_LIMIT)


def _norm_kernel(x_ref, g_ref, o_ref):
    x = x_ref[...]
    y = x * lax.rsqrt(jnp.mean(x * x, axis=-1, keepdims=True) + EPS)
    o_ref[...] = (y * g_ref[...]).astype(o_ref.dtype)


def _rms_norm_rows(x, g, out_dtype, tm):
    n, d = x.shape
    return pl.pallas_call(
        _norm_kernel,
        out_shape=jax.ShapeDtypeStruct((n, d), out_dtype),
        grid=(n // tm,),
        in_specs=[pl.BlockSpec((tm, d), lambda i: (i, 0)),
                  pl.BlockSpec((1, d), lambda i: (0, 0))],
        out_specs=pl.BlockSpec((tm, d), lambda i: (i, 0)),
        compiler_params=_params(1),
        name="rms_norm_rows",
    )(x, g.reshape(1, d))


def _norm2_kernel(xp_ref, xs_ref, g_ref, o_ref, *, p_tiles):
    def emit(x_ref):
        x = x_ref[...]
        y = x * lax.rsqrt(jnp.mean(x * x, axis=-1, keepdims=True) + EPS)
        o_ref[...] = (y * g_ref[...]).astype(o_ref.dtype)

    @pl.when(pl.program_id(0) < p_tiles)
    def _():
        emit(xp_ref)

    @pl.when(pl.program_id(0) >= p_tiles)
    def _():
        emit(xs_ref)


def _split_specs(p_tiles, tm, d):
    return (pl.BlockSpec((tm, d), lambda i: (jnp.minimum(i, p_tiles - 1), 0)),
            pl.BlockSpec((tm, d), lambda i: (jnp.maximum(i - p_tiles, 0), 0)))


def _rms_norm_two(xp, xs, g, tm):
    (n_p, d), n_s = xp.shape, xs.shape[0]
    p_tiles = n_p // tm
    return pl.pallas_call(
        functools.partial(_norm2_kernel, p_tiles=p_tiles),
        out_shape=jax.ShapeDtypeStruct((n_p + n_s, d), BF16),
        grid=((n_p + n_s) // tm,),
        in_specs=[*_split_specs(p_tiles, tm, d), pl.BlockSpec((1, d), lambda i: (0, 0))],
        out_specs=pl.BlockSpec((tm, d), lambda i: (i, 0)),
        compiler_params=_params(1),
        name="rms_norm_mix",
    )(xp, xs, g.reshape(1, d))


def _segment_rms(acc, seg_ref, inv_width):
    ss = jnp.dot((acc * acc).astype(BF16), seg_ref[...], preferred_element_type=F32)
    return ss * inv_width


def _rope(y, rows, cos_ref, sa_ref, sb_ref):
    width = y.shape[1]
    reps = width // LANES
    c = jnp.concatenate([cos_ref[rows, :]] * reps, axis=1)
    sa = jnp.concatenate([sa_ref[rows, :]] * reps, axis=1)
    sb = jnp.concatenate([sb_ref[rows, :]] * reps, axis=1)
    half = ROPE_DIM // 2
    return y * c + pltpu.roll(y, width - half, 1) * sa + pltpu.roll(y, half, 1) * sb


def _ep_plain(acc, rows):
    return acc


def _ep_lowrank(acc, rows):
    lane = lax.broadcasted_iota(jnp.int32, acc.shape, 1)
    return jnp.where(lane < GLA_LOWRANK, acc, 0.0)


def _sigmoid(x):
    return 0.5 * jnp.tanh(0.5 * x) + 0.5


def _ep_sigmoid(acc, rows, b_ref):
    return _sigmoid(acc + b_ref[...])


def _ep_qknorm_rope(acc, rows, g_ref, seg_ref, cos_ref, sa_ref, sb_ref, *, keep_from):
    ms = _segment_rms(acc, seg_ref, 1.0 / SWA_HD)
    y = acc * lax.rsqrt(ms + EPS) * g_ref[...]
    y = _rope(y, rows, cos_ref, sa_ref, sb_ref)
    if keep_from is not None:
        col = lax.broadcasted_iota(jnp.int32, y.shape, 1)
        y = jnp.where(col < keep_from, y, acc)
    return y


def _ep_headnorm(acc, rows, g_ref, *, norm_tiles):
    y = acc * lax.rsqrt(jnp.mean(acc * acc, axis=-1, keepdims=True) + EPS) * g_ref[...]
    if norm_tiles is not None:
        y = jnp.where(pl.program_id(0) < norm_tiles, y, acc)
    return y


def _wmm_kernel(a_ref, w_ref, *rest, w_is_transposed, sub_rows, epilogue):
    extras, o_ref, wbf = rest[:-2], rest[-2], rest[-1]

    @pl.when(pl.program_id(1) == 0)
    def _():
        wbf[...] = w_ref[...].astype(BF16)

    for r0 in range(0, a_ref.shape[0], sub_rows):
        rows = slice(r0, r0 + sub_rows)
        if w_is_transposed:
            acc = lax.dot_general(a_ref[rows, :], wbf[...], (((1,), (1,)), ((), ())), preferred_element_type=F32)
        else:
            acc = jnp.dot(a_ref[rows, :], wbf[...], preferred_element_type=F32)
        o_ref[rows, :] = epilogue(acc, rows, *extras).astype(o_ref.dtype)


def _wmatmul(epilogue, a, w, col0, n_cols, w_is_transposed, extras, extra_specs, out_dtype, tm, tn, name,
             sub_rows=None):
    m, k = a.shape
    sub_rows = tm if sub_rows is None else sub_rows
    assert m % tm == 0 and n_cols % tn == 0 and tm % sub_rows == 0
    if w_is_transposed:
        assert col0 % 8 == 0
        w_spec = pl.BlockSpec((pl.Element(tn), pl.Element(k)), lambda j, i: (pl.multiple_of(col0 + j * tn, 8), 0))
        w_tile = (tn, k)
    else:
        assert col0 % tn == 0
        w_spec = pl.BlockSpec((k, tn), lambda j, i: (0, col0 // tn + j))
        w_tile = (k, tn)
    kernel = functools.partial(_wmm_kernel, w_is_transposed=w_is_transposed, sub_rows=sub_rows, epilogue=epilogue)
    return pl.pallas_call(
        kernel,
        out_shape=jax.ShapeDtypeStruct((m, n_cols), out_dtype),
        grid=(n_cols // tn, m // tm),
        in_specs=[pl.BlockSpec((tm, k), lambda j, i: (i, 0)), w_spec] + list(extra_specs),
        out_specs=pl.BlockSpec((tm, tn), lambda j, i: (i, j)),
        scratch_shapes=[pltpu.VMEM(w_tile, BF16)],
        compiler_params=_params(2),
        name=name,
    )(a, w, *extras)


def _gla_kernel(q_ref, k_ref, v_ref, gg_ref, ga_ref, wa_ref, ba_ref, gn_ref, s0_ref,
                o_ref, sout_ref, s_scr, qt_scr, ku_scr, o_scr, dec_scr, *, chunk, n_chunks):
    t = pl.program_id(1)
    tb = chunk * n_chunks
    heads = [(slice(h * GLA_DK, (h + 1) * GLA_DK), slice(h * GLA_DV, (h + 1) * GLA_DV)) for h in range(GLA_HEADS)]

    @pl.when(t == 0)
    def _():
        s_scr[...] = s0_ref[0]

    row = lax.broadcasted_iota(jnp.int32, (chunk, chunk), 0)
    col = lax.broadcasted_iota(jnp.int32, (chunk, chunk), 1)
    tril = (row >= col).astype(BF16)
    z = jnp.dot(ga_ref[...].astype(BF16), wa_ref[...], preferred_element_type=F32) + ba_ref[...]
    log_a = (jnp.minimum(z, 0.0) - jnp.log(1.0 + jnp.exp(-jnp.abs(z)))) * (1.0 / GLA_NORMALIZER)
    hi = log_a.astype(BF16)
    rest = log_a - hi.astype(F32)
    mid = rest.astype(BF16)
    lo = (rest - mid.astype(F32)).astype(BF16)
    b_parts, last_parts = [], []
    for ci in range(n_chunks):
        crows = slice(ci * chunk, (ci + 1) * chunk)
        b_c = (jnp.dot(tril, hi[crows], preferred_element_type=F32)
               + jnp.dot(tril, mid[crows], preferred_element_type=F32)
               + jnp.dot(tril, lo[crows], preferred_element_type=F32))
        b_last = b_c[chunk - 1:chunk, :]
        b_parts.append(b_c)
        last_parts.append(jnp.broadcast_to(b_last, b_c.shape))
        for h, (ks, _) in enumerate(heads):
            dec_scr[ci, h] = jnp.transpose(jnp.broadcast_to(jnp.exp(b_last[:, ks]), (LANES, GLA_DK)))
    b = jnp.concatenate(b_parts, axis=0)
    b_last_rows = jnp.concatenate(last_parts, axis=0)

    q = q_ref[...] * (GLA_DK ** -0.5)
    k = k_ref[...]
    q_t = (q * jnp.exp(b)).astype(BF16)
    k_t = (k * jnp.exp(-b)).astype(BF16)
    qt_scr[...] = q_t
    ku_scr[...] = (k * jnp.exp(b_last_rows - b)).astype(BF16)

    brow = lax.broadcasted_iota(jnp.int32, (tb, tb), 0)
    bcol = lax.broadcasted_iota(jnp.int32, (tb, tb), 1)
    mask = (bcol >= (brow & -chunk)) & (brow >= bcol)
    for ks, vs in heads:
        att = lax.dot_general(q_t[:, ks], k_t[:, ks], (((1,), (1,)), ((), ())), preferred_element_type=F32)
        att = jnp.where(mask, att, 0.0).astype(BF16)
        o_scr[:, vs] = jnp.dot(att, v_ref[:, vs].astype(BF16), preferred_element_type=F32)

    def one_chunk(ci, carry):
        rows = pl.ds(pl.multiple_of(ci * chunk, chunk), chunk)
        for h, (ks, vs) in enumerate(heads):
            s_old = s_scr[h]
            o_scr[rows, vs] += jnp.dot(qt_scr[rows, ks], s_old.astype(BF16), preferred_element_type=F32)
            decay = jnp.concatenate([dec_scr[ci, h]] * (GLA_DV // LANES), axis=1)
            s_scr[h] = decay * s_old + lax.dot_general(ku_scr[rows, ks], v_ref[rows, vs].astype(BF16),
                                                       (((0,), (0,)), ((), ())), preferred_element_type=F32)
        return carry

    lax.fori_loop(0, n_chunks, one_chunk, 0)

    for _, vs in heads:
        o = o_scr[:, vs]
        on = o * lax.rsqrt(jnp.mean(o * o, axis=-1, keepdims=True) + EPS) * gn_ref[...]
        gg = gg_ref[:, vs]
        o_ref[:, vs] = (on * (gg * _sigmoid(gg))).astype(o_ref.dtype)

    @pl.when(t == pl.num_programs(1) - 1)
    def _():
        sout_ref[0] = s_scr[...]


def _gla(qkvg, ga, wa, ba, gn, s0, batch, seq, row0, tb, chunk):
    nt = seq // tb
    base = row0 // tb
    qk_w = GLA_HEADS * GLA_DK
    v_w = GLA_HEADS * GLA_DV
    rows = lambda b, t: base + b * nt + t
    kernel = functools.partial(_gla_kernel, chunk=chunk, n_chunks=tb // chunk)
    return pl.pallas_call(
        kernel,
        out_shape=(jax.ShapeDtypeStruct((batch * seq, v_w), BF16),
                   jax.ShapeDtypeStruct((batch, GLA_HEADS, GLA_DK, GLA_DV), F32)),
        grid=(batch, nt),
        in_specs=[pl.BlockSpec((tb, qk_w), lambda b, t: (rows(b, t), 0)),
                  pl.BlockSpec((tb, qk_w), lambda b, t: (rows(b, t), 1)),
                  pl.BlockSpec((tb, v_w), lambda b, t: (rows(b, t), 1)),
                  pl.BlockSpec((tb, v_w), lambda b, t: (rows(b, t), 2)),
                  pl.BlockSpec((tb, LANES), lambda b, t: (rows(b, t), 0)),
                  pl.BlockSpec((LANES, qk_w), lambda b, t: (0, 0)),
                  pl.BlockSpec((1, qk_w), lambda b, t: (0, 0)),
                  pl.BlockSpec((1, GLA_DV), lambda b, t: (0, 0)),
                  pl.BlockSpec((1, GLA_HEADS, GLA_DK, GLA_DV), lambda b, t: (b, 0, 0, 0))],
        out_specs=(pl.BlockSpec((tb, v_w), lambda b, t: (b * nt + t, 0)),
                   pl.BlockSpec((1, GLA_HEADS, GLA_DK, GLA_DV), lambda b, t: (b, 0, 0, 0))),
        scratch_shapes=[pltpu.VMEM((GLA_HEADS, GLA_DK, GLA_DV), F32),
                        pltpu.VMEM((tb, qk_w), BF16),
                        pltpu.VMEM((tb, qk_w), BF16),
                        pltpu.VMEM((tb, v_w), F32),
                        pltpu.VMEM((tb // chunk, GLA_HEADS, GLA_DK, LANES), F32)],
        compiler_params=_params(2),
        name="gla_chunks",
    )(qkvg, qkvg, qkvg, qkvg, ga, wa, ba, gn, s0)


def _swa_kernel(sink_ref, q_ref, kp_ref, vp_ref, ko_ref, vo_ref, o_ref, *, tq, prev_from_cache):
    i = pl.program_id(1)
    nk = WINDOW + tq
    k_all = jnp.concatenate([kp_ref[...], ko_ref[...]], axis=0)
    v_all = jnp.concatenate([vp_ref[...], vo_ref[...]], axis=0)
    qc = lax.broadcasted_iota(jnp.int32, (tq, nk), 0) // CHUNK + WINDOW // CHUNK
    kcol = lax.broadcasted_iota(jnp.int32, (tq, nk), 1)
    kc = kcol // CHUNK
    valid = (kc <= qc) & (kc >= qc - WINDOW // CHUNK)
    if not prev_from_cache:
        valid = valid & ((kcol >= WINDOW) | (i > 0))
    lane = lax.broadcasted_iota(jnp.int32, (nk, LANES), 1)
    low = lane < SWA_HD
    low_q = lax.broadcasted_iota(jnp.int32, (tq, LANES), 1) < SWA_HD
    for g in range(SWA_KV_HEADS):
        slab = slice((g // 2) * LANES, (g // 2 + 1) * LANES)
        k2 = k_all[:, slab]
        v2 = v_all[:, slab]
        k2r = pltpu.roll(k2, SWA_HD, 1)
        v2r = pltpu.roll(v2, SWA_HD, 1)
        if g % 2 == 0:
            k_lo, k_hi, v_lo, v_hi = k2, k2r, v2, v2r
        else:
            k_lo, k_hi, v_lo, v_hi = k2r, k2, v2r, v2
        zero = jnp.zeros_like(k2)
        one = jnp.ones_like(k2)
        km = (jnp.where(low, k_lo, zero).astype(BF16), jnp.where(low, zero, k_hi).astype(BF16))
        vm = (jnp.where(low, v_lo, one).astype(BF16), jnp.where(low, one, v_hi).astype(BF16))
        heads = [(j, half) for j in range(SWA_GROUP // 2) for half in range(2)]
        sinks = [sink_ref[g * SWA_GROUP + 2 * j + half] for j, half in heads]
        scores = []
        for j, half in heads:
            qs = q_ref[:, (g * 4 + j) * LANES:(g * 4 + j + 1) * LANES]
            s = lax.dot_general(qs, km[half], (((1,), (1,)), ((), ())), preferred_element_type=F32)
            scores.append(jnp.where(valid, s, NEG_BIG))
        maxes = [jnp.maximum(jnp.max(s, axis=-1, keepdims=True), sk) for s, sk in zip(scores, sinks)]
        exps = [jnp.exp(s - m).astype(BF16) for s, m in zip(scores, maxes)]
        sink_terms = [jnp.exp(sk - m) for sk, m in zip(sinks, maxes)]
        for j in range(SWA_GROUP // 2):
            a_lo = jnp.dot(exps[2 * j], vm[0], preferred_element_type=F32)
            a_hi = jnp.dot(exps[2 * j + 1], vm[1], preferred_element_type=F32)
            num = jnp.where(low_q, a_lo, a_hi)
            den = pltpu.roll(jnp.where(low_q, a_hi, a_lo), SWA_HD, 1)
            den = den + jnp.where(low_q, sink_terms[2 * j], sink_terms[2 * j + 1])
            o_ref[:, (g * 4 + j) * LANES:(g * 4 + j + 1) * LANES] = (num / den).astype(o_ref.dtype)


def _swa(sinks, q, q_row0, k_prev, v_prev, prev_col, k_own, v_own, own_col, own_row0,
         batch, seq, tq, prev_from_cache):
    nt = seq // tq
    qb = q_row0 // tq
    ob = own_row0 // tq
    kv_w = SWA_KV_HEADS * SWA_HD
    if prev_from_cache:
        prev_map = lambda b, t, c: (b, c)
    else:
        per = seq // WINDOW
        prev_map = lambda b, t, c: (b * per + jnp.maximum(t * (tq // WINDOW) - 1, 0), c)
    kernel = functools.partial(_swa_kernel, tq=tq, prev_from_cache=prev_from_cache)
    return pl.pallas_call(
        kernel,
        out_shape=jax.ShapeDtypeStruct((batch * seq, SWA_HEADS * SWA_HD), BF16),
        grid=(batch, nt),
        in_specs=[pl.BlockSpec(memory_space=pltpu.SMEM),
                  pl.BlockSpec((tq, SWA_HEADS * SWA_HD), lambda b, t: (qb + b * nt + t, 0)),
                  pl.BlockSpec((WINDOW, kv_w), lambda b, t: prev_map(b, t, prev_col[0])),
                  pl.BlockSpec((WINDOW, kv_w), lambda b, t: prev_map(b, t, prev_col[1])),
                  pl.BlockSpec((tq, kv_w), lambda b, t: (ob + b * nt + t, own_col[0])),
                  pl.BlockSpec((tq, kv_w), lambda b, t: (ob + b * nt + t, own_col[1]))],
        out_specs=pl.BlockSpec((tq, SWA_HEADS * SWA_HD), lambda b, t: (b * nt + t, 0)),
        compiler_params=_params(2),
        name="swa_band",
    )(sinks, q, k_prev, v_prev, k_own, v_own)


def _mem_attn_head(q, k, v):
    s = lax.dot_general(q, k.astype(BF16), (((1,), (1,)), ((), ())),
                        preferred_element_type=F32) * (MEM_HD ** -0.5)
    m = jnp.max(s, axis=-1, keepdims=True)
    e = jnp.exp(s - m)
    p = (e / jnp.sum(e, axis=-1, keepdims=True)).astype(BF16)
    return jnp.dot(p, v.astype(BF16), preferred_element_type=F32)


def _mem_attn_kernel(q_ref, kv_ref, o_ref, kv_bf):
    @pl.when(pl.program_id(1) == 0)
    def _():
        kv_bf[...] = kv_ref[...].astype(BF16)

    width = MEM_HEADS * MEM_HD
    cols = [slice(h * MEM_HD, (h + 1) * MEM_HD) for h in range(MEM_HEADS)]
    scores = [lax.dot_general(q_ref[:, c], kv_bf[:, c], (((1,), (1,)), ((), ())),
                              preferred_element_type=F32) * (MEM_HD ** -0.5) for c in cols]
    exps = [jnp.exp(s - jnp.max(s, axis=-1, keepdims=True)) for s in scores]
    probs = [(e / jnp.sum(e, axis=-1, keepdims=True)).astype(BF16) for e in exps]
    for c, p in zip(cols, probs):
        v = kv_bf[:, width + c.start:width + c.stop]
        o_ref[:, c] = jnp.dot(p, v, preferred_element_type=F32).astype(o_ref.dtype)


def _mem_attn_cache_kernel(q_ref, k_ref, v_ref, o_ref):
    for h in range(MEM_HEADS):
        cols = slice(h * MEM_HD, (h + 1) * MEM_HD)
        o_ref[:, cols] = _mem_attn_head(q_ref[:, cols], k_ref[0, 0, :, h, :], v_ref[0, 0, :, h, :]).astype(o_ref.dtype)


def _mem_attn_cache(q, q_row0, cache_k, cache_v, batch, seq):
    qb = q_row0 // seq
    width = MEM_HEADS * MEM_HD
    cache_spec = pl.BlockSpec((1, 1, N_MEM, MEM_HEADS, MEM_HD), lambda b: (0, b, 0, 0, 0))
    return pl.pallas_call(
        _mem_attn_cache_kernel,
        out_shape=jax.ShapeDtypeStruct((batch * seq, width), BF16),
        grid=(batch,),
        in_specs=[pl.BlockSpec((seq, width), lambda b: (qb + b, 0)), cache_spec, cache_spec],
        out_specs=pl.BlockSpec((seq, width), lambda b: (b, 0)),
        compiler_params=_params(1),
        name="mem_attn_cache",
    )(q, cache_k, cache_v)


def _mem_attn(q, mem_kv, batch, seq, tq):
    nt = seq // tq
    width = MEM_HEADS * MEM_HD
    return pl.pallas_call(
        _mem_attn_kernel,
        out_shape=jax.ShapeDtypeStruct((batch * seq, width), BF16),
        grid=(batch, nt),
        in_specs=[pl.BlockSpec((tq, width), lambda b, t: (b * nt + t, 0)),
                  pl.BlockSpec((N_MEM, 2 * width), lambda b, t: (b, 0))],
        out_specs=pl.BlockSpec((tq, width), lambda b, t: (b * nt + t, 0)),
        scratch_shapes=[pltpu.VMEM((N_MEM, 2 * width), BF16)],
        compiler_params=_params(2),
        name="mem_attn",
    )(q, mem_kv)


def _merge_kernel(a0p, a0s, a1p, a1s, a2p, a2s, w_ref, g0_ref, g1_ref, g2_ref, o_ref, *, p_tiles):
    def emit(a0_ref, a1_ref, a2_ref):
        acc = g0_ref[...] * jnp.dot(a0_ref[...], w_ref[0], preferred_element_type=F32)
        acc = acc + g1_ref[...] * jnp.dot(a1_ref[...], w_ref[1], preferred_element_type=F32)
        acc = acc + g2_ref[...] * jnp.dot(a2_ref[...], w_ref[2], preferred_element_type=F32)
        o_ref[...] = acc.astype(o_ref.dtype)

    @pl.when(pl.program_id(1) < p_tiles)
    def _():
        emit(a0p, a1p, a2p)

    @pl.when(pl.program_id(1) >= p_tiles)
    def _():
        emit(a0s, a1s, a2s)


def _merge(branches_p, branches_s, w_branch, gates, tm, tn):
    n_p, d = branches_p[0].shape
    n = n_p + branches_s[0].shape[0]
    nj = d // tn
    p_tiles = n_p // tm
    ap = pl.BlockSpec((tm, d), lambda j, i: (jnp.minimum(i, p_tiles - 1), 0))
    asp = pl.BlockSpec((tm, d), lambda j, i: (jnp.maximum(i - p_tiles, 0), 0))
    operands = [a for pair in zip(branches_p, branches_s) for a in pair]
    return pl.pallas_call(
        functools.partial(_merge_kernel, p_tiles=p_tiles),
        out_shape=jax.ShapeDtypeStruct((n, d), BF16),
        grid=(nj, n // tm),
        in_specs=[ap, asp, ap, asp, ap, asp,
                  pl.BlockSpec((3, d, tn), lambda j, i: (0, 0, j), pipeline_mode=pl.Buffered(1)),
                  pl.BlockSpec((tm, tn), lambda j, i: (i, j)),
                  pl.BlockSpec((tm, tn), lambda j, i: (i, nj + j)),
                  pl.BlockSpec((tm, tn), lambda j, i: (i, 2 * nj + j))],
        out_specs=pl.BlockSpec((tm, tn), lambda j, i: (i, j)),
        compiler_params=_params(2),
        name="branch_merge",
    )(*operands, w_branch, gates, gates, gates)


def _outproj_router_kernel(m_ref, w_ref, xp_ref, xs_ref, g_ref, wr_ref, br_ref,
                           x2_ref, hf_ref, eid_ref, wt_ref, *, p_tiles):
    is_prompt = pl.program_id(0) < p_tiles
    tm, d = x2_ref.shape
    for r0 in range(0, tm, ROUTER_SUB_ROWS):
        rows = slice(r0, r0 + ROUTER_SUB_ROWS)
        acc = jnp.dot(m_ref[rows, :], w_ref[...], preferred_element_type=F32)
        x = jnp.where(is_prompt, xp_ref[rows, :], xs_ref[rows, :]) + acc
        x2_ref[rows, :] = x
        hf = x * lax.rsqrt(jnp.mean(x * x, axis=-1, keepdims=True) + EPS) * g_ref[...]
        hb = hf.astype(BF16)
        bits = pltpu.bitcast(hb.astype(F32), jnp.uint32)
        packed = bits[:, d // 2:] | (bits[:, :d // 2] >> 16)
        for s in range(TOKEN_TILE_ROWS):
            hf_ref[pl.ds(r0 * TOKEN_TILE_ROWS + s, ROUTER_SUB_ROWS, stride=TOKEN_TILE_ROWS), :] = (
                packed[:, s * LANES:(s + 1) * LANES])
        logits = jnp.dot(hb, wr_ref[...], preferred_element_type=F32) + br_ref[...]
        eid, wts = _route(logits)
        eid_ref[rows, :] = eid
        wt_ref[rows, :] = wts


def _route(logits):
    lane = lax.broadcasted_iota(jnp.int32, logits.shape, 1).astype(F32)
    big = 1e6
    is_g = lane < N_GROUPS
    lg = jnp.where(is_g, logits, NEG_BIG)
    mg = jnp.max(lg, axis=-1, keepdims=True)
    gsel = jnp.min(jnp.where(is_g & (lg == mg), lane, big), axis=-1, keepdims=True)
    g_w = 1.0 / jnp.sum(jnp.where(is_g, jnp.exp(lg - mg), 0.0), axis=-1, keepdims=True)
    e_lo = N_GROUPS + gsel * EXPERTS_PER_GROUP
    in_grp = (lane >= e_lo) & (lane < e_lo + EXPERTS_PER_GROUP)
    le = jnp.where(in_grp, logits, NEG_BIG)
    me = jnp.max(le, axis=-1, keepdims=True)
    ee = jnp.where(in_grp, jnp.exp(le - me), 0.0)
    pe = ee / jnp.sum(ee, axis=-1, keepdims=True)
    pe = jnp.where(in_grp, pe, -1.0)
    p1 = jnp.max(pe, axis=-1, keepdims=True)
    i1 = jnp.min(jnp.where(pe == p1, lane, big), axis=-1, keepdims=True)
    pe2 = jnp.where(lane == i1, -1.0, pe)
    p2 = jnp.max(pe2, axis=-1, keepdims=True)
    i2 = jnp.min(jnp.where(pe2 == p2, lane, big), axis=-1, keepdims=True)
    tot = p1 + p2
    w1 = g_w * p1 / tot
    w2 = g_w * p2 / tot
    eid = jnp.where(lane == 0.0, i1 - N_GROUPS, jnp.where(lane == 1.0, i2 - N_GROUPS, 0.0))
    return eid.astype(jnp.int32), jnp.where(lane == 0.0, w1, jnp.where(lane == 1.0, w2, 0.0))


def _outproj_router(merged, w_out, xp, xs, g, wr, br, tm):
    n, d = merged.shape
    p_tiles = xp.shape[0] // tm
    const = lambda shape: pl.BlockSpec(shape, lambda i: (0, 0), pipeline_mode=pl.Buffered(1))
    row = lambda width: pl.BlockSpec((tm, width), lambda i: (i, 0))
    return pl.pallas_call(
        functools.partial(_outproj_router_kernel, p_tiles=p_tiles),
        out_shape=(jax.ShapeDtypeStruct((n, d), F32),
                   jax.ShapeDtypeStruct((n * TOKEN_TILE_ROWS, LANES), jnp.uint32),
                   jax.ShapeDtypeStruct((n, LANES), jnp.int32),
                   jax.ShapeDtypeStruct((n, LANES), F32)),
        grid=(n // tm,),
        in_specs=[row(d), const((d, d)), *_split_specs(p_tiles, tm, d),
                  const((1, d)), const((d, LANES)), const((1, LANES))],
        out_specs=(row(d), pl.BlockSpec((tm * TOKEN_TILE_ROWS, LANES), lambda i: (i, 0)),
                   row(LANES), row(LANES)),
        compiler_params=_params(1),
        name="outproj_router",
    )(merged, w_out, xp, xs, g.reshape(1, d), wr, br)


def _moe_kernel(te_ref, nv_ref, nxt_ref, tok_ref, tok_next_ref, dst_ref, hf_hbm, wup_hbm, wdn_hbm, y_hbm,
                xg, yb, wup_f32, wdn_f32, wup_bf, wdn_bf, in_sem, out_sem, w_sem):
    t = pl.program_id(0)
    buf = t % 2
    rows_now = nv_ref[t]
    rows_next = nv_ref[t + 1]
    rows_prev = nv_ref[jnp.maximum(t - 1, 0)]
    valid = rows_now > 0
    valid_next = rows_next > 0
    new_expert = (t == 0) | (te_ref[t] != te_ref[jnp.maximum(t - 1, 0)])

    def weight_copies(e):
        return (pltpu.make_async_copy(wup_hbm.at[e], wup_f32, w_sem.at[0]),
                pltpu.make_async_copy(wdn_hbm.at[e], wdn_f32, w_sem.at[1]))

    in_rows, in_pitch = TOKEN_TILE_ROWS, TOKEN_TILE_PITCH
    out_rows, out_pitch = TOKEN_F32_ROWS, TOKEN_F32_PITCH

    def row_in(row0, r, b):
        return pltpu.make_async_copy(hf_hbm.at[pl.ds(pl.multiple_of(row0, in_rows), in_rows), :],
                                     xg.at[b, pl.ds(r * in_pitch, in_rows), :], in_sem.at[b])

    def row_out(row0, r, b):
        return pltpu.make_async_copy(yb.at[b, pl.ds(r * out_pitch, out_rows), :],
                                     y_hbm.at[pl.ds(pl.multiple_of(row0, out_rows), out_rows), :], out_sem.at[b])

    group = MOE_ROW_GROUP
    n_groups = MOE_TILE // group

    def group_in(b):
        return pltpu.make_async_copy(hf_hbm.at[pl.ds(0, group * in_rows), :],
                                     xg.at[b, pl.ds(0, group * in_rows), :], in_sem.at[b])

    def group_out(b):
        return pltpu.make_async_copy(yb.at[b, pl.ds(0, group * out_rows), :],
                                     y_hbm.at[pl.ds(0, group * out_rows), :], out_sem.at[b])

    def per_started_group(rows, fn):
        for g in range(n_groups):
            pl.when(rows > g * group)(functools.partial(fn, g))

    @pl.when(t == 0)
    def _():
        for c in weight_copies(te_ref[0]):
            c.start(priority=1)
        xg[...] = jnp.zeros_like(xg)

        def first(r, c):
            row_in(tok_ref[0, 0, r], r, 0).start()
            return c
        lax.fori_loop(0, ((rows_now + group - 1) // group) * group, first, 0)
        yb[...] = jnp.zeros_like(yb)
        n_real = y_hbm.shape[0] - 2 * MOE_TILE * out_rows
        for b in range(2):
            spare = pltpu.make_async_copy(
                yb.at[b, pl.ds(0, MOE_TILE * out_rows), :],
                y_hbm.at[pl.ds(n_real + b * MOE_TILE * out_rows, MOE_TILE * out_rows), :], out_sem.at[b])
            spare.start()
            spare.wait()

    @pl.when(valid & new_expert)
    def _():
        for c in weight_copies(te_ref[t]):
            c.wait()
        wup_bf[...] = wup_f32[...].astype(BF16)
        wdn_bf[...] = wdn_f32[...].astype(BF16)

        @pl.when(nxt_ref[t] >= 0)
        def _():
            for c in weight_copies(nxt_ref[t]):
                c.start(priority=1)

    def gather_next(g):
        for r in range(g * group, (g + 1) * group):
            row_in(tok_next_ref[0, 0, r], r, 1 - buf).start()

    def scatter_now(g):
        for r in range(g * group, (g + 1) * group):
            row_out(dst_ref[0, 0, r], r, buf).start(priority=r % 2)

    per_started_group(rows_next, gather_next)

    @pl.when(valid)
    def _():
        per_started_group(rows_now, lambda g: group_in(buf).wait())
        lo, hi = [], []
        for s in range(in_rows):
            word = xg[buf, pl.ds(s, MOE_TILE, stride=in_pitch), :]
            lo.append(pltpu.bitcast(word << 16, F32).astype(BF16))
            hi.append(pltpu.bitcast(word & jnp.uint32(0xFFFF0000), F32).astype(BF16))
        x = jnp.concatenate(lo + hi, axis=1)
        h1 = jnp.dot(x, wup_bf[...], preferred_element_type=F32)
        gate = h1[:, :D_FF]
        up = h1[:, D_FF:]
        act = (gate * _sigmoid(gate)) * up
        ye = jnp.dot(act.astype(BF16), wdn_bf[...], preferred_element_type=F32)
        for s in range(out_rows):
            yb[buf, pl.ds(s, MOE_TILE, stride=out_pitch), :] = ye[:, s * LANES:(s + 1) * LANES]
        per_started_group(rows_now, scatter_now)

        @pl.when(t > 0)
        def _():
            per_started_group(rows_prev, lambda g: group_out(1 - buf).wait())

        @pl.when(jnp.logical_not(valid_next))
        def _():
            per_started_group(rows_now, lambda g: group_out(buf).wait())


def _moe(hf, tile_expert, tile_rows, next_expert, tok_slots, dst_slots, w_up, w_down, out_tokens):
    n_tiles = tile_expert.shape[0]
    d = w_up.shape[1]
    slot_spec = lambda off: pl.BlockSpec((1, 1, MOE_TILE),
                                         lambda t, te, nv, ne: (jnp.minimum(t + off, n_tiles - 1), 0, 0),
                                         memory_space=pltpu.SMEM)
    hbm = pl.BlockSpec(memory_space=pl.ANY)
    grid_spec = pltpu.PrefetchScalarGridSpec(
        num_scalar_prefetch=3,
        grid=(n_tiles,),
        in_specs=[slot_spec(0), slot_spec(1), slot_spec(0), hbm, hbm, hbm],
        out_specs=hbm,
        scratch_shapes=[pltpu.VMEM((2, MOE_TILE * TOKEN_TILE_PITCH, LANES), jnp.uint32),
                        pltpu.VMEM((2, MOE_TILE * TOKEN_F32_PITCH, LANES), F32),
                        pltpu.VMEM((d, 2 * D_FF), F32),
                        pltpu.VMEM((D_FF, d), F32),
                        pltpu.VMEM((d, 2 * D_FF), BF16),
                        pltpu.VMEM((D_FF, d), BF16),
                        pltpu.SemaphoreType.DMA((2,)),
                        pltpu.SemaphoreType.DMA((2,)),
                        pltpu.SemaphoreType.DMA((2,))],
    )
    return pl.pallas_call(
        _moe_kernel,
        out_shape=jax.ShapeDtypeStruct((out_tokens * TOKEN_F32_ROWS, LANES), F32),
        grid_spec=grid_spec,
        compiler_params=_params(1),
        name="moe_experts",
    )(tile_expert, tile_rows, next_expert, tok_slots, tok_slots, dst_slots, hf, w_up, w_down)


def _combine_kernel(x_ref, y0_ref, y1_ref, w_ref, op_ref, os_ref, *, p_tiles):
    tm = x_ref.shape[0]
    w = w_ref[...]
    w0 = w[:, 0:1]
    w1 = w[:, 1:2]

    def emit(o_ref):
        for s in range(TOKEN_F32_ROWS):
            cols = slice(s * LANES, (s + 1) * LANES)
            rows = pl.ds(s, tm, stride=TOKEN_F32_ROWS)
            o_ref[:, cols] = x_ref[:, cols] + (y0_ref[rows, :] * w0 + y1_ref[rows, :] * w1)

    @pl.when(pl.program_id(0) < p_tiles)
    def _():
        emit(op_ref)

    @pl.when(pl.program_id(0) >= p_tiles)
    def _():
        emit(os_ref)


def _combine(x2, yk, wts, n_p, tm):
    n, d = x2.shape
    p_tiles = n_p // tm
    k1 = n // tm
    y_rows = tm * TOKEN_F32_ROWS
    return pl.pallas_call(
        functools.partial(_combine_kernel, p_tiles=p_tiles),
        out_shape=(jax.ShapeDtypeStruct((n_p, d), F32), jax.ShapeDtypeStruct((n - n_p, d), F32)),
        grid=(n // tm,),
        in_specs=[pl.BlockSpec((tm, d), lambda i: (i, 0)),
                  pl.BlockSpec((y_rows, LANES), lambda i: (i, 0)),
                  pl.BlockSpec((y_rows, LANES), lambda i: (k1 + i, 0)),
                  pl.BlockSpec((tm, LANES), lambda i: (i, 0))],
        out_specs=_split_specs(p_tiles, tm, d),
        compiler_params=_params(1),
        name="moe_combine",
    )(x2, yk, yk, wts)


def _rope_tables(pos):
    half = ROPE_DIM // 2
    inv = ROPE_THETA ** (-jnp.arange(half, dtype=F32) / half)
    ang = pos.astype(F32)[:, None] * inv[None, :]
    cos, sin = jnp.cos(ang), jnp.sin(ang)
    n = pos.shape[0]
    pad = jnp.zeros((n, SWA_HD - ROPE_DIM), F32)
    cos_h = jnp.concatenate([cos, cos, pad + 1.0], axis=1)
    sa_h = jnp.concatenate([-sin, jnp.zeros_like(sin), pad], axis=1)
    sb_h = jnp.concatenate([jnp.zeros_like(sin), sin, pad], axis=1)
    reps = LANES // SWA_HD
    return tuple(jnp.tile(a, (1, reps)) for a in (cos_h, sa_h, sb_h))


def _moe_schedule(eid, n_tok, n_tiles):
    a = eid.shape[0]
    order = jnp.argsort(eid, stable=True).astype(jnp.int32)
    counts = jnp.bincount(eid, length=N_EXPERTS).astype(jnp.int32)
    tiles_per = (counts + MOE_TILE - 1) // MOE_TILE
    tile_end = jnp.cumsum(tiles_per)
    tile_start = tile_end - tiles_per
    sorted_start = jnp.cumsum(counts) - counts
    tile_id = jnp.arange(n_tiles, dtype=jnp.int32)
    used = tile_id < tile_end[-1]
    te = jnp.minimum(jnp.sum(tile_end[None, :] <= tile_id[:, None], axis=1), N_EXPERTS - 1).astype(jnp.int32)
    last_used_e = te[jnp.maximum(tile_end[-1] - 1, 0)]
    te = jnp.where(used, te, last_used_e)
    row_in_expert = (tile_id - tile_start[te]) * MOE_TILE
    rows_valid = jnp.where(used, jnp.clip(counts[te] - row_in_expert, 0, MOE_TILE), 0).astype(jnp.int32)
    r = jnp.arange(MOE_TILE, dtype=jnp.int32)[None, :]
    src = sorted_start[te][:, None] + row_in_expert[:, None] + r
    real = r < rows_valid[:, None]
    assign = order[jnp.clip(src, 0, a - 1)]
    tok = jnp.where(real, assign % n_tok, 0) * TOKEN_TILE_ROWS
    spare = a + (tile_id[:, None] % 2) * MOE_TILE + r
    dst = jnp.where(real, assign, spare) * TOKEN_F32_ROWS
    tile_rows = jnp.concatenate([rows_valid, jnp.zeros((1,), jnp.int32)])
    e_id = jnp.arange(N_EXPERTS, dtype=jnp.int32)[None, :]
    later = (e_id > te[:, None]) & (counts[None, :] > 0)
    nxt = jnp.min(jnp.where(later, e_id, N_EXPERTS), axis=1)
    nxt = jnp.where(nxt < N_EXPERTS, nxt, -1).astype(jnp.int32)
    shape = (n_tiles, 1, MOE_TILE)
    return te, tile_rows, nxt, tok.reshape(shape).astype(jnp.int32), dst.reshape(shape).astype(jnp.int32)


def kernel(x_prompt, x_sample, state_gla, cache_swa_k, cache_swa_v, cache_mem_k, cache_mem_v,
           mem_prompt, norm_mix_g, w_in, w_a2, b_a2, gla_norm_g, swa_q_norm_g, swa_k_norm_g,
           swa_sinks, norm_mem_g, w_mem_kv, mem_q_norm_g, mem_k_norm_g, w_gate, b_gate,
           w_branch, w_out, norm_ffn_g, w_router_group, b_router_group, w_router_expert,
           b_router_expert, w_up, w_down):
    bp, tp, d = x_prompt.shape
    bs, ts, _ = x_sample.shape
    n_p, n_s = bp * tp, bs * ts
    n = n_p + n_s
    tm, tn = ROW_TILE, COL_TILE
    tp_rows = PROJ_ROWS if n % PROJ_ROWS == 0 else tm
    heavy_sub = tp_rows // 2
    assert d == D_MODEL and n_p % tm == 0 and n_s % tm == 0 and w_in.shape[0] == 1
    keep_s = cache_swa_k.shape[2]
    assert keep_s == WINDOW and tp % WINDOW == 0

    qk_w = GLA_HEADS * GLA_DK
    v_w = GLA_HEADS * GLA_DV
    c0 = 2 * qk_w + 2 * v_w
    sq_w = SWA_HEADS * SWA_HD
    kv_w = SWA_KV_HEADS * SWA_HD
    mem_w = MEM_HEADS * MEM_HD
    w_in_t = jnp.transpose(w_in[0])
    w_branch_b = w_branch[0].astype(BF16)
    w_out_b = w_out[0].astype(BF16)
    w_a2_b = jnp.pad(w_a2[0], ((0, LANES - GLA_LOWRANK), (0, 0))).astype(BF16)
    w_router = jnp.pad(jnp.concatenate([w_router_group[0], w_router_expert[0]], axis=1),
                       ((0, 0), (0, LANES - N_GROUPS - N_EXPERTS))).astype(BF16)
    b_router = jnp.pad(jnp.concatenate([b_router_group[0], b_router_expert[0]]),
                       (0, LANES - N_GROUPS - N_EXPERTS)).reshape(1, LANES)

    pos = jnp.concatenate([jnp.tile(jnp.arange(tp, dtype=jnp.int32), bp),
                           jnp.tile(PAST_LEN + jnp.arange(ts, dtype=jnp.int32), bs)])
    cos_t, sa_t, sb_t = _rope_tables(pos)
    seg_id = jnp.arange(tn, dtype=jnp.int32) // SWA_HD
    seg = (seg_id[:, None] == seg_id[None, :]).astype(BF16)
    rope_specs = [pl.BlockSpec((tp_rows, LANES), lambda j, i: (i, 0))] * 3
    row_vec = lambda width: pl.BlockSpec((1, width), lambda j, i: (0, 0))
    seg_spec = pl.BlockSpec((tn, tn), lambda j, i: (0, 0))

    xp2 = x_prompt.reshape(n_p, d)
    xs2 = x_sample.reshape(n_s, d)
    h = _rms_norm_two(xp2, xs2, norm_mix_g[0], tm)

    c1 = c0 + GLA_LOWRANK
    c2 = c1 + sq_w
    c3 = c2 + 2 * kv_w
    qkvg = _wmatmul(_ep_plain, h, w_in_t, 0, c0, True, [], [], F32, tp_rows, WIDE_COLS, "proj_gla",
                    sub_rows=heavy_sub)
    ga = _wmatmul(_ep_lowrank, h, w_in_t, c0, LANES, True, [], [], F32, tp_rows, LANES, "proj_gla_lowrank")
    q_gain = jnp.tile(swa_q_norm_g[0] * (SWA_HD ** -0.5), tn // SWA_HD).reshape(1, tn)
    q_swa = _wmatmul(functools.partial(_ep_qknorm_rope, keep_from=None), h, w_in_t, c1, sq_w, True,
                     [q_gain, seg, cos_t, sa_t, sb_t], [row_vec(tn), seg_spec] + rope_specs,
                     BF16, tp_rows, tn, "proj_swa_q", sub_rows=heavy_sub)
    k_gain = jnp.tile(swa_k_norm_g[0], tn // SWA_HD).reshape(1, tn)
    kv_swa = _wmatmul(functools.partial(_ep_qknorm_rope, keep_from=kv_w), h, w_in_t, c2, 2 * kv_w, True,
                      [k_gain, seg, cos_t, sa_t, sb_t], [row_vec(tn), seg_spec] + rope_specs,
                      F32, tp_rows, tn, "proj_swa_kv", sub_rows=heavy_sub)
    q_mem = _wmatmul(functools.partial(_ep_headnorm, norm_tiles=None), h, w_in_t, c3, mem_w, True,
                     [mem_q_norm_g[0].reshape(1, MEM_HD)], [row_vec(MEM_HD)],
                     BF16, tp_rows, MEM_HD, "proj_mem_q", sub_rows=heavy_sub)
    gates = _wmatmul(_ep_sigmoid, h, w_gate[0], 0, 3 * d, False, [b_gate[0].reshape(1, -1)],
                     [pl.BlockSpec((1, WIDE_COLS), lambda j, i: (0, j))], F32, tp_rows, WIDE_COLS, "proj_gates",
                     sub_rows=heavy_sub)

    mem_rows = bp * N_MEM
    hm = _rms_norm_rows(mem_prompt.reshape(mem_rows, d), norm_mem_g[0], BF16, N_MEM)
    mem_kv = _wmatmul(functools.partial(_ep_headnorm, norm_tiles=MEM_HEADS), hm, w_mem_kv[0], 0, 2 * mem_w, False,
                      [mem_k_norm_g[0].reshape(1, MEM_HD)], [row_vec(MEM_HD)], F32, mem_rows, MEM_HD, "mem_kv")

    ba = b_a2[0].reshape(1, qk_w)
    gn = gla_norm_g[0].reshape(1, GLA_DV)
    s0_p = jnp.zeros((bp, GLA_HEADS, GLA_DK, GLA_DV), F32)
    o_gla_p, gla_state_p = _gla(qkvg, ga, w_a2_b, ba, gn, s0_p, bp, tp, 0, 256, CHUNK)
    o_gla_s, gla_state_s = _gla(qkvg, ga, w_a2_b, ba, gn, state_gla[0], bs, ts, n_p, ts, min(CHUNK, ts))

    sinks = swa_sinks[0]
    o_swa_p = _swa(sinks, q_swa, 0, kv_swa, kv_swa, (0, 1), kv_swa, kv_swa, (0, 1), 0,
                   bp, tp, WINDOW, False)
    ck = cache_swa_k[0].reshape(bs * keep_s, kv_w)
    cv = cache_swa_v[0].reshape(bs * keep_s, kv_w)
    o_swa_s = _swa(sinks, q_swa, n_p, ck, cv, (0, 0), kv_swa, kv_swa, (0, 1), n_p,
                   bs, ts, ts, True)

    o_mem_p = _mem_attn(q_mem, mem_kv, bp, tp, tm)
    o_mem_s = _mem_attn_cache(q_mem, n_p, cache_mem_k, cache_mem_v, bs, ts)

    merged = _merge((o_gla_p, o_swa_p, o_mem_p), (o_gla_s, o_swa_s, o_mem_s), w_branch_b, gates, tm, WIDE_COLS)

    x2, hf, eid, wts = _outproj_router(merged, w_out_b, xp2, xs2, norm_ffn_g[0], w_router, b_router, tm)

    n_assign = TOP_K * n
    n_tiles = n_assign // MOE_TILE + N_EXPERTS
    eid_kmajor = jnp.concatenate([eid[:, k] for k in range(TOP_K)])
    tile_expert, tile_rows, next_expert, tok_slots, dst_slots = _moe_schedule(eid_kmajor, n, n_tiles)
    yk = _moe(hf, tile_expert, tile_rows, next_expert, tok_slots, dst_slots, w_up[0], w_down[0],
              n_assign + 2 * MOE_TILE)
    y_p, y_s = _combine(x2, yk, wts, n_p, tm)

    y_p = y_p.reshape(bp, tp, d)
    y_s = y_s.reshape(bs, ts, d)
    kv_p = jnp.stack([kv_swa[(b + 1) * tp - WINDOW:(b + 1) * tp] for b in range(bp)])
    kv_p = kv_p.reshape(bp, WINDOW, 2, SWA_KV_HEADS, SWA_HD)
    kv_s = kv_swa[n_p:].reshape(bs, ts, 2, SWA_KV_HEADS, SWA_HD)
    swk_s = jnp.concatenate([cache_swa_k[0], kv_s[:, :, 0]], axis=1)[:, ts:ts + keep_s]
    swv_s = jnp.concatenate([cache_swa_v[0], kv_s[:, :, 1]], axis=1)[:, ts:ts + keep_s]
    mk_p = mem_kv[:, :mem_w].reshape(bp, N_MEM, MEM_HEADS, MEM_HD)
    mv_p = mem_kv[:, mem_w:].reshape(bp, N_MEM, MEM_HEADS, MEM_HD)
    return (y_p, y_s, gla_state_p[None], kv_p[:, :, 0][None], kv_p[:, :, 1][None], mk_p[None], mv_p[None],
            gla_state_s[None], swk_s[None], swv_s[None])
```

```python
import functools

import jax
import jax.numpy as jnp
from jax import lax
from jax.experimental import pallas as pl
from jax.experimental.pallas import tpu as pltpu

F32 = jnp.float32
BF16 = jnp.bfloat16

D_MODEL = 2048
CHUNK = 64
EPS = 1e-6
PAST_LEN = 1024
GLA_HEADS = 4
GLA_DV = 512
GLA_DK = 256
GLA_LOWRANK = 16
GLA_NORMALIZER = 16.0
SWA_HD = 64
SWA_HEADS = 32
SWA_KV_HEADS = 4
SWA_GROUP = 8
WINDOW = 128
ROPE_DIM = 16
ROPE_THETA = 500000.0
N_MEM = 256
MEM_HEADS = 4
MEM_HD = 512
N_GROUPS = 8
EXPERTS_PER_GROUP = 8
N_EXPERTS = 64
TOP_K = 2
D_FF = 512

LANES = 128
VMEM_LIMIT = 56 * 1024 * 1024
ROW_TILE = 512
PROJ_ROWS = 1536
COL_TILE = 512
WIDE_COLS = 1024
ROUTER_SUB_ROWS = 256
MOE_TILE = 256
MOE_ROW_GROUP = 32
TOKEN_TILE_ROWS = D_MODEL // 2 // LANES
TOKEN_TILE_PITCH = 12
TOKEN_F32_ROWS = D_MODEL // LANES
TOKEN_F32_PITCH = 20
NEG_BIG = -1e30


def _params(n_axes):
    return pltpu.CompilerParams(dimension_semantics=("arbitrary",) * n_axes,
                                vmem_limit_bytes=VMEM_LIMIT)


def _norm_kernel(x_ref, g_ref, o_ref):
    x = x_ref[...]
    y = x * lax.rsqrt(jnp.mean(x * x, axis=-1, keepdims=True) + EPS)
    o_ref[...] = (y * g_ref[...]).astype(o_ref.dtype)


def _rms_norm_rows(x, g, out_dtype, tm):
    n, d = x.shape
    return pl.pallas_call(
        _norm_kernel,
        out_shape=jax.ShapeDtypeStruct((n, d), out_dtype),
        grid=(n // tm,),
        in_specs=[pl.BlockSpec((tm, d), lambda i: (i, 0)),
                  pl.BlockSpec((1, d), lambda i: (0, 0))],
        out_specs=pl.BlockSpec((tm, d), lambda i: (i, 0)),
        compiler_params=_params(1),
        name="rms_norm_rows",
    )(x, g.reshape(1, d))


def _norm2_kernel(xp_ref, xs_ref, g_ref, o_ref, *, p_tiles):
    def emit(x_ref):
        x = x_ref[...]
        y = x * lax.rsqrt(jnp.mean(x * x, axis=-1, keepdims=True) + EPS)
        o_ref[...] = (y * g_ref[...]).astype(o_ref.dtype)

    @pl.when(pl.program_id(0) < p_tiles)
    def _():
        emit(xp_ref)

    @pl.when(pl.program_id(0) >= p_tiles)
    def _():
        emit(xs_ref)


def _split_specs(p_tiles, tm, d):
    return (pl.BlockSpec((tm, d), lambda i: (jnp.minimum(i, p_tiles - 1), 0)),
            pl.BlockSpec((tm, d), lambda i: (jnp.maximum(i - p_tiles, 0), 0)))


def _rms_norm_two(xp, xs, g, tm):
    (n_p, d), n_s = xp.shape, xs.shape[0]
    p_tiles = n_p // tm
    return pl.pallas_call(
        functools.partial(_norm2_kernel, p_tiles=p_tiles),
        out_shape=jax.ShapeDtypeStruct((n_p + n_s, d), BF16),
        grid=((n_p + n_s) // tm,),
        in_specs=[*_split_specs(p_tiles, tm, d), pl.BlockSpec((1, d), lambda i: (0, 0))],
        out_specs=pl.BlockSpec((tm, d), lambda i: (i, 0)),
        compiler_params=_params(1),
        name="rms_norm_mix",
    )(xp, xs, g.reshape(1, d))


def _segment_rms(acc, seg_ref, inv_width):
    ss = jnp.dot((acc * acc).astype(BF16), seg_ref[...], preferred_element_type=F32)
    return ss * inv_width


def _rope(y, rows, cos_ref, sa_ref, sb_ref):
    width = y.shape[1]
    reps = width // LANES
    c = jnp.concatenate([cos_ref[rows, :]] * reps, axis=1)
    sa = jnp.concatenate([sa_ref[rows, :]] * reps, axis=1)
    sb = jnp.concatenate([sb_ref[rows, :]] * reps, axis=1)
    half = ROPE_DIM // 2
    return y * c + pltpu.roll(y, width - half, 1) * sa + pltpu.roll(y, half, 1) * sb


def _ep_plain(acc, rows):
    return acc


def _ep_lowrank(acc, rows):
    lane = lax.broadcasted_iota(jnp.int32, acc.shape, 1)
    return jnp.where(lane < GLA_LOWRANK, acc, 0.0)


def _sigmoid(x):
    return 0.5 * jnp.tanh(0.5 * x) + 0.5


def _ep_sigmoid(acc, rows, b_ref):
    return _sigmoid(acc + b_ref[...])


def _ep_qknorm_rope(acc, rows, g_ref, seg_ref, cos_ref, sa_ref, sb_ref, *, keep_from):
    ms = _segment_rms(acc, seg_ref, 1.0 / SWA_HD)
    y = acc * lax.rsqrt(ms + EPS) * g_ref[...]
    y = _rope(y, rows, cos_ref, sa_ref, sb_ref)
    if keep_from is not None:
        col = lax.broadcasted_iota(jnp.int32, y.shape, 1)
        y = jnp.where(col < keep_from, y, acc)
    return y


def _ep_headnorm(acc, rows, g_ref, *, norm_tiles):
    y = acc * lax.rsqrt(jnp.mean(acc * acc, axis=-1, keepdims=True) + EPS) * g_ref[...]
    if norm_tiles is not None:
        y = jnp.where(pl.program_id(0) < norm_tiles, y, acc)
    return y


def _wmm_kernel(a_ref, w_ref, *rest, w_is_transposed, sub_rows, epilogue):
    extras, o_ref, wbf = rest[:-2], rest[-2], rest[-1]

    @pl.when(pl.program_id(1) == 0)
    def _():
        wbf[...] = w_ref[...].astype(BF16)

    for r0 in range(0, a_ref.shape[0], sub_rows):
        rows = slice(r0, r0 + sub_rows)
        if w_is_transposed:
            acc = lax.dot_general(a_ref[rows, :], wbf[...], (((1,), (1,)), ((), ())), preferred_element_type=F32)
        else:
            acc = jnp.dot(a_ref[rows, :], wbf[...], preferred_element_type=F32)
        o_ref[rows, :] = epilogue(acc, rows, *extras).astype(o_ref.dtype)


def _wmatmul(epilogue, a, w, col0, n_cols, w_is_transposed, extras, extra_specs, out_dtype, tm, tn, name,
             sub_rows=None):
    m, k = a.shape
    sub_rows = tm if sub_rows is None else sub_rows
    assert m % tm == 0 and n_cols % tn == 0 and tm % sub_rows == 0
    if w_is_transposed:
        assert col0 % 8 == 0
        w_spec = pl.BlockSpec((pl.Element(tn), pl.Element(k)), lambda j, i: (pl.multiple_of(col0 + j * tn, 8), 0))
        w_tile = (tn, k)
    else:
        assert col0 % tn == 0
        w_spec = pl.BlockSpec((k, tn), lambda j, i: (0, col0 // tn + j))
        w_tile = (k, tn)
    kernel = functools.partial(_wmm_kernel, w_is_transposed=w_is_transposed, sub_rows=sub_rows, epilogue=epilogue)
    return pl.pallas_call(
        kernel,
        out_shape=jax.ShapeDtypeStruct((m, n_cols), out_dtype),
        grid=(n_cols // tn, m // tm),
        in_specs=[pl.BlockSpec((tm, k), lambda j, i: (i, 0)), w_spec] + list(extra_specs),
        out_specs=pl.BlockSpec((tm, tn), lambda j, i: (i, j)),
        scratch_shapes=[pltpu.VMEM(w_tile, BF16)],
        compiler_params=_params(2),
        name=name,
    )(a, w, *extras)


def _gla_kernel(q_ref, k_ref, v_ref, gg_ref, ga_ref, wa_ref, ba_ref, gn_ref, s0_ref,
                o_ref, sout_ref, s_scr, qt_scr, ku_scr, o_scr, dec_scr, *, chunk, n_chunks):
    t = pl.program_id(1)
    tb = chunk * n_chunks
    heads = [(slice(h * GLA_DK, (h + 1) * GLA_DK), slice(h * GLA_DV, (h + 1) * GLA_DV)) for h in range(GLA_HEADS)]

    @pl.when(t == 0)
    def _():
        s_scr[...] = s0_ref[0]

    row = lax.broadcasted_iota(jnp.int32, (chunk, chunk), 0)
    col = lax.broadcasted_iota(jnp.int32, (chunk, chunk), 1)
    tril = (row >= col).astype(BF16)
    z = jnp.dot(ga_ref[...].astype(BF16), wa_ref[...], preferred_element_type=F32) + ba_ref[...]
    log_a = (jnp.minimum(z, 0.0) - jnp.log(1.0 + jnp.exp(-jnp.abs(z)))) * (1.0 / GLA_NORMALIZER)
    hi = log_a.astype(BF16)
    rest = log_a - hi.astype(F32)
    mid = rest.astype(BF16)
    lo = (rest - mid.astype(F32)).astype(BF16)
    b_parts, last_parts = [], []
    for ci in range(n_chunks):
        crows = slice(ci * chunk, (ci + 1) * chunk)
        b_c = (jnp.dot(tril, hi[crows], preferred_element_type=F32)
               + jnp.dot(tril, mid[crows], preferred_element_type=F32)
               + jnp.dot(tril, lo[crows], preferred_element_type=F32))
        b_last = b_c[chunk - 1:chunk, :]
        b_parts.append(b_c)
        last_parts.append(jnp.broadcast_to(b_last, b_c.shape))
        for h, (ks, _) in enumerate(heads):
            dec_scr[ci, h] = jnp.transpose(jnp.broadcast_to(jnp.exp(b_last[:, ks]), (LANES, GLA_DK)))
    b = jnp.concatenate(b_parts, axis=0)
    b_last_rows = jnp.concatenate(last_parts, axis=0)

    q = q_ref[...] * (GLA_DK ** -0.5)
    k = k_ref[...]
    q_t = (q * jnp.exp(b)).astype(BF16)
    k_t = (k * jnp.exp(-b)).astype(BF16)
    qt_scr[...] = q_t
    ku_scr[...] = (k * jnp.exp(b_last_rows - b)).astype(BF16)

    brow = lax.broadcasted_iota(jnp.int32, (tb, tb), 0)
    bcol = lax.broadcasted_iota(jnp.int32, (tb, tb), 1)
    mask = (bcol >= (brow & -chunk)) & (brow >= bcol)
    for ks, vs in heads:
        att = lax.dot_general(q_t[:, ks], k_t[:, ks], (((1,), (1,)), ((), ())), preferred_element_type=F32)
        att = jnp.where(mask, att, 0.0).astype(BF16)
        o_scr[:, vs] = jnp.dot(att, v_ref[:, vs].astype(BF16), preferred_element_type=F32)

    def one_chunk(ci, carry):
        rows = pl.ds(pl.multiple_of(ci * chunk, chunk), chunk)
        for h, (ks, vs) in enumerate(heads):
            s_old = s_scr[h]
            o_scr[rows, vs] += jnp.dot(qt_scr[rows, ks], s_old.astype(BF16), preferred_element_type=F32)
            decay = jnp.concatenate([dec_scr[ci, h]] * (GLA_DV // LANES), axis=1)
            s_scr[h] = decay * s_old + lax.dot_general(ku_scr[rows, ks], v_ref[rows, vs].astype(BF16),
                                                       (((0,), (0,)), ((), ())), preferred_element_type=F32)
        return carry

    lax.fori_loop(0, n_chunks, one_chunk, 0)

    for _, vs in heads:
        o = o_scr[:, vs]
        on = o * lax.rsqrt(jnp.mean(o * o, axis=-1, keepdims=True) + EPS) * gn_ref[...]
        gg = gg_ref[:, vs]
        o_ref[:, vs] = (on * (gg * _sigmoid(gg))).astype(o_ref.dtype)

    @pl.when(t == pl.num_programs(1) - 1)
    def _():
        sout_ref[0] = s_scr[...]


def _gla(qkvg, ga, wa, ba, gn, s0, batch, seq, row0, tb, chunk):
    nt = seq // tb
    base = row0 // tb
    qk_w = GLA_HEADS * GLA_DK
    v_w = GLA_HEADS * GLA_DV
    rows = lambda b, t: base + b * nt + t
    kernel = functools.partial(_gla_kernel, chunk=chunk, n_chunks=tb // chunk)
    return pl.pallas_call(
        kernel,
        out_shape=(jax.ShapeDtypeStruct((batch * seq, v_w), BF16),
                   jax.ShapeDtypeStruct((batch, GLA_HEADS, GLA_DK, GLA_DV), F32)),
        grid=(batch, nt),
        in_specs=[pl.BlockSpec((tb, qk_w), lambda b, t: (rows(b, t), 0)),
                  pl.BlockSpec((tb, qk_w), lambda b, t: (rows(b, t), 1)),
                  pl.BlockSpec((tb, v_w), lambda b, t: (rows(b, t), 1)),
                  pl.BlockSpec((tb, v_w), lambda b, t: (rows(b, t), 2)),
                  pl.BlockSpec((tb, LANES), lambda b, t: (rows(b, t), 0)),
                  pl.BlockSpec((LANES, qk_w), lambda b, t: (0, 0)),
                  pl.BlockSpec((1, qk_w), lambda b, t: (0, 0)),
                  pl.BlockSpec((1, GLA_DV), lambda b, t: (0, 0)),
                  pl.BlockSpec((1, GLA_HEADS, GLA_DK, GLA_DV), lambda b, t: (b, 0, 0, 0))],
        out_specs=(pl.BlockSpec((tb, v_w), lambda b, t: (b * nt + t, 0)),
                   pl.BlockSpec((1, GLA_HEADS, GLA_DK, GLA_DV), lambda b, t: (b, 0, 0, 0))),
        scratch_shapes=[pltpu.VMEM((GLA_HEADS, GLA_DK, GLA_DV), F32),
                        pltpu.VMEM((tb, qk_w), BF16),
                        pltpu.VMEM((tb, qk_w), BF16),
                        pltpu.VMEM((tb, v_w), F32),
                        pltpu.VMEM((tb // chunk, GLA_HEADS, GLA_DK, LANES), F32)],
        compiler_params=_params(2),
        name="gla_chunks",
    )(qkvg, qkvg, qkvg, qkvg, ga, wa, ba, gn, s0)


def _swa_kernel(sink_ref, q_ref, kp_ref, vp_ref, ko_ref, vo_ref, o_ref, *, tq, prev_from_cache):
    i = pl.program_id(1)
    nk = WINDOW + tq
    k_all = jnp.concatenate([kp_ref[...], ko_ref[...]], axis=0)
    v_all = jnp.concatenate([vp_ref[...], vo_ref[...]], axis=0)
    qc = lax.broadcasted_iota(jnp.int32, (tq, nk), 0) // CHUNK + WINDOW // CHUNK
    kcol = lax.broadcasted_iota(jnp.int32, (tq, nk), 1)
    kc = kcol // CHUNK
    valid = (kc <= qc) & (kc >= qc - WINDOW // CHUNK)
    if not prev_from_cache:
        valid = valid & ((kcol >= WINDOW) | (i > 0))
    lane = lax.broadcasted_iota(jnp.int32, (nk, LANES), 1)
    low = lane < SWA_HD
    low_q = lax.broadcasted_iota(jnp.int32, (tq, LANES), 1) < SWA_HD
    for g in range(SWA_KV_HEADS):
        slab = slice((g // 2) * LANES, (g // 2 + 1) * LANES)
        k2 = k_all[:, slab]
        v2 = v_all[:, slab]
        k2r = pltpu.roll(k2, SWA_HD, 1)
        v2r = pltpu.roll(v2, SWA_HD, 1)
        if g % 2 == 0:
            k_lo, k_hi, v_lo, v_hi = k2, k2r, v2, v2r
        else:
            k_lo, k_hi, v_lo, v_hi = k2r, k2, v2r, v2
        zero = jnp.zeros_like(k2)
        one = jnp.ones_like(k2)
        km = (jnp.where(low, k_lo, zero).astype(BF16), jnp.where(low, zero, k_hi).astype(BF16))
        vm = (jnp.where(low, v_lo, one).astype(BF16), jnp.where(low, one, v_hi).astype(BF16))
        heads = [(j, half) for j in range(SWA_GROUP // 2) for half in range(2)]
        sinks = [sink_ref[g * SWA_GROUP + 2 * j + half] for j, half in heads]
        scores = []
        for j, half in heads:
            qs = q_ref[:, (g * 4 + j) * LANES:(g * 4 + j + 1) * LANES]
            s = lax.dot_general(qs, km[half], (((1,), (1,)), ((), ())), preferred_element_type=F32)
            scores.append(jnp.where(valid, s, NEG_BIG))
        maxes = [jnp.maximum(jnp.max(s, axis=-1, keepdims=True), sk) for s, sk in zip(scores, sinks)]
        exps = [jnp.exp(s - m).astype(BF16) for s, m in zip(scores, maxes)]
        sink_terms = [jnp.exp(sk - m) for sk, m in zip(sinks, maxes)]
        for j in range(SWA_GROUP // 2):
            a_lo = jnp.dot(exps[2 * j], vm[0], preferred_element_type=F32)
            a_hi = jnp.dot(exps[2 * j + 1], vm[1], preferred_element_type=F32)
            num = jnp.where(low_q, a_lo, a_hi)
            den = pltpu.roll(jnp.where(low_q, a_hi, a_lo), SWA_HD, 1)
            den = den + jnp.where(low_q, sink_terms[2 * j], sink_terms[2 * j + 1])
            o_ref[:, (g * 4 + j) * LANES:(g * 4 + j + 1) * LANES] = (num / den).astype(o_ref.dtype)


def _swa(sinks, q, q_row0, k_prev, v_prev, prev_col, k_own, v_own, own_col, own_row0,
         batch, seq, tq, prev_from_cache):
    nt = seq // tq
    qb = q_row0 // tq
    ob = own_row0 // tq
    kv_w = SWA_KV_HEADS * SWA_HD
    if prev_from_cache:
        prev_map = lambda b, t, c: (b, c)
    else:
        per = seq // WINDOW
        prev_map = lambda b, t, c: (b * per + jnp.maximum(t * (tq // WINDOW) - 1, 0), c)
    kernel = functools.partial(_swa_kernel, tq=tq, prev_from_cache=prev_from_cache)
    return pl.pallas_call(
        kernel,
        out_shape=jax.ShapeDtypeStruct((batch * seq, SWA_HEADS * SWA_HD), BF16),
        grid=(batch, nt),
        in_specs=[pl.BlockSpec(memory_space=pltpu.SMEM),
                  pl.BlockSpec((tq, SWA_HEADS * SWA_HD), lambda b, t: (qb + b * nt + t, 0)),
                  pl.BlockSpec((WINDOW, kv_w), lambda b, t: prev_map(b, t, prev_col[0])),
                  pl.BlockSpec((WINDOW, kv_w), lambda b, t: prev_map(b, t, prev_col[1])),
                  pl.BlockSpec((tq, kv_w), lambda b, t: (ob + b * nt + t, own_col[0])),
                  pl.BlockSpec((tq, kv_w), lambda b, t: (ob + b * nt + t, own_col[1]))],
        out_specs=pl.BlockSpec((tq, SWA_HEADS * SWA_HD), lambda b, t: (b * nt + t, 0)),
        compiler_params=_params(2),
        name="swa_band",
    )(sinks, q, k_prev, v_prev, k_own, v_own)


def _mem_attn_head(q, k, v):
    s = lax.dot_general(q, k.astype(BF16), (((1,), (1,)), ((), ())),
                        preferred_element_type=F32) * (MEM_HD ** -0.5)
    m = jnp.max(s, axis=-1, keepdims=True)
    e = jnp.exp(s - m)
    p = (e / jnp.sum(e, axis=-1, keepdims=True)).astype(BF16)
    return jnp.dot(p, v.astype(BF16), preferred_element_type=F32)


def _mem_attn_kernel(q_ref, kv_ref, o_ref, kv_bf):
    @pl.when(pl.program_id(1) == 0)
    def _():
        kv_bf[...] = kv_ref[...].astype(BF16)

    width = MEM_HEADS * MEM_HD
    cols = [slice(h * MEM_HD, (h + 1) * MEM_HD) for h in range(MEM_HEADS)]
    scores = [lax.dot_general(q_ref[:, c], kv_bf[:, c], (((1,), (1,)), ((), ())),
                              preferred_element_type=F32) * (MEM_HD ** -0.5) for c in cols]
    exps = [jnp.exp(s - jnp.max(s, axis=-1, keepdims=True)) for s in scores]
    probs = [(e / jnp.sum(e, axis=-1, keepdims=True)).astype(BF16) for e in exps]
    for c, p in zip(cols, probs):
        v = kv_bf[:, width + c.start:width + c.stop]
        o_ref[:, c] = jnp.dot(p, v, preferred_element_type=F32).astype(o_ref.dtype)


def _mem_attn_cache_kernel(q_ref, k_hbm, v_hbm, o_ref, kbuf, vbuf, sem):
    b = pl.program_id(0)
    slot = b % 2

    def copies(req, s):
        return ([pltpu.make_async_copy(k_hbm.at[0, req, :, h, :], kbuf.at[s, h], sem.at[s, 0])
                 for h in range(MEM_HEADS)]
                + [pltpu.make_async_copy(v_hbm.at[0, req, :, h, :], vbuf.at[s, h], sem.at[s, 1])
                   for h in range(MEM_HEADS)])

    @pl.when(b == 0)
    def _():
        for c in copies(0, 0):
            c.start()

    @pl.when(b + 1 < pl.num_programs(0))
    def _():
        for c in copies(b + 1, 1 - slot):
            c.start()

    for c in copies(b, slot):
        c.wait()
    for h in range(MEM_HEADS):
        cols = slice(h * MEM_HD, (h + 1) * MEM_HD)
        o_ref[:, cols] = _mem_attn_head(q_ref[:, cols], kbuf[slot, h], vbuf[slot, h]).astype(o_ref.dtype)


def _mem_attn_cache(q, q_row0, cache_k, cache_v, batch, seq):
    qb = q_row0 // seq
    width = MEM_HEADS * MEM_HD
    hbm = pl.BlockSpec(memory_space=pl.ANY)
    slabs = pltpu.VMEM((2, MEM_HEADS, N_MEM, MEM_HD), F32)
    return pl.pallas_call(
        _mem_attn_cache_kernel,
        out_shape=jax.ShapeDtypeStruct((batch * seq, width), BF16),
        grid=(batch,),
        in_specs=[pl.BlockSpec((seq, width), lambda b: (qb + b, 0)), hbm, hbm],
        out_specs=pl.BlockSpec((seq, width), lambda b: (b, 0)),
        scratch_shapes=[slabs, slabs, pltpu.SemaphoreType.DMA((2, 2))],
        compiler_params=_params(1),
        name="mem_attn_cache",
    )(q, cache_k, cache_v)


def _mem_attn(q, mem_kv, batch, seq, tq):
    nt = seq // tq
    width = MEM_HEADS * MEM_HD
    return pl.pallas_call(
        _mem_attn_kernel,
        out_shape=jax.ShapeDtypeStruct((batch * seq, width), BF16),
        grid=(batch, nt),
        in_specs=[pl.BlockSpec((tq, width), lambda b, t: (b * nt + t, 0)),
                  pl.BlockSpec((N_MEM, 2 * width), lambda b, t: (b, 0))],
        out_specs=pl.BlockSpec((tq, width), lambda b, t: (b * nt + t, 0)),
        scratch_shapes=[pltpu.VMEM((N_MEM, 2 * width), BF16)],
        compiler_params=_params(2),
        name="mem_attn",
    )(q, mem_kv)


def _merge_kernel(a0p, a0s, a1p, a1s, a2p, a2s, w_ref, g0_ref, g1_ref, g2_ref, o_ref, *, p_tiles):
    def emit(a0_ref, a1_ref, a2_ref):
        acc = g0_ref[...] * jnp.dot(a0_ref[...], w_ref[0], preferred_element_type=F32)
        acc = acc + g1_ref[...] * jnp.dot(a1_ref[...], w_ref[1], preferred_element_type=F32)
        acc = acc + g2_ref[...] * jnp.dot(a2_ref[...], w_ref[2], preferred_element_type=F32)
        o_ref[...] = acc.astype(o_ref.dtype)

    @pl.when(pl.program_id(1) < p_tiles)
    def _():
        emit(a0p, a1p, a2p)

    @pl.when(pl.program_id(1) >= p_tiles)
    def _():
        emit(a0s, a1s, a2s)


def _merge(branches_p, branches_s, w_branch, gates, tm, tn):
    n_p, d = branches_p[0].shape
    n = n_p + branches_s[0].shape[0]
    nj = d // tn
    p_tiles = n_p // tm
    ap = pl.BlockSpec((tm, d), lambda j, i: (jnp.minimum(i, p_tiles - 1), 0))
    asp = pl.BlockSpec((tm, d), lambda j, i: (jnp.maximum(i - p_tiles, 0), 0))
    operands = [a for pair in zip(branches_p, branches_s) for a in pair]
    return pl.pallas_call(
        functools.partial(_merge_kernel, p_tiles=p_tiles),
        out_shape=jax.ShapeDtypeStruct((n, d), BF16),
        grid=(nj, n // tm),
        in_specs=[ap, asp, ap, asp, ap, asp,
                  pl.BlockSpec((3, d, tn), lambda j, i: (0, 0, j), pipeline_mode=pl.Buffered(1)),
                  pl.BlockSpec((tm, tn), lambda j, i: (i, j)),
                  pl.BlockSpec((tm, tn), lambda j, i: (i, nj + j)),
                  pl.BlockSpec((tm, tn), lambda j, i: (i, 2 * nj + j))],
        out_specs=pl.BlockSpec((tm, tn), lambda j, i: (i, j)),
        compiler_params=_params(2),
        name="branch_merge",
    )(*operands, w_branch, gates, gates, gates)


def _outproj_router_kernel(m_ref, w_ref, xp_ref, xs_ref, g_ref, wr_ref, br_ref,
                           x2_ref, hf_ref, eid_ref, wt_ref, *, p_tiles):
    is_prompt = pl.program_id(0) < p_tiles
    tm, d = x2_ref.shape
    for r0 in range(0, tm, ROUTER_SUB_ROWS):
        rows = slice(r0, r0 + ROUTER_SUB_ROWS)
        acc = jnp.dot(m_ref[rows, :], w_ref[...], preferred_element_type=F32)
        x = jnp.where(is_prompt, xp_ref[rows, :], xs_ref[rows, :]) + acc
        x2_ref[rows, :] = x
        hf = x * lax.rsqrt(jnp.mean(x * x, axis=-1, keepdims=True) + EPS) * g_ref[...]
        hb = hf.astype(BF16)
        bits = pltpu.bitcast(hb.astype(F32), jnp.uint32)
        packed = bits[:, d // 2:] | (bits[:, :d // 2] >> 16)
        for s in range(TOKEN_TILE_ROWS):
            hf_ref[pl.ds(r0 * TOKEN_TILE_ROWS + s, ROUTER_SUB_ROWS, stride=TOKEN_TILE_ROWS), :] = (
                packed[:, s * LANES:(s + 1) * LANES])
        logits = jnp.dot(hb, wr_ref[...], preferred_element_type=F32) + br_ref[...]
        eid, wts = _route(logits)
        eid_ref[rows, :] = eid
        wt_ref[rows, :] = wts


def _route(logits):
    lane = lax.broadcasted_iota(jnp.int32, logits.shape, 1).astype(F32)
    big = 1e6
    is_g = lane < N_GROUPS
    lg = jnp.where(is_g, logits, NEG_BIG)
    mg = jnp.max(lg, axis=-1, keepdims=True)
    gsel = jnp.min(jnp.where(is_g & (lg == mg), lane, big), axis=-1, keepdims=True)
    g_w = 1.0 / jnp.sum(jnp.where(is_g, jnp.exp(lg - mg), 0.0), axis=-1, keepdims=True)
    e_lo = N_GROUPS + gsel * EXPERTS_PER_GROUP
    in_grp = (lane >= e_lo) & (lane < e_lo + EXPERTS_PER_GROUP)
    le = jnp.where(in_grp, logits, NEG_BIG)
    me = jnp.max(le, axis=-1, keepdims=True)
    ee = jnp.where(in_grp, jnp.exp(le - me), 0.0)
    pe = ee / jnp.sum(ee, axis=-1, keepdims=True)
    pe = jnp.where(in_grp, pe, -1.0)
    p1 = jnp.max(pe, axis=-1, keepdims=True)
    i1 = jnp.min(jnp.where(pe == p1, lane, big), axis=-1, keepdims=True)
    pe2 = jnp.where(lane == i1, -1.0, pe)
    p2 = jnp.max(pe2, axis=-1, keepdims=True)
    i2 = jnp.min(jnp.where(pe2 == p2, lane, big), axis=-1, keepdims=True)
    tot = p1 + p2
    w1 = g_w * p1 / tot
    w2 = g_w * p2 / tot
    eid = jnp.where(lane == 0.0, i1 - N_GROUPS, jnp.where(lane == 1.0, i2 - N_GROUPS, 0.0))
    return eid.astype(jnp.int32), jnp.where(lane == 0.0, w1, jnp.where(lane == 1.0, w2, 0.0))


def _outproj_router(merged, w_out, xp, xs, g, wr, br, tm):
    n, d = merged.shape
    p_tiles = xp.shape[0] // tm
    const = lambda shape: pl.BlockSpec(shape, lambda i: (0, 0), pipeline_mode=pl.Buffered(1))
    row = lambda width: pl.BlockSpec((tm, width), lambda i: (i, 0))
    return pl.pallas_call(
        functools.partial(_outproj_router_kernel, p_tiles=p_tiles),
        out_shape=(jax.ShapeDtypeStruct((n, d), F32),
                   jax.ShapeDtypeStruct((n * TOKEN_TILE_ROWS, LANES), jnp.uint32),
                   jax.ShapeDtypeStruct((n, LANES), jnp.int32),
                   jax.ShapeDtypeStruct((n, LANES), F32)),
        grid=(n // tm,),
        in_specs=[row(d), const((d, d)), *_split_specs(p_tiles, tm, d),
                  const((1, d)), const((d, LANES)), const((1, LANES))],
        out_specs=(row(d), pl.BlockSpec((tm * TOKEN_TILE_ROWS, LANES), lambda i: (i, 0)),
                   row(LANES), row(LANES)),
        compiler_params=_params(1),
        name="outproj_router",
    )(merged, w_out, xp, xs, g.reshape(1, d), wr, br)


def _moe_kernel(te_ref, nv_ref, nxt_ref, tok_ref, tok_next_ref, dst_ref, hf_hbm, wup_hbm, wdn_hbm, y_hbm,
                xg, yb, wup_f32, wdn_f32, wup_bf, wdn_bf, in_sem, out_sem, w_sem):
    t = pl.program_id(0)
    buf = t % 2
    rows_now = nv_ref[t]
    rows_next = nv_ref[t + 1]
    rows_prev = nv_ref[jnp.maximum(t - 1, 0)]
    valid = rows_now > 0
    valid_next = rows_next > 0
    new_expert = (t == 0) | (te_ref[t] != te_ref[jnp.maximum(t - 1, 0)])

    def weight_copies(e):
        return (pltpu.make_async_copy(wup_hbm.at[e], wup_f32, w_sem.at[0]),
                pltpu.make_async_copy(wdn_hbm.at[e], wdn_f32, w_sem.at[1]))

    in_rows, in_pitch = TOKEN_TILE_ROWS, TOKEN_TILE_PITCH
    out_rows, out_pitch = TOKEN_F32_ROWS, TOKEN_F32_PITCH

    def row_in(row0, r, b):
        return pltpu.make_async_copy(hf_hbm.at[pl.ds(pl.multiple_of(row0, in_rows), in_rows), :],
                                     xg.at[b, pl.ds(r * in_pitch, in_rows), :], in_sem.at[b])

    def row_out(row0, r, b):
        return pltpu.make_async_copy(yb.at[b, pl.ds(r * out_pitch, out_rows), :],
                                     y_hbm.at[pl.ds(pl.multiple_of(row0, out_rows), out_rows), :], out_sem.at[b])

    group = MOE_ROW_GROUP
    n_groups = MOE_TILE // group

    def group_in(b):
        return pltpu.make_async_copy(hf_hbm.at[pl.ds(0, group * in_rows), :],
                                     xg.at[b, pl.ds(0, group * in_rows), :], in_sem.at[b])

    def group_out(b):
        return pltpu.make_async_copy(yb.at[b, pl.ds(0, group * out_rows), :],
                                     y_hbm.at[pl.ds(0, group * out_rows), :], out_sem.at[b])

    def per_started_group(rows, fn):
        for g in range(n_groups):
            pl.when(rows > g * group)(functools.partial(fn, g))

    @pl.when(t == 0)
    def _():
        for c in weight_copies(te_ref[0]):
            c.start(priority=1)
        xg[...] = jnp.zeros_like(xg)

        def first(r, c):
            row_in(tok_ref[0, 0, r], r, 0).start()
            return c
        lax.fori_loop(0, ((rows_now + group - 1) // group) * group, first, 0)
        yb[...] = jnp.zeros_like(yb)
        n_real = y_hbm.shape[0] - 2 * MOE_TILE * out_rows
        for b in range(2):
            spare = pltpu.make_async_copy(
                yb.at[b, pl.ds(0, MOE_TILE * out_rows), :],
                y_hbm.at[pl.ds(n_real + b * MOE_TILE * out_rows, MOE_TILE * out_rows), :], out_sem.at[b])
            spare.start()
            spare.wait()

    @pl.when(valid & new_expert)
    def _():
        for c in weight_copies(te_ref[t]):
            c.wait()
        wup_bf[...] = wup_f32[...].astype(BF16)
        wdn_bf[...] = wdn_f32[...].astype(BF16)

        @pl.when(nxt_ref[t] >= 0)
        def _():
            for c in weight_copies(nxt_ref[t]):
                c.start(priority=1)

    def gather_next(g):
        for r in range(g * group, (g + 1) * group):
            row_in(tok_next_ref[0, 0, r], r, 1 - buf).start()

    def scatter_now(g):
        for r in range(g * group, (g + 1) * group):
            row_out(dst_ref[0, 0, r], r, buf).start(priority=r % 2)

    per_started_group(rows_next, gather_next)

    @pl.when(valid)
    def _():
        per_started_group(rows_now, lambda g: group_in(buf).wait())
        lo, hi = [], []
        for s in range(in_rows):
            word = xg[buf, pl.ds(s, MOE_TILE, stride=in_pitch), :]
            lo.append(pltpu.bitcast(word << 16, F32).astype(BF16))
            hi.append(pltpu.bitcast(word & jnp.uint32(0xFFFF0000), F32).astype(BF16))
        x = jnp.concatenate(lo + hi, axis=1)
        h1 = jnp.dot(x, wup_bf[...], preferred_element_type=F32)
        gate = h1[:, :D_FF]
        up = h1[:, D_FF:]
        act = (gate * _sigmoid(gate)) * up
        ye = jnp.dot(act.astype(BF16), wdn_bf[...], preferred_element_type=F32)
        for s in range(out_rows):
            yb[buf, pl.ds(s, MOE_TILE, stride=out_pitch), :] = ye[:, s * LANES:(s + 1) * LANES]
        per_started_group(rows_now, scatter_now)

        @pl.when(t > 0)
        def _():
            per_started_group(rows_prev, lambda g: group_out(1 - buf).wait())

        @pl.when(jnp.logical_not(valid_next))
        def _():
            per_started_group(rows_now, lambda g: group_out(buf).wait())


def _moe(hf, tile_expert, tile_rows, next_expert, tok_slots, dst_slots, w_up, w_down, out_tokens):
    n_tiles = tile_expert.shape[0]
    d = w_up.shape[1]
    slot_spec = lambda off: pl.BlockSpec((1, 1, MOE_TILE),
                                         lambda t, te, nv, ne: (jnp.minimum(t + off, n_tiles - 1), 0, 0),
                                         memory_space=pltpu.SMEM)
    hbm = pl.BlockSpec(memory_space=pl.ANY)
    grid_spec = pltpu.PrefetchScalarGridSpec(
        num_scalar_prefetch=3,
        grid=(n_tiles,),
        in_specs=[slot_spec(0), slot_spec(1), slot_spec(0), hbm, hbm, hbm],
        out_specs=hbm,
        scratch_shapes=[pltpu.VMEM((2, MOE_TILE * TOKEN_TILE_PITCH, LANES), jnp.uint32),
                        pltpu.VMEM((2, MOE_TILE * TOKEN_F32_PITCH, LANES), F32),
                        pltpu.VMEM((d, 2 * D_FF), F32),
                        pltpu.VMEM((D_FF, d), F32),
                        pltpu.VMEM((d, 2 * D_FF), BF16),
                        pltpu.VMEM((D_FF, d), BF16),
                        pltpu.SemaphoreType.DMA((2,)),
                        pltpu.SemaphoreType.DMA((2,)),
                        pltpu.SemaphoreType.DMA((2,))],
    )
    return pl.pallas_call(
        _moe_kernel,
        out_shape=jax.ShapeDtypeStruct((out_tokens * TOKEN_F32_ROWS, LANES), F32),
        grid_spec=grid_spec,
        compiler_params=_params(1),
        name="moe_experts",
    )(tile_expert, tile_rows, next_expert, tok_slots, tok_slots, dst_slots, hf, w_up, w_down)


def _combine_kernel(x_ref, y0_ref, y1_ref, w_ref, op_ref, os_ref, *, p_tiles):
    tm = x_ref.shape[0]
    w = w_ref[...]
    w0 = w[:, 0:1]
    w1 = w[:, 1:2]

    def emit(o_ref):
        for s in range(TOKEN_F32_ROWS):
            cols = slice(s * LANES, (s + 1) * LANES)
            rows = pl.ds(s, tm, stride=TOKEN_F32_ROWS)
            o_ref[:, cols] = x_ref[:, cols] + (y0_ref[rows, :] * w0 + y1_ref[rows, :] * w1)

    @pl.when(pl.program_id(0) < p_tiles)
    def _():
        emit(op_ref)

    @pl.when(pl.program_id(0) >= p_tiles)
    def _():
        emit(os_ref)


def _combine(x2, yk, wts, n_p, tm):
    n, d = x2.shape
    p_tiles = n_p // tm
    k1 = n // tm
    y_rows = tm * TOKEN_F32_ROWS
    return pl.pallas_call(
        functools.partial(_combine_kernel, p_tiles=p_tiles),
        out_shape=(jax.ShapeDtypeStruct((n_p, d), F32), jax.ShapeDtypeStruct((n - n_p, d), F32)),
        grid=(n // tm,),
        in_specs=[pl.BlockSpec((tm, d), lambda i: (i, 0)),
                  pl.BlockSpec((y_rows, LANES), lambda i: (i, 0)),
                  pl.BlockSpec((y_rows, LANES), lambda i: (k1 + i, 0)),
                  pl.BlockSpec((tm, LANES), lambda i: (i, 0))],
        out_specs=_split_specs(p_tiles, tm, d),
        compiler_params=_params(1),
        name="moe_combine",
    )(x2, yk, yk, wts)


def _rope_tables(pos):
    half = ROPE_DIM // 2
    inv = ROPE_THETA ** (-jnp.arange(half, dtype=F32) / half)
    ang = pos.astype(F32)[:, None] * inv[None, :]
    cos, sin = jnp.cos(ang), jnp.sin(ang)
    n = pos.shape[0]
    pad = jnp.zeros((n, SWA_HD - ROPE_DIM), F32)
    cos_h = jnp.concatenate([cos, cos, pad + 1.0], axis=1)
    sa_h = jnp.concatenate([-sin, jnp.zeros_like(sin), pad], axis=1)
    sb_h = jnp.concatenate([jnp.zeros_like(sin), sin, pad], axis=1)
    reps = LANES // SWA_HD
    return tuple(jnp.tile(a, (1, reps)) for a in (cos_h, sa_h, sb_h))


def _moe_schedule(eid, n_tok, n_tiles):
    a = eid.shape[0]
    order = jnp.argsort(eid, stable=True).astype(jnp.int32)
    counts = jnp.bincount(eid, length=N_EXPERTS).astype(jnp.int32)
    tiles_per = (counts + MOE_TILE - 1) // MOE_TILE
    tile_end = jnp.cumsum(tiles_per)
    tile_start = tile_end - tiles_per
    sorted_start = jnp.cumsum(counts) - counts
    tile_id = jnp.arange(n_tiles, dtype=jnp.int32)
    used = tile_id < tile_end[-1]
    te = jnp.minimum(jnp.sum(tile_end[None, :] <= tile_id[:, None], axis=1), N_EXPERTS - 1).astype(jnp.int32)
    last_used_e = te[jnp.maximum(tile_end[-1] - 1, 0)]
    te = jnp.where(used, te, last_used_e)
    row_in_expert = (tile_id - tile_start[te]) * MOE_TILE
    rows_valid = jnp.where(used, jnp.clip(counts[te] - row_in_expert, 0, MOE_TILE), 0).astype(jnp.int32)
    r = jnp.arange(MOE_TILE, dtype=jnp.int32)[None, :]
    src = sorted_start[te][:, None] + row_in_expert[:, None] + r
    real = r < rows_valid[:, None]
    assign = order[jnp.clip(src, 0, a - 1)]
    tok = jnp.where(real, assign % n_tok, 0) * TOKEN_TILE_ROWS
    spare = a + (tile_id[:, None] % 2) * MOE_TILE + r
    dst = jnp.where(real, assign, spare) * TOKEN_F32_ROWS
    tile_rows = jnp.concatenate([rows_valid, jnp.zeros((1,), jnp.int32)])
    e_id = jnp.arange(N_EXPERTS, dtype=jnp.int32)[None, :]
    later = (e_id > te[:, None]) & (counts[None, :] > 0)
    nxt = jnp.min(jnp.where(later, e_id, N_EXPERTS), axis=1)
    nxt = jnp.where(nxt < N_EXPERTS, nxt, -1).astype(jnp.int32)
    shape = (n_tiles, 1, MOE_TILE)
    return te, tile_rows, nxt, tok.reshape(shape).astype(jnp.int32), dst.reshape(shape).astype(jnp.int32)


def kernel(x_prompt, x_sample, state_gla, cache_swa_k, cache_swa_v, cache_mem_k, cache_mem_v,
           mem_prompt, norm_mix_g, w_in, w_a2, b_a2, gla_norm_g, swa_q_norm_g, swa_k_norm_g,
           swa_sinks, norm_mem_g, w_mem_kv, mem_q_norm_g, mem_k_norm_g, w_gate, b_gate,
           w_branch, w_out, norm_ffn_g, w_router_group, b_router_group, w_router_expert,
           b_router_expert, w_up, w_down):
    bp, tp, d = x_prompt.shape
    bs, ts, _ = x_sample.shape
    n_p, n_s = bp * tp, bs * ts
    n = n_p + n_s
    tm, tn = ROW_TILE, COL_TILE
    tp_rows = PROJ_ROWS if n % PROJ_ROWS == 0 else tm
    heavy_sub = tp_rows // 2
    assert d == D_MODEL and n_p % tm == 0 and n_s % tm == 0 and w_in.shape[0] == 1
    keep_s = cache_swa_k.shape[2]
    assert keep_s == WINDOW and tp % WINDOW == 0

    qk_w = GLA_HEADS * GLA_DK
    v_w = GLA_HEADS * GLA_DV
    c0 = 2 * qk_w + 2 * v_w
    sq_w = SWA_HEADS * SWA_HD
    kv_w = SWA_KV_HEADS * SWA_HD
    mem_w = MEM_HEADS * MEM_HD
    w_in_t = jnp.transpose(w_in[0])
    w_branch_b = w_branch[0].astype(BF16)
    w_out_b = w_out[0].astype(BF16)
    w_a2_b = jnp.pad(w_a2[0], ((0, LANES - GLA_LOWRANK), (0, 0))).astype(BF16)
    w_router = jnp.pad(jnp.concatenate([w_router_group[0], w_router_expert[0]], axis=1),
                       ((0, 0), (0, LANES - N_GROUPS - N_EXPERTS))).astype(BF16)
    b_router = jnp.pad(jnp.concatenate([b_router_group[0], b_router_expert[0]]),
                       (0, LANES - N_GROUPS - N_EXPERTS)).reshape(1, LANES)

    pos = jnp.concatenate([jnp.tile(jnp.arange(tp, dtype=jnp.int32), bp),
                           jnp.tile(PAST_LEN + jnp.arange(ts, dtype=jnp.int32), bs)])
    cos_t, sa_t, sb_t = _rope_tables(pos)
    seg_id = jnp.arange(tn, dtype=jnp.int32) // SWA_HD
    seg = (seg_id[:, None] == seg_id[None, :]).astype(BF16)
    rope_specs = [pl.BlockSpec((tp_rows, LANES), lambda j, i: (i, 0))] * 3
    row_vec = lambda width: pl.BlockSpec((1, width), lambda j, i: (0, 0))
    seg_spec = pl.BlockSpec((tn, tn), lambda j, i: (0, 0))

    xp2 = x_prompt.reshape(n_p, d)
    xs2 = x_sample.reshape(n_s, d)
    h = _rms_norm_two(xp2, xs2, norm_mix_g[0], tm)

    c1 = c0 + GLA_LOWRANK
    c2 = c1 + sq_w
    c3 = c2 + 2 * kv_w
    qkvg = _wmatmul(_ep_plain, h, w_in_t, 0, c0, True, [], [], F32, tp_rows, WIDE_COLS, "proj_gla",
                    sub_rows=heavy_sub)
    ga = _wmatmul(_ep_lowrank, h, w_in_t, c0, LANES, True, [], [], F32, tp_rows, LANES, "proj_gla_lowrank")
    q_gain = jnp.tile(swa_q_norm_g[0] * (SWA_HD ** -0.5), tn // SWA_HD).reshape(1, tn)
    q_swa = _wmatmul(functools.partial(_ep_qknorm_rope, keep_from=None), h, w_in_t, c1, sq_w, True,
                     [q_gain, seg, cos_t, sa_t, sb_t], [row_vec(tn), seg_spec] + rope_specs,
                     BF16, tp_rows, tn, "proj_swa_q", sub_rows=heavy_sub)
    k_gain = jnp.tile(swa_k_norm_g[0], tn // SWA_HD).reshape(1, tn)
    kv_swa = _wmatmul(functools.partial(_ep_qknorm_rope, keep_from=kv_w), h, w_in_t, c2, 2 * kv_w, True,
                      [k_gain, seg, cos_t, sa_t, sb_t], [row_vec(tn), seg_spec] + rope_specs,
                      F32, tp_rows, tn, "proj_swa_kv", sub_rows=heavy_sub)
    q_mem = _wmatmul(functools.partial(_ep_headnorm, norm_tiles=None), h, w_in_t, c3, mem_w, True,
                     [mem_q_norm_g[0].reshape(1, MEM_HD)], [row_vec(MEM_HD)],
                     BF16, tp_rows, MEM_HD, "proj_mem_q", sub_rows=heavy_sub)
    gates = _wmatmul(_ep_sigmoid, h, w_gate[0], 0, 3 * d, False, [b_gate[0].reshape(1, -1)],
                     [pl.BlockSpec((1, WIDE_COLS), lambda j, i: (0, j))], F32, tp_rows, WIDE_COLS, "proj_gates",
                     sub_rows=heavy_sub)

    mem_rows = bp * N_MEM
    hm = _rms_norm_rows(mem_prompt.reshape(mem_rows, d), norm_mem_g[0], BF16, N_MEM)
    mem_kv = _wmatmul(functools.partial(_ep_headnorm, norm_tiles=MEM_HEADS), hm, w_mem_kv[0], 0, 2 * mem_w, False,
                      [mem_k_norm_g[0].reshape(1, MEM_HD)], [row_vec(MEM_HD)], F32, mem_rows, MEM_HD, "mem_kv")

    ba = b_a2[0].reshape(1, qk_w)
    gn = gla_norm_g[0].reshape(1, GLA_DV)
    s0_p = jnp.zeros((bp, GLA_HEADS, GLA_DK, GLA_DV), F32)
    o_gla_p, gla_state_p = _gla(qkvg, ga, w_a2_b, ba, gn, s0_p, bp, tp, 0, 256, CHUNK)
    o_gla_s, gla_state_s = _gla(qkvg, ga, w_a2_b, ba, gn, state_gla[0], bs, ts, n_p, ts, min(CHUNK, ts))

    sinks = swa_sinks[0]
    o_swa_p = _swa(sinks, q_swa, 0, kv_swa, kv_swa, (0, 1), kv_swa, kv_swa, (0, 1), 0,
                   bp, tp, WINDOW, False)
    ck = cache_swa_k[0].reshape(bs * keep_s, kv_w)
    cv = cache_swa_v[0].reshape(bs * keep_s, kv_w)
    o_swa_s = _swa(sinks, q_swa, n_p, ck, cv, (0, 0), kv_swa, kv_swa, (0, 1), n_p,
                   bs, ts, ts, True)

    o_mem_p = _mem_attn(q_mem, mem_kv, bp, tp, tm)
    o_mem_s = _mem_attn_cache(q_mem, n_p, cache_mem_k, cache_mem_v, bs, ts)

    merged = _merge((o_gla_p, o_swa_p, o_mem_p), (o_gla_s, o_swa_s, o_mem_s), w_branch_b, gates, tm, WIDE_COLS)

    x2, hf, eid, wts = _outproj_router(merged, w_out_b, xp2, xs2, norm_ffn_g[0], w_router, b_router, tm)

    n_assign = TOP_K * n
    n_tiles = n_assign // MOE_TILE + N_EXPERTS
    eid_kmajor = jnp.concatenate([eid[:, k] for k in range(TOP_K)])
    tile_expert, tile_rows, next_expert, tok_slots, dst_slots = _moe_schedule(eid_kmajor, n, n_tiles)
    yk = _moe(hf, tile_expert, tile_rows, next_expert, tok_slots, dst_slots, w_up[0], w_down[0],
              n_assign + 2 * MOE_TILE)
    y_p, y_s = _combine(x2, yk, wts, n_p, tm)

    y_p = y_p.reshape(bp, tp, d)
    y_s = y_s.reshape(bs, ts, d)
    kv_p = jnp.stack([kv_swa[(b + 1) * tp - WINDOW:(b + 1) * tp] for b in range(bp)])
    kv_p = kv_p.reshape(bp, WINDOW, 2, SWA_KV_HEADS, SWA_HD)
    kv_s = kv_swa[n_p:].reshape(bs, ts, 2, SWA_KV_HEADS, SWA_HD)
    swk_s = jnp.concatenate([cache_swa_k[0], kv_s[:, :, 0]], axis=1)[:, ts:ts + keep_s]
    swv_s = jnp.concatenate([cache_swa_v[0], kv_s[:, :, 1]], axis=1)[:, ts:ts + keep_s]
    mk_p = mem_kv[:, :mem_w].reshape(bp, N_MEM, MEM_HEADS, MEM_HD)
    mv_p = mem_kv[:, mem_w:].reshape(bp, N_MEM, MEM_HEADS, MEM_HD)
    return (y_p, y_s, gla_state_p[None], kv_p[:, :, 0][None], kv_p[:, :, 1][None], mk_p[None], mv_p[None],
            gla_state_s[None], swk_s[None], swv_s[None])
```

```python
import functools

import jax
import jax.numpy as jnp
from jax import lax
from jax.experimental import pallas as pl
from jax.experimental.pallas import tpu as pltpu

F32 = jnp.float32
BF16 = jnp.bfloat16

D_MODEL = 2048
CHUNK = 64
EPS = 1e-6
PAST_LEN = 1024
GLA_HEADS = 4
GLA_DV = 512
GLA_DK = 256
GLA_LOWRANK = 16
GLA_NORMALIZER = 16.0
SWA_HD = 64
SWA_HEADS = 32
SWA_KV_HEADS = 4
SWA_GROUP = 8
WINDOW = 128
ROPE_DIM = 16
ROPE_THETA = 500000.0
N_MEM = 256
MEM_HEADS = 4
MEM_HD = 512
N_GROUPS = 8
EXPERTS_PER_GROUP = 8
N_EXPERTS = 64
TOP_K = 2
D_FF = 512

LANES = 128
VMEM_LIMIT = 56 * 1024 * 1024
ROW_TILE = 512
PROJ_ROWS = 1536
COL_TILE = 512
WIDE_COLS = 1024
ROUTER_SUB_ROWS = 256
MOE_TILE = 256
MOE_ROW_GROUP = 32
TOKEN_TILE_ROWS = D_MODEL // 2 // LANES
TOKEN_TILE_PITCH = 12
TOKEN_F32_ROWS = D_MODEL // LANES
TOKEN_F32_PITCH = 20
NEG_BIG = -1e30


def _params(n_axes):
    return pltpu.CompilerParams(dimension_semantics=("arbitrary",) * n_axes,
                                vmem_limit_bytes=VMEM_LIMIT)


def _norm_kernel(x_ref, g_ref, o_ref):
    x = x_ref[...]
    y = x * lax.rsqrt(jnp.mean(x * x, axis=-1, keepdims=True) + EPS)
    o_ref[...] = (y * g_ref[...]).astype(o_ref.dtype)


def _rms_norm_rows(x, g, out_dtype, tm):
    n, d = x.shape
    return pl.pallas_call(
        _norm_kernel,
        out_shape=jax.ShapeDtypeStruct((n, d), out_dtype),
        grid=(n // tm,),
        in_specs=[pl.BlockSpec((tm, d), lambda i: (i, 0)),
                  pl.BlockSpec((1, d), lambda i: (0, 0))],
        out_specs=pl.BlockSpec((tm, d), lambda i: (i, 0)),
        compiler_params=_params(1),
        name="rms_norm_rows",
    )(x, g.reshape(1, d))


def _norm2_kernel(xp_ref, xs_ref, g_ref, o_ref, *, p_tiles):
    def emit(x_ref):
        x = x_ref[...]
        y = x * lax.rsqrt(jnp.mean(x * x, axis=-1, keepdims=True) + EPS)
        o_ref[...] = (y * g_ref[...]).astype(o_ref.dtype)

    @pl.when(pl.program_id(0) < p_tiles)
    def _():
        emit(xp_ref)

    @pl.when(pl.program_id(0) >= p_tiles)
    def _():
        emit(xs_ref)


def _split_specs(p_tiles, tm, d):
    return (pl.BlockSpec((tm, d), lambda i: (jnp.minimum(i, p_tiles - 1), 0)),
            pl.BlockSpec((tm, d), lambda i: (jnp.maximum(i - p_tiles, 0), 0)))


def _rms_norm_two(xp, xs, g, tm):
    (n_p, d), n_s = xp.shape, xs.shape[0]
    p_tiles = n_p // tm
    return pl.pallas_call(
        functools.partial(_norm2_kernel, p_tiles=p_tiles),
        out_shape=jax.ShapeDtypeStruct((n_p + n_s, d), BF16),
        grid=((n_p + n_s) // tm,),
        in_specs=[*_split_specs(p_tiles, tm, d), pl.BlockSpec((1, d), lambda i: (0, 0))],
        out_specs=pl.BlockSpec((tm, d), lambda i: (i, 0)),
        compiler_params=_params(1),
        name="rms_norm_mix",
    )(xp, xs, g.reshape(1, d))


def _segment_rms(acc, seg_ref, inv_width):
    ss = jnp.dot((acc * acc).astype(BF16), seg_ref[...], preferred_element_type=F32)
    return ss * inv_width


def _rope(y, rows, cos_ref, sa_ref, sb_ref):
    width = y.shape[1]
    reps = width // LANES
    c = jnp.concatenate([cos_ref[rows, :]] * reps, axis=1)
    sa = jnp.concatenate([sa_ref[rows, :]] * reps, axis=1)
    sb = jnp.concatenate([sb_ref[rows, :]] * reps, axis=1)
    half = ROPE_DIM // 2
    return y * c + pltpu.roll(y, width - half, 1) * sa + pltpu.roll(y, half, 1) * sb


def _ep_plain(acc, rows):
    return acc


def _ep_lowrank(acc, rows):
    lane = lax.broadcasted_iota(jnp.int32, acc.shape, 1)
    return jnp.where(lane < GLA_LOWRANK, acc, 0.0)


def _sigmoid(x):
    return 0.5 * jnp.tanh(0.5 * x) + 0.5


def _ep_sigmoid(acc, rows, b_ref):
    return _sigmoid(acc + b_ref[...])


def _ep_qknorm_rope(acc, rows, g_ref, seg_ref, cos_ref, sa_ref, sb_ref, *, keep_from):
    ms = _segment_rms(acc, seg_ref, 1.0 / SWA_HD)
    y = acc * lax.rsqrt(ms + EPS) * g_ref[...]
    y = _rope(y, rows, cos_ref, sa_ref, sb_ref)
    if keep_from is not None:
        col = lax.broadcasted_iota(jnp.int32, y.shape, 1)
        y = jnp.where(col < keep_from, y, acc)
    return y


def _ep_headnorm(acc, rows, g_ref, *, norm_tiles):
    y = acc * lax.rsqrt(jnp.mean(acc * acc, axis=-1, keepdims=True) + EPS) * g_ref[...]
    if norm_tiles is not None:
        y = jnp.where(pl.program_id(0) < norm_tiles, y, acc)
    return y


def _wmm_kernel(a_ref, w_ref, *rest, w_is_transposed, sub_rows, epilogue):
    extras, o_ref, wbf = rest[:-2], rest[-2], rest[-1]

    @pl.when(pl.program_id(1) == 0)
    def _():
        wbf[...] = w_ref[...].astype(BF16)

    for r0 in range(0, a_ref.shape[0], sub_rows):
        rows = slice(r0, r0 + sub_rows)
        if w_is_transposed:
            acc = lax.dot_general(a_ref[rows, :], wbf[...], (((1,), (1,)), ((), ())), preferred_element_type=F32)
        else:
            acc = jnp.dot(a_ref[rows, :], wbf[...], preferred_element_type=F32)
        o_ref[rows, :] = epilogue(acc, rows, *extras).astype(o_ref.dtype)


def _wmatmul(epilogue, a, w, col0, n_cols, w_is_transposed, extras, extra_specs, out_dtype, tm, tn, name,
             sub_rows=None):
    m, k = a.shape
    sub_rows = tm if sub_rows is None else sub_rows
    assert m % tm == 0 and n_cols % tn == 0 and tm % sub_rows == 0
    if w_is_transposed:
        assert col0 % 8 == 0
        w_spec = pl.BlockSpec((pl.Element(tn), pl.Element(k)), lambda j, i: (pl.multiple_of(col0 + j * tn, 8), 0))
        w_tile = (tn, k)
    else:
        assert col0 % tn == 0
        w_spec = pl.BlockSpec((k, tn), lambda j, i: (0, col0 // tn + j))
        w_tile = (k, tn)
    kernel = functools.partial(_wmm_kernel, w_is_transposed=w_is_transposed, sub_rows=sub_rows, epilogue=epilogue)
    return pl.pallas_call(
        kernel,
        out_shape=jax.ShapeDtypeStruct((m, n_cols), out_dtype),
        grid=(n_cols // tn, m // tm),
        in_specs=[pl.BlockSpec((tm, k), lambda j, i: (i, 0)), w_spec] + list(extra_specs),
        out_specs=pl.BlockSpec((tm, tn), lambda j, i: (i, j)),
        scratch_shapes=[pltpu.VMEM(w_tile, BF16)],
        compiler_params=_params(2),
        name=name,
    )(a, w, *extras)


def _gla_kernel(q_ref, k_ref, v_ref, gg_ref, ga_ref, wa_ref, ba_ref, gn_ref, *rest, chunk, n_chunks, has_s0):
    s0_ref = rest[0] if has_s0 else None
    o_ref, sout_ref, s_scr, qt_scr, ku_scr, o_scr, dec_scr = rest[1:] if has_s0 else rest
    _gla_block(q_ref, k_ref, v_ref, gg_ref, ga_ref, wa_ref, ba_ref, gn_ref, s0_ref,
               o_ref, sout_ref, s_scr, qt_scr, ku_scr, o_scr, dec_scr, chunk=chunk, n_chunks=n_chunks)


def _gla_block(q_ref, k_ref, v_ref, gg_ref, ga_ref, wa_ref, ba_ref, gn_ref, s0_ref,
               o_ref, sout_ref, s_scr, qt_scr, ku_scr, o_scr, dec_scr, *, chunk, n_chunks):
    t = pl.program_id(1)
    tb = chunk * n_chunks
    heads = [(slice(h * GLA_DK, (h + 1) * GLA_DK), slice(h * GLA_DV, (h + 1) * GLA_DV)) for h in range(GLA_HEADS)]

    @pl.when(t == 0)
    def _():
        s_scr[...] = jnp.zeros_like(s_scr) if s0_ref is None else s0_ref[0]

    row = lax.broadcasted_iota(jnp.int32, (chunk, chunk), 0)
    col = lax.broadcasted_iota(jnp.int32, (chunk, chunk), 1)
    tril = (row >= col).astype(BF16)
    z = jnp.dot(ga_ref[...].astype(BF16), wa_ref[...], preferred_element_type=F32) + ba_ref[...]
    log_a = (jnp.minimum(z, 0.0) - jnp.log(1.0 + jnp.exp(-jnp.abs(z)))) * (1.0 / GLA_NORMALIZER)
    hi = log_a.astype(BF16)
    rest = log_a - hi.astype(F32)
    mid = rest.astype(BF16)
    lo = (rest - mid.astype(F32)).astype(BF16)
    b_parts, last_parts = [], []
    for ci in range(n_chunks):
        crows = slice(ci * chunk, (ci + 1) * chunk)
        b_c = (jnp.dot(tril, hi[crows], preferred_element_type=F32)
               + jnp.dot(tril, mid[crows], preferred_element_type=F32)
               + jnp.dot(tril, lo[crows], preferred_element_type=F32))
        b_last = b_c[chunk - 1:chunk, :]
        b_parts.append(b_c)
        last_parts.append(jnp.broadcast_to(b_last, b_c.shape))
        for h, (ks, _) in enumerate(heads):
            dec_scr[ci, h] = jnp.transpose(jnp.broadcast_to(jnp.exp(b_last[:, ks]), (LANES, GLA_DK)))
    b = jnp.concatenate(b_parts, axis=0)
    b_last_rows = jnp.concatenate(last_parts, axis=0)

    q = q_ref[...] * (GLA_DK ** -0.5)
    k = k_ref[...]
    q_t = (q * jnp.exp(b)).astype(BF16)
    k_t = (k * jnp.exp(-b)).astype(BF16)
    qt_scr[...] = q_t
    ku_scr[...] = (k * jnp.exp(b_last_rows - b)).astype(BF16)

    brow = lax.broadcasted_iota(jnp.int32, (tb, tb), 0)
    bcol = lax.broadcasted_iota(jnp.int32, (tb, tb), 1)
    mask = (bcol >= (brow & -chunk)) & (brow >= bcol)
    for ks, vs in heads:
        att = lax.dot_general(q_t[:, ks], k_t[:, ks], (((1,), (1,)), ((), ())), preferred_element_type=F32)
        att = jnp.where(mask, att, 0.0).astype(BF16)
        o_scr[:, vs] = jnp.dot(att, v_ref[:, vs].astype(BF16), preferred_element_type=F32)

    def one_chunk(ci, carry):
        rows = pl.ds(pl.multiple_of(ci * chunk, chunk), chunk)
        for h, (ks, vs) in enumerate(heads):
            s_old = s_scr[h]
            o_scr[rows, vs] += jnp.dot(qt_scr[rows, ks], s_old.astype(BF16), preferred_element_type=F32)
            decay = jnp.concatenate([dec_scr[ci, h]] * (GLA_DV // LANES), axis=1)
            s_scr[h] = decay * s_old + lax.dot_general(ku_scr[rows, ks], v_ref[rows, vs].astype(BF16),
                                                       (((0,), (0,)), ((), ())), preferred_element_type=F32)
        return carry

    lax.fori_loop(0, n_chunks, one_chunk, 0)

    for _, vs in heads:
        o = o_scr[:, vs]
        on = o * lax.rsqrt(jnp.mean(o * o, axis=-1, keepdims=True) + EPS) * gn_ref[...]
        gg = gg_ref[:, vs]
        o_ref[:, vs] = (on * (gg * _sigmoid(gg))).astype(o_ref.dtype)

    @pl.when(t == pl.num_programs(1) - 1)
    def _():
        sout_ref[0] = s_scr[...]


def _gla(qkvg, ga, wa, ba, gn, s0, batch, seq, row0, tb, chunk):
    nt = seq // tb
    base = row0 // tb
    qk_w = GLA_HEADS * GLA_DK
    v_w = GLA_HEADS * GLA_DV
    rows = lambda b, t: base + b * nt + t
    has_s0 = s0 is not None
    kernel = functools.partial(_gla_kernel, chunk=chunk, n_chunks=tb // chunk, has_s0=has_s0)
    state_spec = pl.BlockSpec((1, GLA_HEADS, GLA_DK, GLA_DV), lambda b, t: (b, 0, 0, 0))
    return pl.pallas_call(
        kernel,
        out_shape=(jax.ShapeDtypeStruct((batch * seq, v_w), BF16),
                   jax.ShapeDtypeStruct((batch, GLA_HEADS, GLA_DK, GLA_DV), F32)),
        grid=(batch, nt),
        in_specs=[pl.BlockSpec((tb, qk_w), lambda b, t: (rows(b, t), 0)),
                  pl.BlockSpec((tb, qk_w), lambda b, t: (rows(b, t), 1)),
                  pl.BlockSpec((tb, v_w), lambda b, t: (rows(b, t), 1)),
                  pl.BlockSpec((tb, v_w), lambda b, t: (rows(b, t), 2)),
                  pl.BlockSpec((tb, LANES), lambda b, t: (rows(b, t), 0)),
                  pl.BlockSpec((LANES, qk_w), lambda b, t: (0, 0)),
                  pl.BlockSpec((1, qk_w), lambda b, t: (0, 0)),
                  pl.BlockSpec((1, GLA_DV), lambda b, t: (0, 0))] + ([state_spec] if has_s0 else []),
        out_specs=(pl.BlockSpec((tb, v_w), lambda b, t: (b * nt + t, 0)),
                   pl.BlockSpec((1, GLA_HEADS, GLA_DK, GLA_DV), lambda b, t: (b, 0, 0, 0))),
        scratch_shapes=[pltpu.VMEM((GLA_HEADS, GLA_DK, GLA_DV), F32),
                        pltpu.VMEM((tb, qk_w), BF16),
                        pltpu.VMEM((tb, qk_w), BF16),
                        pltpu.VMEM((tb, v_w), F32),
                        pltpu.VMEM((tb // chunk, GLA_HEADS, GLA_DK, LANES), F32)],
        compiler_params=_params(2),
        name="gla_chunks",
    )(qkvg, qkvg, qkvg, qkvg, ga, wa, ba, gn, *([s0] if has_s0 else []))


def _swa_kernel(sink_ref, q_ref, kp_ref, vp_ref, ko_ref, vo_ref, o_ref, bias_scr, *, tq, prev_from_cache):
    i = pl.program_id(1)
    nk = WINDOW + tq
    k_all = jnp.concatenate([kp_ref[...], ko_ref[...]], axis=0)
    v_all = jnp.concatenate([vp_ref[...], vo_ref[...]], axis=0)
    qc = lax.broadcasted_iota(jnp.int32, (tq, nk), 0) // CHUNK + WINDOW // CHUNK
    kcol = lax.broadcasted_iota(jnp.int32, (tq, nk), 1)
    kc = kcol // CHUNK
    valid = (kc <= qc) & (kc >= qc - WINDOW // CHUNK)
    if not prev_from_cache:
        valid = valid & ((kcol >= WINDOW) | (i > 0))
    bias_scr[...] = jnp.where(valid, 0.0, NEG_BIG)
    lane = lax.broadcasted_iota(jnp.int32, (nk, LANES), 1)
    low = lane < SWA_HD
    low_q = lax.broadcasted_iota(jnp.int32, (tq, LANES), 1) < SWA_HD
    for g in range(SWA_KV_HEADS):
        slab = slice((g // 2) * LANES, (g // 2 + 1) * LANES)
        k2 = k_all[:, slab]
        v2 = v_all[:, slab]
        k2r = pltpu.roll(k2, SWA_HD, 1)
        v2r = pltpu.roll(v2, SWA_HD, 1)
        if g % 2 == 0:
            k_lo, k_hi, v_lo, v_hi = k2, k2r, v2, v2r
        else:
            k_lo, k_hi, v_lo, v_hi = k2r, k2, v2r, v2
        zero = jnp.zeros_like(k2)
        one = jnp.ones_like(k2)
        km = (jnp.where(low, k_lo, zero).astype(BF16), jnp.where(low, zero, k_hi).astype(BF16))
        vm = (jnp.where(low, v_lo, one).astype(BF16), jnp.where(low, one, v_hi).astype(BF16))
        heads = [(j, half) for j in range(SWA_GROUP // 2) for half in range(2)]
        sinks = [sink_ref[g * SWA_GROUP + 2 * j + half] for j, half in heads]
        scores = []
        for j, half in heads:
            qs = q_ref[:, (g * 4 + j) * LANES:(g * 4 + j + 1) * LANES]
            s = lax.dot_general(qs, km[half], (((1,), (1,)), ((), ())), preferred_element_type=F32)
            scores.append(s + bias_scr[...])
        maxes = [jnp.maximum(jnp.max(s, axis=-1, keepdims=True), sk) for s, sk in zip(scores, sinks)]
        exps = [jnp.exp(s - m).astype(BF16) for s, m in zip(scores, maxes)]
        sink_terms = [jnp.exp(sk - m) for sk, m in zip(sinks, maxes)]
        for j in range(SWA_GROUP // 2):
            a_lo = jnp.dot(exps[2 * j], vm[0], preferred_element_type=F32)
            a_hi = jnp.dot(exps[2 * j + 1], vm[1], preferred_element_type=F32)
            num = jnp.where(low_q, a_lo, a_hi)
            den = pltpu.roll(jnp.where(low_q, a_hi, a_lo), SWA_HD, 1)
            den = den + jnp.where(low_q, sink_terms[2 * j], sink_terms[2 * j + 1])
            o_ref[:, (g * 4 + j) * LANES:(g * 4 + j + 1) * LANES] = (num / den).astype(o_ref.dtype)


def _swa(sinks, q, q_row0, k_prev, v_prev, prev_col, k_own, v_own, own_col, own_row0,
         batch, seq, tq, prev_from_cache):
    nt = seq // tq
    qb = q_row0 // tq
    ob = own_row0 // tq
    kv_w = SWA_KV_HEADS * SWA_HD
    if prev_from_cache:
        prev_map = lambda b, t, c: (b, c)
    else:
        per = seq // WINDOW
        prev_map = lambda b, t, c: (b * per + jnp.maximum(t * (tq // WINDOW) - 1, 0), c)
    kernel = functools.partial(_swa_kernel, tq=tq, prev_from_cache=prev_from_cache)
    return pl.pallas_call(
        kernel,
        out_shape=jax.ShapeDtypeStruct((batch * seq, SWA_HEADS * SWA_HD), BF16),
        grid=(batch, nt),
        in_specs=[pl.BlockSpec(memory_space=pltpu.SMEM),
                  pl.BlockSpec((tq, SWA_HEADS * SWA_HD), lambda b, t: (qb + b * nt + t, 0)),
                  pl.BlockSpec((WINDOW, kv_w), lambda b, t: prev_map(b, t, prev_col[0])),
                  pl.BlockSpec((WINDOW, kv_w), lambda b, t: prev_map(b, t, prev_col[1])),
                  pl.BlockSpec((tq, kv_w), lambda b, t: (ob + b * nt + t, own_col[0])),
                  pl.BlockSpec((tq, kv_w), lambda b, t: (ob + b * nt + t, own_col[1]))],
        out_specs=pl.BlockSpec((tq, SWA_HEADS * SWA_HD), lambda b, t: (b * nt + t, 0)),
        scratch_shapes=[pltpu.VMEM((tq, WINDOW + tq), F32)],
        compiler_params=_params(2),
        name="swa_band",
    )(sinks, q, k_prev, v_prev, k_own, v_own)


def _mem_attn_head(q, k, v):
    s = lax.dot_general(q, k.astype(BF16), (((1,), (1,)), ((), ())),
                        preferred_element_type=F32) * (MEM_HD ** -0.5)
    m = jnp.max(s, axis=-1, keepdims=True)
    e = jnp.exp(s - m)
    p = (e / jnp.sum(e, axis=-1, keepdims=True)).astype(BF16)
    return jnp.dot(p, v.astype(BF16), preferred_element_type=F32)


def _mem_attn_kernel(q_ref, kv_ref, o_ref, kv_bf):
    @pl.when(pl.program_id(1) == 0)
    def _():
        kv_bf[...] = kv_ref[...].astype(BF16)

    width = MEM_HEADS * MEM_HD
    cols = [slice(h * MEM_HD, (h + 1) * MEM_HD) for h in range(MEM_HEADS)]
    scores = [lax.dot_general(q_ref[:, c], kv_bf[:, c], (((1,), (1,)), ((), ())),
                              preferred_element_type=F32) * (MEM_HD ** -0.5) for c in cols]
    exps = [jnp.exp(s - jnp.max(s, axis=-1, keepdims=True)) for s in scores]
    probs = [(e / jnp.sum(e, axis=-1, keepdims=True)).astype(BF16) for e in exps]
    for c, p in zip(cols, probs):
        v = kv_bf[:, width + c.start:width + c.stop]
        o_ref[:, c] = jnp.dot(p, v, preferred_element_type=F32).astype(o_ref.dtype)


def _mem_attn_cache_kernel(q_ref, k_hbm, v_hbm, o_ref, kbuf, vbuf, sem):
    b = pl.program_id(0)
    slot = b % 2

    def copies(req, s):
        return ([pltpu.make_async_copy(k_hbm.at[0, req, :, h, :], kbuf.at[s, h], sem.at[s, 0])
                 for h in range(MEM_HEADS)]
                + [pltpu.make_async_copy(v_hbm.at[0, req, :, h, :], vbuf.at[s, h], sem.at[s, 1])
                   for h in range(MEM_HEADS)])

    @pl.when(b == 0)
    def _():
        for c in copies(0, 0):
            c.start()

    @pl.when(b + 1 < pl.num_programs(0))
    def _():
        for c in copies(b + 1, 1 - slot):
            c.start()

    for c in copies(b, slot):
        c.wait()
    for h in range(MEM_HEADS):
        cols = slice(h * MEM_HD, (h + 1) * MEM_HD)
        o_ref[:, cols] = _mem_attn_head(q_ref[:, cols], kbuf[slot, h], vbuf[slot, h]).astype(o_ref.dtype)


def _mem_attn_cache(q, q_row0, cache_k, cache_v, batch, seq):
    qb = q_row0 // seq
    width = MEM_HEADS * MEM_HD
    hbm = pl.BlockSpec(memory_space=pl.ANY)
    slabs = pltpu.VMEM((2, MEM_HEADS, N_MEM, MEM_HD), F32)
    return pl.pallas_call(
        _mem_attn_cache_kernel,
        out_shape=jax.ShapeDtypeStruct((batch * seq, width), BF16),
        grid=(batch,),
        in_specs=[pl.BlockSpec((seq, width), lambda b: (qb + b, 0)), hbm, hbm],
        out_specs=pl.BlockSpec((seq, width), lambda b: (b, 0)),
        scratch_shapes=[slabs, slabs, pltpu.SemaphoreType.DMA((2, 2))],
        compiler_params=_params(1),
        name="mem_attn_cache",
    )(q, cache_k, cache_v)


def _mem_attn(q, mem_kv, batch, seq, tq):
    nt = seq // tq
    width = MEM_HEADS * MEM_HD
    return pl.pallas_call(
        _mem_attn_kernel,
        out_shape=jax.ShapeDtypeStruct((batch * seq, width), BF16),
        grid=(batch, nt),
        in_specs=[pl.BlockSpec((tq, width), lambda b, t: (b * nt + t, 0)),
                  pl.BlockSpec((N_MEM, 2 * width), lambda b, t: (b, 0))],
        out_specs=pl.BlockSpec((tq, width), lambda b, t: (b * nt + t, 0)),
        scratch_shapes=[pltpu.VMEM((N_MEM, 2 * width), BF16)],
        compiler_params=_params(2),
        name="mem_attn",
    )(q, mem_kv)


def _merge_kernel(a0p, a0s, a1p, a1s, a2p, a2s, w_ref, g0_ref, g1_ref, g2_ref, o_ref, *, p_tiles):
    def emit(a0_ref, a1_ref, a2_ref):
        acc = g0_ref[...] * jnp.dot(a0_ref[...], w_ref[0], preferred_element_type=F32)
        acc = acc + g1_ref[...] * jnp.dot(a1_ref[...], w_ref[1], preferred_element_type=F32)
        acc = acc + g2_ref[...] * jnp.dot(a2_ref[...], w_ref[2], preferred_element_type=F32)
        o_ref[...] = acc.astype(o_ref.dtype)

    @pl.when(pl.program_id(1) < p_tiles)
    def _():
        emit(a0p, a1p, a2p)

    @pl.when(pl.program_id(1) >= p_tiles)
    def _():
        emit(a0s, a1s, a2s)


def _merge(branches_p, branches_s, w_branch, gates, tm, tn):
    n_p, d = branches_p[0].shape
    n = n_p + branches_s[0].shape[0]
    nj = d // tn
    p_tiles = n_p // tm
    ap = pl.BlockSpec((tm, d), lambda j, i: (jnp.minimum(i, p_tiles - 1), 0))
    asp = pl.BlockSpec((tm, d), lambda j, i: (jnp.maximum(i - p_tiles, 0), 0))
    operands = [a for pair in zip(branches_p, branches_s) for a in pair]
    return pl.pallas_call(
        functools.partial(_merge_kernel, p_tiles=p_tiles),
        out_shape=jax.ShapeDtypeStruct((n, d), BF16),
        grid=(nj, n // tm),
        in_specs=[ap, asp, ap, asp, ap, asp,
                  pl.BlockSpec((3, d, tn), lambda j, i: (0, 0, j), pipeline_mode=pl.Buffered(1)),
                  pl.BlockSpec((tm, tn), lambda j, i: (i, j)),
                  pl.BlockSpec((tm, tn), lambda j, i: (i, nj + j)),
                  pl.BlockSpec((tm, tn), lambda j, i: (i, 2 * nj + j))],
        out_specs=pl.BlockSpec((tm, tn), lambda j, i: (i, j)),
        compiler_params=_params(2),
        name="branch_merge",
    )(*operands, w_branch, gates, gates, gates)


def _outproj_router_kernel(m_ref, w_ref, xp_ref, xs_ref, g_ref, wr_ref, br_ref,
                           x2_ref, hf_ref, eid_ref, wt_ref, *, p_tiles):
    is_prompt = pl.program_id(0) < p_tiles
    tm, d = x2_ref.shape
    for r0 in range(0, tm, ROUTER_SUB_ROWS):
        rows = slice(r0, r0 + ROUTER_SUB_ROWS)
        acc = jnp.dot(m_ref[rows, :], w_ref[...], preferred_element_type=F32)
        x = jnp.where(is_prompt, xp_ref[rows, :], xs_ref[rows, :]) + acc
        x2_ref[rows, :] = x
        hf = x * lax.rsqrt(jnp.mean(x * x, axis=-1, keepdims=True) + EPS) * g_ref[...]
        hb = hf.astype(BF16)
        bits = pltpu.bitcast(hb.astype(F32), jnp.uint32)
        packed = bits[:, d // 2:] | (bits[:, :d // 2] >> 16)
        for s in range(TOKEN_TILE_ROWS):
            hf_ref[pl.ds(r0 * TOKEN_TILE_ROWS + s, ROUTER_SUB_ROWS, stride=TOKEN_TILE_ROWS), :] = (
                packed[:, s * LANES:(s + 1) * LANES])
        logits = jnp.dot(hb, wr_ref[...], preferred_element_type=F32) + br_ref[...]
        eid, wts = _route(logits)
        eid_ref[rows, :] = eid
        wt_ref[rows, :] = wts


def _route(logits):
    lane = lax.broadcasted_iota(jnp.int32, logits.shape, 1).astype(F32)
    big = 1e6
    is_g = lane < N_GROUPS
    lg = jnp.where(is_g, logits, NEG_BIG)
    mg = jnp.max(lg, axis=-1, keepdims=True)
    gsel = jnp.min(jnp.where(is_g & (lg == mg), lane, big), axis=-1, keepdims=True)
    g_w = 1.0 / jnp.sum(jnp.where(is_g, jnp.exp(lg - mg), 0.0), axis=-1, keepdims=True)
    e_lo = N_GROUPS + gsel * EXPERTS_PER_GROUP
    in_grp = (lane >= e_lo) & (lane < e_lo + EXPERTS_PER_GROUP)
    le = jnp.where(in_grp, logits, NEG_BIG)
    me = jnp.max(le, axis=-1, keepdims=True)
    ee = jnp.where(in_grp, jnp.exp(le - me), 0.0)
    pe = ee / jnp.sum(ee, axis=-1, keepdims=True)
    pe = jnp.where(in_grp, pe, -1.0)
    p1 = jnp.max(pe, axis=-1, keepdims=True)
    i1 = jnp.min(jnp.where(pe == p1, lane, big), axis=-1, keepdims=True)
    pe2 = jnp.where(lane == i1, -1.0, pe)
    p2 = jnp.max(pe2, axis=-1, keepdims=True)
    i2 = jnp.min(jnp.where(pe2 == p2, lane, big), axis=-1, keepdims=True)
    tot = p1 + p2
    w1 = g_w * p1 / tot
    w2 = g_w * p2 / tot
    eid = jnp.where(lane == 0.0, i1 - N_GROUPS, jnp.where(lane == 1.0, i2 - N_GROUPS, 0.0))
    return eid.astype(jnp.int32), jnp.where(lane == 0.0, w1, jnp.where(lane == 1.0, w2, 0.0))


def _outproj_router(merged, w_out, xp, xs, g, wr, br, tm):
    n, d = merged.shape
    p_tiles = xp.shape[0] // tm
    const = lambda shape: pl.BlockSpec(shape, lambda i: (0, 0), pipeline_mode=pl.Buffered(1))
    row = lambda width: pl.BlockSpec((tm, width), lambda i: (i, 0))
    return pl.pallas_call(
        functools.partial(_outproj_router_kernel, p_tiles=p_tiles),
        out_shape=(jax.ShapeDtypeStruct((n, d), F32),
                   jax.ShapeDtypeStruct((n * TOKEN_TILE_ROWS, LANES), jnp.uint32),
                   jax.ShapeDtypeStruct((n, LANES), jnp.int32),
                   jax.ShapeDtypeStruct((n, LANES), F32)),
        grid=(n // tm,),
        in_specs=[row(d), const((d, d)), *_split_specs(p_tiles, tm, d),
                  const((1, d)), const((d, LANES)), const((1, LANES))],
        out_specs=(row(d), pl.BlockSpec((tm * TOKEN_TILE_ROWS, LANES), lambda i: (i, 0)),
                   row(LANES), row(LANES)),
        compiler_params=_params(1),
        name="outproj_router",
    )(merged, w_out, xp, xs, g.reshape(1, d), wr, br)


def _moe_kernel(te_ref, nv_ref, nxt_ref, tok_ref, tok_next_ref, dst_ref, hf_hbm, wup_hbm, wdn_hbm, y_hbm,
                xg, yb, wup_f32, wdn_f32, wup_bf, wdn_bf, in_sem, out_sem, w_sem):
    t = pl.program_id(0)
    buf = t % 2
    rows_now = nv_ref[t]
    rows_next = nv_ref[t + 1]
    rows_prev = nv_ref[jnp.maximum(t - 1, 0)]
    valid = rows_now > 0
    valid_next = rows_next > 0
    new_expert = (t == 0) | (te_ref[t] != te_ref[jnp.maximum(t - 1, 0)])

    def weight_copies(e):
        return (pltpu.make_async_copy(wup_hbm.at[e], wup_f32, w_sem.at[0]),
                pltpu.make_async_copy(wdn_hbm.at[e], wdn_f32, w_sem.at[1]))

    in_rows, in_pitch = TOKEN_TILE_ROWS, TOKEN_TILE_PITCH
    out_rows, out_pitch = TOKEN_F32_ROWS, TOKEN_F32_PITCH

    def row_in(row0, r, b):
        return pltpu.make_async_copy(hf_hbm.at[pl.ds(pl.multiple_of(row0, in_rows), in_rows), :],
                                     xg.at[b, pl.ds(r * in_pitch, in_rows), :], in_sem.at[b])

    def row_out(row0, r, b):
        return pltpu.make_async_copy(yb.at[b, pl.ds(r * out_pitch, out_rows), :],
                                     y_hbm.at[pl.ds(pl.multiple_of(row0, out_rows), out_rows), :], out_sem.at[b])

    group = MOE_ROW_GROUP
    n_groups = MOE_TILE // group

    def group_in(b):
        return pltpu.make_async_copy(hf_hbm.at[pl.ds(0, group * in_rows), :],
                                     xg.at[b, pl.ds(0, group * in_rows), :], in_sem.at[b])

    def group_out(b):
        return pltpu.make_async_copy(yb.at[b, pl.ds(0, group * out_rows), :],
                                     y_hbm.at[pl.ds(0, group * out_rows), :], out_sem.at[b])

    def per_started_group(rows, fn):
        for g in range(n_groups):
            pl.when(rows > g * group)(functools.partial(fn, g))

    @pl.when(t == 0)
    def _():
        for c in weight_copies(te_ref[0]):
            c.start(priority=1)
        xg[...] = jnp.zeros_like(xg)

        def first(r, c):
            row_in(tok_ref[0, 0, r], r, 0).start()
            return c
        lax.fori_loop(0, ((rows_now + group - 1) // group) * group, first, 0)
        yb[...] = jnp.zeros_like(yb)
        n_real = y_hbm.shape[0] - 2 * MOE_TILE * out_rows
        for b in range(2):
            spare = pltpu.make_async_copy(
                yb.at[b, pl.ds(0, MOE_TILE * out_rows), :],
                y_hbm.at[pl.ds(n_real + b * MOE_TILE * out_rows, MOE_TILE * out_rows), :], out_sem.at[b])
            spare.start()
            spare.wait()

    @pl.when(valid & new_expert)
    def _():
        for c in weight_copies(te_ref[t]):
            c.wait()
        wup_bf[...] = wup_f32[...].astype(BF16)
        wdn_bf[...] = wdn_f32[...].astype(BF16)

        @pl.when(nxt_ref[t] >= 0)
        def _():
            for c in weight_copies(nxt_ref[t]):
                c.start(priority=1)

    def gather_next(g):
        for r in range(g * group, (g + 1) * group):
            row_in(tok_next_ref[0, 0, r], r, 1 - buf).start()

    def scatter_now(g):
        for r in range(g * group, (g + 1) * group):
            row_out(dst_ref[0, 0, r], r, buf).start(priority=r % 2)

    per_started_group(rows_next, gather_next)

    @pl.when(valid)
    def _():
        per_started_group(rows_now, lambda g: group_in(buf).wait())
        lo, hi = [], []
        for s in range(in_rows):
            word = xg[buf, pl.ds(s, MOE_TILE, stride=in_pitch), :]
            lo.append(pltpu.bitcast(word << 16, F32).astype(BF16))
            hi.append(pltpu.bitcast(word & jnp.uint32(0xFFFF0000), F32).astype(BF16))
        x = jnp.concatenate(lo + hi, axis=1)
        h1 = jnp.dot(x, wup_bf[...], preferred_element_type=F32)
        gate = h1[:, :D_FF]
        up = h1[:, D_FF:]
        act = (gate * _sigmoid(gate)) * up
        ye = jnp.dot(act.astype(BF16), wdn_bf[...], preferred_element_type=F32)
        for s in range(out_rows):
            yb[buf, pl.ds(s, MOE_TILE, stride=out_pitch), :] = ye[:, s * LANES:(s + 1) * LANES]
        per_started_group(rows_now, scatter_now)

        @pl.when(t > 0)
        def _():
            per_started_group(rows_prev, lambda g: group_out(1 - buf).wait())

        @pl.when(jnp.logical_not(valid_next))
        def _():
            per_started_group(rows_now, lambda g: group_out(buf).wait())


def _moe(hf, tile_expert, tile_rows, next_expert, tok_slots, dst_slots, w_up, w_down, out_tokens):
    n_tiles = tile_expert.shape[0]
    d = w_up.shape[1]
    slot_spec = lambda off: pl.BlockSpec((1, 1, MOE_TILE),
                                         lambda t, te, nv, ne: (jnp.minimum(t + off, n_tiles - 1), 0, 0),
                                         memory_space=pltpu.SMEM)
    hbm = pl.BlockSpec(memory_space=pl.ANY)
    grid_spec = pltpu.PrefetchScalarGridSpec(
        num_scalar_prefetch=3,
        grid=(n_tiles,),
        in_specs=[slot_spec(0), slot_spec(1), slot_spec(0), hbm, hbm, hbm],
        out_specs=hbm,
        scratch_shapes=[pltpu.VMEM((2, MOE_TILE * TOKEN_TILE_PITCH, LANES), jnp.uint32),
                        pltpu.VMEM((2, MOE_TILE * TOKEN_F32_PITCH, LANES), F32),
                        pltpu.VMEM((d, 2 * D_FF), F32),
                        pltpu.VMEM((D_FF, d), F32),
                        pltpu.VMEM((d, 2 * D_FF), BF16),
                        pltpu.VMEM((D_FF, d), BF16),
                        pltpu.SemaphoreType.DMA((2,)),
                        pltpu.SemaphoreType.DMA((2,)),
                        pltpu.SemaphoreType.DMA((2,))],
    )
    return pl.pallas_call(
        _moe_kernel,
        out_shape=jax.ShapeDtypeStruct((out_tokens * TOKEN_F32_ROWS, LANES), F32),
        grid_spec=grid_spec,
        compiler_params=_params(1),
        name="moe_experts",
    )(tile_expert, tile_rows, next_expert, tok_slots, tok_slots, dst_slots, hf, w_up, w_down)


def _combine_kernel(x_ref, y0_ref, y1_ref, w_ref, op_ref, os_ref, *, p_tiles):
    tm = x_ref.shape[0]
    w = w_ref[...]
    w0 = w[:, 0:1]
    w1 = w[:, 1:2]

    def emit(o_ref):
        for s in range(TOKEN_F32_ROWS):
            cols = slice(s * LANES, (s + 1) * LANES)
            rows = pl.ds(s, tm, stride=TOKEN_F32_ROWS)
            o_ref[:, cols] = x_ref[:, cols] + (y0_ref[rows, :] * w0 + y1_ref[rows, :] * w1)

    @pl.when(pl.program_id(0) < p_tiles)
    def _():
        emit(op_ref)

    @pl.when(pl.program_id(0) >= p_tiles)
    def _():
        emit(os_ref)


def _combine(x2, yk, wts, n_p, tm):
    n, d = x2.shape
    p_tiles = n_p // tm
    k1 = n // tm
    y_rows = tm * TOKEN_F32_ROWS
    return pl.pallas_call(
        functools.partial(_combine_kernel, p_tiles=p_tiles),
        out_shape=(jax.ShapeDtypeStruct((n_p, d), F32), jax.ShapeDtypeStruct((n - n_p, d), F32)),
        grid=(n // tm,),
        in_specs=[pl.BlockSpec((tm, d), lambda i: (i, 0)),
                  pl.BlockSpec((y_rows, LANES), lambda i: (i, 0)),
                  pl.BlockSpec((y_rows, LANES), lambda i: (k1 + i, 0)),
                  pl.BlockSpec((tm, LANES), lambda i: (i, 0))],
        out_specs=_split_specs(p_tiles, tm, d),
        compiler_params=_params(1),
        name="moe_combine",
    )(x2, yk, yk, wts)


def _rope_tables(pos):
    half = ROPE_DIM // 2
    inv = ROPE_THETA ** (-jnp.arange(half, dtype=F32) / half)
    ang = pos.astype(F32)[:, None] * inv[None, :]
    cos, sin = jnp.cos(ang), jnp.sin(ang)
    n = pos.shape[0]
    pad = jnp.zeros((n, SWA_HD - ROPE_DIM), F32)
    cos_h = jnp.concatenate([cos, cos, pad + 1.0], axis=1)
    sa_h = jnp.concatenate([-sin, jnp.zeros_like(sin), pad], axis=1)
    sb_h = jnp.concatenate([jnp.zeros_like(sin), sin, pad], axis=1)
    reps = LANES // SWA_HD
    return tuple(jnp.tile(a, (1, reps)) for a in (cos_h, sa_h, sb_h))


def _moe_schedule(eid, n_tok, n_tiles):
    a = eid.shape[0]
    order = jnp.argsort(eid, stable=True).astype(jnp.int32)
    counts = jnp.bincount(eid, length=N_EXPERTS).astype(jnp.int32)
    tiles_per = (counts + MOE_TILE - 1) // MOE_TILE
    tile_end = jnp.cumsum(tiles_per)
    tile_start = tile_end - tiles_per
    sorted_start = jnp.cumsum(counts) - counts
    tile_id = jnp.arange(n_tiles, dtype=jnp.int32)
    used = tile_id < tile_end[-1]
    te = jnp.minimum(jnp.sum(tile_end[None, :] <= tile_id[:, None], axis=1), N_EXPERTS - 1).astype(jnp.int32)
    last_used_e = te[jnp.maximum(tile_end[-1] - 1, 0)]
    te = jnp.where(used, te, last_used_e)
    row_in_expert = (tile_id - tile_start[te]) * MOE_TILE
    rows_valid = jnp.where(used, jnp.clip(counts[te] - row_in_expert, 0, MOE_TILE), 0).astype(jnp.int32)
    r = jnp.arange(MOE_TILE, dtype=jnp.int32)[None, :]
    src = sorted_start[te][:, None] + row_in_expert[:, None] + r
    real = r < rows_valid[:, None]
    assign = order[jnp.clip(src, 0, a - 1)]
    tok = jnp.where(real, assign % n_tok, 0) * TOKEN_TILE_ROWS
    spare = a + (tile_id[:, None] % 2) * MOE_TILE + r
    dst = jnp.where(real, assign, spare) * TOKEN_F32_ROWS
    tile_rows = jnp.concatenate([rows_valid, jnp.zeros((1,), jnp.int32)])
    e_id = jnp.arange(N_EXPERTS, dtype=jnp.int32)[None, :]
    later = (e_id > te[:, None]) & (counts[None, :] > 0)
    nxt = jnp.min(jnp.where(later, e_id, N_EXPERTS), axis=1)
    nxt = jnp.where(nxt < N_EXPERTS, nxt, -1).astype(jnp.int32)
    shape = (n_tiles, 1, MOE_TILE)
    return te, tile_rows, nxt, tok.reshape(shape).astype(jnp.int32), dst.reshape(shape).astype(jnp.int32)


def kernel(x_prompt, x_sample, state_gla, cache_swa_k, cache_swa_v, cache_mem_k, cache_mem_v,
           mem_prompt, norm_mix_g, w_in, w_a2, b_a2, gla_norm_g, swa_q_norm_g, swa_k_norm_g,
           swa_sinks, norm_mem_g, w_mem_kv, mem_q_norm_g, mem_k_norm_g, w_gate, b_gate,
           w_branch, w_out, norm_ffn_g, w_router_group, b_router_group, w_router_expert,
           b_router_expert, w_up, w_down):
    bp, tp, d = x_prompt.shape
    bs, ts, _ = x_sample.shape
    n_p, n_s = bp * tp, bs * ts
    n = n_p + n_s
    tm, tn = ROW_TILE, COL_TILE
    tp_rows = PROJ_ROWS if n % PROJ_ROWS == 0 else tm
    heavy_sub = tp_rows // 2
    assert d == D_MODEL and n_p % tm == 0 and n_s % tm == 0 and w_in.shape[0] == 1
    keep_s = cache_swa_k.shape[2]
    assert keep_s == WINDOW and tp % WINDOW == 0

    qk_w = GLA_HEADS * GLA_DK
    v_w = GLA_HEADS * GLA_DV
    c0 = 2 * qk_w + 2 * v_w
    sq_w = SWA_HEADS * SWA_HD
    kv_w = SWA_KV_HEADS * SWA_HD
    mem_w = MEM_HEADS * MEM_HD
    w_in_t = jnp.transpose(w_in[0])
    w_branch_b = w_branch[0].astype(BF16)
    w_out_b = w_out[0].astype(BF16)
    w_a2_b = jnp.pad(w_a2[0], ((0, LANES - GLA_LOWRANK), (0, 0))).astype(BF16)
    w_router = jnp.pad(jnp.concatenate([w_router_group[0], w_router_expert[0]], axis=1),
                       ((0, 0), (0, LANES - N_GROUPS - N_EXPERTS))).astype(BF16)
    b_router = jnp.pad(jnp.concatenate([b_router_group[0], b_router_expert[0]]),
                       (0, LANES - N_GROUPS - N_EXPERTS)).reshape(1, LANES)

    pos = jnp.concatenate([jnp.tile(jnp.arange(tp, dtype=jnp.int32), bp),
                           jnp.tile(PAST_LEN + jnp.arange(ts, dtype=jnp.int32), bs)])
    cos_t, sa_t, sb_t = _rope_tables(pos)
    seg_id = jnp.arange(tn, dtype=jnp.int32) // SWA_HD
    seg = (seg_id[:, None] == seg_id[None, :]).astype(BF16)
    rope_specs = [pl.BlockSpec((tp_rows, LANES), lambda j, i: (i, 0))] * 3
    row_vec = lambda width: pl.BlockSpec((1, width), lambda j, i: (0, 0))
    seg_spec = pl.BlockSpec((tn, tn), lambda j, i: (0, 0))

    xp2 = x_prompt.reshape(n_p, d)
    xs2 = x_sample.reshape(n_s, d)
    h = _rms_norm_two(xp2, xs2, norm_mix_g[0], tm)

    c1 = c0 + GLA_LOWRANK
    c2 = c1 + sq_w
    c3 = c2 + 2 * kv_w
    qkvg = _wmatmul(_ep_plain, h, w_in_t, 0, c0, True, [], [], F32, tp_rows, WIDE_COLS, "proj_gla",
                    sub_rows=heavy_sub)
    ga = _wmatmul(_ep_lowrank, h, w_in_t, c0, LANES, True, [], [], F32, tp_rows, LANES, "proj_gla_lowrank")
    q_gain = jnp.tile(swa_q_norm_g[0] * (SWA_HD ** -0.5), tn // SWA_HD).reshape(1, tn)
    q_swa = _wmatmul(functools.partial(_ep_qknorm_rope, keep_from=None), h, w_in_t, c1, sq_w, True,
                     [q_gain, seg, cos_t, sa_t, sb_t], [row_vec(tn), seg_spec] + rope_specs,
                     BF16, tp_rows, tn, "proj_swa_q", sub_rows=heavy_sub)
    k_gain = jnp.tile(swa_k_norm_g[0], tn // SWA_HD).reshape(1, tn)
    kv_swa = _wmatmul(functools.partial(_ep_qknorm_rope, keep_from=kv_w), h, w_in_t, c2, 2 * kv_w, True,
                      [k_gain, seg, cos_t, sa_t, sb_t], [row_vec(tn), seg_spec] + rope_specs,
                      F32, tp_rows, tn, "proj_swa_kv", sub_rows=heavy_sub)
    q_mem = _wmatmul(functools.partial(_ep_headnorm, norm_tiles=None), h, w_in_t, c3, mem_w, True,
                     [mem_q_norm_g[0].reshape(1, MEM_HD)], [row_vec(MEM_HD)],
                     BF16, tp_rows, MEM_HD, "proj_mem_q", sub_rows=heavy_sub)
    gates = _wmatmul(_ep_sigmoid, h, w_gate[0], 0, 3 * d, False, [b_gate[0].reshape(1, -1)],
                     [pl.BlockSpec((1, WIDE_COLS), lambda j, i: (0, j))], F32, tp_rows, WIDE_COLS, "proj_gates",
                     sub_rows=heavy_sub)

    mem_rows = bp * N_MEM
    hm = _rms_norm_rows(mem_prompt.reshape(mem_rows, d), norm_mem_g[0], BF16, N_MEM)
    mem_kv = _wmatmul(functools.partial(_ep_headnorm, norm_tiles=MEM_HEADS), hm, w_mem_kv[0], 0, 2 * mem_w, False,
                      [mem_k_norm_g[0].reshape(1, MEM_HD)], [row_vec(MEM_HD)], F32, mem_rows, MEM_HD, "mem_kv")

    ba = b_a2[0].reshape(1, qk_w)
    gn = gla_norm_g[0].reshape(1, GLA_DV)
    o_gla_p, gla_state_p = _gla(qkvg, ga, w_a2_b, ba, gn, None, bp, tp, 0, 256, CHUNK)
    o_gla_s, gla_state_s = _gla(qkvg, ga, w_a2_b, ba, gn, state_gla[0], bs, ts, n_p, ts, min(CHUNK, ts))

    sinks = swa_sinks[0]
    o_swa_p = _swa(sinks, q_swa, 0, kv_swa, kv_swa, (0, 1), kv_swa, kv_swa, (0, 1), 0,
                   bp, tp, WINDOW, False)
    ck = cache_swa_k[0].reshape(bs * keep_s, kv_w)
    cv = cache_swa_v[0].reshape(bs * keep_s, kv_w)
    o_swa_s = _swa(sinks, q_swa, n_p, ck, cv, (0, 0), kv_swa, kv_swa, (0, 1), n_p,
                   bs, ts, ts, True)

    o_mem_p = _mem_attn(q_mem, mem_kv, bp, tp, tm)
    o_mem_s = _mem_attn_cache(q_mem, n_p, cache_mem_k, cache_mem_v, bs, ts)

    merged = _merge((o_gla_p, o_swa_p, o_mem_p), (o_gla_s, o_swa_s, o_mem_s), w_branch_b, gates, tm, WIDE_COLS)

    x2, hf, eid, wts = _outproj_router(merged, w_out_b, xp2, xs2, norm_ffn_g[0], w_router, b_router, tm)

    n_assign = TOP_K * n
    n_tiles = n_assign // MOE_TILE + N_EXPERTS
    eid_kmajor = jnp.concatenate([eid[:, k] for k in range(TOP_K)])
    tile_expert, tile_rows, next_expert, tok_slots, dst_slots = _moe_schedule(eid_kmajor, n, n_tiles)
    yk = _moe(hf, tile_expert, tile_rows, next_expert, tok_slots, dst_slots, w_up[0], w_down[0],
              n_assign + 2 * MOE_TILE)
    y_p, y_s = _combine(x2, yk, wts, n_p, tm)

    y_p = y_p.reshape(bp, tp, d)
    y_s = y_s.reshape(bs, ts, d)
    kv_p = jnp.stack([kv_swa[(b + 1) * tp - WINDOW:(b + 1) * tp] for b in range(bp)])
    kv_p = kv_p.reshape(bp, WINDOW, 2, SWA_KV_HEADS, SWA_HD)
    kv_s = kv_swa[n_p:].reshape(bs, ts, 2, SWA_KV_HEADS, SWA_HD)
    swk_s = jnp.concatenate([cache_swa_k[0], kv_s[:, :, 0]], axis=1)[:, ts:ts + keep_s]
    swv_s = jnp.concatenate([cache_swa_v[0], kv_s[:, :, 1]], axis=1)[:, ts:ts + keep_s]
    mk_p = mem_kv[:, :mem_w].reshape(bp, N_MEM, MEM_HEADS, MEM_HD)
    mv_p = mem_kv[:, mem_w:].reshape(bp, N_MEM, MEM_HEADS, MEM_HD)
    return (y_p, y_s, gla_state_p[None], kv_p[:, :, 0][None], kv_p[:, :, 1][None], mk_p[None], mv_p[None],
            gla_state_s[None], swk_s[None], swv_s[None])
```

```python
import functools

import jax
import jax.numpy as jnp
from jax import lax
from jax.experimental import pallas as pl
from jax.experimental.pallas import tpu as pltpu

F32 = jnp.float32
BF16 = jnp.bfloat16

D_MODEL = 2048
CHUNK = 64
EPS = 1e-6
PAST_LEN = 1024
GLA_HEADS = 4
GLA_DV = 512
GLA_DK = 256
GLA_LOWRANK = 16
GLA_NORMALIZER = 16.0
SWA_HD = 64
SWA_HEADS = 32
SWA_KV_HEADS = 4
SWA_GROUP = 8
WINDOW = 128
ROPE_DIM = 16
ROPE_THETA = 500000.0
N_MEM = 256
MEM_HEADS = 4
MEM_HD = 512
N_GROUPS = 8
EXPERTS_PER_GROUP = 8
N_EXPERTS = 64
TOP_K = 2
D_FF = 512

LANES = 128
VMEM_LIMIT = 56 * 1024 * 1024
ROW_TILE = 512
PROJ_ROWS = 1536
COL_TILE = 512
WIDE_COLS = 1024
ROUTER_SUB_ROWS = 256
MOE_TILE = 256
MOE_ROW_GROUP = 32
TOKEN_TILE_ROWS = D_MODEL // 2 // LANES
TOKEN_TILE_PITCH = 12
TOKEN_F32_ROWS = D_MODEL // LANES
TOKEN_F32_PITCH = 20
NEG_BIG = -1e30


def _params(n_axes):
    return pltpu.CompilerParams(dimension_semantics=("arbitrary",) * n_axes,
                                vmem_limit_bytes=VMEM_LIMIT)


def _norm_kernel(x_ref, g_ref, o_ref):
    x = x_ref[...]
    y = x * lax.rsqrt(jnp.mean(x * x, axis=-1, keepdims=True) + EPS)
    o_ref[...] = (y * g_ref[...]).astype(o_ref.dtype)


def _rms_norm_rows(x, g, out_dtype, tm):
    n, d = x.shape
    return pl.pallas_call(
        _norm_kernel,
        out_shape=jax.ShapeDtypeStruct((n, d), out_dtype),
        grid=(n // tm,),
        in_specs=[pl.BlockSpec((tm, d), lambda i: (i, 0)),
                  pl.BlockSpec((1, d), lambda i: (0, 0))],
        out_specs=pl.BlockSpec((tm, d), lambda i: (i, 0)),
        compiler_params=_params(1),
        name="rms_norm_rows",
    )(x, g.reshape(1, d))


def _norm2_kernel(xp_ref, xs_ref, g_ref, o_ref, *, p_tiles):
    def emit(x_ref):
        x = x_ref[...]
        y = x * lax.rsqrt(jnp.mean(x * x, axis=-1, keepdims=True) + EPS)
        o_ref[...] = (y * g_ref[...]).astype(o_ref.dtype)

    @pl.when(pl.program_id(0) < p_tiles)
    def _():
        emit(xp_ref)

    @pl.when(pl.program_id(0) >= p_tiles)
    def _():
        emit(xs_ref)


def _split_specs(p_tiles, tm, d):
    return (pl.BlockSpec((tm, d), lambda i: (jnp.minimum(i, p_tiles - 1), 0)),
            pl.BlockSpec((tm, d), lambda i: (jnp.maximum(i - p_tiles, 0), 0)))


def _rms_norm_two(xp, xs, g, tm):
    (n_p, d), n_s = xp.shape, xs.shape[0]
    p_tiles = n_p // tm
    return pl.pallas_call(
        functools.partial(_norm2_kernel, p_tiles=p_tiles),
        out_shape=jax.ShapeDtypeStruct((n_p + n_s, d), BF16),
        grid=((n_p + n_s) // tm,),
        in_specs=[*_split_specs(p_tiles, tm, d), pl.BlockSpec((1, d), lambda i: (0, 0))],
        out_specs=pl.BlockSpec((tm, d), lambda i: (i, 0)),
        compiler_params=_params(1),
        name="rms_norm_mix",
    )(xp, xs, g.reshape(1, d))


def _segment_rms(acc, seg_ref, inv_width):
    ss = jnp.dot((acc * acc).astype(BF16), seg_ref[...], preferred_element_type=F32)
    return ss * inv_width


def _rope(y, rows, cos_ref, sa_ref, sb_ref):
    width = y.shape[1]
    reps = width // LANES
    c = jnp.concatenate([cos_ref[rows, :]] * reps, axis=1)
    sa = jnp.concatenate([sa_ref[rows, :]] * reps, axis=1)
    sb = jnp.concatenate([sb_ref[rows, :]] * reps, axis=1)
    half = ROPE_DIM // 2
    return y * c + pltpu.roll(y, width - half, 1) * sa + pltpu.roll(y, half, 1) * sb


def _ep_plain(acc, rows):
    return acc


def _ep_lowrank(acc, rows):
    lane = lax.broadcasted_iota(jnp.int32, acc.shape, 1)
    return jnp.where(lane < GLA_LOWRANK, acc, 0.0)


def _sigmoid(x):
    return 0.5 * jnp.tanh(0.5 * x) + 0.5


def _ep_sigmoid(acc, rows, b_ref):
    return _sigmoid(acc + b_ref[...])


def _ep_qknorm_rope(acc, rows, g_ref, seg_ref, cos_ref, sa_ref, sb_ref, *, keep_from):
    ms = _segment_rms(acc, seg_ref, 1.0 / SWA_HD)
    y = acc * lax.rsqrt(ms + EPS) * g_ref[...]
    y = _rope(y, rows, cos_ref, sa_ref, sb_ref)
    if keep_from is not None:
        col = lax.broadcasted_iota(jnp.int32, y.shape, 1)
        y = jnp.where(col < keep_from, y, acc)
    return y


def _ep_headnorm(acc, rows, g_ref, *, norm_tiles):
    y = acc * lax.rsqrt(jnp.mean(acc * acc, axis=-1, keepdims=True) + EPS) * g_ref[...]
    if norm_tiles is not None:
        y = jnp.where(pl.program_id(0) < norm_tiles, y, acc)
    return y


def _wmm_kernel(a_ref, w_ref, *rest, w_is_transposed, sub_rows, epilogue):
    extras, o_ref, wbf = rest[:-2], rest[-2], rest[-1]

    @pl.when(pl.program_id(1) == 0)
    def _():
        wbf[...] = w_ref[...].astype(BF16)

    for r0 in range(0, a_ref.shape[0], sub_rows):
        rows = slice(r0, r0 + sub_rows)
        if w_is_transposed:
            acc = lax.dot_general(a_ref[rows, :], wbf[...], (((1,), (1,)), ((), ())), preferred_element_type=F32)
        else:
            acc = jnp.dot(a_ref[rows, :], wbf[...], preferred_element_type=F32)
        o_ref[rows, :] = epilogue(acc, rows, *extras).astype(o_ref.dtype)


def _wmatmul(epilogue, a, w, col0, n_cols, w_is_transposed, extras, extra_specs, out_dtype, tm, tn, name,
             sub_rows=None):
    m, k = a.shape
    sub_rows = tm if sub_rows is None else sub_rows
    assert m % tm == 0 and n_cols % tn == 0 and tm % sub_rows == 0
    if w_is_transposed:
        assert col0 % 8 == 0
        w_spec = pl.BlockSpec((pl.Element(tn), pl.Element(k)), lambda j, i: (pl.multiple_of(col0 + j * tn, 8), 0))
        w_tile = (tn, k)
    else:
        assert col0 % tn == 0
        w_spec = pl.BlockSpec((k, tn), lambda j, i: (0, col0 // tn + j))
        w_tile = (k, tn)
    kernel = functools.partial(_wmm_kernel, w_is_transposed=w_is_transposed, sub_rows=sub_rows, epilogue=epilogue)
    return pl.pallas_call(
        kernel,
        out_shape=jax.ShapeDtypeStruct((m, n_cols), out_dtype),
        grid=(n_cols // tn, m // tm),
        in_specs=[pl.BlockSpec((tm, k), lambda j, i: (i, 0)), w_spec] + list(extra_specs),
        out_specs=pl.BlockSpec((tm, tn), lambda j, i: (i, j)),
        scratch_shapes=[pltpu.VMEM(w_tile, BF16)],
        compiler_params=_params(2),
        name=name,
    )(a, w, *extras)


def _gla_kernel(q_ref, k_ref, v_ref, gg_ref, ga_ref, wa_ref, ba_ref, gn_ref, *rest, chunk, n_chunks, has_s0):
    s0_ref = rest[0] if has_s0 else None
    o_ref, sout_ref, s_scr, qt_scr, ku_scr, o_scr, dec_scr = rest[1:] if has_s0 else rest
    _gla_block(q_ref, k_ref, v_ref, gg_ref, ga_ref, wa_ref, ba_ref, gn_ref, s0_ref,
               o_ref, sout_ref, s_scr, qt_scr, ku_scr, o_scr, dec_scr, chunk=chunk, n_chunks=n_chunks)


def _gla_block(q_ref, k_ref, v_ref, gg_ref, ga_ref, wa_ref, ba_ref, gn_ref, s0_ref,
               o_ref, sout_ref, s_scr, qt_scr, ku_scr, o_scr, dec_scr, *, chunk, n_chunks):
    t = pl.program_id(1)
    tb = chunk * n_chunks
    heads = [(slice(h * GLA_DK, (h + 1) * GLA_DK), slice(h * GLA_DV, (h + 1) * GLA_DV)) for h in range(GLA_HEADS)]

    @pl.when(t == 0)
    def _():
        s_scr[...] = jnp.zeros_like(s_scr) if s0_ref is None else s0_ref[0]

    row = lax.broadcasted_iota(jnp.int32, (chunk, chunk), 0)
    col = lax.broadcasted_iota(jnp.int32, (chunk, chunk), 1)
    tril = (row >= col).astype(BF16)
    z = jnp.dot(ga_ref[...].astype(BF16), wa_ref[...], preferred_element_type=F32) + ba_ref[...]
    log_a = (jnp.minimum(z, 0.0) - jnp.log(1.0 + jnp.exp(-jnp.abs(z)))) * (1.0 / GLA_NORMALIZER)
    hi = log_a.astype(BF16)
    rest = log_a - hi.astype(F32)
    mid = rest.astype(BF16)
    lo = (rest - mid.astype(F32)).astype(BF16)
    b_parts, last_parts = [], []
    for ci in range(n_chunks):
        crows = slice(ci * chunk, (ci + 1) * chunk)
        b_c = (jnp.dot(tril, hi[crows], preferred_element_type=F32)
               + jnp.dot(tril, mid[crows], preferred_element_type=F32)
               + jnp.dot(tril, lo[crows], preferred_element_type=F32))
        b_last = b_c[chunk - 1:chunk, :]
        b_parts.append(b_c)
        last_parts.append(jnp.broadcast_to(b_last, b_c.shape))
        for h, (ks, _) in enumerate(heads):
            dec_scr[ci, h] = jnp.transpose(jnp.broadcast_to(jnp.exp(b_last[:, ks]), (LANES, GLA_DK)))
    b = jnp.concatenate(b_parts, axis=0)
    b_last_rows = jnp.concatenate(last_parts, axis=0)

    q = q_ref[...] * (GLA_DK ** -0.5)
    k = k_ref[...]
    q_t = (q * jnp.exp(b)).astype(BF16)
    k_t = (k * jnp.exp(-b)).astype(BF16)
    qt_scr[...] = q_t
    ku_scr[...] = (k * jnp.exp(b_last_rows - b)).astype(BF16)

    brow = lax.broadcasted_iota(jnp.int32, (tb, tb), 0)
    bcol = lax.broadcasted_iota(jnp.int32, (tb, tb), 1)
    mask = (bcol >= (brow & -chunk)) & (brow >= bcol)
    for ks, vs in heads:
        att = lax.dot_general(q_t[:, ks], k_t[:, ks], (((1,), (1,)), ((), ())), preferred_element_type=F32)
        att = jnp.where(mask, att, 0.0).astype(BF16)
        o_scr[:, vs] = jnp.dot(att, v_ref[:, vs].astype(BF16), preferred_element_type=F32)

    def one_chunk(ci, carry):
        rows = pl.ds(pl.multiple_of(ci * chunk, chunk), chunk)
        for h, (ks, vs) in enumerate(heads):
            s_old = s_scr[h]
            o_scr[rows, vs] += jnp.dot(qt_scr[rows, ks], s_old.astype(BF16), preferred_element_type=F32)
            decay = jnp.concatenate([dec_scr[ci, h]] * (GLA_DV // LANES), axis=1)
            s_scr[h] = decay * s_old + lax.dot_general(ku_scr[rows, ks], v_ref[rows, vs].astype(BF16),
                                                       (((0,), (0,)), ((), ())), preferred_element_type=F32)
        return carry

    lax.fori_loop(0, n_chunks, one_chunk, 0)

    for _, vs in heads:
        o = o_scr[:, vs]
        on = o * lax.rsqrt(jnp.mean(o * o, axis=-1, keepdims=True) + EPS) * gn_ref[...]
        gg = gg_ref[:, vs]
        o_ref[:, vs] = (on * (gg * _sigmoid(gg))).astype(o_ref.dtype)

    @pl.when(t == pl.num_programs(1) - 1)
    def _():
        sout_ref[0] = s_scr[...]


def _gla(qkvg, ga, wa, ba, gn, s0, batch, seq, row0, tb, chunk):
    nt = seq // tb
    base = row0 // tb
    qk_w = GLA_HEADS * GLA_DK
    v_w = GLA_HEADS * GLA_DV
    rows = lambda b, t: base + b * nt + t
    has_s0 = s0 is not None
    kernel = functools.partial(_gla_kernel, chunk=chunk, n_chunks=tb // chunk, has_s0=has_s0)
    state_spec = pl.BlockSpec((1, GLA_HEADS, GLA_DK, GLA_DV), lambda b, t: (b, 0, 0, 0))
    return pl.pallas_call(
        kernel,
        out_shape=(jax.ShapeDtypeStruct((batch * seq, v_w), BF16),
                   jax.ShapeDtypeStruct((batch, GLA_HEADS, GLA_DK, GLA_DV), F32)),
        grid=(batch, nt),
        in_specs=[pl.BlockSpec((tb, qk_w), lambda b, t: (rows(b, t), 0)),
                  pl.BlockSpec((tb, qk_w), lambda b, t: (rows(b, t), 1)),
                  pl.BlockSpec((tb, v_w), lambda b, t: (rows(b, t), 1)),
                  pl.BlockSpec((tb, v_w), lambda b, t: (rows(b, t), 2)),
                  pl.BlockSpec((tb, LANES), lambda b, t: (rows(b, t), 0)),
                  pl.BlockSpec((LANES, qk_w), lambda b, t: (0, 0)),
                  pl.BlockSpec((1, qk_w), lambda b, t: (0, 0)),
                  pl.BlockSpec((1, GLA_DV), lambda b, t: (0, 0))] + ([state_spec] if has_s0 else []),
        out_specs=(pl.BlockSpec((tb, v_w), lambda b, t: (b * nt + t, 0)),
                   pl.BlockSpec((1, GLA_HEADS, GLA_DK, GLA_DV), lambda b, t: (b, 0, 0, 0))),
        scratch_shapes=[pltpu.VMEM((GLA_HEADS, GLA_DK, GLA_DV), F32),
                        pltpu.VMEM((tb, qk_w), BF16),
                        pltpu.VMEM((tb, qk_w), BF16),
                        pltpu.VMEM((tb, v_w), F32),
                        pltpu.VMEM((tb // chunk, GLA_HEADS, GLA_DK, LANES), F32)],
        compiler_params=_params(2),
        name="gla_chunks",
    )(qkvg, qkvg, qkvg, qkvg, ga, wa, ba, gn, *([s0] if has_s0 else []))


def _swa_kernel(sink_ref, q_ref, kp_ref, vp_ref, ko_ref, vo_ref, o_ref, bias_scr, *, tq, prev_from_cache):
    i = pl.program_id(1)
    nk = WINDOW + tq
    k_all = jnp.concatenate([kp_ref[...], ko_ref[...]], axis=0)
    v_all = jnp.concatenate([vp_ref[...], vo_ref[...]], axis=0)
    qc = lax.broadcasted_iota(jnp.int32, (tq, nk), 0) // CHUNK + WINDOW // CHUNK
    kcol = lax.broadcasted_iota(jnp.int32, (tq, nk), 1)
    kc = kcol // CHUNK
    valid = (kc <= qc) & (kc >= qc - WINDOW // CHUNK)
    if not prev_from_cache:
        valid = valid & ((kcol >= WINDOW) | (i > 0))
    bias_scr[...] = jnp.where(valid, 0.0, NEG_BIG)
    lane = lax.broadcasted_iota(jnp.int32, (nk, LANES), 1)
    low = lane < SWA_HD
    low_q = lax.broadcasted_iota(jnp.int32, (tq, LANES), 1) < SWA_HD
    for g in range(SWA_KV_HEADS):
        slab = slice((g // 2) * LANES, (g // 2 + 1) * LANES)
        k2 = k_all[:, slab]
        v2 = v_all[:, slab]
        k2r = pltpu.roll(k2, SWA_HD, 1)
        v2r = pltpu.roll(v2, SWA_HD, 1)
        if g % 2 == 0:
            k_lo, k_hi, v_lo, v_hi = k2, k2r, v2, v2r
        else:
            k_lo, k_hi, v_lo, v_hi = k2r, k2, v2r, v2
        zero = jnp.zeros_like(k2)
        one = jnp.ones_like(k2)
        km = (jnp.where(low, k_lo, zero).astype(BF16), jnp.where(low, zero, k_hi).astype(BF16))
        vm = (jnp.where(low, v_lo, one).astype(BF16), jnp.where(low, one, v_hi).astype(BF16))
        heads = [(j, half) for j in range(SWA_GROUP // 2) for half in range(2)]
        sinks = [sink_ref[g * SWA_GROUP + 2 * j + half] for j, half in heads]
        scores = []
        for j, half in heads:
            qs = q_ref[:, (g * 4 + j) * LANES:(g * 4 + j + 1) * LANES]
            s = lax.dot_general(qs, km[half], (((1,), (1,)), ((), ())), preferred_element_type=F32)
            scores.append(s + bias_scr[...])
        maxes = [jnp.maximum(jnp.max(s, axis=-1, keepdims=True), sk) for s, sk in zip(scores, sinks)]
        exps = [jnp.exp(s - m).astype(BF16) for s, m in zip(scores, maxes)]
        sink_terms = [jnp.exp(sk - m) for sk, m in zip(sinks, maxes)]
        for j in range(SWA_GROUP // 2):
            a_lo = jnp.dot(exps[2 * j], vm[0], preferred_element_type=F32)
            a_hi = jnp.dot(exps[2 * j + 1], vm[1], preferred_element_type=F32)
            num = jnp.where(low_q, a_lo, a_hi)
            den = pltpu.roll(jnp.where(low_q, a_hi, a_lo), SWA_HD, 1)
            den = den + jnp.where(low_q, sink_terms[2 * j], sink_terms[2 * j + 1])
            o_ref[:, (g * 4 + j) * LANES:(g * 4 + j + 1) * LANES] = (num / den).astype(o_ref.dtype)


def _swa(sinks, q, q_row0, k_prev, v_prev, prev_col, k_own, v_own, own_col, own_row0,
         batch, seq, tq, prev_from_cache):
    nt = seq // tq
    qb = q_row0 // tq
    ob = own_row0 // tq
    kv_w = SWA_KV_HEADS * SWA_HD
    if prev_from_cache:
        prev_map = lambda b, t, c: (b, c)
    else:
        per = seq // WINDOW
        prev_map = lambda b, t, c: (b * per + jnp.maximum(t * (tq // WINDOW) - 1, 0), c)
    kernel = functools.partial(_swa_kernel, tq=tq, prev_from_cache=prev_from_cache)
    return pl.pallas_call(
        kernel,
        out_shape=jax.ShapeDtypeStruct((batch * seq, SWA_HEADS * SWA_HD), BF16),
        grid=(batch, nt),
        in_specs=[pl.BlockSpec(memory_space=pltpu.SMEM),
                  pl.BlockSpec((tq, SWA_HEADS * SWA_HD), lambda b, t: (qb + b * nt + t, 0)),
                  pl.BlockSpec((WINDOW, kv_w), lambda b, t: prev_map(b, t, prev_col[0])),
                  pl.BlockSpec((WINDOW, kv_w), lambda b, t: prev_map(b, t, prev_col[1])),
                  pl.BlockSpec((tq, kv_w), lambda b, t: (ob + b * nt + t, own_col[0])),
                  pl.BlockSpec((tq, kv_w), lambda b, t: (ob + b * nt + t, own_col[1]))],
        out_specs=pl.BlockSpec((tq, SWA_HEADS * SWA_HD), lambda b, t: (b * nt + t, 0)),
        scratch_shapes=[pltpu.VMEM((tq, WINDOW + tq), F32)],
        compiler_params=_params(2),
        name="swa_band",
    )(sinks, q, k_prev, v_prev, k_own, v_own)


def _mem_attn_head(q, k, v):
    s = lax.dot_general(q, k.astype(BF16), (((1,), (1,)), ((), ())),
                        preferred_element_type=F32) * (MEM_HD ** -0.5)
    m = jnp.max(s, axis=-1, keepdims=True)
    e = jnp.exp(s - m)
    p = (e / jnp.sum(e, axis=-1, keepdims=True)).astype(BF16)
    return jnp.dot(p, v.astype(BF16), preferred_element_type=F32)


def _mem_attn_kernel(q_ref, kv_ref, o_ref, kv_bf):
    @pl.when(pl.program_id(1) == 0)
    def _():
        kv_bf[...] = kv_ref[...].astype(BF16)

    width = MEM_HEADS * MEM_HD
    cols = [slice(h * MEM_HD, (h + 1) * MEM_HD) for h in range(MEM_HEADS)]
    scores = [lax.dot_general(q_ref[:, c], kv_bf[:, c], (((1,), (1,)), ((), ())),
                              preferred_element_type=F32) * (MEM_HD ** -0.5) for c in cols]
    exps = [jnp.exp(s - jnp.max(s, axis=-1, keepdims=True)) for s in scores]
    probs = [(e / jnp.sum(e, axis=-1, keepdims=True)).astype(BF16) for e in exps]
    for c, p in zip(cols, probs):
        v = kv_bf[:, width + c.start:width + c.stop]
        o_ref[:, c] = jnp.dot(p, v, preferred_element_type=F32).astype(o_ref.dtype)


def _mem_attn_cache_kernel(q_ref, k_hbm, v_hbm, o_ref, kbuf, vbuf, sem):
    b = pl.program_id(0)
    slot = b % 2

    def copies(req, s):
        return ([pltpu.make_async_copy(k_hbm.at[0, req, :, h, :], kbuf.at[s, h], sem.at[s, 0])
                 for h in range(MEM_HEADS)]
                + [pltpu.make_async_copy(v_hbm.at[0, req, :, h, :], vbuf.at[s, h], sem.at[s, 1])
                   for h in range(MEM_HEADS)])

    @pl.when(b == 0)
    def _():
        for c in copies(0, 0):
            c.start()

    @pl.when(b + 1 < pl.num_programs(0))
    def _():
        for c in copies(b + 1, 1 - slot):
            c.start()

    for c in copies(b, slot):
        c.wait()
    for h in range(MEM_HEADS):
        cols = slice(h * MEM_HD, (h + 1) * MEM_HD)
        o_ref[:, cols] = _mem_attn_head(q_ref[:, cols], kbuf[slot, h], vbuf[slot, h]).astype(o_ref.dtype)


def _mem_attn_cache(q, q_row0, cache_k, cache_v, batch, seq):
    qb = q_row0 // seq
    width = MEM_HEADS * MEM_HD
    hbm = pl.BlockSpec(memory_space=pl.ANY)
    slabs = pltpu.VMEM((2, MEM_HEADS, N_MEM, MEM_HD), F32)
    return pl.pallas_call(
        _mem_attn_cache_kernel,
        out_shape=jax.ShapeDtypeStruct((batch * seq, width), BF16),
        grid=(batch,),
        in_specs=[pl.BlockSpec((seq, width), lambda b: (qb + b, 0)), hbm, hbm],
        out_specs=pl.BlockSpec((seq, width), lambda b: (b, 0)),
        scratch_shapes=[slabs, slabs, pltpu.SemaphoreType.DMA((2, 2))],
        compiler_params=_params(1),
        name="mem_attn_cache",
    )(q, cache_k, cache_v)


def _mem_attn(q, mem_kv, batch, seq, tq):
    nt = seq // tq
    width = MEM_HEADS * MEM_HD
    return pl.pallas_call(
        _mem_attn_kernel,
        out_shape=jax.ShapeDtypeStruct((batch * seq, width), BF16),
        grid=(batch, nt),
        in_specs=[pl.BlockSpec((tq, width), lambda b, t: (b * nt + t, 0)),
                  pl.BlockSpec((N_MEM, 2 * width), lambda b, t: (b, 0))],
        out_specs=pl.BlockSpec((tq, width), lambda b, t: (b * nt + t, 0)),
        scratch_shapes=[pltpu.VMEM((N_MEM, 2 * width), BF16)],
        compiler_params=_params(2),
        name="mem_attn",
    )(q, mem_kv)


def _merge_kernel(a0p, a0s, a1p, a1s, a2p, a2s, w_ref, g0_ref, g1_ref, g2_ref, o_ref, *, p_tiles):
    def emit(a0_ref, a1_ref, a2_ref):
        acc = g0_ref[...] * jnp.dot(a0_ref[...], w_ref[0], preferred_element_type=F32)
        acc = acc + g1_ref[...] * jnp.dot(a1_ref[...], w_ref[1], preferred_element_type=F32)
        acc = acc + g2_ref[...] * jnp.dot(a2_ref[...], w_ref[2], preferred_element_type=F32)
        o_ref[...] = acc.astype(o_ref.dtype)

    @pl.when(pl.program_id(1) < p_tiles)
    def _():
        emit(a0p, a1p, a2p)

    @pl.when(pl.program_id(1) >= p_tiles)
    def _():
        emit(a0s, a1s, a2s)


def _merge(branches_p, branches_s, w_branch, gates, tm, tn):
    n_p, d = branches_p[0].shape
    n = n_p + branches_s[0].shape[0]
    nj = d // tn
    p_tiles = n_p // tm
    ap = pl.BlockSpec((tm, d), lambda j, i: (jnp.minimum(i, p_tiles - 1), 0))
    asp = pl.BlockSpec((tm, d), lambda j, i: (jnp.maximum(i - p_tiles, 0), 0))
    operands = [a for pair in zip(branches_p, branches_s) for a in pair]
    return pl.pallas_call(
        functools.partial(_merge_kernel, p_tiles=p_tiles),
        out_shape=jax.ShapeDtypeStruct((n, d), BF16),
        grid=(nj, n // tm),
        in_specs=[ap, asp, ap, asp, ap, asp,
                  pl.BlockSpec((3, d, tn), lambda j, i: (0, 0, j), pipeline_mode=pl.Buffered(1)),
                  pl.BlockSpec((tm, tn), lambda j, i: (i, j)),
                  pl.BlockSpec((tm, tn), lambda j, i: (i, nj + j)),
                  pl.BlockSpec((tm, tn), lambda j, i: (i, 2 * nj + j))],
        out_specs=pl.BlockSpec((tm, tn), lambda j, i: (i, j)),
        compiler_params=_params(2),
        name="branch_merge",
    )(*operands, w_branch, gates, gates, gates)


def _outproj_router_kernel(m_ref, w_ref, xp_ref, xs_ref, g_ref, wr_ref, br_ref,
                           x2_ref, hf_ref, eid_ref, wt_ref, *, p_tiles):
    is_prompt = pl.program_id(0) < p_tiles
    tm, d = x2_ref.shape
    for r0 in range(0, tm, ROUTER_SUB_ROWS):
        rows = slice(r0, r0 + ROUTER_SUB_ROWS)
        acc = jnp.dot(m_ref[rows, :], w_ref[...], preferred_element_type=F32)
        x = jnp.where(is_prompt, xp_ref[rows, :], xs_ref[rows, :]) + acc
        x2_ref[rows, :] = x
        hf = x * lax.rsqrt(jnp.mean(x * x, axis=-1, keepdims=True) + EPS) * g_ref[...]
        hb = hf.astype(BF16)
        bits = pltpu.bitcast(hb.astype(F32), jnp.uint32)
        packed = bits[:, d // 2:] | (bits[:, :d // 2] >> 16)
        for s in range(TOKEN_TILE_ROWS):
            hf_ref[pl.ds(r0 * TOKEN_TILE_ROWS + s, ROUTER_SUB_ROWS, stride=TOKEN_TILE_ROWS), :] = (
                packed[:, s * LANES:(s + 1) * LANES])
        logits = jnp.dot(hb, wr_ref[...], preferred_element_type=F32) + br_ref[...]
        eid, wts = _route(logits)
        eid_ref[rows, :] = eid
        wt_ref[rows, :] = wts


def _route(logits):
    lane = lax.broadcasted_iota(jnp.int32, logits.shape, 1).astype(F32)
    big = 1e6
    is_g = lane < N_GROUPS
    lg = jnp.where(is_g, logits, NEG_BIG)
    mg = jnp.max(lg, axis=-1, keepdims=True)
    gsel = jnp.min(jnp.where(is_g & (lg == mg), lane, big), axis=-1, keepdims=True)
    g_w = 1.0 / jnp.sum(jnp.where(is_g, jnp.exp(lg - mg), 0.0), axis=-1, keepdims=True)
    e_lo = N_GROUPS + gsel * EXPERTS_PER_GROUP
    in_grp = (lane >= e_lo) & (lane < e_lo + EXPERTS_PER_GROUP)
    le = jnp.where(in_grp, logits, NEG_BIG)
    me = jnp.max(le, axis=-1, keepdims=True)
    ee = jnp.where(in_grp, jnp.exp(le - me), 0.0)
    pe = ee / jnp.sum(ee, axis=-1, keepdims=True)
    pe = jnp.where(in_grp, pe, -1.0)
    p1 = jnp.max(pe, axis=-1, keepdims=True)
    i1 = jnp.min(jnp.where(pe == p1, lane, big), axis=-1, keepdims=True)
    pe2 = jnp.where(lane == i1, -1.0, pe)
    p2 = jnp.max(pe2, axis=-1, keepdims=True)
    i2 = jnp.min(jnp.where(pe2 == p2, lane, big), axis=-1, keepdims=True)
    tot = p1 + p2
    w1 = g_w * p1 / tot
    w2 = g_w * p2 / tot
    eid = jnp.where(lane == 0.0, i1 - N_GROUPS, jnp.where(lane == 1.0, i2 - N_GROUPS, 0.0))
    return eid.astype(jnp.int32), jnp.where(lane == 0.0, w1, jnp.where(lane == 1.0, w2, 0.0))


def _outproj_router(merged, w_out, xp, xs, g, wr, br, tm):
    n, d = merged.shape
    p_tiles = xp.shape[0] // tm
    const = lambda shape: pl.BlockSpec(shape, lambda i: (0, 0), pipeline_mode=pl.Buffered(1))
    row = lambda width: pl.BlockSpec((tm, width), lambda i: (i, 0))
    return pl.pallas_call(
        functools.partial(_outproj_router_kernel, p_tiles=p_tiles),
        out_shape=(jax.ShapeDtypeStruct((n, d), F32),
                   jax.ShapeDtypeStruct((n * TOKEN_TILE_ROWS, LANES), jnp.uint32),
                   jax.ShapeDtypeStruct((n, LANES), jnp.int32),
                   jax.ShapeDtypeStruct((n, LANES), F32)),
        grid=(n // tm,),
        in_specs=[row(d), const((d, d)), *_split_specs(p_tiles, tm, d),
                  const((1, d)), const((d, LANES)), const((1, LANES))],
        out_specs=(row(d), pl.BlockSpec((tm * TOKEN_TILE_ROWS, LANES), lambda i: (i, 0)),
                   row(LANES), row(LANES)),
        compiler_params=_params(1),
        name="outproj_router",
    )(merged, w_out, xp, xs, g.reshape(1, d), wr, br)


def _moe_kernel(te_ref, nv_ref, nxt_ref, tok_ref, tok_next_ref, dst_ref, hf_hbm, wup_hbm, wdn_hbm, y_hbm,
                xg, yb, wup_f32, wdn_f32, wup_bf, wdn_bf, in_sem, out_sem, w_sem):
    t = pl.program_id(0)
    buf = t % 2
    rows_now = nv_ref[t]
    rows_next = nv_ref[t + 1]
    rows_prev = nv_ref[jnp.maximum(t - 1, 0)]
    valid = rows_now > 0
    valid_next = rows_next > 0
    new_expert = (t == 0) | (te_ref[t] != te_ref[jnp.maximum(t - 1, 0)])

    def weight_copies(e):
        return (pltpu.make_async_copy(wup_hbm.at[e], wup_f32, w_sem.at[0]),
                pltpu.make_async_copy(wdn_hbm.at[e], wdn_f32, w_sem.at[1]))

    in_rows, in_pitch = TOKEN_TILE_ROWS, TOKEN_TILE_PITCH
    out_rows, out_pitch = TOKEN_F32_ROWS, TOKEN_F32_PITCH

    def row_in(row0, r, b):
        return pltpu.make_async_copy(hf_hbm.at[pl.ds(pl.multiple_of(row0, in_rows), in_rows), :],
                                     xg.at[b, pl.ds(r * in_pitch, in_rows), :], in_sem.at[b])

    def row_out(row0, r, b):
        return pltpu.make_async_copy(yb.at[b, pl.ds(r * out_pitch, out_rows), :],
                                     y_hbm.at[pl.ds(pl.multiple_of(row0, out_rows), out_rows), :], out_sem.at[b])

    group = MOE_ROW_GROUP
    n_groups = MOE_TILE // group

    def group_in(b):
        return pltpu.make_async_copy(hf_hbm.at[pl.ds(0, group * in_rows), :],
                                     xg.at[b, pl.ds(0, group * in_rows), :], in_sem.at[b])

    def group_out(b):
        return pltpu.make_async_copy(yb.at[b, pl.ds(0, group * out_rows), :],
                                     y_hbm.at[pl.ds(0, group * out_rows), :], out_sem.at[b])

    def per_started_group(rows, fn):
        for g in range(n_groups):
            pl.when(rows > g * group)(functools.partial(fn, g))

    @pl.when(t == 0)
    def _():
        for c in weight_copies(te_ref[0]):
            c.start(priority=1)
        xg[...] = jnp.zeros_like(xg)

        def first(r, c):
            row_in(tok_ref[0, 0, r], r, 0).start()
            return c
        lax.fori_loop(0, ((rows_now + group - 1) // group) * group, first, 0)
        yb[...] = jnp.zeros_like(yb)
        n_real = y_hbm.shape[0] - 2 * MOE_TILE * out_rows
        for b in range(2):
            spare = pltpu.make_async_copy(
                yb.at[b, pl.ds(0, MOE_TILE * out_rows), :],
                y_hbm.at[pl.ds(n_real + b * MOE_TILE * out_rows, MOE_TILE * out_rows), :], out_sem.at[b])
            spare.start()
            spare.wait()

    @pl.when(valid & new_expert)
    def _():
        for c in weight_copies(te_ref[t]):
            c.wait()
        wup_bf[...] = wup_f32[...].astype(BF16)
        wdn_bf[...] = wdn_f32[...].astype(BF16)

        @pl.when(nxt_ref[t] >= 0)
        def _():
            for c in weight_copies(nxt_ref[t]):
                c.start(priority=1)

    def gather_next(g):
        for r in range(g * group, (g + 1) * group):
            row_in(tok_next_ref[0, 0, r], r, 1 - buf).start()

    def scatter_now(g):
        for r in range(g * group, (g + 1) * group):
            row_out(dst_ref[0, 0, r], r, buf).start(priority=r % 2)

    per_started_group(rows_next, gather_next)

    @pl.when(valid)
    def _():
        per_started_group(rows_now, lambda g: group_in(buf).wait())
        lo, hi = [], []
        for s in range(in_rows):
            word = xg[buf, pl.ds(s, MOE_TILE, stride=in_pitch), :]
            lo.append(pltpu.bitcast(word << 16, F32).astype(BF16))
            hi.append(pltpu.bitcast(word & jnp.uint32(0xFFFF0000), F32).astype(BF16))
        x = jnp.concatenate(lo + hi, axis=1)
        h1 = jnp.dot(x, wup_bf[...], preferred_element_type=F32)
        gate = h1[:, :D_FF]
        up = h1[:, D_FF:]
        act = (gate * _sigmoid(gate)) * up
        ye = jnp.dot(act.astype(BF16), wdn_bf[...], preferred_element_type=F32)
        for s in range(out_rows):
            yb[buf, pl.ds(s, MOE_TILE, stride=out_pitch), :] = ye[:, s * LANES:(s + 1) * LANES]
        per_started_group(rows_now, scatter_now)

        @pl.when(t > 0)
        def _():
            per_started_group(rows_prev, lambda g: group_out(1 - buf).wait())

        @pl.when(jnp.logical_not(valid_next))
        def _():
            per_started_group(rows_now, lambda g: group_out(buf).wait())


def _moe(hf, tile_expert, tile_rows, next_expert, tok_slots, dst_slots, w_up, w_down, out_tokens):
    n_tiles = tile_expert.shape[0]
    d = w_up.shape[1]
    slot_spec = lambda off: pl.BlockSpec((1, 1, MOE_TILE),
                                         lambda t, te, nv, ne: (jnp.minimum(t + off, n_tiles - 1), 0, 0),
                                         memory_space=pltpu.SMEM)
    hbm = pl.BlockSpec(memory_space=pl.ANY)
    grid_spec = pltpu.PrefetchScalarGridSpec(
        num_scalar_prefetch=3,
        grid=(n_tiles,),
        in_specs=[slot_spec(0), slot_spec(1), slot_spec(0), hbm, hbm, hbm],
        out_specs=hbm,
        scratch_shapes=[pltpu.VMEM((2, MOE_TILE * TOKEN_TILE_PITCH, LANES), jnp.uint32),
                        pltpu.VMEM((2, MOE_TILE * TOKEN_F32_PITCH, LANES), F32),
                        pltpu.VMEM((d, 2 * D_FF), F32),
                        pltpu.VMEM((D_FF, d), F32),
                        pltpu.VMEM((d, 2 * D_FF), BF16),
                        pltpu.VMEM((D_FF, d), BF16),
                        pltpu.SemaphoreType.DMA((2,)),
                        pltpu.SemaphoreType.DMA((2,)),
                        pltpu.SemaphoreType.DMA((2,))],
    )
    return pl.pallas_call(
        _moe_kernel,
        out_shape=jax.ShapeDtypeStruct((out_tokens * TOKEN_F32_ROWS, LANES), F32),
        grid_spec=grid_spec,
        compiler_params=_params(1),
        name="moe_experts",
    )(tile_expert, tile_rows, next_expert, tok_slots, tok_slots, dst_slots, hf, w_up, w_down)


def _combine_kernel(x_ref, y0_ref, y1_ref, w_ref, op_ref, os_ref, *, p_tiles):
    tm = x_ref.shape[0]
    w = w_ref[...]
    w0 = w[:, 0:1]
    w1 = w[:, 1:2]

    def emit(o_ref):
        for s in range(TOKEN_F32_ROWS):
            cols = slice(s * LANES, (s + 1) * LANES)
            rows = pl.ds(s, tm, stride=TOKEN_F32_ROWS)
            o_ref[:, cols] = x_ref[:, cols] + (y0_ref[rows, :] * w0 + y1_ref[rows, :] * w1)

    @pl.when(pl.program_id(0) < p_tiles)
    def _():
        emit(op_ref)

    @pl.when(pl.program_id(0) >= p_tiles)
    def _():
        emit(os_ref)


def _combine(x2, yk, wts, n_p, tm):
    n, d = x2.shape
    p_tiles = n_p // tm
    k1 = n // tm
    y_rows = tm * TOKEN_F32_ROWS
    return pl.pallas_call(
        functools.partial(_combine_kernel, p_tiles=p_tiles),
        out_shape=(jax.ShapeDtypeStruct((n_p, d), F32), jax.ShapeDtypeStruct((n - n_p, d), F32)),
        grid=(n // tm,),
        in_specs=[pl.BlockSpec((tm, d), lambda i: (i, 0)),
                  pl.BlockSpec((y_rows, LANES), lambda i: (i, 0)),
                  pl.BlockSpec((y_rows, LANES), lambda i: (k1 + i, 0)),
                  pl.BlockSpec((tm, LANES), lambda i: (i, 0))],
        out_specs=_split_specs(p_tiles, tm, d),
        compiler_params=_params(1),
        name="moe_combine",
    )(x2, yk, yk, wts)


def _rope_tables(pos):
    half = ROPE_DIM // 2
    inv = ROPE_THETA ** (-jnp.arange(half, dtype=F32) / half)
    ang = pos.astype(F32)[:, None] * inv[None, :]
    cos, sin = jnp.cos(ang), jnp.sin(ang)
    n = pos.shape[0]
    pad = jnp.zeros((n, SWA_HD - ROPE_DIM), F32)
    cos_h = jnp.concatenate([cos, cos, pad + 1.0], axis=1)
    sa_h = jnp.concatenate([-sin, jnp.zeros_like(sin), pad], axis=1)
    sb_h = jnp.concatenate([jnp.zeros_like(sin), sin, pad], axis=1)
    reps = LANES // SWA_HD
    return tuple(jnp.tile(a, (1, reps)) for a in (cos_h, sa_h, sb_h))


def _moe_schedule(eid, n_tok, n_tiles):
    a = eid.shape[0]
    order = jnp.argsort(eid, stable=True).astype(jnp.int32)
    counts = jnp.bincount(eid, length=N_EXPERTS).astype(jnp.int32)
    tiles_per = (counts + MOE_TILE - 1) // MOE_TILE
    tile_end = jnp.cumsum(tiles_per)
    tile_start = tile_end - tiles_per
    sorted_start = jnp.cumsum(counts) - counts
    tile_id = jnp.arange(n_tiles, dtype=jnp.int32)
    used = tile_id < tile_end[-1]
    te = jnp.minimum(jnp.sum(tile_end[None, :] <= tile_id[:, None], axis=1), N_EXPERTS - 1).astype(jnp.int32)
    last_used_e = te[jnp.maximum(tile_end[-1] - 1, 0)]
    te = jnp.where(used, te, last_used_e)
    row_in_expert = (tile_id - tile_start[te]) * MOE_TILE
    rows_valid = jnp.where(used, jnp.clip(counts[te] - row_in_expert, 0, MOE_TILE), 0).astype(jnp.int32)
    r = jnp.arange(MOE_TILE, dtype=jnp.int32)[None, :]
    src = sorted_start[te][:, None] + row_in_expert[:, None] + r
    real = r < rows_valid[:, None]
    assign = order[jnp.clip(src, 0, a - 1)]
    tok = jnp.where(real, assign % n_tok, 0) * TOKEN_TILE_ROWS
    spare = a + (tile_id[:, None] % 2) * MOE_TILE + r
    dst = jnp.where(real, assign, spare) * TOKEN_F32_ROWS
    tile_rows = jnp.concatenate([rows_valid, jnp.zeros((1,), jnp.int32)])
    e_id = jnp.arange(N_EXPERTS, dtype=jnp.int32)[None, :]
    later = (e_id > te[:, None]) & (counts[None, :] > 0)
    nxt = jnp.min(jnp.where(later, e_id, N_EXPERTS), axis=1)
    nxt = jnp.where(nxt < N_EXPERTS, nxt, -1).astype(jnp.int32)
    shape = (n_tiles, 1, MOE_TILE)
    return te, tile_rows, nxt, tok.reshape(shape).astype(jnp.int32), dst.reshape(shape).astype(jnp.int32)


def kernel(x_prompt, x_sample, state_gla, cache_swa_k, cache_swa_v, cache_mem_k, cache_mem_v,
           mem_prompt, norm_mix_g, w_in, w_a2, b_a2, gla_norm_g, swa_q_norm_g, swa_k_norm_g,
           swa_sinks, norm_mem_g, w_mem_kv, mem_q_norm_g, mem_k_norm_g, w_gate, b_gate,
           w_branch, w_out, norm_ffn_g, w_router_group, b_router_group, w_router_expert,
           b_router_expert, w_up, w_down):
    bp, tp, d = x_prompt.shape
    bs, ts, _ = x_sample.shape
    n_p, n_s = bp * tp, bs * ts
    n = n_p + n_s
    tm, tn = ROW_TILE, COL_TILE
    tp_rows = PROJ_ROWS if n % PROJ_ROWS == 0 else tm
    heavy_sub = tp_rows // 2
    norm_sub = tp_rows // 4
    assert d == D_MODEL and n_p % tm == 0 and n_s % tm == 0 and w_in.shape[0] == 1
    keep_s = cache_swa_k.shape[2]
    assert keep_s == WINDOW and tp % WINDOW == 0

    qk_w = GLA_HEADS * GLA_DK
    v_w = GLA_HEADS * GLA_DV
    c0 = 2 * qk_w + 2 * v_w
    sq_w = SWA_HEADS * SWA_HD
    kv_w = SWA_KV_HEADS * SWA_HD
    mem_w = MEM_HEADS * MEM_HD
    w_in_t = jnp.transpose(w_in[0])
    w_branch_b = w_branch[0].astype(BF16)
    w_out_b = w_out[0].astype(BF16)
    w_a2_b = jnp.pad(w_a2[0], ((0, LANES - GLA_LOWRANK), (0, 0))).astype(BF16)
    w_router = jnp.pad(jnp.concatenate([w_router_group[0], w_router_expert[0]], axis=1),
                       ((0, 0), (0, LANES - N_GROUPS - N_EXPERTS))).astype(BF16)
    b_router = jnp.pad(jnp.concatenate([b_router_group[0], b_router_expert[0]]),
                       (0, LANES - N_GROUPS - N_EXPERTS)).reshape(1, LANES)

    pos = jnp.concatenate([jnp.tile(jnp.arange(tp, dtype=jnp.int32), bp),
                           jnp.tile(PAST_LEN + jnp.arange(ts, dtype=jnp.int32), bs)])
    cos_t, sa_t, sb_t = _rope_tables(pos)
    seg_id = jnp.arange(tn, dtype=jnp.int32) // SWA_HD
    seg = (seg_id[:, None] == seg_id[None, :]).astype(BF16)
    rope_specs = [pl.BlockSpec((tp_rows, LANES), lambda j, i: (i, 0))] * 3
    row_vec = lambda width: pl.BlockSpec((1, width), lambda j, i: (0, 0))
    seg_spec = pl.BlockSpec((tn, tn), lambda j, i: (0, 0))

    xp2 = x_prompt.reshape(n_p, d)
    xs2 = x_sample.reshape(n_s, d)
    h = _rms_norm_two(xp2, xs2, norm_mix_g[0], tm)

    c1 = c0 + GLA_LOWRANK
    c2 = c1 + sq_w
    c3 = c2 + 2 * kv_w
    qkvg = _wmatmul(_ep_plain, h, w_in_t, 0, c0, True, [], [], F32, tp_rows, WIDE_COLS, "proj_gla",
                    sub_rows=heavy_sub)
    ga = _wmatmul(_ep_lowrank, h, w_in_t, c0, LANES, True, [], [], F32, tp_rows, LANES, "proj_gla_lowrank")
    q_gain = jnp.tile(swa_q_norm_g[0] * (SWA_HD ** -0.5), tn // SWA_HD).reshape(1, tn)
    q_swa = _wmatmul(functools.partial(_ep_qknorm_rope, keep_from=None), h, w_in_t, c1, sq_w, True,
                     [q_gain, seg, cos_t, sa_t, sb_t], [row_vec(tn), seg_spec] + rope_specs,
                     BF16, tp_rows, tn, "proj_swa_q", sub_rows=norm_sub)
    k_gain = jnp.tile(swa_k_norm_g[0], tn // SWA_HD).reshape(1, tn)
    kv_swa = _wmatmul(functools.partial(_ep_qknorm_rope, keep_from=kv_w), h, w_in_t, c2, 2 * kv_w, True,
                      [k_gain, seg, cos_t, sa_t, sb_t], [row_vec(tn), seg_spec] + rope_specs,
                      F32, tp_rows, tn, "proj_swa_kv", sub_rows=norm_sub)
    q_mem = _wmatmul(functools.partial(_ep_headnorm, norm_tiles=None), h, w_in_t, c3, mem_w, True,
                     [mem_q_norm_g[0].reshape(1, MEM_HD)], [row_vec(MEM_HD)],
                     BF16, tp_rows, MEM_HD, "proj_mem_q", sub_rows=norm_sub)
    gates = _wmatmul(_ep_sigmoid, h, w_gate[0], 0, 3 * d, False, [b_gate[0].reshape(1, -1)],
                     [pl.BlockSpec((1, WIDE_COLS), lambda j, i: (0, j))], F32, tp_rows, WIDE_COLS, "proj_gates",
                     sub_rows=heavy_sub)

    mem_rows = bp * N_MEM
    hm = _rms_norm_rows(mem_prompt.reshape(mem_rows, d), norm_mem_g[0], BF16, N_MEM)
    mem_kv = _wmatmul(functools.partial(_ep_headnorm, norm_tiles=MEM_HEADS), hm, w_mem_kv[0], 0, 2 * mem_w, False,
                      [mem_k_norm_g[0].reshape(1, MEM_HD)], [row_vec(MEM_HD)], F32, mem_rows, MEM_HD, "mem_kv")

    ba = b_a2[0].reshape(1, qk_w)
    gn = gla_norm_g[0].reshape(1, GLA_DV)
    o_gla_p, gla_state_p = _gla(qkvg, ga, w_a2_b, ba, gn, None, bp, tp, 0, 256, CHUNK)
    o_gla_s, gla_state_s = _gla(qkvg, ga, w_a2_b, ba, gn, state_gla[0], bs, ts, n_p, ts, min(CHUNK, ts))

    sinks = swa_sinks[0]
    o_swa_p = _swa(sinks, q_swa, 0, kv_swa, kv_swa, (0, 1), kv_swa, kv_swa, (0, 1), 0,
                   bp, tp, WINDOW, False)
    ck = cache_swa_k[0].reshape(bs * keep_s, kv_w)
    cv = cache_swa_v[0].reshape(bs * keep_s, kv_w)
    o_swa_s = _swa(sinks, q_swa, n_p, ck, cv, (0, 0), kv_swa, kv_swa, (0, 1), n_p,
                   bs, ts, ts, True)

    o_mem_p = _mem_attn(q_mem, mem_kv, bp, tp, tm)
    o_mem_s = _mem_attn_cache(q_mem, n_p, cache_mem_k, cache_mem_v, bs, ts)

    merged = _merge((o_gla_p, o_swa_p, o_mem_p), (o_gla_s, o_swa_s, o_mem_s), w_branch_b, gates, tm, WIDE_COLS)

    x2, hf, eid, wts = _outproj_router(merged, w_out_b, xp2, xs2, norm_ffn_g[0], w_router, b_router, tm)

    n_assign = TOP_K * n
    n_tiles = n_assign // MOE_TILE + N_EXPERTS
    eid_kmajor = jnp.concatenate([eid[:, k] for k in range(TOP_K)])
    tile_expert, tile_rows, next_expert, tok_slots, dst_slots = _moe_schedule(eid_kmajor, n, n_tiles)
    yk = _moe(hf, tile_expert, tile_rows, next_expert, tok_slots, dst_slots, w_up[0], w_down[0],
              n_assign + 2 * MOE_TILE)
    y_p, y_s = _combine(x2, yk, wts, n_p, tm)

    y_p = y_p.reshape(bp, tp, d)
    y_s = y_s.reshape(bs, ts, d)
    kv_p = jnp.stack([kv_swa[(b + 1) * tp - WINDOW:(b + 1) * tp] for b in range(bp)])
    kv_p = kv_p.reshape(bp, WINDOW, 2, SWA_KV_HEADS, SWA_HD)
    kv_s = kv_swa[n_p:].reshape(bs, ts, 2, SWA_KV_HEADS, SWA_HD)
    swk_s = jnp.concatenate([cache_swa_k[0], kv_s[:, :, 0]], axis=1)[:, ts:ts + keep_s]
    swv_s = jnp.concatenate([cache_swa_v[0], kv_s[:, :, 1]], axis=1)[:, ts:ts + keep_s]
    mk_p = mem_kv[:, :mem_w].reshape(bp, N_MEM, MEM_HEADS, MEM_HD)
    mv_p = mem_kv[:, mem_w:].reshape(bp, N_MEM, MEM_HEADS, MEM_HD)
    return (y_p, y_s, gla_state_p[None], kv_p[:, :, 0][None], kv_p[:, :, 1][None], mk_p[None], mv_p[None],
            gla_state_s[None], swk_s[None], swv_s[None])
```

```python
import functools

import jax
import jax.numpy as jnp
from jax import lax
from jax.experimental import pallas as pl
from jax.experimental.pallas import tpu as pltpu

F32 = jnp.float32
BF16 = jnp.bfloat16

D_MODEL = 2048
CHUNK = 64
EPS = 1e-6
PAST_LEN = 1024
GLA_HEADS = 4
GLA_DV = 512
GLA_DK = 256
GLA_LOWRANK = 16
GLA_NORMALIZER = 16.0
SWA_HD = 64
SWA_HEADS = 32
SWA_KV_HEADS = 4
SWA_GROUP = 8
WINDOW = 128
ROPE_DIM = 16
ROPE_THETA = 500000.0
N_MEM = 256
MEM_HEADS = 4
MEM_HD = 512
N_GROUPS = 8
EXPERTS_PER_GROUP = 8
N_EXPERTS = 64
TOP_K = 2
D_FF = 512

LANES = 128
VMEM_LIMIT = 56 * 1024 * 1024
ROW_TILE = 512
PROJ_ROWS = 1536
COL_TILE = 512
WIDE_COLS = 1024
SWA_STAGE_KV_HEADS = 2
ROUTER_SUB_ROWS = 256
MOE_TILE = 256
MOE_ROW_GROUP = 32
TOKEN_TILE_ROWS = D_MODEL // 2 // LANES
TOKEN_TILE_PITCH = 12
TOKEN_F32_ROWS = D_MODEL // LANES
TOKEN_F32_PITCH = 20
NEG_BIG = -1e30


def _params(n_axes):
    return pltpu.CompilerParams(dimension_semantics=("arbitrary",) * n_axes,
                                vmem_limit_bytes=VMEM_LIMIT)


def _norm_kernel(x_ref, g_ref, o_ref):
    x = x_ref[...]
    y = x * lax.rsqrt(jnp.mean(x * x, axis=-1, keepdims=True) + EPS)
    o_ref[...] = (y * g_ref[...]).astype(o_ref.dtype)


def _rms_norm_rows(x, g, out_dtype, tm):
    n, d = x.shape
    return pl.pallas_call(
        _norm_kernel,
        out_shape=jax.ShapeDtypeStruct((n, d), out_dtype),
        grid=(n // tm,),
        in_specs=[pl.BlockSpec((tm, d), lambda i: (i, 0)),
                  pl.BlockSpec((1, d), lambda i: (0, 0))],
        out_specs=pl.BlockSpec((tm, d), lambda i: (i, 0)),
        compiler_params=_params(1),
        name="rms_norm_rows",
    )(x, g.reshape(1, d))


def _norm2_kernel(xp_ref, xs_ref, g_ref, o_ref, *, p_tiles):
    def emit(x_ref):
        x = x_ref[...]
        y = x * lax.rsqrt(jnp.mean(x * x, axis=-1, keepdims=True) + EPS)
        o_ref[...] = (y * g_ref[...]).astype(o_ref.dtype)

    @pl.when(pl.program_id(0) < p_tiles)
    def _():
        emit(xp_ref)

    @pl.when(pl.program_id(0) >= p_tiles)
    def _():
        emit(xs_ref)


def _split_specs(p_tiles, tm, d):
    return (pl.BlockSpec((tm, d), lambda i: (jnp.minimum(i, p_tiles - 1), 0)),
            pl.BlockSpec((tm, d), lambda i: (jnp.maximum(i - p_tiles, 0), 0)))


def _rms_norm_two(xp, xs, g, tm):
    (n_p, d), n_s = xp.shape, xs.shape[0]
    p_tiles = n_p // tm
    return pl.pallas_call(
        functools.partial(_norm2_kernel, p_tiles=p_tiles),
        out_shape=jax.ShapeDtypeStruct((n_p + n_s, d), BF16),
        grid=((n_p + n_s) // tm,),
        in_specs=[*_split_specs(p_tiles, tm, d), pl.BlockSpec((1, d), lambda i: (0, 0))],
        out_specs=pl.BlockSpec((tm, d), lambda i: (i, 0)),
        compiler_params=_params(1),
        name="rms_norm_mix",
    )(xp, xs, g.reshape(1, d))


def _segment_rms(acc, seg_ref, inv_width):
    ss = jnp.dot((acc * acc).astype(BF16), seg_ref[...], preferred_element_type=F32)
    return ss * inv_width


def _rope(y, rows, cos_ref, sa_ref, sb_ref):
    width = y.shape[1]
    reps = width // LANES
    c = jnp.concatenate([cos_ref[rows, :]] * reps, axis=1)
    sa = jnp.concatenate([sa_ref[rows, :]] * reps, axis=1)
    sb = jnp.concatenate([sb_ref[rows, :]] * reps, axis=1)
    half = ROPE_DIM // 2
    return y * c + pltpu.roll(y, width - half, 1) * sa + pltpu.roll(y, half, 1) * sb


def _ep_plain(acc, rows):
    return acc


def _ep_lowrank(acc, rows):
    lane = lax.broadcasted_iota(jnp.int32, acc.shape, 1)
    return jnp.where(lane < GLA_LOWRANK, acc, 0.0)


def _sigmoid(x):
    return 0.5 * jnp.tanh(0.5 * x) + 0.5


def _ep_sigmoid(acc, rows, b_ref):
    return _sigmoid(acc + b_ref[...])


def _ep_qknorm_rope(acc, rows, g_ref, seg_ref, cos_ref, sa_ref, sb_ref, *, keep_from):
    ms = _segment_rms(acc, seg_ref, 1.0 / SWA_HD)
    y = acc * lax.rsqrt(ms + EPS) * g_ref[...]
    y = _rope(y, rows, cos_ref, sa_ref, sb_ref)
    if keep_from is not None:
        col = lax.broadcasted_iota(jnp.int32, y.shape, 1)
        y = jnp.where(col < keep_from, y, acc)
    return y


def _ep_headnorm(acc, rows, g_ref, *, norm_tiles):
    y = acc * lax.rsqrt(jnp.mean(acc * acc, axis=-1, keepdims=True) + EPS) * g_ref[...]
    if norm_tiles is not None:
        y = jnp.where(pl.program_id(0) < norm_tiles, y, acc)
    return y


def _wmm_kernel(a_ref, w_ref, *rest, w_is_transposed, sub_rows, epilogue):
    extras, o_ref, wbf = rest[:-2], rest[-2], rest[-1]

    @pl.when(pl.program_id(1) == 0)
    def _():
        wbf[...] = w_ref[...].astype(BF16)

    for r0 in range(0, a_ref.shape[0], sub_rows):
        rows = slice(r0, r0 + sub_rows)
        if w_is_transposed:
            acc = lax.dot_general(a_ref[rows, :], wbf[...], (((1,), (1,)), ((), ())), preferred_element_type=F32)
        else:
            acc = jnp.dot(a_ref[rows, :], wbf[...], preferred_element_type=F32)
        o_ref[rows, :] = epilogue(acc, rows, *extras).astype(o_ref.dtype)


def _wmatmul(epilogue, a, w, col0, n_cols, w_is_transposed, extras, extra_specs, out_dtype, tm, tn, name,
             sub_rows=None):
    m, k = a.shape
    sub_rows = tm if sub_rows is None else sub_rows
    assert m % tm == 0 and n_cols % tn == 0 and tm % sub_rows == 0
    if w_is_transposed:
        assert col0 % 8 == 0
        w_spec = pl.BlockSpec((pl.Element(tn), pl.Element(k)), lambda j, i: (pl.multiple_of(col0 + j * tn, 8), 0))
        w_tile = (tn, k)
    else:
        assert col0 % tn == 0
        w_spec = pl.BlockSpec((k, tn), lambda j, i: (0, col0 // tn + j))
        w_tile = (k, tn)
    kernel = functools.partial(_wmm_kernel, w_is_transposed=w_is_transposed, sub_rows=sub_rows, epilogue=epilogue)
    return pl.pallas_call(
        kernel,
        out_shape=jax.ShapeDtypeStruct((m, n_cols), out_dtype),
        grid=(n_cols // tn, m // tm),
        in_specs=[pl.BlockSpec((tm, k), lambda j, i: (i, 0)), w_spec] + list(extra_specs),
        out_specs=pl.BlockSpec((tm, tn), lambda j, i: (i, j)),
        scratch_shapes=[pltpu.VMEM(w_tile, BF16)],
        compiler_params=_params(2),
        name=name,
    )(a, w, *extras)


def _gla_kernel(q_ref, k_ref, v_ref, gg_ref, ga_ref, wa_ref, ba_ref, gn_ref, *rest, chunk, n_chunks, has_s0):
    s0_ref = rest[0] if has_s0 else None
    o_ref, sout_ref, s_scr, qt_scr, ku_scr, o_scr, dec_scr = rest[1:] if has_s0 else rest
    _gla_block(q_ref, k_ref, v_ref, gg_ref, ga_ref, wa_ref, ba_ref, gn_ref, s0_ref,
               o_ref, sout_ref, s_scr, qt_scr, ku_scr, o_scr, dec_scr, chunk=chunk, n_chunks=n_chunks)


def _gla_block(q_ref, k_ref, v_ref, gg_ref, ga_ref, wa_ref, ba_ref, gn_ref, s0_ref,
               o_ref, sout_ref, s_scr, qt_scr, ku_scr, o_scr, dec_scr, *, chunk, n_chunks):
    t = pl.program_id(1)
    tb = chunk * n_chunks
    heads = [(slice(h * GLA_DK, (h + 1) * GLA_DK), slice(h * GLA_DV, (h + 1) * GLA_DV)) for h in range(GLA_HEADS)]

    @pl.when(t == 0)
    def _():
        s_scr[...] = jnp.zeros_like(s_scr) if s0_ref is None else s0_ref[0]

    row = lax.broadcasted_iota(jnp.int32, (chunk, chunk), 0)
    col = lax.broadcasted_iota(jnp.int32, (chunk, chunk), 1)
    tril = (row >= col).astype(BF16)
    z = jnp.dot(ga_ref[...].astype(BF16), wa_ref[...], preferred_element_type=F32) + ba_ref[...]
    log_a = (jnp.minimum(z, 0.0) - jnp.log(1.0 + jnp.exp(-jnp.abs(z)))) * (1.0 / GLA_NORMALIZER)
    hi = log_a.astype(BF16)
    rest = log_a - hi.astype(F32)
    mid = rest.astype(BF16)
    lo = (rest - mid.astype(F32)).astype(BF16)
    b_parts, last_parts = [], []
    for ci in range(n_chunks):
        crows = slice(ci * chunk, (ci + 1) * chunk)
        b_c = (jnp.dot(tril, hi[crows], preferred_element_type=F32)
               + jnp.dot(tril, mid[crows], preferred_element_type=F32)
               + jnp.dot(tril, lo[crows], preferred_element_type=F32))
        b_last = b_c[chunk - 1:chunk, :]
        b_parts.append(b_c)
        last_parts.append(jnp.broadcast_to(b_last, b_c.shape))
        for h, (ks, _) in enumerate(heads):
            dec_scr[ci, h] = jnp.transpose(jnp.broadcast_to(jnp.exp(b_last[:, ks]), (LANES, GLA_DK)))
    b = jnp.concatenate(b_parts, axis=0)
    b_last_rows = jnp.concatenate(last_parts, axis=0)

    q = q_ref[...] * (GLA_DK ** -0.5)
    k = k_ref[...]
    q_t = (q * jnp.exp(b)).astype(BF16)
    k_t = (k * jnp.exp(-b)).astype(BF16)
    qt_scr[...] = q_t
    ku_scr[...] = (k * jnp.exp(b_last_rows - b)).astype(BF16)

    brow = lax.broadcasted_iota(jnp.int32, (tb, tb), 0)
    bcol = lax.broadcasted_iota(jnp.int32, (tb, tb), 1)
    mask = (bcol >= (brow & -chunk)) & (brow >= bcol)
    for ks, vs in heads:
        att = lax.dot_general(q_t[:, ks], k_t[:, ks], (((1,), (1,)), ((), ())), preferred_element_type=F32)
        att = jnp.where(mask, att, 0.0).astype(BF16)
        o_scr[:, vs] = jnp.dot(att, v_ref[:, vs].astype(BF16), preferred_element_type=F32)

    def one_chunk(ci, carry):
        rows = pl.ds(pl.multiple_of(ci * chunk, chunk), chunk)
        for h, (ks, vs) in enumerate(heads):
            s_old = s_scr[h]
            o_scr[rows, vs] += jnp.dot(qt_scr[rows, ks], s_old.astype(BF16), preferred_element_type=F32)
            decay = jnp.concatenate([dec_scr[ci, h]] * (GLA_DV // LANES), axis=1)
            s_scr[h] = decay * s_old + lax.dot_general(ku_scr[rows, ks], v_ref[rows, vs].astype(BF16),
                                                       (((0,), (0,)), ((), ())), preferred_element_type=F32)
        return carry

    lax.fori_loop(0, n_chunks, one_chunk, 0)

    for _, vs in heads:
        o = o_scr[:, vs]
        on = o * lax.rsqrt(jnp.mean(o * o, axis=-1, keepdims=True) + EPS) * gn_ref[...]
        gg = gg_ref[:, vs]
        o_ref[:, vs] = (on * (gg * _sigmoid(gg))).astype(o_ref.dtype)

    @pl.when(t == pl.num_programs(1) - 1)
    def _():
        sout_ref[0] = s_scr[...]


def _gla(qkvg, ga, wa, ba, gn, s0, batch, seq, row0, tb, chunk):
    nt = seq // tb
    base = row0 // tb
    qk_w = GLA_HEADS * GLA_DK
    v_w = GLA_HEADS * GLA_DV
    rows = lambda b, t: base + b * nt + t
    has_s0 = s0 is not None
    kernel = functools.partial(_gla_kernel, chunk=chunk, n_chunks=tb // chunk, has_s0=has_s0)
    state_spec = pl.BlockSpec((1, GLA_HEADS, GLA_DK, GLA_DV), lambda b, t: (b, 0, 0, 0))
    return pl.pallas_call(
        kernel,
        out_shape=(jax.ShapeDtypeStruct((batch * seq, v_w), BF16),
                   jax.ShapeDtypeStruct((batch, GLA_HEADS, GLA_DK, GLA_DV), F32)),
        grid=(batch, nt),
        in_specs=[pl.BlockSpec((tb, qk_w), lambda b, t: (rows(b, t), 0)),
                  pl.BlockSpec((tb, qk_w), lambda b, t: (rows(b, t), 1)),
                  pl.BlockSpec((tb, v_w), lambda b, t: (rows(b, t), 1)),
                  pl.BlockSpec((tb, v_w), lambda b, t: (rows(b, t), 2)),
                  pl.BlockSpec((tb, LANES), lambda b, t: (rows(b, t), 0)),
                  pl.BlockSpec((LANES, qk_w), lambda b, t: (0, 0)),
                  pl.BlockSpec((1, qk_w), lambda b, t: (0, 0)),
                  pl.BlockSpec((1, GLA_DV), lambda b, t: (0, 0))] + ([state_spec] if has_s0 else []),
        out_specs=(pl.BlockSpec((tb, v_w), lambda b, t: (b * nt + t, 0)),
                   pl.BlockSpec((1, GLA_HEADS, GLA_DK, GLA_DV), lambda b, t: (b, 0, 0, 0))),
        scratch_shapes=[pltpu.VMEM((GLA_HEADS, GLA_DK, GLA_DV), F32),
                        pltpu.VMEM((tb, qk_w), BF16),
                        pltpu.VMEM((tb, qk_w), BF16),
                        pltpu.VMEM((tb, v_w), F32),
                        pltpu.VMEM((tb // chunk, GLA_HEADS, GLA_DK, LANES), F32)],
        compiler_params=_params(2),
        name="gla_chunks",
    )(qkvg, qkvg, qkvg, qkvg, ga, wa, ba, gn, *([s0] if has_s0 else []))


def _swa_kernel(sink_ref, q_ref, kp_ref, vp_ref, ko_ref, vo_ref, o_ref, bias_scr, *, tq, prev_from_cache):
    i = pl.program_id(1)
    nk = WINDOW + tq
    k_all = jnp.concatenate([kp_ref[...], ko_ref[...]], axis=0)
    v_all = jnp.concatenate([vp_ref[...], vo_ref[...]], axis=0)
    qc = lax.broadcasted_iota(jnp.int32, (tq, nk), 0) // CHUNK + WINDOW // CHUNK
    kcol = lax.broadcasted_iota(jnp.int32, (tq, nk), 1)
    kc = kcol // CHUNK
    valid = (kc <= qc) & (kc >= qc - WINDOW // CHUNK)
    if not prev_from_cache:
        valid = valid & ((kcol >= WINDOW) | (i > 0))
    bias_scr[...] = jnp.where(valid, 0.0, NEG_BIG)
    lane = lax.broadcasted_iota(jnp.int32, (nk, LANES), 1)
    low = lane < SWA_HD
    low_q = lax.broadcasted_iota(jnp.int32, (tq, LANES), 1) < SWA_HD
    kms, vms = [], []
    for g in range(SWA_KV_HEADS):
        slab = slice((g // 2) * LANES, (g // 2 + 1) * LANES)
        k2 = k_all[:, slab]
        v2 = v_all[:, slab]
        k2r = pltpu.roll(k2, SWA_HD, 1)
        v2r = pltpu.roll(v2, SWA_HD, 1)
        if g % 2 == 0:
            k_lo, k_hi, v_lo, v_hi = k2, k2r, v2, v2r
        else:
            k_lo, k_hi, v_lo, v_hi = k2r, k2, v2r, v2
        zero = jnp.zeros_like(k2)
        one = jnp.ones_like(k2)
        kms.append((jnp.where(low, k_lo, zero).astype(BF16), jnp.where(low, zero, k_hi).astype(BF16)))
        vms.append((jnp.where(low, v_lo, one).astype(BF16), jnp.where(low, one, v_hi).astype(BF16)))

    for g0 in range(0, SWA_KV_HEADS, SWA_STAGE_KV_HEADS):
        heads = [(g, j, half) for g in range(g0, g0 + SWA_STAGE_KV_HEADS)
                 for j in range(SWA_GROUP // 2) for half in range(2)]
        sinks = [sink_ref[g * SWA_GROUP + 2 * j + half] for g, j, half in heads]
        scores = []
        for g, j, half in heads:
            qs = q_ref[:, (g * 4 + j) * LANES:(g * 4 + j + 1) * LANES]
            s = lax.dot_general(qs, kms[g][half], (((1,), (1,)), ((), ())), preferred_element_type=F32)
            scores.append(s + bias_scr[...])
        maxes = [jnp.maximum(jnp.max(s, axis=-1, keepdims=True), sk) for s, sk in zip(scores, sinks)]
        exps = [jnp.exp(s - m).astype(BF16) for s, m in zip(scores, maxes)]
        sink_terms = [jnp.exp(sk - m) for sk, m in zip(sinks, maxes)]
        for idx in range(0, len(heads), 2):
            g, j, _ = heads[idx]
            a_lo = jnp.dot(exps[idx], vms[g][0], preferred_element_type=F32)
            a_hi = jnp.dot(exps[idx + 1], vms[g][1], preferred_element_type=F32)
            num = jnp.where(low_q, a_lo, a_hi)
            den = pltpu.roll(jnp.where(low_q, a_hi, a_lo), SWA_HD, 1)
            den = den + jnp.where(low_q, sink_terms[idx], sink_terms[idx + 1])
            o_ref[:, (g * 4 + j) * LANES:(g * 4 + j + 1) * LANES] = (num / den).astype(o_ref.dtype)


def _swa(sinks, q, q_row0, k_prev, v_prev, prev_col, k_own, v_own, own_col, own_row0,
         batch, seq, tq, prev_from_cache):
    nt = seq // tq
    qb = q_row0 // tq
    ob = own_row0 // tq
    kv_w = SWA_KV_HEADS * SWA_HD
    if prev_from_cache:
        prev_map = lambda b, t, c: (b, c)
    else:
        per = seq // WINDOW
        prev_map = lambda b, t, c: (b * per + jnp.maximum(t * (tq // WINDOW) - 1, 0), c)
    kernel = functools.partial(_swa_kernel, tq=tq, prev_from_cache=prev_from_cache)
    return pl.pallas_call(
        kernel,
        out_shape=jax.ShapeDtypeStruct((batch * seq, SWA_HEADS * SWA_HD), BF16),
        grid=(batch, nt),
        in_specs=[pl.BlockSpec(memory_space=pltpu.SMEM),
                  pl.BlockSpec((tq, SWA_HEADS * SWA_HD), lambda b, t: (qb + b * nt + t, 0)),
                  pl.BlockSpec((WINDOW, kv_w), lambda b, t: prev_map(b, t, prev_col[0])),
                  pl.BlockSpec((WINDOW, kv_w), lambda b, t: prev_map(b, t, prev_col[1])),
                  pl.BlockSpec((tq, kv_w), lambda b, t: (ob + b * nt + t, own_col[0])),
                  pl.BlockSpec((tq, kv_w), lambda b, t: (ob + b * nt + t, own_col[1]))],
        out_specs=pl.BlockSpec((tq, SWA_HEADS * SWA_HD), lambda b, t: (b * nt + t, 0)),
        scratch_shapes=[pltpu.VMEM((tq, WINDOW + tq), F32)],
        compiler_params=_params(2),
        name="swa_band",
    )(sinks, q, k_prev, v_prev, k_own, v_own)


def _mem_attn_head(q, k, v):
    s = lax.dot_general(q, k.astype(BF16), (((1,), (1,)), ((), ())),
                        preferred_element_type=F32) * (MEM_HD ** -0.5)
    m = jnp.max(s, axis=-1, keepdims=True)
    e = jnp.exp(s - m)
    p = (e / jnp.sum(e, axis=-1, keepdims=True)).astype(BF16)
    return jnp.dot(p, v.astype(BF16), preferred_element_type=F32)


def _mem_attn_kernel(q_ref, kv_ref, o_ref, kv_bf):
    @pl.when(pl.program_id(1) == 0)
    def _():
        kv_bf[...] = kv_ref[...].astype(BF16)

    width = MEM_HEADS * MEM_HD
    cols = [slice(h * MEM_HD, (h + 1) * MEM_HD) for h in range(MEM_HEADS)]
    scores = [lax.dot_general(q_ref[:, c], kv_bf[:, c], (((1,), (1,)), ((), ())),
                              preferred_element_type=F32) * (MEM_HD ** -0.5) for c in cols]
    exps = [jnp.exp(s - jnp.max(s, axis=-1, keepdims=True)) for s in scores]
    probs = [(e / jnp.sum(e, axis=-1, keepdims=True)).astype(BF16) for e in exps]
    for c, p in zip(cols, probs):
        v = kv_bf[:, width + c.start:width + c.stop]
        o_ref[:, c] = jnp.dot(p, v, preferred_element_type=F32).astype(o_ref.dtype)


def _mem_attn_cache_kernel(q_ref, k_hbm, v_hbm, o_ref, kbuf, vbuf, sem):
    b = pl.program_id(0)
    slot = b % 2

    def copies(req, s):
        return ([pltpu.make_async_copy(k_hbm.at[0, req, :, h, :], kbuf.at[s, h], sem.at[s, 0])
                 for h in range(MEM_HEADS)]
                + [pltpu.make_async_copy(v_hbm.at[0, req, :, h, :], vbuf.at[s, h], sem.at[s, 1])
                   for h in range(MEM_HEADS)])

    @pl.when(b == 0)
    def _():
        for c in copies(0, 0):
            c.start()

    @pl.when(b + 1 < pl.num_programs(0))
    def _():
        for c in copies(b + 1, 1 - slot):
            c.start()

    for c in copies(b, slot):
        c.wait()
    for h in range(MEM_HEADS):
        cols = slice(h * MEM_HD, (h + 1) * MEM_HD)
        o_ref[:, cols] = _mem_attn_head(q_ref[:, cols], kbuf[slot, h], vbuf[slot, h]).astype(o_ref.dtype)


def _mem_attn_cache(q, q_row0, cache_k, cache_v, batch, seq):
    qb = q_row0 // seq
    width = MEM_HEADS * MEM_HD
    hbm = pl.BlockSpec(memory_space=pl.ANY)
    slabs = pltpu.VMEM((2, MEM_HEADS, N_MEM, MEM_HD), F32)
    return pl.pallas_call(
        _mem_attn_cache_kernel,
        out_shape=jax.ShapeDtypeStruct((batch * seq, width), BF16),
        grid=(batch,),
        in_specs=[pl.BlockSpec((seq, width), lambda b: (qb + b, 0)), hbm, hbm],
        out_specs=pl.BlockSpec((seq, width), lambda b: (b, 0)),
        scratch_shapes=[slabs, slabs, pltpu.SemaphoreType.DMA((2, 2))],
        compiler_params=_params(1),
        name="mem_attn_cache",
    )(q, cache_k, cache_v)


def _mem_attn(q, mem_kv, batch, seq, tq):
    nt = seq // tq
    width = MEM_HEADS * MEM_HD
    return pl.pallas_call(
        _mem_attn_kernel,
        out_shape=jax.ShapeDtypeStruct((batch * seq, width), BF16),
        grid=(batch, nt),
        in_specs=[pl.BlockSpec((tq, width), lambda b, t: (b * nt + t, 0)),
                  pl.BlockSpec((N_MEM, 2 * width), lambda b, t: (b, 0))],
        out_specs=pl.BlockSpec((tq, width), lambda b, t: (b * nt + t, 0)),
        scratch_shapes=[pltpu.VMEM((N_MEM, 2 * width), BF16)],
        compiler_params=_params(2),
        name="mem_attn",
    )(q, mem_kv)


def _merge_kernel(a0p, a0s, a1p, a1s, a2p, a2s, w_ref, g0_ref, g1_ref, g2_ref, o_ref, *, p_tiles):
    def emit(a0_ref, a1_ref, a2_ref):
        acc = g0_ref[...] * jnp.dot(a0_ref[...], w_ref[0], preferred_element_type=F32)
        acc = acc + g1_ref[...] * jnp.dot(a1_ref[...], w_ref[1], preferred_element_type=F32)
        acc = acc + g2_ref[...] * jnp.dot(a2_ref[...], w_ref[2], preferred_element_type=F32)
        o_ref[...] = acc.astype(o_ref.dtype)

    @pl.when(pl.program_id(1) < p_tiles)
    def _():
        emit(a0p, a1p, a2p)

    @pl.when(pl.program_id(1) >= p_tiles)
    def _():
        emit(a0s, a1s, a2s)


def _merge(branches_p, branches_s, w_branch, gates, tm, tn):
    n_p, d = branches_p[0].shape
    n = n_p + branches_s[0].shape[0]
    nj = d // tn
    p_tiles = n_p // tm
    ap = pl.BlockSpec((tm, d), lambda j, i: (jnp.minimum(i, p_tiles - 1), 0))
    asp = pl.BlockSpec((tm, d), lambda j, i: (jnp.maximum(i - p_tiles, 0), 0))
    operands = [a for pair in zip(branches_p, branches_s) for a in pair]
    return pl.pallas_call(
        functools.partial(_merge_kernel, p_tiles=p_tiles),
        out_shape=jax.ShapeDtypeStruct((n, d), BF16),
        grid=(nj, n // tm),
        in_specs=[ap, asp, ap, asp, ap, asp,
                  pl.BlockSpec((3, d, tn), lambda j, i: (0, 0, j), pipeline_mode=pl.Buffered(1)),
                  pl.BlockSpec((tm, tn), lambda j, i: (i, j)),
                  pl.BlockSpec((tm, tn), lambda j, i: (i, nj + j)),
                  pl.BlockSpec((tm, tn), lambda j, i: (i, 2 * nj + j))],
        out_specs=pl.BlockSpec((tm, tn), lambda j, i: (i, j)),
        compiler_params=_params(2),
        name="branch_merge",
    )(*operands, w_branch, gates, gates, gates)


def _outproj_router_kernel(m_ref, w_ref, xp_ref, xs_ref, g_ref, wr_ref, br_ref,
                           x2_ref, hf_ref, eid_ref, wt_ref, *, p_tiles):
    is_prompt = pl.program_id(0) < p_tiles
    tm, d = x2_ref.shape
    for r0 in range(0, tm, ROUTER_SUB_ROWS):
        rows = slice(r0, r0 + ROUTER_SUB_ROWS)
        acc = jnp.dot(m_ref[rows, :], w_ref[...], preferred_element_type=F32)
        x = jnp.where(is_prompt, xp_ref[rows, :], xs_ref[rows, :]) + acc
        x2_ref[rows, :] = x
        hf = x * lax.rsqrt(jnp.mean(x * x, axis=-1, keepdims=True) + EPS) * g_ref[...]
        hb = hf.astype(BF16)
        bits = pltpu.bitcast(hb.astype(F32), jnp.uint32)
        packed = bits[:, d // 2:] | (bits[:, :d // 2] >> 16)
        for s in range(TOKEN_TILE_ROWS):
            hf_ref[pl.ds(r0 * TOKEN_TILE_ROWS + s, ROUTER_SUB_ROWS, stride=TOKEN_TILE_ROWS), :] = (
                packed[:, s * LANES:(s + 1) * LANES])
        logits = jnp.dot(hb, wr_ref[...], preferred_element_type=F32) + br_ref[...]
        eid, wts = _route(logits)
        eid_ref[rows, :] = eid
        wt_ref[rows, :] = wts


def _route(logits):
    lane = lax.broadcasted_iota(jnp.int32, logits.shape, 1).astype(F32)
    big = 1e6
    is_g = lane < N_GROUPS
    lg = jnp.where(is_g, logits, NEG_BIG)
    mg = jnp.max(lg, axis=-1, keepdims=True)
    gsel = jnp.min(jnp.where(is_g & (lg == mg), lane, big), axis=-1, keepdims=True)
    g_w = 1.0 / jnp.sum(jnp.where(is_g, jnp.exp(lg - mg), 0.0), axis=-1, keepdims=True)
    e_lo = N_GROUPS + gsel * EXPERTS_PER_GROUP
    in_grp = (lane >= e_lo) & (lane < e_lo + EXPERTS_PER_GROUP)
    le = jnp.where(in_grp, logits, NEG_BIG)
    me = jnp.max(le, axis=-1, keepdims=True)
    ee = jnp.where(in_grp, jnp.exp(le - me), 0.0)
    pe = ee / jnp.sum(ee, axis=-1, keepdims=True)
    pe = jnp.where(in_grp, pe, -1.0)
    p1 = jnp.max(pe, axis=-1, keepdims=True)
    i1 = jnp.min(jnp.where(pe == p1, lane, big), axis=-1, keepdims=True)
    pe2 = jnp.where(lane == i1, -1.0, pe)
    p2 = jnp.max(pe2, axis=-1, keepdims=True)
    i2 = jnp.min(jnp.where(pe2 == p2, lane, big), axis=-1, keepdims=True)
    tot = p1 + p2
    w1 = g_w * p1 / tot
    w2 = g_w * p2 / tot
    eid = jnp.where(lane == 0.0, i1 - N_GROUPS, jnp.where(lane == 1.0, i2 - N_GROUPS, 0.0))
    return eid.astype(jnp.int32), jnp.where(lane == 0.0, w1, jnp.where(lane == 1.0, w2, 0.0))


def _outproj_router(merged, w_out, xp, xs, g, wr, br, tm):
    n, d = merged.shape
    p_tiles = xp.shape[0] // tm
    const = lambda shape: pl.BlockSpec(shape, lambda i: (0, 0), pipeline_mode=pl.Buffered(1))
    row = lambda width: pl.BlockSpec((tm, width), lambda i: (i, 0))
    return pl.pallas_call(
        functools.partial(_outproj_router_kernel, p_tiles=p_tiles),
        out_shape=(jax.ShapeDtypeStruct((n, d), F32),
                   jax.ShapeDtypeStruct((n * TOKEN_TILE_ROWS, LANES), jnp.uint32),
                   jax.ShapeDtypeStruct((n, LANES), jnp.int32),
                   jax.ShapeDtypeStruct((n, LANES), F32)),
        grid=(n // tm,),
        in_specs=[row(d), const((d, d)), *_split_specs(p_tiles, tm, d),
                  const((1, d)), const((d, LANES)), const((1, LANES))],
        out_specs=(row(d), pl.BlockSpec((tm * TOKEN_TILE_ROWS, LANES), lambda i: (i, 0)),
                   row(LANES), row(LANES)),
        compiler_params=_params(1),
        name="outproj_router",
    )(merged, w_out, xp, xs, g.reshape(1, d), wr, br)


def _moe_kernel(te_ref, nv_ref, nxt_ref, tok_ref, tok_next_ref, dst_ref, hf_hbm, wup_hbm, wdn_hbm, y_hbm,
                xg, yb, wup_f32, wdn_f32, wup_bf, wdn_bf, in_sem, out_sem, w_sem):
    t = pl.program_id(0)
    buf = t % 2
    rows_now = nv_ref[t]
    rows_next = nv_ref[t + 1]
    rows_prev = nv_ref[jnp.maximum(t - 1, 0)]
    valid = rows_now > 0
    valid_next = rows_next > 0
    new_expert = (t == 0) | (te_ref[t] != te_ref[jnp.maximum(t - 1, 0)])

    def weight_copies(e):
        return (pltpu.make_async_copy(wup_hbm.at[e], wup_f32, w_sem.at[0]),
                pltpu.make_async_copy(wdn_hbm.at[e], wdn_f32, w_sem.at[1]))

    in_rows, in_pitch = TOKEN_TILE_ROWS, TOKEN_TILE_PITCH
    out_rows, out_pitch = TOKEN_F32_ROWS, TOKEN_F32_PITCH

    def row_in(row0, r, b):
        return pltpu.make_async_copy(hf_hbm.at[pl.ds(pl.multiple_of(row0, in_rows), in_rows), :],
                                     xg.at[b, pl.ds(r * in_pitch, in_rows), :], in_sem.at[b])

    def row_out(row0, r, b):
        return pltpu.make_async_copy(yb.at[b, pl.ds(r * out_pitch, out_rows), :],
                                     y_hbm.at[pl.ds(pl.multiple_of(row0, out_rows), out_rows), :], out_sem.at[b])

    group = MOE_ROW_GROUP
    n_groups = MOE_TILE // group

    def group_in(b):
        return pltpu.make_async_copy(hf_hbm.at[pl.ds(0, group * in_rows), :],
                                     xg.at[b, pl.ds(0, group * in_rows), :], in_sem.at[b])

    def group_out(b):
        return pltpu.make_async_copy(yb.at[b, pl.ds(0, group * out_rows), :],
                                     y_hbm.at[pl.ds(0, group * out_rows), :], out_sem.at[b])

    def per_started_group(rows, fn):
        for g in range(n_groups):
            pl.when(rows > g * group)(functools.partial(fn, g))

    @pl.when(t == 0)
    def _():
        for c in weight_copies(te_ref[0]):
            c.start(priority=1)
        xg[...] = jnp.zeros_like(xg)

        def first(r, c):
            row_in(tok_ref[0, 0, r], r, 0).start()
            return c
        lax.fori_loop(0, ((rows_now + group - 1) // group) * group, first, 0)
        yb[...] = jnp.zeros_like(yb)
        n_real = y_hbm.shape[0] - 2 * MOE_TILE * out_rows
        for b in range(2):
            spare = pltpu.make_async_copy(
                yb.at[b, pl.ds(0, MOE_TILE * out_rows), :],
                y_hbm.at[pl.ds(n_real + b * MOE_TILE * out_rows, MOE_TILE * out_rows), :], out_sem.at[b])
            spare.start()
            spare.wait()

    @pl.when(valid & new_expert)
    def _():
        for c in weight_copies(te_ref[t]):
            c.wait()
        wup_bf[...] = wup_f32[...].astype(BF16)
        wdn_bf[...] = wdn_f32[...].astype(BF16)

        @pl.when(nxt_ref[t] >= 0)
        def _():
            for c in weight_copies(nxt_ref[t]):
                c.start(priority=1)

    def gather_next(g):
        for r in range(g * group, (g + 1) * group):
            row_in(tok_next_ref[0, 0, r], r, 1 - buf).start()

    def scatter_now(g):
        for r in range(g * group, (g + 1) * group):
            row_out(dst_ref[0, 0, r], r, buf).start(priority=r % 2)

    per_started_group(rows_next, gather_next)

    @pl.when(valid)
    def _():
        per_started_group(rows_now, lambda g: group_in(buf).wait())
        lo, hi = [], []
        for s in range(in_rows):
            word = xg[buf, pl.ds(s, MOE_TILE, stride=in_pitch), :]
            lo.append(pltpu.bitcast(word << 16, F32).astype(BF16))
            hi.append(pltpu.bitcast(word & jnp.uint32(0xFFFF0000), F32).astype(BF16))
        x = jnp.concatenate(lo + hi, axis=1)
        h1 = jnp.dot(x, wup_bf[...], preferred_element_type=F32)
        gate = h1[:, :D_FF]
        up = h1[:, D_FF:]
        act = (gate * _sigmoid(gate)) * up
        ye = jnp.dot(act.astype(BF16), wdn_bf[...], preferred_element_type=F32)
        for s in range(out_rows):
            yb[buf, pl.ds(s, MOE_TILE, stride=out_pitch), :] = ye[:, s * LANES:(s + 1) * LANES]
        per_started_group(rows_now, scatter_now)

        @pl.when(t > 0)
        def _():
            per_started_group(rows_prev, lambda g: group_out(1 - buf).wait())

        @pl.when(jnp.logical_not(valid_next))
        def _():
            per_started_group(rows_now, lambda g: group_out(buf).wait())


def _moe(hf, tile_expert, tile_rows, next_expert, tok_slots, dst_slots, w_up, w_down, out_tokens):
    n_tiles = tile_expert.shape[0]
    d = w_up.shape[1]
    slot_spec = lambda off: pl.BlockSpec((1, 1, MOE_TILE),
                                         lambda t, te, nv, ne: (jnp.minimum(t + off, n_tiles - 1), 0, 0),
                                         memory_space=pltpu.SMEM)
    hbm = pl.BlockSpec(memory_space=pl.ANY)
    grid_spec = pltpu.PrefetchScalarGridSpec(
        num_scalar_prefetch=3,
        grid=(n_tiles,),
        in_specs=[slot_spec(0), slot_spec(1), slot_spec(0), hbm, hbm, hbm],
        out_specs=hbm,
        scratch_shapes=[pltpu.VMEM((2, MOE_TILE * TOKEN_TILE_PITCH, LANES), jnp.uint32),
                        pltpu.VMEM((2, MOE_TILE * TOKEN_F32_PITCH, LANES), F32),
                        pltpu.VMEM((d, 2 * D_FF), F32),
                        pltpu.VMEM((D_FF, d), F32),
                        pltpu.VMEM((d, 2 * D_FF), BF16),
                        pltpu.VMEM((D_FF, d), BF16),
                        pltpu.SemaphoreType.DMA((2,)),
                        pltpu.SemaphoreType.DMA((2,)),
                        pltpu.SemaphoreType.DMA((2,))],
    )
    return pl.pallas_call(
        _moe_kernel,
        out_shape=jax.ShapeDtypeStruct((out_tokens * TOKEN_F32_ROWS, LANES), F32),
        grid_spec=grid_spec,
        compiler_params=_params(1),
        name="moe_experts",
    )(tile_expert, tile_rows, next_expert, tok_slots, tok_slots, dst_slots, hf, w_up, w_down)


def _combine_kernel(x_ref, y0_ref, y1_ref, w_ref, op_ref, os_ref, *, p_tiles):
    tm = x_ref.shape[0]
    w = w_ref[...]
    w0 = w[:, 0:1]
    w1 = w[:, 1:2]

    def emit(o_ref):
        for s in range(TOKEN_F32_ROWS):
            cols = slice(s * LANES, (s + 1) * LANES)
            rows = pl.ds(s, tm, stride=TOKEN_F32_ROWS)
            o_ref[:, cols] = x_ref[:, cols] + (y0_ref[rows, :] * w0 + y1_ref[rows, :] * w1)

    @pl.when(pl.program_id(0) < p_tiles)
    def _():
        emit(op_ref)

    @pl.when(pl.program_id(0) >= p_tiles)
    def _():
        emit(os_ref)


def _combine(x2, yk, wts, n_p, tm):
    n, d = x2.shape
    p_tiles = n_p // tm
    k1 = n // tm
    y_rows = tm * TOKEN_F32_ROWS
    return pl.pallas_call(
        functools.partial(_combine_kernel, p_tiles=p_tiles),
        out_shape=(jax.ShapeDtypeStruct((n_p, d), F32), jax.ShapeDtypeStruct((n - n_p, d), F32)),
        grid=(n // tm,),
        in_specs=[pl.BlockSpec((tm, d), lambda i: (i, 0)),
                  pl.BlockSpec((y_rows, LANES), lambda i: (i, 0)),
                  pl.BlockSpec((y_rows, LANES), lambda i: (k1 + i, 0)),
                  pl.BlockSpec((tm, LANES), lambda i: (i, 0))],
        out_specs=_split_specs(p_tiles, tm, d),
        compiler_params=_params(1),
        name="moe_combine",
    )(x2, yk, yk, wts)


def _rope_tables(pos):
    half = ROPE_DIM // 2
    inv = ROPE_THETA ** (-jnp.arange(half, dtype=F32) / half)
    ang = pos.astype(F32)[:, None] * inv[None, :]
    cos, sin = jnp.cos(ang), jnp.sin(ang)
    n = pos.shape[0]
    pad = jnp.zeros((n, SWA_HD - ROPE_DIM), F32)
    cos_h = jnp.concatenate([cos, cos, pad + 1.0], axis=1)
    sa_h = jnp.concatenate([-sin, jnp.zeros_like(sin), pad], axis=1)
    sb_h = jnp.concatenate([jnp.zeros_like(sin), sin, pad], axis=1)
    reps = LANES // SWA_HD
    return tuple(jnp.tile(a, (1, reps)) for a in (cos_h, sa_h, sb_h))


def _moe_schedule(eid, n_tok, n_tiles):
    a = eid.shape[0]
    order = jnp.argsort(eid, stable=True).astype(jnp.int32)
    counts = jnp.bincount(eid, length=N_EXPERTS).astype(jnp.int32)
    tiles_per = (counts + MOE_TILE - 1) // MOE_TILE
    tile_end = jnp.cumsum(tiles_per)
    tile_start = tile_end - tiles_per
    sorted_start = jnp.cumsum(counts) - counts
    tile_id = jnp.arange(n_tiles, dtype=jnp.int32)
    used = tile_id < tile_end[-1]
    te = jnp.minimum(jnp.sum(tile_end[None, :] <= tile_id[:, None], axis=1), N_EXPERTS - 1).astype(jnp.int32)
    last_used_e = te[jnp.maximum(tile_end[-1] - 1, 0)]
    te = jnp.where(used, te, last_used_e)
    row_in_expert = (tile_id - tile_start[te]) * MOE_TILE
    rows_valid = jnp.where(used, jnp.clip(counts[te] - row_in_expert, 0, MOE_TILE), 0).astype(jnp.int32)
    r = jnp.arange(MOE_TILE, dtype=jnp.int32)[None, :]
    src = sorted_start[te][:, None] + row_in_expert[:, None] + r
    real = r < rows_valid[:, None]
    assign = order[jnp.clip(src, 0, a - 1)]
    tok = jnp.where(real, assign % n_tok, 0) * TOKEN_TILE_ROWS
    spare = a + (tile_id[:, None] % 2) * MOE_TILE + r
    dst = jnp.where(real, assign, spare) * TOKEN_F32_ROWS
    tile_rows = jnp.concatenate([rows_valid, jnp.zeros((1,), jnp.int32)])
    e_id = jnp.arange(N_EXPERTS, dtype=jnp.int32)[None, :]
    later = (e_id > te[:, None]) & (counts[None, :] > 0)
    nxt = jnp.min(jnp.where(later, e_id, N_EXPERTS), axis=1)
    nxt = jnp.where(nxt < N_EXPERTS, nxt, -1).astype(jnp.int32)
    shape = (n_tiles, 1, MOE_TILE)
    return te, tile_rows, nxt, tok.reshape(shape).astype(jnp.int32), dst.reshape(shape).astype(jnp.int32)


def kernel(x_prompt, x_sample, state_gla, cache_swa_k, cache_swa_v, cache_mem_k, cache_mem_v,
           mem_prompt, norm_mix_g, w_in, w_a2, b_a2, gla_norm_g, swa_q_norm_g, swa_k_norm_g,
           swa_sinks, norm_mem_g, w_mem_kv, mem_q_norm_g, mem_k_norm_g, w_gate, b_gate,
           w_branch, w_out, norm_ffn_g, w_router_group, b_router_group, w_router_expert,
           b_router_expert, w_up, w_down):
    bp, tp, d = x_prompt.shape
    bs, ts, _ = x_sample.shape
    n_p, n_s = bp * tp, bs * ts
    n = n_p + n_s
    tm, tn = ROW_TILE, COL_TILE
    tp_rows = PROJ_ROWS if n % PROJ_ROWS == 0 else tm
    heavy_sub = tp_rows // 2
    norm_sub = tp_rows // 4
    assert d == D_MODEL and n_p % tm == 0 and n_s % tm == 0 and w_in.shape[0] == 1
    keep_s = cache_swa_k.shape[2]
    assert keep_s == WINDOW and tp % WINDOW == 0

    qk_w = GLA_HEADS * GLA_DK
    v_w = GLA_HEADS * GLA_DV
    c0 = 2 * qk_w + 2 * v_w
    sq_w = SWA_HEADS * SWA_HD
    kv_w = SWA_KV_HEADS * SWA_HD
    mem_w = MEM_HEADS * MEM_HD
    w_in_t = jnp.transpose(w_in[0])
    w_branch_b = w_branch[0].astype(BF16)
    w_out_b = w_out[0].astype(BF16)
    w_a2_b = jnp.pad(w_a2[0], ((0, LANES - GLA_LOWRANK), (0, 0))).astype(BF16)
    w_router = jnp.pad(jnp.concatenate([w_router_group[0], w_router_expert[0]], axis=1),
                       ((0, 0), (0, LANES - N_GROUPS - N_EXPERTS))).astype(BF16)
    b_router = jnp.pad(jnp.concatenate([b_router_group[0], b_router_expert[0]]),
                       (0, LANES - N_GROUPS - N_EXPERTS)).reshape(1, LANES)

    pos = jnp.concatenate([jnp.tile(jnp.arange(tp, dtype=jnp.int32), bp),
                           jnp.tile(PAST_LEN + jnp.arange(ts, dtype=jnp.int32), bs)])
    cos_t, sa_t, sb_t = _rope_tables(pos)
    seg_id = jnp.arange(tn, dtype=jnp.int32) // SWA_HD
    seg = (seg_id[:, None] == seg_id[None, :]).astype(BF16)
    rope_specs = [pl.BlockSpec((tp_rows, LANES), lambda j, i: (i, 0))] * 3
    row_vec = lambda width: pl.BlockSpec((1, width), lambda j, i: (0, 0))
    seg_spec = pl.BlockSpec((tn, tn), lambda j, i: (0, 0))

    xp2 = x_prompt.reshape(n_p, d)
    xs2 = x_sample.reshape(n_s, d)
    h = _rms_norm_two(xp2, xs2, norm_mix_g[0], tm)

    c1 = c0 + GLA_LOWRANK
    c2 = c1 + sq_w
    c3 = c2 + 2 * kv_w
    qkvg = _wmatmul(_ep_plain, h, w_in_t, 0, c0, True, [], [], F32, tp_rows, WIDE_COLS, "proj_gla",
                    sub_rows=heavy_sub)
    ga = _wmatmul(_ep_lowrank, h, w_in_t, c0, LANES, True, [], [], F32, tp_rows, LANES, "proj_gla_lowrank")
    q_gain = jnp.tile(swa_q_norm_g[0] * (SWA_HD ** -0.5), tn // SWA_HD).reshape(1, tn)
    q_swa = _wmatmul(functools.partial(_ep_qknorm_rope, keep_from=None), h, w_in_t, c1, sq_w, True,
                     [q_gain, seg, cos_t, sa_t, sb_t], [row_vec(tn), seg_spec] + rope_specs,
                     BF16, tp_rows, tn, "proj_swa_q", sub_rows=norm_sub)
    k_gain = jnp.tile(swa_k_norm_g[0], tn // SWA_HD).reshape(1, tn)
    kv_swa = _wmatmul(functools.partial(_ep_qknorm_rope, keep_from=kv_w), h, w_in_t, c2, 2 * kv_w, True,
                      [k_gain, seg, cos_t, sa_t, sb_t], [row_vec(tn), seg_spec] + rope_specs,
                      F32, tp_rows, tn, "proj_swa_kv", sub_rows=norm_sub)
    q_mem = _wmatmul(functools.partial(_ep_headnorm, norm_tiles=None), h, w_in_t, c3, mem_w, True,
                     [mem_q_norm_g[0].reshape(1, MEM_HD)], [row_vec(MEM_HD)],
                     BF16, tp_rows, MEM_HD, "proj_mem_q", sub_rows=norm_sub)
    gates = _wmatmul(_ep_sigmoid, h, w_gate[0], 0, 3 * d, False, [b_gate[0].reshape(1, -1)],
                     [pl.BlockSpec((1, WIDE_COLS), lambda j, i: (0, j))], F32, tp_rows, WIDE_COLS, "proj_gates",
                     sub_rows=heavy_sub)

    mem_rows = bp * N_MEM
    hm = _rms_norm_rows(mem_prompt.reshape(mem_rows, d), norm_mem_g[0], BF16, N_MEM)
    mem_kv = _wmatmul(functools.partial(_ep_headnorm, norm_tiles=MEM_HEADS), hm, w_mem_kv[0], 0, 2 * mem_w, False,
                      [mem_k_norm_g[0].reshape(1, MEM_HD)], [row_vec(MEM_HD)], F32, mem_rows, MEM_HD, "mem_kv")

    ba = b_a2[0].reshape(1, qk_w)
    gn = gla_norm_g[0].reshape(1, GLA_DV)
    o_gla_p, gla_state_p = _gla(qkvg, ga, w_a2_b, ba, gn, None, bp, tp, 0, 256, CHUNK)
    o_gla_s, gla_state_s = _gla(qkvg, ga, w_a2_b, ba, gn, state_gla[0], bs, ts, n_p, ts, min(CHUNK, ts))

    sinks = swa_sinks[0]
    o_swa_p = _swa(sinks, q_swa, 0, kv_swa, kv_swa, (0, 1), kv_swa, kv_swa, (0, 1), 0,
                   bp, tp, WINDOW, False)
    ck = cache_swa_k[0].reshape(bs * keep_s, kv_w)
    cv = cache_swa_v[0].reshape(bs * keep_s, kv_w)
    o_swa_s = _swa(sinks, q_swa, n_p, ck, cv, (0, 0), kv_swa, kv_swa, (0, 1), n_p,
                   bs, ts, ts, True)

    o_mem_p = _mem_attn(q_mem, mem_kv, bp, tp, tm)
    o_mem_s = _mem_attn_cache(q_mem, n_p, cache_mem_k, cache_mem_v, bs, ts)

    merged = _merge((o_gla_p, o_swa_p, o_mem_p), (o_gla_s, o_swa_s, o_mem_s), w_branch_b, gates, tm, WIDE_COLS)

    x2, hf, eid, wts = _outproj_router(merged, w_out_b, xp2, xs2, norm_ffn_g[0], w_router, b_router, tm)

    n_assign = TOP_K * n
    n_tiles = n_assign // MOE_TILE + N_EXPERTS
    eid_kmajor = jnp.concatenate([eid[:, k] for k in range(TOP_K)])
    tile_expert, tile_rows, next_expert, tok_slots, dst_slots = _moe_schedule(eid_kmajor, n, n_tiles)
    yk = _moe(hf, tile_expert, tile_rows, next_expert, tok_slots, dst_slots, w_up[0], w_down[0],
              n_assign + 2 * MOE_TILE)
    y_p, y_s = _combine(x2, yk, wts, n_p, tm)

    y_p = y_p.reshape(bp, tp, d)
    y_s = y_s.reshape(bs, ts, d)
    kv_p = jnp.stack([kv_swa[(b + 1) * tp - WINDOW:(b + 1) * tp] for b in range(bp)])
    kv_p = kv_p.reshape(bp, WINDOW, 2, SWA_KV_HEADS, SWA_HD)
    kv_s = kv_swa[n_p:].reshape(bs, ts, 2, SWA_KV_HEADS, SWA_HD)
    swk_s = jnp.concatenate([cache_swa_k[0], kv_s[:, :, 0]], axis=1)[:, ts:ts + keep_s]
    swv_s = jnp.concatenate([cache_swa_v[0], kv_s[:, :, 1]], axis=1)[:, ts:ts + keep_s]
    mk_p = mem_kv[:, :mem_w].reshape(bp, N_MEM, MEM_HEADS, MEM_HD)
    mv_p = mem_kv[:, mem_w:].reshape(bp, N_MEM, MEM_HEADS, MEM_HD)
    return (y_p, y_s, gla_state_p[None], kv_p[:, :, 0][None], kv_p[:, :, 1][None], mk_p[None], mv_p[None],
            gla_state_s[None], swk_s[None], swv_s[None])
```

```python
import functools

import jax
import jax.numpy as jnp
from jax import lax
from jax.experimental import pallas as pl
from jax.experimental.pallas import tpu as pltpu

F32 = jnp.float32
BF16 = jnp.bfloat16

D_MODEL = 2048
CHUNK = 64
EPS = 1e-6
PAST_LEN = 1024
GLA_HEADS = 4
GLA_DV = 512
GLA_DK = 256
GLA_LOWRANK = 16
GLA_NORMALIZER = 16.0
SWA_HD = 64
SWA_HEADS = 32
SWA_KV_HEADS = 4
SWA_GROUP = 8
WINDOW = 128
ROPE_DIM = 16
ROPE_THETA = 500000.0
N_MEM = 256
MEM_HEADS = 4
MEM_HD = 512
N_GROUPS = 8
EXPERTS_PER_GROUP = 8
N_EXPERTS = 64
TOP_K = 2
D_FF = 512

LANES = 128
VMEM_LIMIT = 56 * 1024 * 1024
ROW_TILE = 512
PROJ_ROWS = 1536
COL_TILE = 512
WIDE_COLS = 1024
MEM_ATTN_ROWS = 1024
SWA_STAGE_KV_HEADS = 2
ROUTER_SUB_ROWS = 256
MOE_TILE = 256
MOE_ROW_GROUP = 32
TOKEN_TILE_ROWS = D_MODEL // 2 // LANES
TOKEN_TILE_PITCH = 12
TOKEN_F32_ROWS = D_MODEL // LANES
TOKEN_F32_PITCH = 20
NEG_BIG = -1e30


def _params(n_axes):
    return pltpu.CompilerParams(dimension_semantics=("arbitrary",) * n_axes,
                                vmem_limit_bytes=VMEM_LIMIT)


def _norm_kernel(x_ref, g_ref, o_ref):
    x = x_ref[...]
    y = x * lax.rsqrt(jnp.mean(x * x, axis=-1, keepdims=True) + EPS)
    o_ref[...] = (y * g_ref[...]).astype(o_ref.dtype)


def _rms_norm_rows(x, g, out_dtype, tm):
    n, d = x.shape
    return pl.pallas_call(
        _norm_kernel,
        out_shape=jax.ShapeDtypeStruct((n, d), out_dtype),
        grid=(n // tm,),
        in_specs=[pl.BlockSpec((tm, d), lambda i: (i, 0)),
                  pl.BlockSpec((1, d), lambda i: (0, 0))],
        out_specs=pl.BlockSpec((tm, d), lambda i: (i, 0)),
        compiler_params=_params(1),
        name="rms_norm_rows",
    )(x, g.reshape(1, d))


def _norm2_kernel(xp_ref, xs_ref, g_ref, o_ref, *, p_tiles):
    def emit(x_ref):
        x = x_ref[...]
        y = x * lax.rsqrt(jnp.mean(x * x, axis=-1, keepdims=True) + EPS)
        o_ref[...] = (y * g_ref[...]).astype(o_ref.dtype)

    @pl.when(pl.program_id(0) < p_tiles)
    def _():
        emit(xp_ref)

    @pl.when(pl.program_id(0) >= p_tiles)
    def _():
        emit(xs_ref)


def _split_specs(p_tiles, tm, d):
    return (pl.BlockSpec((tm, d), lambda i: (jnp.minimum(i, p_tiles - 1), 0)),
            pl.BlockSpec((tm, d), lambda i: (jnp.maximum(i - p_tiles, 0), 0)))


def _rms_norm_two(xp, xs, g, tm):
    (n_p, d), n_s = xp.shape, xs.shape[0]
    p_tiles = n_p // tm
    return pl.pallas_call(
        functools.partial(_norm2_kernel, p_tiles=p_tiles),
        out_shape=jax.ShapeDtypeStruct((n_p + n_s, d), BF16),
        grid=((n_p + n_s) // tm,),
        in_specs=[*_split_specs(p_tiles, tm, d), pl.BlockSpec((1, d), lambda i: (0, 0))],
        out_specs=pl.BlockSpec((tm, d), lambda i: (i, 0)),
        compiler_params=_params(1),
        name="rms_norm_mix",
    )(xp, xs, g.reshape(1, d))


def _segment_rms(acc, seg_ref, inv_width):
    ss = jnp.dot((acc * acc).astype(BF16), seg_ref[...], preferred_element_type=F32)
    return ss * inv_width


def _rope(y, rows, cos_ref, sa_ref, sb_ref):
    width = y.shape[1]
    reps = width // LANES
    c = jnp.concatenate([cos_ref[rows, :]] * reps, axis=1)
    sa = jnp.concatenate([sa_ref[rows, :]] * reps, axis=1)
    sb = jnp.concatenate([sb_ref[rows, :]] * reps, axis=1)
    half = ROPE_DIM // 2
    return y * c + pltpu.roll(y, width - half, 1) * sa + pltpu.roll(y, half, 1) * sb


def _ep_plain(acc, rows):
    return acc


def _ep_lowrank(acc, rows):
    lane = lax.broadcasted_iota(jnp.int32, acc.shape, 1)
    return jnp.where(lane < GLA_LOWRANK, acc, 0.0)


def _sigmoid(x):
    return 0.5 * jnp.tanh(0.5 * x) + 0.5


def _ep_sigmoid(acc, rows, b_ref):
    return _sigmoid(acc + b_ref[...])


def _ep_qknorm_rope(acc, rows, g_ref, seg_ref, cos_ref, sa_ref, sb_ref, *, keep_from):
    ms = _segment_rms(acc, seg_ref, 1.0 / SWA_HD)
    y = acc * lax.rsqrt(ms + EPS) * g_ref[...]
    y = _rope(y, rows, cos_ref, sa_ref, sb_ref)
    if keep_from is not None:
        col = lax.broadcasted_iota(jnp.int32, y.shape, 1)
        y = jnp.where(col < keep_from, y, acc)
    return y


def _ep_headnorm(acc, rows, g_ref, *, norm_tiles):
    y = acc * lax.rsqrt(jnp.mean(acc * acc, axis=-1, keepdims=True) + EPS) * g_ref[...]
    if norm_tiles is not None:
        y = jnp.where(pl.program_id(0) < norm_tiles, y, acc)
    return y


def _wmm_kernel(a_ref, w_ref, *rest, w_is_transposed, sub_rows, epilogue):
    extras, o_ref, wbf = rest[:-2], rest[-2], rest[-1]

    @pl.when(pl.program_id(1) == 0)
    def _():
        wbf[...] = w_ref[...].astype(BF16)

    for r0 in range(0, a_ref.shape[0], sub_rows):
        rows = slice(r0, r0 + sub_rows)
        if w_is_transposed:
            acc = lax.dot_general(a_ref[rows, :], wbf[...], (((1,), (1,)), ((), ())), preferred_element_type=F32)
        else:
            acc = jnp.dot(a_ref[rows, :], wbf[...], preferred_element_type=F32)
        o_ref[rows, :] = epilogue(acc, rows, *extras).astype(o_ref.dtype)


def _wmatmul(epilogue, a, w, col0, n_cols, w_is_transposed, extras, extra_specs, out_dtype, tm, tn, name,
             sub_rows=None):
    m, k = a.shape
    sub_rows = tm if sub_rows is None else sub_rows
    assert m % tm == 0 and n_cols % tn == 0 and tm % sub_rows == 0
    if w_is_transposed:
        assert col0 % 8 == 0
        w_spec = pl.BlockSpec((pl.Element(tn), pl.Element(k)), lambda j, i: (pl.multiple_of(col0 + j * tn, 8), 0))
        w_tile = (tn, k)
    else:
        assert col0 % tn == 0
        w_spec = pl.BlockSpec((k, tn), lambda j, i: (0, col0 // tn + j))
        w_tile = (k, tn)
    kernel = functools.partial(_wmm_kernel, w_is_transposed=w_is_transposed, sub_rows=sub_rows, epilogue=epilogue)
    return pl.pallas_call(
        kernel,
        out_shape=jax.ShapeDtypeStruct((m, n_cols), out_dtype),
        grid=(n_cols // tn, m // tm),
        in_specs=[pl.BlockSpec((tm, k), lambda j, i: (i, 0)), w_spec] + list(extra_specs),
        out_specs=pl.BlockSpec((tm, tn), lambda j, i: (i, j)),
        scratch_shapes=[pltpu.VMEM(w_tile, BF16)],
        compiler_params=_params(2),
        name=name,
    )(a, w, *extras)


def _gla_kernel(q_ref, k_ref, v_ref, gg_ref, ga_ref, wa_ref, ba_ref, gn_ref, *rest, chunk, n_chunks, has_s0):
    s0_ref = rest[0] if has_s0 else None
    o_ref, sout_ref, s_scr, qt_scr, ku_scr, o_scr, dec_scr = rest[1:] if has_s0 else rest
    _gla_block(q_ref, k_ref, v_ref, gg_ref, ga_ref, wa_ref, ba_ref, gn_ref, s0_ref,
               o_ref, sout_ref, s_scr, qt_scr, ku_scr, o_scr, dec_scr, chunk=chunk, n_chunks=n_chunks)


def _gla_block(q_ref, k_ref, v_ref, gg_ref, ga_ref, wa_ref, ba_ref, gn_ref, s0_ref,
               o_ref, sout_ref, s_scr, qt_scr, ku_scr, o_scr, dec_scr, *, chunk, n_chunks):
    t = pl.program_id(1)
    tb = chunk * n_chunks
    heads = [(slice(h * GLA_DK, (h + 1) * GLA_DK), slice(h * GLA_DV, (h + 1) * GLA_DV)) for h in range(GLA_HEADS)]

    @pl.when(t == 0)
    def _():
        s_scr[...] = jnp.zeros_like(s_scr) if s0_ref is None else s0_ref[0]

    row = lax.broadcasted_iota(jnp.int32, (chunk, chunk), 0)
    col = lax.broadcasted_iota(jnp.int32, (chunk, chunk), 1)
    tril = (row >= col).astype(BF16)
    z = jnp.dot(ga_ref[...].astype(BF16), wa_ref[...], preferred_element_type=F32) + ba_ref[...]
    log_a = (jnp.minimum(z, 0.0) - jnp.log(1.0 + jnp.exp(-jnp.abs(z)))) * (1.0 / GLA_NORMALIZER)
    hi = log_a.astype(BF16)
    rest = log_a - hi.astype(F32)
    mid = rest.astype(BF16)
    lo = (rest - mid.astype(F32)).astype(BF16)
    b_parts, last_parts = [], []
    for ci in range(n_chunks):
        crows = slice(ci * chunk, (ci + 1) * chunk)
        b_c = (jnp.dot(tril, hi[crows], preferred_element_type=F32)
               + jnp.dot(tril, mid[crows], preferred_element_type=F32)
               + jnp.dot(tril, lo[crows], preferred_element_type=F32))
        b_last = b_c[chunk - 1:chunk, :]
        b_parts.append(b_c)
        last_parts.append(jnp.broadcast_to(b_last, b_c.shape))
        for h, (ks, _) in enumerate(heads):
            dec_scr[ci, h] = jnp.transpose(jnp.broadcast_to(jnp.exp(b_last[:, ks]), (LANES, GLA_DK)))
    b = jnp.concatenate(b_parts, axis=0)
    b_last_rows = jnp.concatenate(last_parts, axis=0)

    q = q_ref[...] * (GLA_DK ** -0.5)
    k = k_ref[...]
    q_t = (q * jnp.exp(b)).astype(BF16)
    k_t = (k * jnp.exp(-b)).astype(BF16)
    qt_scr[...] = q_t
    ku_scr[...] = (k * jnp.exp(b_last_rows - b)).astype(BF16)

    brow = lax.broadcasted_iota(jnp.int32, (tb, tb), 0)
    bcol = lax.broadcasted_iota(jnp.int32, (tb, tb), 1)
    mask = (bcol >= (brow & -chunk)) & (brow >= bcol)
    for ks, vs in heads:
        att = lax.dot_general(q_t[:, ks], k_t[:, ks], (((1,), (1,)), ((), ())), preferred_element_type=F32)
        att = jnp.where(mask, att, 0.0).astype(BF16)
        o_scr[:, vs] = jnp.dot(att, v_ref[:, vs].astype(BF16), preferred_element_type=F32)

    def one_chunk(ci, carry):
        rows = pl.ds(pl.multiple_of(ci * chunk, chunk), chunk)
        for h, (ks, vs) in enumerate(heads):
            s_old = s_scr[h]
            o_scr[rows, vs] += jnp.dot(qt_scr[rows, ks], s_old.astype(BF16), preferred_element_type=F32)
            decay = jnp.concatenate([dec_scr[ci, h]] * (GLA_DV // LANES), axis=1)
            s_scr[h] = decay * s_old + lax.dot_general(ku_scr[rows, ks], v_ref[rows, vs].astype(BF16),
                                                       (((0,), (0,)), ((), ())), preferred_element_type=F32)
        return carry

    lax.fori_loop(0, n_chunks, one_chunk, 0)

    for _, vs in heads:
        o = o_scr[:, vs]
        on = o * lax.rsqrt(jnp.mean(o * o, axis=-1, keepdims=True) + EPS) * gn_ref[...]
        gg = gg_ref[:, vs]
        o_ref[:, vs] = (on * (gg * _sigmoid(gg))).astype(o_ref.dtype)

    @pl.when(t == pl.num_programs(1) - 1)
    def _():
        sout_ref[0] = s_scr[...]


def _gla(qkvg, ga, wa, ba, gn, s0, batch, seq, row0, tb, chunk):
    nt = seq // tb
    base = row0 // tb
    qk_w = GLA_HEADS * GLA_DK
    v_w = GLA_HEADS * GLA_DV
    rows = lambda b, t: base + b * nt + t
    has_s0 = s0 is not None
    kernel = functools.partial(_gla_kernel, chunk=chunk, n_chunks=tb // chunk, has_s0=has_s0)
    state_spec = pl.BlockSpec((1, GLA_HEADS, GLA_DK, GLA_DV), lambda b, t: (b, 0, 0, 0))
    return pl.pallas_call(
        kernel,
        out_shape=(jax.ShapeDtypeStruct((batch * seq, v_w), BF16),
                   jax.ShapeDtypeStruct((batch, GLA_HEADS, GLA_DK, GLA_DV), F32)),
        grid=(batch, nt),
        in_specs=[pl.BlockSpec((tb, qk_w), lambda b, t: (rows(b, t), 0)),
                  pl.BlockSpec((tb, qk_w), lambda b, t: (rows(b, t), 1)),
                  pl.BlockSpec((tb, v_w), lambda b, t: (rows(b, t), 1)),
                  pl.BlockSpec((tb, v_w), lambda b, t: (rows(b, t), 2)),
                  pl.BlockSpec((tb, LANES), lambda b, t: (rows(b, t), 0)),
                  pl.BlockSpec((LANES, qk_w), lambda b, t: (0, 0)),
                  pl.BlockSpec((1, qk_w), lambda b, t: (0, 0)),
                  pl.BlockSpec((1, GLA_DV), lambda b, t: (0, 0))] + ([state_spec] if has_s0 else []),
        out_specs=(pl.BlockSpec((tb, v_w), lambda b, t: (b * nt + t, 0)),
                   pl.BlockSpec((1, GLA_HEADS, GLA_DK, GLA_DV), lambda b, t: (b, 0, 0, 0))),
        scratch_shapes=[pltpu.VMEM((GLA_HEADS, GLA_DK, GLA_DV), F32),
                        pltpu.VMEM((tb, qk_w), BF16),
                        pltpu.VMEM((tb, qk_w), BF16),
                        pltpu.VMEM((tb, v_w), F32),
                        pltpu.VMEM((tb // chunk, GLA_HEADS, GLA_DK, LANES), F32)],
        compiler_params=_params(2),
        name="gla_chunks",
    )(qkvg, qkvg, qkvg, qkvg, ga, wa, ba, gn, *([s0] if has_s0 else []))


def _swa_kernel(sink_ref, q_ref, kp_ref, vp_ref, ko_ref, vo_ref, o_ref, bias_scr, *, tq, prev_from_cache):
    i = pl.program_id(1)
    nk = WINDOW + tq
    k_all = jnp.concatenate([kp_ref[...], ko_ref[...]], axis=0)
    v_all = jnp.concatenate([vp_ref[...], vo_ref[...]], axis=0)
    qc = lax.broadcasted_iota(jnp.int32, (tq, nk), 0) // CHUNK + WINDOW // CHUNK
    kcol = lax.broadcasted_iota(jnp.int32, (tq, nk), 1)
    kc = kcol // CHUNK
    valid = (kc <= qc) & (kc >= qc - WINDOW // CHUNK)
    if not prev_from_cache:
        valid = valid & ((kcol >= WINDOW) | (i > 0))
    bias_scr[...] = jnp.where(valid, 0.0, NEG_BIG)
    lane = lax.broadcasted_iota(jnp.int32, (nk, LANES), 1)
    low = lane < SWA_HD
    low_q = lax.broadcasted_iota(jnp.int32, (tq, LANES), 1) < SWA_HD
    kms, vms = [], []
    for g in range(SWA_KV_HEADS):
        slab = slice((g // 2) * LANES, (g // 2 + 1) * LANES)
        k2 = k_all[:, slab]
        v2 = v_all[:, slab]
        k2r = pltpu.roll(k2, SWA_HD, 1)
        v2r = pltpu.roll(v2, SWA_HD, 1)
        if g % 2 == 0:
            k_lo, k_hi, v_lo, v_hi = k2, k2r, v2, v2r
        else:
            k_lo, k_hi, v_lo, v_hi = k2r, k2, v2r, v2
        zero = jnp.zeros_like(k2)
        one = jnp.ones_like(k2)
        kms.append((jnp.where(low, k_lo, zero).astype(BF16), jnp.where(low, zero, k_hi).astype(BF16)))
        vms.append((jnp.where(low, v_lo, one).astype(BF16), jnp.where(low, one, v_hi).astype(BF16)))

    for g0 in range(0, SWA_KV_HEADS, SWA_STAGE_KV_HEADS):
        heads = [(g, j, half) for g in range(g0, g0 + SWA_STAGE_KV_HEADS)
                 for j in range(SWA_GROUP // 2) for half in range(2)]
        sinks = [sink_ref[g * SWA_GROUP + 2 * j + half] for g, j, half in heads]
        scores = []
        for g, j, half in heads:
            qs = q_ref[:, (g * 4 + j) * LANES:(g * 4 + j + 1) * LANES]
            s = lax.dot_general(qs, kms[g][half], (((1,), (1,)), ((), ())), preferred_element_type=F32)
            scores.append(s + bias_scr[...])
        maxes = [jnp.maximum(jnp.max(s, axis=-1, keepdims=True), sk) for s, sk in zip(scores, sinks)]
        exps = [jnp.exp(s - m).astype(BF16) for s, m in zip(scores, maxes)]
        sink_terms = [jnp.exp(sk - m) for sk, m in zip(sinks, maxes)]
        for idx in range(0, len(heads), 2):
            g, j, _ = heads[idx]
            a_lo = jnp.dot(exps[idx], vms[g][0], preferred_element_type=F32)
            a_hi = jnp.dot(exps[idx + 1], vms[g][1], preferred_element_type=F32)
            num = jnp.where(low_q, a_lo, a_hi)
            den = pltpu.roll(jnp.where(low_q, a_hi, a_lo), SWA_HD, 1)
            den = den + jnp.where(low_q, sink_terms[idx], sink_terms[idx + 1])
            o_ref[:, (g * 4 + j) * LANES:(g * 4 + j + 1) * LANES] = (num / den).astype(o_ref.dtype)


def _swa(sinks, q, q_row0, k_prev, v_prev, prev_col, k_own, v_own, own_col, own_row0,
         batch, seq, tq, prev_from_cache):
    nt = seq // tq
    qb = q_row0 // tq
    ob = own_row0 // tq
    kv_w = SWA_KV_HEADS * SWA_HD
    if prev_from_cache:
        prev_map = lambda b, t, c: (b, c)
    else:
        per = seq // WINDOW
        prev_map = lambda b, t, c: (b * per + jnp.maximum(t * (tq // WINDOW) - 1, 0), c)
    kernel = functools.partial(_swa_kernel, tq=tq, prev_from_cache=prev_from_cache)
    return pl.pallas_call(
        kernel,
        out_shape=jax.ShapeDtypeStruct((batch * seq, SWA_HEADS * SWA_HD), BF16),
        grid=(batch, nt),
        in_specs=[pl.BlockSpec(memory_space=pltpu.SMEM),
                  pl.BlockSpec((tq, SWA_HEADS * SWA_HD), lambda b, t: (qb + b * nt + t, 0)),
                  pl.BlockSpec((WINDOW, kv_w), lambda b, t: prev_map(b, t, prev_col[0])),
                  pl.BlockSpec((WINDOW, kv_w), lambda b, t: prev_map(b, t, prev_col[1])),
                  pl.BlockSpec((tq, kv_w), lambda b, t: (ob + b * nt + t, own_col[0])),
                  pl.BlockSpec((tq, kv_w), lambda b, t: (ob + b * nt + t, own_col[1]))],
        out_specs=pl.BlockSpec((tq, SWA_HEADS * SWA_HD), lambda b, t: (b * nt + t, 0)),
        scratch_shapes=[pltpu.VMEM((tq, WINDOW + tq), F32)],
        compiler_params=_params(2),
        name="swa_band",
    )(sinks, q, k_prev, v_prev, k_own, v_own)


def _mem_attn_head(q, k, v):
    s = lax.dot_general(q, k.astype(BF16), (((1,), (1,)), ((), ())),
                        preferred_element_type=F32) * (MEM_HD ** -0.5)
    m = jnp.max(s, axis=-1, keepdims=True)
    e = jnp.exp(s - m)
    p = (e / jnp.sum(e, axis=-1, keepdims=True)).astype(BF16)
    return jnp.dot(p, v.astype(BF16), preferred_element_type=F32)


def _mem_attn_kernel(q_ref, kv_ref, o_ref, kv_bf):
    @pl.when(pl.program_id(1) == 0)
    def _():
        kv_bf[...] = kv_ref[...].astype(BF16)

    width = MEM_HEADS * MEM_HD
    cols = [slice(h * MEM_HD, (h + 1) * MEM_HD) for h in range(MEM_HEADS)]
    scores = [lax.dot_general(q_ref[:, c], kv_bf[:, c], (((1,), (1,)), ((), ())),
                              preferred_element_type=F32) * (MEM_HD ** -0.5) for c in cols]
    exps = [jnp.exp(s - jnp.max(s, axis=-1, keepdims=True)) for s in scores]
    probs = [(e / jnp.sum(e, axis=-1, keepdims=True)).astype(BF16) for e in exps]
    for c, p in zip(cols, probs):
        v = kv_bf[:, width + c.start:width + c.stop]
        o_ref[:, c] = jnp.dot(p, v, preferred_element_type=F32).astype(o_ref.dtype)


def _mem_attn_cache_kernel(q_ref, k_hbm, v_hbm, o_ref, kbuf, vbuf, sem):
    b = pl.program_id(0)
    slot = b % 2

    def copies(req, s):
        return ([pltpu.make_async_copy(k_hbm.at[0, req, :, h, :], kbuf.at[s, h], sem.at[s, 0])
                 for h in range(MEM_HEADS)]
                + [pltpu.make_async_copy(v_hbm.at[0, req, :, h, :], vbuf.at[s, h], sem.at[s, 1])
                   for h in range(MEM_HEADS)])

    @pl.when(b == 0)
    def _():
        for c in copies(0, 0):
            c.start()

    @pl.when(b + 1 < pl.num_programs(0))
    def _():
        for c in copies(b + 1, 1 - slot):
            c.start()

    for c in copies(b, slot):
        c.wait()
    for h in range(MEM_HEADS):
        cols = slice(h * MEM_HD, (h + 1) * MEM_HD)
        o_ref[:, cols] = _mem_attn_head(q_ref[:, cols], kbuf[slot, h], vbuf[slot, h]).astype(o_ref.dtype)


def _mem_attn_cache(q, q_row0, cache_k, cache_v, batch, seq):
    qb = q_row0 // seq
    width = MEM_HEADS * MEM_HD
    hbm = pl.BlockSpec(memory_space=pl.ANY)
    slabs = pltpu.VMEM((2, MEM_HEADS, N_MEM, MEM_HD), F32)
    return pl.pallas_call(
        _mem_attn_cache_kernel,
        out_shape=jax.ShapeDtypeStruct((batch * seq, width), BF16),
        grid=(batch,),
        in_specs=[pl.BlockSpec((seq, width), lambda b: (qb + b, 0)), hbm, hbm],
        out_specs=pl.BlockSpec((seq, width), lambda b: (b, 0)),
        scratch_shapes=[slabs, slabs, pltpu.SemaphoreType.DMA((2, 2))],
        compiler_params=_params(1),
        name="mem_attn_cache",
    )(q, cache_k, cache_v)


def _mem_attn(q, mem_kv, batch, seq, tq):
    nt = seq // tq
    width = MEM_HEADS * MEM_HD
    return pl.pallas_call(
        _mem_attn_kernel,
        out_shape=jax.ShapeDtypeStruct((batch * seq, width), BF16),
        grid=(batch, nt),
        in_specs=[pl.BlockSpec((tq, width), lambda b, t: (b * nt + t, 0)),
                  pl.BlockSpec((N_MEM, 2 * width), lambda b, t: (b, 0))],
        out_specs=pl.BlockSpec((tq, width), lambda b, t: (b * nt + t, 0)),
        scratch_shapes=[pltpu.VMEM((N_MEM, 2 * width), BF16)],
        compiler_params=_params(2),
        name="mem_attn",
    )(q, mem_kv)


def _merge_kernel(a0p, a0s, a1p, a1s, a2p, a2s, w_ref, g0_ref, g1_ref, g2_ref, o_ref, *, p_tiles):
    def emit(a0_ref, a1_ref, a2_ref):
        acc = g0_ref[...] * jnp.dot(a0_ref[...], w_ref[0], preferred_element_type=F32)
        acc = acc + g1_ref[...] * jnp.dot(a1_ref[...], w_ref[1], preferred_element_type=F32)
        acc = acc + g2_ref[...] * jnp.dot(a2_ref[...], w_ref[2], preferred_element_type=F32)
        o_ref[...] = acc.astype(o_ref.dtype)

    @pl.when(pl.program_id(1) < p_tiles)
    def _():
        emit(a0p, a1p, a2p)

    @pl.when(pl.program_id(1) >= p_tiles)
    def _():
        emit(a0s, a1s, a2s)


def _merge(branches_p, branches_s, w_branch, gates, tm, tn):
    n_p, d = branches_p[0].shape
    n = n_p + branches_s[0].shape[0]
    nj = d // tn
    p_tiles = n_p // tm
    ap = pl.BlockSpec((tm, d), lambda j, i: (jnp.minimum(i, p_tiles - 1), 0))
    asp = pl.BlockSpec((tm, d), lambda j, i: (jnp.maximum(i - p_tiles, 0), 0))
    operands = [a for pair in zip(branches_p, branches_s) for a in pair]
    return pl.pallas_call(
        functools.partial(_merge_kernel, p_tiles=p_tiles),
        out_shape=jax.ShapeDtypeStruct((n, d), BF16),
        grid=(nj, n // tm),
        in_specs=[ap, asp, ap, asp, ap, asp,
                  pl.BlockSpec((3, d, tn), lambda j, i: (0, 0, j), pipeline_mode=pl.Buffered(1)),
                  pl.BlockSpec((tm, tn), lambda j, i: (i, j)),
                  pl.BlockSpec((tm, tn), lambda j, i: (i, nj + j)),
                  pl.BlockSpec((tm, tn), lambda j, i: (i, 2 * nj + j))],
        out_specs=pl.BlockSpec((tm, tn), lambda j, i: (i, j)),
        compiler_params=_params(2),
        name="branch_merge",
    )(*operands, w_branch, gates, gates, gates)


def _outproj_router_kernel(m_ref, w_ref, xp_ref, xs_ref, g_ref, wr_ref, br_ref,
                           x2_ref, hf_ref, eid_ref, wt_ref, *, p_tiles):
    is_prompt = pl.program_id(0) < p_tiles
    tm, d = x2_ref.shape
    for r0 in range(0, tm, ROUTER_SUB_ROWS):
        rows = slice(r0, r0 + ROUTER_SUB_ROWS)
        acc = jnp.dot(m_ref[rows, :], w_ref[...], preferred_element_type=F32)
        x = jnp.where(is_prompt, xp_ref[rows, :], xs_ref[rows, :]) + acc
        x2_ref[rows, :] = x
        hf = x * lax.rsqrt(jnp.mean(x * x, axis=-1, keepdims=True) + EPS) * g_ref[...]
        hb = hf.astype(BF16)
        bits = pltpu.bitcast(hb.astype(F32), jnp.uint32)
        packed = bits[:, d // 2:] | (bits[:, :d // 2] >> 16)
        for s in range(TOKEN_TILE_ROWS):
            hf_ref[pl.ds(r0 * TOKEN_TILE_ROWS + s, ROUTER_SUB_ROWS, stride=TOKEN_TILE_ROWS), :] = (
                packed[:, s * LANES:(s + 1) * LANES])
        logits = jnp.dot(hb, wr_ref[...], preferred_element_type=F32) + br_ref[...]
        eid, wts = _route(logits)
        eid_ref[rows, :] = eid
        wt_ref[rows, :] = wts


def _route(logits):
    lane = lax.broadcasted_iota(jnp.int32, logits.shape, 1).astype(F32)
    big = 1e6
    is_g = lane < N_GROUPS
    lg = jnp.where(is_g, logits, NEG_BIG)
    mg = jnp.max(lg, axis=-1, keepdims=True)
    gsel = jnp.min(jnp.where(is_g & (lg == mg), lane, big), axis=-1, keepdims=True)
    g_w = 1.0 / jnp.sum(jnp.where(is_g, jnp.exp(lg - mg), 0.0), axis=-1, keepdims=True)
    e_lo = N_GROUPS + gsel * EXPERTS_PER_GROUP
    in_grp = (lane >= e_lo) & (lane < e_lo + EXPERTS_PER_GROUP)
    le = jnp.where(in_grp, logits, NEG_BIG)
    me = jnp.max(le, axis=-1, keepdims=True)
    ee = jnp.where(in_grp, jnp.exp(le - me), 0.0)
    pe = ee / jnp.sum(ee, axis=-1, keepdims=True)
    pe = jnp.where(in_grp, pe, -1.0)
    p1 = jnp.max(pe, axis=-1, keepdims=True)
    i1 = jnp.min(jnp.where(pe == p1, lane, big), axis=-1, keepdims=True)
    pe2 = jnp.where(lane == i1, -1.0, pe)
    p2 = jnp.max(pe2, axis=-1, keepdims=True)
    i2 = jnp.min(jnp.where(pe2 == p2, lane, big), axis=-1, keepdims=True)
    tot = p1 + p2
    w1 = g_w * p1 / tot
    w2 = g_w * p2 / tot
    eid = jnp.where(lane == 0.0, i1 - N_GROUPS, jnp.where(lane == 1.0, i2 - N_GROUPS, 0.0))
    return eid.astype(jnp.int32), jnp.where(lane == 0.0, w1, jnp.where(lane == 1.0, w2, 0.0))


def _outproj_router(merged, w_out, xp, xs, g, wr, br, tm):
    n, d = merged.shape
    p_tiles = xp.shape[0] // tm
    const = lambda shape: pl.BlockSpec(shape, lambda i: (0, 0), pipeline_mode=pl.Buffered(1))
    row = lambda width: pl.BlockSpec((tm, width), lambda i: (i, 0))
    return pl.pallas_call(
        functools.partial(_outproj_router_kernel, p_tiles=p_tiles),
        out_shape=(jax.ShapeDtypeStruct((n, d), F32),
                   jax.ShapeDtypeStruct((n * TOKEN_TILE_ROWS, LANES), jnp.uint32),
                   jax.ShapeDtypeStruct((n, LANES), jnp.int32),
                   jax.ShapeDtypeStruct((n, LANES), F32)),
        grid=(n // tm,),
        in_specs=[row(d), const((d, d)), *_split_specs(p_tiles, tm, d),
                  const((1, d)), const((d, LANES)), const((1, LANES))],
        out_specs=(row(d), pl.BlockSpec((tm * TOKEN_TILE_ROWS, LANES), lambda i: (i, 0)),
                   row(LANES), row(LANES)),
        compiler_params=_params(1),
        name="outproj_router",
    )(merged, w_out, xp, xs, g.reshape(1, d), wr, br)


def _moe_kernel(te_ref, nv_ref, nxt_ref, tok_ref, tok_next_ref, dst_ref, hf_hbm, wup_hbm, wdn_hbm, y_hbm,
                xg, yb, wup_f32, wdn_f32, wup_bf, wdn_bf, in_sem, out_sem, w_sem):
    t = pl.program_id(0)
    buf = t % 2
    rows_now = nv_ref[t]
    rows_next = nv_ref[t + 1]
    rows_prev = nv_ref[jnp.maximum(t - 1, 0)]
    valid = rows_now > 0
    valid_next = rows_next > 0
    new_expert = (t == 0) | (te_ref[t] != te_ref[jnp.maximum(t - 1, 0)])

    def weight_copies(e):
        return (pltpu.make_async_copy(wup_hbm.at[e], wup_f32, w_sem.at[0]),
                pltpu.make_async_copy(wdn_hbm.at[e], wdn_f32, w_sem.at[1]))

    in_rows, in_pitch = TOKEN_TILE_ROWS, TOKEN_TILE_PITCH
    out_rows, out_pitch = TOKEN_F32_ROWS, TOKEN_F32_PITCH

    def row_in(row0, r, b):
        return pltpu.make_async_copy(hf_hbm.at[pl.ds(pl.multiple_of(row0, in_rows), in_rows), :],
                                     xg.at[b, pl.ds(r * in_pitch, in_rows), :], in_sem.at[b])

    def row_out(row0, r, b):
        return pltpu.make_async_copy(yb.at[b, pl.ds(r * out_pitch, out_rows), :],
                                     y_hbm.at[pl.ds(pl.multiple_of(row0, out_rows), out_rows), :], out_sem.at[b])

    group = MOE_ROW_GROUP
    n_groups = MOE_TILE // group

    def group_in(b):
        return pltpu.make_async_copy(hf_hbm.at[pl.ds(0, group * in_rows), :],
                                     xg.at[b, pl.ds(0, group * in_rows), :], in_sem.at[b])

    def group_out(b):
        return pltpu.make_async_copy(yb.at[b, pl.ds(0, group * out_rows), :],
                                     y_hbm.at[pl.ds(0, group * out_rows), :], out_sem.at[b])

    def per_started_group(rows, fn):
        for g in range(n_groups):
            pl.when(rows > g * group)(functools.partial(fn, g))

    @pl.when(t == 0)
    def _():
        for c in weight_copies(te_ref[0]):
            c.start(priority=1)
        xg[...] = jnp.zeros_like(xg)

        def first(r, c):
            row_in(tok_ref[0, 0, r], r, 0).start()
            return c
        lax.fori_loop(0, ((rows_now + group - 1) // group) * group, first, 0)
        yb[...] = jnp.zeros_like(yb)
        n_real = y_hbm.shape[0] - 2 * MOE_TILE * out_rows
        for b in range(2):
            spare = pltpu.make_async_copy(
                yb.at[b, pl.ds(0, MOE_TILE * out_rows), :],
                y_hbm.at[pl.ds(n_real + b * MOE_TILE * out_rows, MOE_TILE * out_rows), :], out_sem.at[b])
            spare.start()
            spare.wait()

    @pl.when(valid & new_expert)
    def _():
        for c in weight_copies(te_ref[t]):
            c.wait()
        wup_bf[...] = wup_f32[...].astype(BF16)
        wdn_bf[...] = wdn_f32[...].astype(BF16)

        @pl.when(nxt_ref[t] >= 0)
        def _():
            for c in weight_copies(nxt_ref[t]):
                c.start(priority=1)

    def gather_next(g):
        for r in range(g * group, (g + 1) * group):
            row_in(tok_next_ref[0, 0, r], r, 1 - buf).start()

    def scatter_now(g):
        for r in range(g * group, (g + 1) * group):
            row_out(dst_ref[0, 0, r], r, buf).start(priority=r % 2)

    per_started_group(rows_next, gather_next)

    @pl.when(valid)
    def _():
        per_started_group(rows_now, lambda g: group_in(buf).wait())
        lo, hi = [], []
        for s in range(in_rows):
            word = xg[buf, pl.ds(s, MOE_TILE, stride=in_pitch), :]
            lo.append(pltpu.bitcast(word << 16, F32).astype(BF16))
            hi.append(pltpu.bitcast(word & jnp.uint32(0xFFFF0000), F32).astype(BF16))
        x = jnp.concatenate(lo + hi, axis=1)
        h1 = jnp.dot(x, wup_bf[...], preferred_element_type=F32)
        gate = h1[:, :D_FF]
        up = h1[:, D_FF:]
        act = (gate * _sigmoid(gate)) * up
        ye = jnp.dot(act.astype(BF16), wdn_bf[...], preferred_element_type=F32)
        for s in range(out_rows):
            yb[buf, pl.ds(s, MOE_TILE, stride=out_pitch), :] = ye[:, s * LANES:(s + 1) * LANES]
        per_started_group(rows_now, scatter_now)

        @pl.when(t > 0)
        def _():
            per_started_group(rows_prev, lambda g: group_out(1 - buf).wait())

        @pl.when(jnp.logical_not(valid_next))
        def _():
            per_started_group(rows_now, lambda g: group_out(buf).wait())


def _moe(hf, tile_expert, tile_rows, next_expert, tok_slots, dst_slots, w_up, w_down, out_tokens):
    n_tiles = tile_expert.shape[0]
    d = w_up.shape[1]
    slot_spec = lambda off: pl.BlockSpec((1, 1, MOE_TILE),
                                         lambda t, te, nv, ne: (jnp.minimum(t + off, n_tiles - 1), 0, 0),
                                         memory_space=pltpu.SMEM)
    hbm = pl.BlockSpec(memory_space=pl.ANY)
    grid_spec = pltpu.PrefetchScalarGridSpec(
        num_scalar_prefetch=3,
        grid=(n_tiles,),
        in_specs=[slot_spec(0), slot_spec(1), slot_spec(0), hbm, hbm, hbm],
        out_specs=hbm,
        scratch_shapes=[pltpu.VMEM((2, MOE_TILE * TOKEN_TILE_PITCH, LANES), jnp.uint32),
                        pltpu.VMEM((2, MOE_TILE * TOKEN_F32_PITCH, LANES), F32),
                        pltpu.VMEM((d, 2 * D_FF), F32),
                        pltpu.VMEM((D_FF, d), F32),
                        pltpu.VMEM((d, 2 * D_FF), BF16),
                        pltpu.VMEM((D_FF, d), BF16),
                        pltpu.SemaphoreType.DMA((2,)),
                        pltpu.SemaphoreType.DMA((2,)),
                        pltpu.SemaphoreType.DMA((2,))],
    )
    return pl.pallas_call(
        _moe_kernel,
        out_shape=jax.ShapeDtypeStruct((out_tokens * TOKEN_F32_ROWS, LANES), F32),
        grid_spec=grid_spec,
        compiler_params=_params(1),
        name="moe_experts",
    )(tile_expert, tile_rows, next_expert, tok_slots, tok_slots, dst_slots, hf, w_up, w_down)


def _combine_kernel(x_ref, y0_ref, y1_ref, w_ref, op_ref, os_ref, *, p_tiles):
    tm = x_ref.shape[0]
    w = w_ref[...]
    w0 = w[:, 0:1]
    w1 = w[:, 1:2]

    def emit(o_ref):
        for s in range(TOKEN_F32_ROWS):
            cols = slice(s * LANES, (s + 1) * LANES)
            rows = pl.ds(s, tm, stride=TOKEN_F32_ROWS)
            o_ref[:, cols] = x_ref[:, cols] + (y0_ref[rows, :] * w0 + y1_ref[rows, :] * w1)

    @pl.when(pl.program_id(0) < p_tiles)
    def _():
        emit(op_ref)

    @pl.when(pl.program_id(0) >= p_tiles)
    def _():
        emit(os_ref)


def _combine(x2, yk, wts, n_p, tm):
    n, d = x2.shape
    p_tiles = n_p // tm
    k1 = n // tm
    y_rows = tm * TOKEN_F32_ROWS
    return pl.pallas_call(
        functools.partial(_combine_kernel, p_tiles=p_tiles),
        out_shape=(jax.ShapeDtypeStruct((n_p, d), F32), jax.ShapeDtypeStruct((n - n_p, d), F32)),
        grid=(n // tm,),
        in_specs=[pl.BlockSpec((tm, d), lambda i: (i, 0)),
                  pl.BlockSpec((y_rows, LANES), lambda i: (i, 0)),
                  pl.BlockSpec((y_rows, LANES), lambda i: (k1 + i, 0)),
                  pl.BlockSpec((tm, LANES), lambda i: (i, 0))],
        out_specs=_split_specs(p_tiles, tm, d),
        compiler_params=_params(1),
        name="moe_combine",
    )(x2, yk, yk, wts)


def _rope_tables(pos):
    half = ROPE_DIM // 2
    inv = ROPE_THETA ** (-jnp.arange(half, dtype=F32) / half)
    ang = pos.astype(F32)[:, None] * inv[None, :]
    cos, sin = jnp.cos(ang), jnp.sin(ang)
    n = pos.shape[0]
    pad = jnp.zeros((n, SWA_HD - ROPE_DIM), F32)
    cos_h = jnp.concatenate([cos, cos, pad + 1.0], axis=1)
    sa_h = jnp.concatenate([-sin, jnp.zeros_like(sin), pad], axis=1)
    sb_h = jnp.concatenate([jnp.zeros_like(sin), sin, pad], axis=1)
    reps = LANES // SWA_HD
    return tuple(jnp.tile(a, (1, reps)) for a in (cos_h, sa_h, sb_h))


def _moe_schedule(eid, n_tok, n_tiles):
    a = eid.shape[0]
    order = jnp.argsort(eid, stable=True).astype(jnp.int32)
    counts = jnp.bincount(eid, length=N_EXPERTS).astype(jnp.int32)
    tiles_per = (counts + MOE_TILE - 1) // MOE_TILE
    tile_end = jnp.cumsum(tiles_per)
    tile_start = tile_end - tiles_per
    sorted_start = jnp.cumsum(counts) - counts
    tile_id = jnp.arange(n_tiles, dtype=jnp.int32)
    used = tile_id < tile_end[-1]
    te = jnp.minimum(jnp.sum(tile_end[None, :] <= tile_id[:, None], axis=1), N_EXPERTS - 1).astype(jnp.int32)
    last_used_e = te[jnp.maximum(tile_end[-1] - 1, 0)]
    te = jnp.where(used, te, last_used_e)
    row_in_expert = (tile_id - tile_start[te]) * MOE_TILE
    rows_valid = jnp.where(used, jnp.clip(counts[te] - row_in_expert, 0, MOE_TILE), 0).astype(jnp.int32)
    r = jnp.arange(MOE_TILE, dtype=jnp.int32)[None, :]
    src = sorted_start[te][:, None] + row_in_expert[:, None] + r
    real = r < rows_valid[:, None]
    assign = order[jnp.clip(src, 0, a - 1)]
    tok = jnp.where(real, assign % n_tok, 0) * TOKEN_TILE_ROWS
    spare = a + (tile_id[:, None] % 2) * MOE_TILE + r
    dst = jnp.where(real, assign, spare) * TOKEN_F32_ROWS
    tile_rows = jnp.concatenate([rows_valid, jnp.zeros((1,), jnp.int32)])
    e_id = jnp.arange(N_EXPERTS, dtype=jnp.int32)[None, :]
    later = (e_id > te[:, None]) & (counts[None, :] > 0)
    nxt = jnp.min(jnp.where(later, e_id, N_EXPERTS), axis=1)
    nxt = jnp.where(nxt < N_EXPERTS, nxt, -1).astype(jnp.int32)
    shape = (n_tiles, 1, MOE_TILE)
    return te, tile_rows, nxt, tok.reshape(shape).astype(jnp.int32), dst.reshape(shape).astype(jnp.int32)


def kernel(x_prompt, x_sample, state_gla, cache_swa_k, cache_swa_v, cache_mem_k, cache_mem_v,
           mem_prompt, norm_mix_g, w_in, w_a2, b_a2, gla_norm_g, swa_q_norm_g, swa_k_norm_g,
           swa_sinks, norm_mem_g, w_mem_kv, mem_q_norm_g, mem_k_norm_g, w_gate, b_gate,
           w_branch, w_out, norm_ffn_g, w_router_group, b_router_group, w_router_expert,
           b_router_expert, w_up, w_down):
    bp, tp, d = x_prompt.shape
    bs, ts, _ = x_sample.shape
    n_p, n_s = bp * tp, bs * ts
    n = n_p + n_s
    tm, tn = ROW_TILE, COL_TILE
    tp_rows = PROJ_ROWS if n % PROJ_ROWS == 0 else tm
    heavy_sub = tp_rows // 2
    norm_sub = tp_rows // 4
    assert d == D_MODEL and n_p % tm == 0 and n_s % tm == 0 and w_in.shape[0] == 1
    keep_s = cache_swa_k.shape[2]
    assert keep_s == WINDOW and tp % WINDOW == 0

    qk_w = GLA_HEADS * GLA_DK
    v_w = GLA_HEADS * GLA_DV
    c0 = 2 * qk_w + 2 * v_w
    sq_w = SWA_HEADS * SWA_HD
    kv_w = SWA_KV_HEADS * SWA_HD
    mem_w = MEM_HEADS * MEM_HD
    w_in_t = jnp.transpose(w_in[0])
    w_branch_b = w_branch[0].astype(BF16)
    w_out_b = w_out[0].astype(BF16)
    w_a2_b = jnp.pad(w_a2[0], ((0, LANES - GLA_LOWRANK), (0, 0))).astype(BF16)
    w_router = jnp.pad(jnp.concatenate([w_router_group[0], w_router_expert[0]], axis=1),
                       ((0, 0), (0, LANES - N_GROUPS - N_EXPERTS))).astype(BF16)
    b_router = jnp.pad(jnp.concatenate([b_router_group[0], b_router_expert[0]]),
                       (0, LANES - N_GROUPS - N_EXPERTS)).reshape(1, LANES)

    pos = jnp.concatenate([jnp.tile(jnp.arange(tp, dtype=jnp.int32), bp),
                           jnp.tile(PAST_LEN + jnp.arange(ts, dtype=jnp.int32), bs)])
    cos_t, sa_t, sb_t = _rope_tables(pos)
    seg_id = jnp.arange(tn, dtype=jnp.int32) // SWA_HD
    seg = (seg_id[:, None] == seg_id[None, :]).astype(BF16)
    rope_specs = [pl.BlockSpec((tp_rows, LANES), lambda j, i: (i, 0))] * 3
    row_vec = lambda width: pl.BlockSpec((1, width), lambda j, i: (0, 0))
    seg_spec = pl.BlockSpec((tn, tn), lambda j, i: (0, 0))

    xp2 = x_prompt.reshape(n_p, d)
    xs2 = x_sample.reshape(n_s, d)
    h = _rms_norm_two(xp2, xs2, norm_mix_g[0], tm)

    c1 = c0 + GLA_LOWRANK
    c2 = c1 + sq_w
    c3 = c2 + 2 * kv_w
    qkvg = _wmatmul(_ep_plain, h, w_in_t, 0, c0, True, [], [], F32, tp_rows, WIDE_COLS, "proj_gla",
                    sub_rows=heavy_sub)
    ga = _wmatmul(_ep_lowrank, h, w_in_t, c0, LANES, True, [], [], F32, tp_rows, LANES, "proj_gla_lowrank")
    q_gain = jnp.tile(swa_q_norm_g[0] * (SWA_HD ** -0.5), tn // SWA_HD).reshape(1, tn)
    q_swa = _wmatmul(functools.partial(_ep_qknorm_rope, keep_from=None), h, w_in_t, c1, sq_w, True,
                     [q_gain, seg, cos_t, sa_t, sb_t], [row_vec(tn), seg_spec] + rope_specs,
                     BF16, tp_rows, tn, "proj_swa_q", sub_rows=norm_sub)
    k_gain = jnp.tile(swa_k_norm_g[0], tn // SWA_HD).reshape(1, tn)
    kv_swa = _wmatmul(functools.partial(_ep_qknorm_rope, keep_from=kv_w), h, w_in_t, c2, 2 * kv_w, True,
                      [k_gain, seg, cos_t, sa_t, sb_t], [row_vec(tn), seg_spec] + rope_specs,
                      F32, tp_rows, tn, "proj_swa_kv", sub_rows=norm_sub)
    q_mem = _wmatmul(functools.partial(_ep_headnorm, norm_tiles=None), h, w_in_t, c3, mem_w, True,
                     [mem_q_norm_g[0].reshape(1, MEM_HD)], [row_vec(MEM_HD)],
                     BF16, tp_rows, MEM_HD, "proj_mem_q", sub_rows=norm_sub)
    gates = _wmatmul(_ep_sigmoid, h, w_gate[0], 0, 3 * d, False, [b_gate[0].reshape(1, -1)],
                     [pl.BlockSpec((1, WIDE_COLS), lambda j, i: (0, j))], F32, tp_rows, WIDE_COLS, "proj_gates",
                     sub_rows=heavy_sub)

    mem_rows = bp * N_MEM
    hm = _rms_norm_rows(mem_prompt.reshape(mem_rows, d), norm_mem_g[0], BF16, N_MEM)
    mem_kv = _wmatmul(functools.partial(_ep_headnorm, norm_tiles=MEM_HEADS), hm, w_mem_kv[0], 0, 2 * mem_w, False,
                      [mem_k_norm_g[0].reshape(1, MEM_HD)], [row_vec(MEM_HD)], F32, mem_rows, MEM_HD, "mem_kv")

    ba = b_a2[0].reshape(1, qk_w)
    gn = gla_norm_g[0].reshape(1, GLA_DV)
    o_gla_p, gla_state_p = _gla(qkvg, ga, w_a2_b, ba, gn, None, bp, tp, 0, 256, CHUNK)
    o_gla_s, gla_state_s = _gla(qkvg, ga, w_a2_b, ba, gn, state_gla[0], bs, ts, n_p, ts, min(CHUNK, ts))

    sinks = swa_sinks[0]
    o_swa_p = _swa(sinks, q_swa, 0, kv_swa, kv_swa, (0, 1), kv_swa, kv_swa, (0, 1), 0,
                   bp, tp, WINDOW, False)
    ck = cache_swa_k[0].reshape(bs * keep_s, kv_w)
    cv = cache_swa_v[0].reshape(bs * keep_s, kv_w)
    o_swa_s = _swa(sinks, q_swa, n_p, ck, cv, (0, 0), kv_swa, kv_swa, (0, 1), n_p,
                   bs, ts, ts, True)

    o_mem_p = _mem_attn(q_mem, mem_kv, bp, tp, min(tp, MEM_ATTN_ROWS))
    o_mem_s = _mem_attn_cache(q_mem, n_p, cache_mem_k, cache_mem_v, bs, ts)

    merged = _merge((o_gla_p, o_swa_p, o_mem_p), (o_gla_s, o_swa_s, o_mem_s), w_branch_b, gates, tm, WIDE_COLS)

    x2, hf, eid, wts = _outproj_router(merged, w_out_b, xp2, xs2, norm_ffn_g[0], w_router, b_router, tm)

    n_assign = TOP_K * n
    n_tiles = n_assign // MOE_TILE + N_EXPERTS
    eid_kmajor = jnp.concatenate([eid[:, k] for k in range(TOP_K)])
    tile_expert, tile_rows, next_expert, tok_slots, dst_slots = _moe_schedule(eid_kmajor, n, n_tiles)
    yk = _moe(hf, tile_expert, tile_rows, next_expert, tok_slots, dst_slots, w_up[0], w_down[0],
              n_assign + 2 * MOE_TILE)
    y_p, y_s = _combine(x2, yk, wts, n_p, tm)

    y_p = y_p.reshape(bp, tp, d)
    y_s = y_s.reshape(bs, ts, d)
    kv_p = jnp.stack([kv_swa[(b + 1) * tp - WINDOW:(b + 1) * tp] for b in range(bp)])
    kv_p = kv_p.reshape(bp, WINDOW, 2, SWA_KV_HEADS, SWA_HD)
    kv_s = kv_swa[n_p:].reshape(bs, ts, 2, SWA_KV_HEADS, SWA_HD)
    swk_s = jnp.concatenate([cache_swa_k[0], kv_s[:, :, 0]], axis=1)[:, ts:ts + keep_s]
    swv_s = jnp.concatenate([cache_swa_v[0], kv_s[:, :, 1]], axis=1)[:, ts:ts + keep_s]
    mk_p = mem_kv[:, :mem_w].reshape(bp, N_MEM, MEM_HEADS, MEM_HD)
    mv_p = mem_kv[:, mem_w:].reshape(bp, N_MEM, MEM_HEADS, MEM_HD)
    return (y_p, y_s, gla_state_p[None], kv_p[:, :, 0][None], kv_p[:, :, 1][None], mk_p[None], mv_p[None],
            gla_state_s[None], swk_s[None], swv_s[None])
```

```python
import functools

import jax
import jax.numpy as jnp
from jax import lax
from jax.experimental import pallas as pl
from jax.experimental.pallas import tpu as pltpu

F32 = jnp.float32
BF16 = jnp.bfloat16

D_MODEL = 2048
CHUNK = 64
EPS = 1e-6
PAST_LEN = 1024
GLA_HEADS = 4
GLA_DV = 512
GLA_DK = 256
GLA_LOWRANK = 16
GLA_NORMALIZER = 16.0
SWA_HD = 64
SWA_HEADS = 32
SWA_KV_HEADS = 4
SWA_GROUP = 8
WINDOW = 128
ROPE_DIM = 16
ROPE_THETA = 500000.0
N_MEM = 256
MEM_HEADS = 4
MEM_HD = 512
N_GROUPS = 8
EXPERTS_PER_GROUP = 8
N_EXPERTS = 64
TOP_K = 2
D_FF = 512

LANES = 128
VMEM_LIMIT = 56 * 1024 * 1024
ROW_TILE = 512
PROJ_ROWS = 1536
COL_TILE = 512
WIDE_COLS = 1024
GLA_BLOCK_ROWS = 512
MEM_ATTN_ROWS = 1024
SWA_STAGE_KV_HEADS = 2
ROUTER_SUB_ROWS = 256
MOE_TILE = 256
MOE_ROW_GROUP = 32
TOKEN_TILE_ROWS = D_MODEL // 2 // LANES
TOKEN_TILE_PITCH = 12
TOKEN_F32_ROWS = D_MODEL // LANES
TOKEN_F32_PITCH = 20
NEG_BIG = -1e30


def _params(n_axes):
    return pltpu.CompilerParams(dimension_semantics=("arbitrary",) * n_axes,
                                vmem_limit_bytes=VMEM_LIMIT)


def _norm_kernel(x_ref, g_ref, o_ref):
    x = x_ref[...]
    y = x * lax.rsqrt(jnp.mean(x * x, axis=-1, keepdims=True) + EPS)
    o_ref[...] = (y * g_ref[...]).astype(o_ref.dtype)


def _rms_norm_rows(x, g, out_dtype, tm):
    n, d = x.shape
    return pl.pallas_call(
        _norm_kernel,
        out_shape=jax.ShapeDtypeStruct((n, d), out_dtype),
        grid=(n // tm,),
        in_specs=[pl.BlockSpec((tm, d), lambda i: (i, 0)),
                  pl.BlockSpec((1, d), lambda i: (0, 0))],
        out_specs=pl.BlockSpec((tm, d), lambda i: (i, 0)),
        compiler_params=_params(1),
        name="rms_norm_rows",
    )(x, g.reshape(1, d))


def _norm2_kernel(xp_ref, xs_ref, g_ref, o_ref, *, p_tiles):
    def emit(x_ref):
        x = x_ref[...]
        y = x * lax.rsqrt(jnp.mean(x * x, axis=-1, keepdims=True) + EPS)
        o_ref[...] = (y * g_ref[...]).astype(o_ref.dtype)

    @pl.when(pl.program_id(0) < p_tiles)
    def _():
        emit(xp_ref)

    @pl.when(pl.program_id(0) >= p_tiles)
    def _():
        emit(xs_ref)


def _split_specs(p_tiles, tm, d):
    return (pl.BlockSpec((tm, d), lambda i: (jnp.minimum(i, p_tiles - 1), 0)),
            pl.BlockSpec((tm, d), lambda i: (jnp.maximum(i - p_tiles, 0), 0)))


def _rms_norm_two(xp, xs, g, tm):
    (n_p, d), n_s = xp.shape, xs.shape[0]
    p_tiles = n_p // tm
    return pl.pallas_call(
        functools.partial(_norm2_kernel, p_tiles=p_tiles),
        out_shape=jax.ShapeDtypeStruct((n_p + n_s, d), BF16),
        grid=((n_p + n_s) // tm,),
        in_specs=[*_split_specs(p_tiles, tm, d), pl.BlockSpec((1, d), lambda i: (0, 0))],
        out_specs=pl.BlockSpec((tm, d), lambda i: (i, 0)),
        compiler_params=_params(1),
        name="rms_norm_mix",
    )(xp, xs, g.reshape(1, d))


def _segment_rms(acc, seg_ref, inv_width):
    ss = jnp.dot((acc * acc).astype(BF16), seg_ref[...], preferred_element_type=F32)
    return ss * inv_width


def _rope(y, rows, cos_ref, sa_ref, sb_ref):
    width = y.shape[1]
    reps = width // LANES
    c = jnp.concatenate([cos_ref[rows, :]] * reps, axis=1)
    sa = jnp.concatenate([sa_ref[rows, :]] * reps, axis=1)
    sb = jnp.concatenate([sb_ref[rows, :]] * reps, axis=1)
    half = ROPE_DIM // 2
    return y * c + pltpu.roll(y, width - half, 1) * sa + pltpu.roll(y, half, 1) * sb


def _ep_plain(acc, rows):
    return acc


def _ep_lowrank(acc, rows):
    lane = lax.broadcasted_iota(jnp.int32, acc.shape, 1)
    return jnp.where(lane < GLA_LOWRANK, acc, 0.0)


def _sigmoid(x):
    return 0.5 * jnp.tanh(0.5 * x) + 0.5


def _ep_sigmoid(acc, rows, b_ref):
    return _sigmoid(acc + b_ref[...])


def _ep_qknorm_rope(acc, rows, g_ref, seg_ref, cos_ref, sa_ref, sb_ref, *, keep_from):
    ms = _segment_rms(acc, seg_ref, 1.0 / SWA_HD)
    y = acc * lax.rsqrt(ms + EPS) * g_ref[...]
    y = _rope(y, rows, cos_ref, sa_ref, sb_ref)
    if keep_from is not None:
        col = lax.broadcasted_iota(jnp.int32, y.shape, 1)
        y = jnp.where(col < keep_from, y, acc)
    return y


def _ep_headnorm(acc, rows, g_ref, *, norm_tiles):
    y = acc * lax.rsqrt(jnp.mean(acc * acc, axis=-1, keepdims=True) + EPS) * g_ref[...]
    if norm_tiles is not None:
        y = jnp.where(pl.program_id(0) < norm_tiles, y, acc)
    return y


def _wmm_kernel(a_ref, w_ref, *rest, w_is_transposed, sub_rows, epilogue):
    extras, o_ref, wbf = rest[:-2], rest[-2], rest[-1]

    @pl.when(pl.program_id(1) == 0)
    def _():
        wbf[...] = w_ref[...].astype(BF16)

    for r0 in range(0, a_ref.shape[0], sub_rows):
        rows = slice(r0, r0 + sub_rows)
        if w_is_transposed:
            acc = lax.dot_general(a_ref[rows, :], wbf[...], (((1,), (1,)), ((), ())), preferred_element_type=F32)
        else:
            acc = jnp.dot(a_ref[rows, :], wbf[...], preferred_element_type=F32)
        o_ref[rows, :] = epilogue(acc, rows, *extras).astype(o_ref.dtype)


def _kv_lowrank_kernel(a_ref, wkv_ref, wlr_ref, g_ref, seg_ref, cos_ref, sa_ref, sb_ref,
                       kv_ref, lr_ref, wkv_bf, wlr_bf, *, sub_rows, keep_from):
    @pl.when(pl.program_id(0) == 0)
    def _():
        wkv_bf[...] = wkv_ref[...].astype(BF16)
        wlr_bf[...] = wlr_ref[...].astype(BF16)

    nt = (((1,), (1,)), ((), ()))
    for r0 in range(0, a_ref.shape[0], sub_rows):
        rows = slice(r0, r0 + sub_rows)
        a = a_ref[rows, :]
        acc = lax.dot_general(a, wkv_bf[...], nt, preferred_element_type=F32)
        kv_ref[rows, :] = _ep_qknorm_rope(acc, rows, g_ref, seg_ref, cos_ref, sa_ref, sb_ref, keep_from=keep_from)
        lr_ref[rows, :] = _ep_lowrank(lax.dot_general(a, wlr_bf[...], nt, preferred_element_type=F32), rows)


def _kv_lowrank(a, w_t, kv_col0, lr_col0, gain, seg, cos_t, sa_t, sb_t, tm, sub_rows):
    m, k = a.shape
    kv_cols = 2 * SWA_KV_HEADS * SWA_HD
    assert m % tm == 0 and tm % sub_rows == 0 and kv_col0 % 8 == 0 and lr_col0 % 8 == 0
    w_rows = lambda rows, row0: pl.BlockSpec((pl.Element(rows), pl.Element(k)), lambda i: (row0, 0))
    const = lambda shape: pl.BlockSpec(shape, lambda i: (0, 0))
    row = lambda width: pl.BlockSpec((tm, width), lambda i: (i, 0))
    kernel = functools.partial(_kv_lowrank_kernel, sub_rows=sub_rows, keep_from=kv_cols // 2)
    return pl.pallas_call(
        kernel,
        out_shape=(jax.ShapeDtypeStruct((m, kv_cols), F32), jax.ShapeDtypeStruct((m, LANES), F32)),
        grid=(m // tm,),
        in_specs=[row(k), w_rows(kv_cols, kv_col0), w_rows(LANES, lr_col0),
                  const((1, kv_cols)), const((kv_cols, kv_cols)), row(LANES), row(LANES), row(LANES)],
        out_specs=(row(kv_cols), row(LANES)),
        scratch_shapes=[pltpu.VMEM((kv_cols, k), BF16), pltpu.VMEM((LANES, k), BF16)],
        compiler_params=_params(1),
        name="proj_swa_kv_lowrank",
    )(a, w_t, w_t, gain, seg, cos_t, sa_t, sb_t)


def _wmatmul(epilogue, a, w, col0, n_cols, w_is_transposed, extras, extra_specs, out_dtype, tm, tn, name,
             sub_rows=None):
    m, k = a.shape
    sub_rows = tm if sub_rows is None else sub_rows
    assert m % tm == 0 and n_cols % tn == 0 and tm % sub_rows == 0
    if w_is_transposed:
        assert col0 % 8 == 0
        w_spec = pl.BlockSpec((pl.Element(tn), pl.Element(k)), lambda j, i: (pl.multiple_of(col0 + j * tn, 8), 0))
        w_tile = (tn, k)
    else:
        assert col0 % tn == 0
        w_spec = pl.BlockSpec((k, tn), lambda j, i: (0, col0 // tn + j))
        w_tile = (k, tn)
    kernel = functools.partial(_wmm_kernel, w_is_transposed=w_is_transposed, sub_rows=sub_rows, epilogue=epilogue)
    return pl.pallas_call(
        kernel,
        out_shape=jax.ShapeDtypeStruct((m, n_cols), out_dtype),
        grid=(n_cols // tn, m // tm),
        in_specs=[pl.BlockSpec((tm, k), lambda j, i: (i, 0)), w_spec] + list(extra_specs),
        out_specs=pl.BlockSpec((tm, tn), lambda j, i: (i, j)),
        scratch_shapes=[pltpu.VMEM(w_tile, BF16)],
        compiler_params=_params(2),
        name=name,
    )(a, w, *extras)


def _gla_kernel(q_ref, k_ref, v_ref, gg_ref, ga_ref, wa_ref, ba_ref, gn_ref, *rest, chunk, n_chunks, has_s0):
    s0_ref = rest[0] if has_s0 else None
    o_ref, sout_ref, s_scr, qt_scr, ku_scr, o_scr, dec_scr = rest[1:] if has_s0 else rest
    _gla_block(q_ref, k_ref, v_ref, gg_ref, ga_ref, wa_ref, ba_ref, gn_ref, s0_ref,
               o_ref, sout_ref, s_scr, qt_scr, ku_scr, o_scr, dec_scr, chunk=chunk, n_chunks=n_chunks)


def _gla_block(q_ref, k_ref, v_ref, gg_ref, ga_ref, wa_ref, ba_ref, gn_ref, s0_ref,
               o_ref, sout_ref, s_scr, qt_scr, ku_scr, o_scr, dec_scr, *, chunk, n_chunks):
    t = pl.program_id(1)
    tb = chunk * n_chunks
    heads = [(slice(h * GLA_DK, (h + 1) * GLA_DK), slice(h * GLA_DV, (h + 1) * GLA_DV)) for h in range(GLA_HEADS)]

    @pl.when(t == 0)
    def _():
        s_scr[...] = jnp.zeros_like(s_scr) if s0_ref is None else s0_ref[0]

    row = lax.broadcasted_iota(jnp.int32, (chunk, chunk), 0)
    col = lax.broadcasted_iota(jnp.int32, (chunk, chunk), 1)
    tril = (row >= col).astype(BF16)
    z = jnp.dot(ga_ref[...].astype(BF16), wa_ref[...], preferred_element_type=F32) + ba_ref[...]
    log_a = (jnp.minimum(z, 0.0) - jnp.log(1.0 + jnp.exp(-jnp.abs(z)))) * (1.0 / GLA_NORMALIZER)
    hi = log_a.astype(BF16)
    rest = log_a - hi.astype(F32)
    mid = rest.astype(BF16)
    lo = (rest - mid.astype(F32)).astype(BF16)
    b_parts, last_parts = [], []
    for ci in range(n_chunks):
        crows = slice(ci * chunk, (ci + 1) * chunk)
        b_c = (jnp.dot(tril, hi[crows], preferred_element_type=F32)
               + jnp.dot(tril, mid[crows], preferred_element_type=F32)
               + jnp.dot(tril, lo[crows], preferred_element_type=F32))
        b_last = b_c[chunk - 1:chunk, :]
        b_parts.append(b_c)
        last_parts.append(jnp.broadcast_to(b_last, b_c.shape))
        for h, (ks, _) in enumerate(heads):
            dec_scr[ci, h] = jnp.transpose(jnp.broadcast_to(jnp.exp(b_last[:, ks]), (LANES, GLA_DK)))
    b = jnp.concatenate(b_parts, axis=0)
    b_last_rows = jnp.concatenate(last_parts, axis=0)

    q = q_ref[...] * (GLA_DK ** -0.5)
    k = k_ref[...]
    q_t = (q * jnp.exp(b)).astype(BF16)
    k_t = (k * jnp.exp(-b)).astype(BF16)
    qt_scr[...] = q_t
    ku_scr[...] = (k * jnp.exp(b_last_rows - b)).astype(BF16)

    brow = lax.broadcasted_iota(jnp.int32, (tb, tb), 0)
    bcol = lax.broadcasted_iota(jnp.int32, (tb, tb), 1)
    mask = (bcol >= (brow & -chunk)) & (brow >= bcol)
    for ks, vs in heads:
        att = lax.dot_general(q_t[:, ks], k_t[:, ks], (((1,), (1,)), ((), ())), preferred_element_type=F32)
        att = jnp.where(mask, att, 0.0).astype(BF16)
        o_scr[:, vs] = jnp.dot(att, v_ref[:, vs].astype(BF16), preferred_element_type=F32)

    def one_chunk(ci, carry):
        rows = pl.ds(pl.multiple_of(ci * chunk, chunk), chunk)
        for h, (ks, vs) in enumerate(heads):
            s_old = s_scr[h]
            o_scr[rows, vs] += jnp.dot(qt_scr[rows, ks], s_old.astype(BF16), preferred_element_type=F32)
            decay = jnp.concatenate([dec_scr[ci, h]] * (GLA_DV // LANES), axis=1)
            s_scr[h] = decay * s_old + lax.dot_general(ku_scr[rows, ks], v_ref[rows, vs].astype(BF16),
                                                       (((0,), (0,)), ((), ())), preferred_element_type=F32)
        return carry

    lax.fori_loop(0, n_chunks, one_chunk, 0)

    for _, vs in heads:
        o = o_scr[:, vs]
        on = o * lax.rsqrt(jnp.mean(o * o, axis=-1, keepdims=True) + EPS) * gn_ref[...]
        gg = gg_ref[:, vs]
        o_ref[:, vs] = (on * (gg * _sigmoid(gg))).astype(o_ref.dtype)

    @pl.when(t == pl.num_programs(1) - 1)
    def _():
        sout_ref[0] = s_scr[...]


def _gla(qkvg, ga, wa, ba, gn, s0, batch, seq, row0, tb, chunk):
    assert seq % tb == 0 and tb % chunk == 0 and row0 % tb == 0
    nt = seq // tb
    base = row0 // tb
    qk_w = GLA_HEADS * GLA_DK
    v_w = GLA_HEADS * GLA_DV
    rows = lambda b, t: base + b * nt + t
    has_s0 = s0 is not None
    kernel = functools.partial(_gla_kernel, chunk=chunk, n_chunks=tb // chunk, has_s0=has_s0)
    state_spec = pl.BlockSpec((1, GLA_HEADS, GLA_DK, GLA_DV), lambda b, t: (b, 0, 0, 0))
    return pl.pallas_call(
        kernel,
        out_shape=(jax.ShapeDtypeStruct((batch * seq, v_w), BF16),
                   jax.ShapeDtypeStruct((batch, GLA_HEADS, GLA_DK, GLA_DV), F32)),
        grid=(batch, nt),
        in_specs=[pl.BlockSpec((tb, qk_w), lambda b, t: (rows(b, t), 0)),
                  pl.BlockSpec((tb, qk_w), lambda b, t: (rows(b, t), 1)),
                  pl.BlockSpec((tb, v_w), lambda b, t: (rows(b, t), 1)),
                  pl.BlockSpec((tb, v_w), lambda b, t: (rows(b, t), 2)),
                  pl.BlockSpec((tb, LANES), lambda b, t: (rows(b, t), 0)),
                  pl.BlockSpec((LANES, qk_w), lambda b, t: (0, 0)),
                  pl.BlockSpec((1, qk_w), lambda b, t: (0, 0)),
                  pl.BlockSpec((1, GLA_DV), lambda b, t: (0, 0))] + ([state_spec] if has_s0 else []),
        out_specs=(pl.BlockSpec((tb, v_w), lambda b, t: (b * nt + t, 0)),
                   pl.BlockSpec((1, GLA_HEADS, GLA_DK, GLA_DV), lambda b, t: (b, 0, 0, 0))),
        scratch_shapes=[pltpu.VMEM((GLA_HEADS, GLA_DK, GLA_DV), F32),
                        pltpu.VMEM((tb, qk_w), BF16),
                        pltpu.VMEM((tb, qk_w), BF16),
                        pltpu.VMEM((tb, v_w), F32),
                        pltpu.VMEM((tb // chunk, GLA_HEADS, GLA_DK, LANES), F32)],
        compiler_params=_params(2),
        name="gla_chunks",
    )(qkvg, qkvg, qkvg, qkvg, ga, wa, ba, gn, *([s0] if has_s0 else []))


def _swa_kernel(sink_ref, q_ref, kp_ref, vp_ref, ko_ref, vo_ref, o_ref, bias_scr, *, tq, prev_from_cache):
    i = pl.program_id(1)
    nk = WINDOW + tq
    k_all = jnp.concatenate([kp_ref[...], ko_ref[...]], axis=0)
    v_all = jnp.concatenate([vp_ref[...], vo_ref[...]], axis=0)
    qc = lax.broadcasted_iota(jnp.int32, (tq, nk), 0) // CHUNK + WINDOW // CHUNK
    kcol = lax.broadcasted_iota(jnp.int32, (tq, nk), 1)
    kc = kcol // CHUNK
    valid = (kc <= qc) & (kc >= qc - WINDOW // CHUNK)
    if not prev_from_cache:
        valid = valid & ((kcol >= WINDOW) | (i > 0))
    bias_scr[...] = jnp.where(valid, 0.0, NEG_BIG)
    lane = lax.broadcasted_iota(jnp.int32, (nk, LANES), 1)
    low = lane < SWA_HD
    low_q = lax.broadcasted_iota(jnp.int32, (tq, LANES), 1) < SWA_HD
    kms, vms = [], []
    for g in range(SWA_KV_HEADS):
        slab = slice((g // 2) * LANES, (g // 2 + 1) * LANES)
        k2 = k_all[:, slab]
        v2 = v_all[:, slab]
        k2r = pltpu.roll(k2, SWA_HD, 1)
        v2r = pltpu.roll(v2, SWA_HD, 1)
        if g % 2 == 0:
            k_lo, k_hi, v_lo, v_hi = k2, k2r, v2, v2r
        else:
            k_lo, k_hi, v_lo, v_hi = k2r, k2, v2r, v2
        zero = jnp.zeros_like(k2)
        one = jnp.ones_like(k2)
        kms.append((jnp.where(low, k_lo, zero).astype(BF16), jnp.where(low, zero, k_hi).astype(BF16)))
        vms.append((jnp.where(low, v_lo, one).astype(BF16), jnp.where(low, one, v_hi).astype(BF16)))

    for g0 in range(0, SWA_KV_HEADS, SWA_STAGE_KV_HEADS):
        heads = [(g, j, half) for g in range(g0, g0 + SWA_STAGE_KV_HEADS)
                 for j in range(SWA_GROUP // 2) for half in range(2)]
        sinks = [sink_ref[g * SWA_GROUP + 2 * j + half] for g, j, half in heads]
        scores = []
        for g, j, half in heads:
            qs = q_ref[:, (g * 4 + j) * LANES:(g * 4 + j + 1) * LANES]
            s = lax.dot_general(qs, kms[g][half], (((1,), (1,)), ((), ())), preferred_element_type=F32)
            scores.append(s + bias_scr[...])
        maxes = [jnp.maximum(jnp.max(s, axis=-1, keepdims=True), sk) for s, sk in zip(scores, sinks)]
        exps = [jnp.exp(s - m).astype(BF16) for s, m in zip(scores, maxes)]
        sink_terms = [jnp.exp(sk - m) for sk, m in zip(sinks, maxes)]
        for idx in range(0, len(heads), 2):
            g, j, _ = heads[idx]
            a_lo = jnp.dot(exps[idx], vms[g][0], preferred_element_type=F32)
            a_hi = jnp.dot(exps[idx + 1], vms[g][1], preferred_element_type=F32)
            num = jnp.where(low_q, a_lo, a_hi)
            den = pltpu.roll(jnp.where(low_q, a_hi, a_lo), SWA_HD, 1)
            den = den + jnp.where(low_q, sink_terms[idx], sink_terms[idx + 1])
            o_ref[:, (g * 4 + j) * LANES:(g * 4 + j + 1) * LANES] = (num / den).astype(o_ref.dtype)


def _swa(sinks, q, q_row0, k_prev, v_prev, prev_col, k_own, v_own, own_col, own_row0,
         batch, seq, tq, prev_from_cache):
    nt = seq // tq
    qb = q_row0 // tq
    ob = own_row0 // tq
    kv_w = SWA_KV_HEADS * SWA_HD
    if prev_from_cache:
        prev_map = lambda b, t, c: (b, c)
    else:
        per = seq // WINDOW
        prev_map = lambda b, t, c: (b * per + jnp.maximum(t * (tq // WINDOW) - 1, 0), c)
    kernel = functools.partial(_swa_kernel, tq=tq, prev_from_cache=prev_from_cache)
    return pl.pallas_call(
        kernel,
        out_shape=jax.ShapeDtypeStruct((batch * seq, SWA_HEADS * SWA_HD), BF16),
        grid=(batch, nt),
        in_specs=[pl.BlockSpec(memory_space=pltpu.SMEM),
                  pl.BlockSpec((tq, SWA_HEADS * SWA_HD), lambda b, t: (qb + b * nt + t, 0)),
                  pl.BlockSpec((WINDOW, kv_w), lambda b, t: prev_map(b, t, prev_col[0])),
                  pl.BlockSpec((WINDOW, kv_w), lambda b, t: prev_map(b, t, prev_col[1])),
                  pl.BlockSpec((tq, kv_w), lambda b, t: (ob + b * nt + t, own_col[0])),
                  pl.BlockSpec((tq, kv_w), lambda b, t: (ob + b * nt + t, own_col[1]))],
        out_specs=pl.BlockSpec((tq, SWA_HEADS * SWA_HD), lambda b, t: (b * nt + t, 0)),
        scratch_shapes=[pltpu.VMEM((tq, WINDOW + tq), F32)],
        compiler_params=_params(2),
        name="swa_band",
    )(sinks, q, k_prev, v_prev, k_own, v_own)


def _mem_attn_head(q, k, v):
    s = lax.dot_general(q, k.astype(BF16), (((1,), (1,)), ((), ())),
                        preferred_element_type=F32) * (MEM_HD ** -0.5)
    m = jnp.max(s, axis=-1, keepdims=True)
    e = jnp.exp(s - m)
    p = (e / jnp.sum(e, axis=-1, keepdims=True)).astype(BF16)
    return jnp.dot(p, v.astype(BF16), preferred_element_type=F32)


def _mem_attn_kernel(q_ref, kv_ref, o_ref, kv_bf):
    @pl.when(pl.program_id(1) == 0)
    def _():
        kv_bf[...] = kv_ref[...].astype(BF16)

    width = MEM_HEADS * MEM_HD
    cols = [slice(h * MEM_HD, (h + 1) * MEM_HD) for h in range(MEM_HEADS)]
    scores = [lax.dot_general(q_ref[:, c], kv_bf[:, c], (((1,), (1,)), ((), ())),
                              preferred_element_type=F32) * (MEM_HD ** -0.5) for c in cols]
    exps = [jnp.exp(s - jnp.max(s, axis=-1, keepdims=True)) for s in scores]
    probs = [(e / jnp.sum(e, axis=-1, keepdims=True)).astype(BF16) for e in exps]
    for c, p in zip(cols, probs):
        v = kv_bf[:, width + c.start:width + c.stop]
        o_ref[:, c] = jnp.dot(p, v, preferred_element_type=F32).astype(o_ref.dtype)


def _mem_attn_cache_kernel(q_ref, k_hbm, v_hbm, o_ref, kbuf, vbuf, sem):
    b = pl.program_id(0)
    slot = b % 2

    def copies(req, s):
        return ([pltpu.make_async_copy(k_hbm.at[0, req, :, h, :], kbuf.at[s, h], sem.at[s, 0])
                 for h in range(MEM_HEADS)]
                + [pltpu.make_async_copy(v_hbm.at[0, req, :, h, :], vbuf.at[s, h], sem.at[s, 1])
                   for h in range(MEM_HEADS)])

    @pl.when(b == 0)
    def _():
        for c in copies(0, 0):
            c.start()

    @pl.when(b + 1 < pl.num_programs(0))
    def _():
        for c in copies(b + 1, 1 - slot):
            c.start()

    for c in copies(b, slot):
        c.wait()
    for h in range(MEM_HEADS):
        cols = slice(h * MEM_HD, (h + 1) * MEM_HD)
        o_ref[:, cols] = _mem_attn_head(q_ref[:, cols], kbuf[slot, h], vbuf[slot, h]).astype(o_ref.dtype)


def _mem_attn_cache(q, q_row0, cache_k, cache_v, batch, seq):
    qb = q_row0 // seq
    width = MEM_HEADS * MEM_HD
    hbm = pl.BlockSpec(memory_space=pl.ANY)
    slabs = pltpu.VMEM((2, MEM_HEADS, N_MEM, MEM_HD), F32)
    return pl.pallas_call(
        _mem_attn_cache_kernel,
        out_shape=jax.ShapeDtypeStruct((batch * seq, width), BF16),
        grid=(batch,),
        in_specs=[pl.BlockSpec((seq, width), lambda b: (qb + b, 0)), hbm, hbm],
        out_specs=pl.BlockSpec((seq, width), lambda b: (b, 0)),
        scratch_shapes=[slabs, slabs, pltpu.SemaphoreType.DMA((2, 2))],
        compiler_params=_params(1),
        name="mem_attn_cache",
    )(q, cache_k, cache_v)


def _mem_attn(q, mem_kv, batch, seq, tq):
    nt = seq // tq
    width = MEM_HEADS * MEM_HD
    return pl.pallas_call(
        _mem_attn_kernel,
        out_shape=jax.ShapeDtypeStruct((batch * seq, width), BF16),
        grid=(batch, nt),
        in_specs=[pl.BlockSpec((tq, width), lambda b, t: (b * nt + t, 0)),
                  pl.BlockSpec((N_MEM, 2 * width), lambda b, t: (b, 0))],
        out_specs=pl.BlockSpec((tq, width), lambda b, t: (b * nt + t, 0)),
        scratch_shapes=[pltpu.VMEM((N_MEM, 2 * width), BF16)],
        compiler_params=_params(2),
        name="mem_attn",
    )(q, mem_kv)


def _merge_kernel(a0p, a0s, a1p, a1s, a2p, a2s, w_ref, g0_ref, g1_ref, g2_ref, o_ref, *, p_tiles):
    def emit(a0_ref, a1_ref, a2_ref):
        acc = g0_ref[...] * jnp.dot(a0_ref[...], w_ref[0], preferred_element_type=F32)
        acc = acc + g1_ref[...] * jnp.dot(a1_ref[...], w_ref[1], preferred_element_type=F32)
        acc = acc + g2_ref[...] * jnp.dot(a2_ref[...], w_ref[2], preferred_element_type=F32)
        o_ref[...] = acc.astype(o_ref.dtype)

    @pl.when(pl.program_id(1) < p_tiles)
    def _():
        emit(a0p, a1p, a2p)

    @pl.when(pl.program_id(1) >= p_tiles)
    def _():
        emit(a0s, a1s, a2s)


def _merge(branches_p, branches_s, w_branch, gates, tm, tn):
    n_p, d = branches_p[0].shape
    n = n_p + branches_s[0].shape[0]
    nj = d // tn
    p_tiles = n_p // tm
    ap = pl.BlockSpec((tm, d), lambda j, i: (jnp.minimum(i, p_tiles - 1), 0))
    asp = pl.BlockSpec((tm, d), lambda j, i: (jnp.maximum(i - p_tiles, 0), 0))
    operands = [a for pair in zip(branches_p, branches_s) for a in pair]
    return pl.pallas_call(
        functools.partial(_merge_kernel, p_tiles=p_tiles),
        out_shape=jax.ShapeDtypeStruct((n, d), BF16),
        grid=(nj, n // tm),
        in_specs=[ap, asp, ap, asp, ap, asp,
                  pl.BlockSpec((3, d, tn), lambda j, i: (0, 0, j), pipeline_mode=pl.Buffered(1)),
                  pl.BlockSpec((tm, tn), lambda j, i: (i, j)),
                  pl.BlockSpec((tm, tn), lambda j, i: (i, nj + j)),
                  pl.BlockSpec((tm, tn), lambda j, i: (i, 2 * nj + j))],
        out_specs=pl.BlockSpec((tm, tn), lambda j, i: (i, j)),
        compiler_params=_params(2),
        name="branch_merge",
    )(*operands, w_branch, gates, gates, gates)


def _outproj_router_kernel(m_ref, w_ref, xp_ref, xs_ref, g_ref, wr_ref, br_ref,
                           x2_ref, hf_ref, eid_ref, wt_ref, *, p_tiles):
    is_prompt = pl.program_id(0) < p_tiles
    tm, d = x2_ref.shape
    for r0 in range(0, tm, ROUTER_SUB_ROWS):
        rows = slice(r0, r0 + ROUTER_SUB_ROWS)
        acc = jnp.dot(m_ref[rows, :], w_ref[...], preferred_element_type=F32)
        x = jnp.where(is_prompt, xp_ref[rows, :], xs_ref[rows, :]) + acc
        x2_ref[rows, :] = x
        hf = x * lax.rsqrt(jnp.mean(x * x, axis=-1, keepdims=True) + EPS) * g_ref[...]
        hb = hf.astype(BF16)
        bits = pltpu.bitcast(hb.astype(F32), jnp.uint32)
        packed = bits[:, d // 2:] | (bits[:, :d // 2] >> 16)
        for s in range(TOKEN_TILE_ROWS):
            hf_ref[pl.ds(r0 * TOKEN_TILE_ROWS + s, ROUTER_SUB_ROWS, stride=TOKEN_TILE_ROWS), :] = (
                packed[:, s * LANES:(s + 1) * LANES])
        logits = jnp.dot(hb, wr_ref[...], preferred_element_type=F32) + br_ref[...]
        eid, wts = _route(logits)
        eid_ref[rows, :] = eid
        wt_ref[rows, :] = wts


def _route(logits):
    lane = lax.broadcasted_iota(jnp.int32, logits.shape, 1).astype(F32)
    big = 1e6
    is_g = lane < N_GROUPS
    lg = jnp.where(is_g, logits, NEG_BIG)
    mg = jnp.max(lg, axis=-1, keepdims=True)
    gsel = jnp.min(jnp.where(is_g & (lg == mg), lane, big), axis=-1, keepdims=True)
    g_w = 1.0 / jnp.sum(jnp.where(is_g, jnp.exp(lg - mg), 0.0), axis=-1, keepdims=True)
    e_lo = N_GROUPS + gsel * EXPERTS_PER_GROUP
    in_grp = (lane >= e_lo) & (lane < e_lo + EXPERTS_PER_GROUP)
    le = jnp.where(in_grp, logits, NEG_BIG)
    me = jnp.max(le, axis=-1, keepdims=True)
    ee = jnp.where(in_grp, jnp.exp(le - me), 0.0)
    pe = ee / jnp.sum(ee, axis=-1, keepdims=True)
    pe = jnp.where(in_grp, pe, -1.0)
    p1 = jnp.max(pe, axis=-1, keepdims=True)
    i1 = jnp.min(jnp.where(pe == p1, lane, big), axis=-1, keepdims=True)
    pe2 = jnp.where(lane == i1, -1.0, pe)
    p2 = jnp.max(pe2, axis=-1, keepdims=True)
    i2 = jnp.min(jnp.where(pe2 == p2, lane, big), axis=-1, keepdims=True)
    tot = p1 + p2
    w1 = g_w * p1 / tot
    w2 = g_w * p2 / tot
    eid = jnp.where(lane == 0.0, i1 - N_GROUPS, jnp.where(lane == 1.0, i2 - N_GROUPS, 0.0))
    return eid.astype(jnp.int32), jnp.where(lane == 0.0, w1, jnp.where(lane == 1.0, w2, 0.0))


def _outproj_router(merged, w_out, xp, xs, g, wr, br, tm):
    n, d = merged.shape
    p_tiles = xp.shape[0] // tm
    const = lambda shape: pl.BlockSpec(shape, lambda i: (0, 0), pipeline_mode=pl.Buffered(1))
    row = lambda width: pl.BlockSpec((tm, width), lambda i: (i, 0))
    return pl.pallas_call(
        functools.partial(_outproj_router_kernel, p_tiles=p_tiles),
        out_shape=(jax.ShapeDtypeStruct((n, d), F32),
                   jax.ShapeDtypeStruct((n * TOKEN_TILE_ROWS, LANES), jnp.uint32),
                   jax.ShapeDtypeStruct((n, LANES), jnp.int32),
                   jax.ShapeDtypeStruct((n, LANES), F32)),
        grid=(n // tm,),
        in_specs=[row(d), const((d, d)), *_split_specs(p_tiles, tm, d),
                  const((1, d)), const((d, LANES)), const((1, LANES))],
        out_specs=(row(d), pl.BlockSpec((tm * TOKEN_TILE_ROWS, LANES), lambda i: (i, 0)),
                   row(LANES), row(LANES)),
        compiler_params=_params(1),
        name="outproj_router",
    )(merged, w_out, xp, xs, g.reshape(1, d), wr, br)


def _moe_kernel(te_ref, nv_ref, nxt_ref, tok_ref, tok_next_ref, dst_ref, hf_hbm, wup_hbm, wdn_hbm, y_hbm,
                xg, yb, wup_f32, wdn_f32, wup_bf, wdn_bf, in_sem, out_sem, w_sem):
    t = pl.program_id(0)
    buf = t % 2
    rows_now = nv_ref[t]
    rows_next = nv_ref[t + 1]
    rows_prev = nv_ref[jnp.maximum(t - 1, 0)]
    valid = rows_now > 0
    valid_next = rows_next > 0
    new_expert = (t == 0) | (te_ref[t] != te_ref[jnp.maximum(t - 1, 0)])

    def weight_copies(e):
        return (pltpu.make_async_copy(wup_hbm.at[e], wup_f32, w_sem.at[0]),
                pltpu.make_async_copy(wdn_hbm.at[e], wdn_f32, w_sem.at[1]))

    in_rows, in_pitch = TOKEN_TILE_ROWS, TOKEN_TILE_PITCH
    out_rows, out_pitch = TOKEN_F32_ROWS, TOKEN_F32_PITCH

    def row_in(row0, r, b):
        return pltpu.make_async_copy(hf_hbm.at[pl.ds(pl.multiple_of(row0, in_rows), in_rows), :],
                                     xg.at[b, pl.ds(r * in_pitch, in_rows), :], in_sem.at[b])

    def row_out(row0, r, b):
        return pltpu.make_async_copy(yb.at[b, pl.ds(r * out_pitch, out_rows), :],
                                     y_hbm.at[pl.ds(pl.multiple_of(row0, out_rows), out_rows), :], out_sem.at[b])

    group = MOE_ROW_GROUP
    n_groups = MOE_TILE // group

    def group_in(b):
        return pltpu.make_async_copy(hf_hbm.at[pl.ds(0, group * in_rows), :],
                                     xg.at[b, pl.ds(0, group * in_rows), :], in_sem.at[b])

    def group_out(b):
        return pltpu.make_async_copy(yb.at[b, pl.ds(0, group * out_rows), :],
                                     y_hbm.at[pl.ds(0, group * out_rows), :], out_sem.at[b])

    def per_started_group(rows, fn):
        for g in range(n_groups):
            pl.when(rows > g * group)(functools.partial(fn, g))

    @pl.when(t == 0)
    def _():
        for c in weight_copies(te_ref[0]):
            c.start(priority=1)
        xg[...] = jnp.zeros_like(xg)

        def first(r, c):
            row_in(tok_ref[0, 0, r], r, 0).start()
            return c
        lax.fori_loop(0, ((rows_now + group - 1) // group) * group, first, 0)
        yb[...] = jnp.zeros_like(yb)
        n_real = y_hbm.shape[0] - 2 * MOE_TILE * out_rows
        for b in range(2):
            spare = pltpu.make_async_copy(
                yb.at[b, pl.ds(0, MOE_TILE * out_rows), :],
                y_hbm.at[pl.ds(n_real + b * MOE_TILE * out_rows, MOE_TILE * out_rows), :], out_sem.at[b])
            spare.start()
            spare.wait()

    @pl.when(valid & new_expert)
    def _():
        for c in weight_copies(te_ref[t]):
            c.wait()
        wup_bf[...] = wup_f32[...].astype(BF16)
        wdn_bf[...] = wdn_f32[...].astype(BF16)

        @pl.when(nxt_ref[t] >= 0)
        def _():
            for c in weight_copies(nxt_ref[t]):
                c.start(priority=1)

    def gather_next(g):
        for r in range(g * group, (g + 1) * group):
            row_in(tok_next_ref[0, 0, r], r, 1 - buf).start()

    def scatter_now(g):
        for r in range(g * group, (g + 1) * group):
            row_out(dst_ref[0, 0, r], r, buf).start(priority=r % 2)

    per_started_group(rows_next, gather_next)

    @pl.when(valid)
    def _():
        per_started_group(rows_now, lambda g: group_in(buf).wait())
        lo, hi = [], []
        for s in range(in_rows):
            word = xg[buf, pl.ds(s, MOE_TILE, stride=in_pitch), :]
            lo.append(pltpu.bitcast(word << 16, F32).astype(BF16))
            hi.append(pltpu.bitcast(word & jnp.uint32(0xFFFF0000), F32).astype(BF16))
        x = jnp.concatenate(lo + hi, axis=1)
        h1 = jnp.dot(x, wup_bf[...], preferred_element_type=F32)
        gate = h1[:, :D_FF]
        up = h1[:, D_FF:]
        act = (gate * _sigmoid(gate)) * up
        ye = jnp.dot(act.astype(BF16), wdn_bf[...], preferred_element_type=F32)
        for s in range(out_rows):
            yb[buf, pl.ds(s, MOE_TILE, stride=out_pitch), :] = ye[:, s * LANES:(s + 1) * LANES]
        per_started_group(rows_now, scatter_now)

        @pl.when(t > 0)
        def _():
            per_started_group(rows_prev, lambda g: group_out(1 - buf).wait())

        @pl.when(jnp.logical_not(valid_next))
        def _():
            per_started_group(rows_now, lambda g: group_out(buf).wait())


def _moe(hf, tile_expert, tile_rows, next_expert, tok_slots, dst_slots, w_up, w_down, out_tokens):
    n_tiles = tile_expert.shape[0]
    d = w_up.shape[1]
    slot_spec = lambda off: pl.BlockSpec((1, 1, MOE_TILE),
                                         lambda t, te, nv, ne: (jnp.minimum(t + off, n_tiles - 1), 0, 0),
                                         memory_space=pltpu.SMEM)
    hbm = pl.BlockSpec(memory_space=pl.ANY)
    grid_spec = pltpu.PrefetchScalarGridSpec(
        num_scalar_prefetch=3,
        grid=(n_tiles,),
        in_specs=[slot_spec(0), slot_spec(1), slot_spec(0), hbm, hbm, hbm],
        out_specs=hbm,
        scratch_shapes=[pltpu.VMEM((2, MOE_TILE * TOKEN_TILE_PITCH, LANES), jnp.uint32),
                        pltpu.VMEM((2, MOE_TILE * TOKEN_F32_PITCH, LANES), F32),
                        pltpu.VMEM((d, 2 * D_FF), F32),
                        pltpu.VMEM((D_FF, d), F32),
                        pltpu.VMEM((d, 2 * D_FF), BF16),
                        pltpu.VMEM((D_FF, d), BF16),
                        pltpu.SemaphoreType.DMA((2,)),
                        pltpu.SemaphoreType.DMA((2,)),
                        pltpu.SemaphoreType.DMA((2,))],
    )
    return pl.pallas_call(
        _moe_kernel,
        out_shape=jax.ShapeDtypeStruct((out_tokens * TOKEN_F32_ROWS, LANES), F32),
        grid_spec=grid_spec,
        compiler_params=_params(1),
        name="moe_experts",
    )(tile_expert, tile_rows, next_expert, tok_slots, tok_slots, dst_slots, hf, w_up, w_down)


def _combine_kernel(x_ref, y0_ref, y1_ref, w_ref, op_ref, os_ref, *, p_tiles):
    tm = x_ref.shape[0]
    w = w_ref[...]
    w0 = w[:, 0:1]
    w1 = w[:, 1:2]

    def emit(o_ref):
        for s in range(TOKEN_F32_ROWS):
            cols = slice(s * LANES, (s + 1) * LANES)
            rows = pl.ds(s, tm, stride=TOKEN_F32_ROWS)
            o_ref[:, cols] = x_ref[:, cols] + (y0_ref[rows, :] * w0 + y1_ref[rows, :] * w1)

    @pl.when(pl.program_id(0) < p_tiles)
    def _():
        emit(op_ref)

    @pl.when(pl.program_id(0) >= p_tiles)
    def _():
        emit(os_ref)


def _combine(x2, yk, wts, n_p, tm):
    n, d = x2.shape
    p_tiles = n_p // tm
    k1 = n // tm
    y_rows = tm * TOKEN_F32_ROWS
    return pl.pallas_call(
        functools.partial(_combine_kernel, p_tiles=p_tiles),
        out_shape=(jax.ShapeDtypeStruct((n_p, d), F32), jax.ShapeDtypeStruct((n - n_p, d), F32)),
        grid=(n // tm,),
        in_specs=[pl.BlockSpec((tm, d), lambda i: (i, 0)),
                  pl.BlockSpec((y_rows, LANES), lambda i: (i, 0)),
                  pl.BlockSpec((y_rows, LANES), lambda i: (k1 + i, 0)),
                  pl.BlockSpec((tm, LANES), lambda i: (i, 0))],
        out_specs=_split_specs(p_tiles, tm, d),
        compiler_params=_params(1),
        name="moe_combine",
    )(x2, yk, yk, wts)


def _rope_tables(pos):
    half = ROPE_DIM // 2
    inv = ROPE_THETA ** (-jnp.arange(half, dtype=F32) / half)
    ang = pos.astype(F32)[:, None] * inv[None, :]
    cos, sin = jnp.cos(ang), jnp.sin(ang)
    n = pos.shape[0]
    pad = jnp.zeros((n, SWA_HD - ROPE_DIM), F32)
    cos_h = jnp.concatenate([cos, cos, pad + 1.0], axis=1)
    sa_h = jnp.concatenate([-sin, jnp.zeros_like(sin), pad], axis=1)
    sb_h = jnp.concatenate([jnp.zeros_like(sin), sin, pad], axis=1)
    reps = LANES // SWA_HD
    return tuple(jnp.tile(a, (1, reps)) for a in (cos_h, sa_h, sb_h))


def _moe_schedule(eid, n_tok, n_tiles):
    a = eid.shape[0]
    order = jnp.argsort(eid, stable=True).astype(jnp.int32)
    counts = jnp.bincount(eid, length=N_EXPERTS).astype(jnp.int32)
    tiles_per = (counts + MOE_TILE - 1) // MOE_TILE
    tile_end = jnp.cumsum(tiles_per)
    tile_start = tile_end - tiles_per
    sorted_start = jnp.cumsum(counts) - counts
    tile_id = jnp.arange(n_tiles, dtype=jnp.int32)
    used = tile_id < tile_end[-1]
    te = jnp.minimum(jnp.sum(tile_end[None, :] <= tile_id[:, None], axis=1), N_EXPERTS - 1).astype(jnp.int32)
    last_used_e = te[jnp.maximum(tile_end[-1] - 1, 0)]
    te = jnp.where(used, te, last_used_e)
    row_in_expert = (tile_id - tile_start[te]) * MOE_TILE
    rows_valid = jnp.where(used, jnp.clip(counts[te] - row_in_expert, 0, MOE_TILE), 0).astype(jnp.int32)
    r = jnp.arange(MOE_TILE, dtype=jnp.int32)[None, :]
    src = sorted_start[te][:, None] + row_in_expert[:, None] + r
    real = r < rows_valid[:, None]
    assign = order[jnp.clip(src, 0, a - 1)]
    tok = jnp.where(real, assign % n_tok, 0) * TOKEN_TILE_ROWS
    spare = a + (tile_id[:, None] % 2) * MOE_TILE + r
    dst = jnp.where(real, assign, spare) * TOKEN_F32_ROWS
    tile_rows = jnp.concatenate([rows_valid, jnp.zeros((1,), jnp.int32)])
    e_id = jnp.arange(N_EXPERTS, dtype=jnp.int32)[None, :]
    later = (e_id > te[:, None]) & (counts[None, :] > 0)
    nxt = jnp.min(jnp.where(later, e_id, N_EXPERTS), axis=1)
    nxt = jnp.where(nxt < N_EXPERTS, nxt, -1).astype(jnp.int32)
    shape = (n_tiles, 1, MOE_TILE)
    return te, tile_rows, nxt, tok.reshape(shape).astype(jnp.int32), dst.reshape(shape).astype(jnp.int32)


def kernel(x_prompt, x_sample, state_gla, cache_swa_k, cache_swa_v, cache_mem_k, cache_mem_v,
           mem_prompt, norm_mix_g, w_in, w_a2, b_a2, gla_norm_g, swa_q_norm_g, swa_k_norm_g,
           swa_sinks, norm_mem_g, w_mem_kv, mem_q_norm_g, mem_k_norm_g, w_gate, b_gate,
           w_branch, w_out, norm_ffn_g, w_router_group, b_router_group, w_router_expert,
           b_router_expert, w_up, w_down):
    bp, tp, d = x_prompt.shape
    bs, ts, _ = x_sample.shape
    n_p, n_s = bp * tp, bs * ts
    n = n_p + n_s
    tm, tn = ROW_TILE, COL_TILE
    tp_rows = PROJ_ROWS if n % PROJ_ROWS == 0 else tm
    heavy_sub = tp_rows // 2
    norm_sub = tp_rows // 4
    assert d == D_MODEL and n_p % tm == 0 and n_s % tm == 0 and w_in.shape[0] == 1
    keep_s = cache_swa_k.shape[2]
    assert keep_s == WINDOW and tp % WINDOW == 0

    qk_w = GLA_HEADS * GLA_DK
    v_w = GLA_HEADS * GLA_DV
    c0 = 2 * qk_w + 2 * v_w
    sq_w = SWA_HEADS * SWA_HD
    kv_w = SWA_KV_HEADS * SWA_HD
    mem_w = MEM_HEADS * MEM_HD
    w_in_t = jnp.transpose(w_in[0])
    w_branch_b = w_branch[0].astype(BF16)
    w_out_b = w_out[0].astype(BF16)
    w_a2_b = jnp.pad(w_a2[0], ((0, LANES - GLA_LOWRANK), (0, 0))).astype(BF16)
    w_router = jnp.pad(jnp.concatenate([w_router_group[0], w_router_expert[0]], axis=1),
                       ((0, 0), (0, LANES - N_GROUPS - N_EXPERTS))).astype(BF16)
    b_router = jnp.pad(jnp.concatenate([b_router_group[0], b_router_expert[0]]),
                       (0, LANES - N_GROUPS - N_EXPERTS)).reshape(1, LANES)

    pos = jnp.concatenate([jnp.tile(jnp.arange(tp, dtype=jnp.int32), bp),
                           jnp.tile(PAST_LEN + jnp.arange(ts, dtype=jnp.int32), bs)])
    cos_t, sa_t, sb_t = _rope_tables(pos)
    seg_id = jnp.arange(tn, dtype=jnp.int32) // SWA_HD
    seg = (seg_id[:, None] == seg_id[None, :]).astype(BF16)
    rope_specs = [pl.BlockSpec((tp_rows, LANES), lambda j, i: (i, 0))] * 3
    row_vec = lambda width: pl.BlockSpec((1, width), lambda j, i: (0, 0))
    seg_spec = pl.BlockSpec((tn, tn), lambda j, i: (0, 0))

    xp2 = x_prompt.reshape(n_p, d)
    xs2 = x_sample.reshape(n_s, d)
    h = _rms_norm_two(xp2, xs2, norm_mix_g[0], tm)

    c1 = c0 + GLA_LOWRANK
    c2 = c1 + sq_w
    c3 = c2 + 2 * kv_w
    qkvg = _wmatmul(_ep_plain, h, w_in_t, 0, c0, True, [], [], F32, tp_rows, WIDE_COLS, "proj_gla",
                    sub_rows=heavy_sub)
    q_gain = jnp.tile(swa_q_norm_g[0] * (SWA_HD ** -0.5), tn // SWA_HD).reshape(1, tn)
    q_swa = _wmatmul(functools.partial(_ep_qknorm_rope, keep_from=None), h, w_in_t, c1, sq_w, True,
                     [q_gain, seg, cos_t, sa_t, sb_t], [row_vec(tn), seg_spec] + rope_specs,
                     BF16, tp_rows, tn, "proj_swa_q", sub_rows=norm_sub)
    k_gain = jnp.tile(swa_k_norm_g[0], tn // SWA_HD).reshape(1, tn)
    kv_swa, ga = _kv_lowrank(h, w_in_t, c2, c0, k_gain, seg, cos_t, sa_t, sb_t, tp_rows, norm_sub)
    q_mem = _wmatmul(functools.partial(_ep_headnorm, norm_tiles=None), h, w_in_t, c3, mem_w, True,
                     [mem_q_norm_g[0].reshape(1, MEM_HD)], [row_vec(MEM_HD)],
                     BF16, tp_rows, MEM_HD, "proj_mem_q", sub_rows=norm_sub)
    gates = _wmatmul(_ep_sigmoid, h, w_gate[0], 0, 3 * d, False, [b_gate[0].reshape(1, -1)],
                     [pl.BlockSpec((1, WIDE_COLS), lambda j, i: (0, j))], F32, tp_rows, WIDE_COLS, "proj_gates",
                     sub_rows=heavy_sub)

    mem_rows = bp * N_MEM
    hm = _rms_norm_rows(mem_prompt.reshape(mem_rows, d), norm_mem_g[0], BF16, N_MEM)
    mem_kv = _wmatmul(functools.partial(_ep_headnorm, norm_tiles=MEM_HEADS), hm, w_mem_kv[0], 0, 2 * mem_w, False,
                      [mem_k_norm_g[0].reshape(1, MEM_HD)], [row_vec(MEM_HD)], F32, mem_rows, MEM_HD, "mem_kv")

    ba = b_a2[0].reshape(1, qk_w)
    gn = gla_norm_g[0].reshape(1, GLA_DV)
    o_gla_p, gla_state_p = _gla(qkvg, ga, w_a2_b, ba, gn, None, bp, tp, 0, GLA_BLOCK_ROWS, CHUNK)
    o_gla_s, gla_state_s = _gla(qkvg, ga, w_a2_b, ba, gn, state_gla[0], bs, ts, n_p, ts, min(CHUNK, ts))

    sinks = swa_sinks[0]
    o_swa_p = _swa(sinks, q_swa, 0, kv_swa, kv_swa, (0, 1), kv_swa, kv_swa, (0, 1), 0,
                   bp, tp, WINDOW, False)
    ck = cache_swa_k[0].reshape(bs * keep_s, kv_w)
    cv = cache_swa_v[0].reshape(bs * keep_s, kv_w)
    o_swa_s = _swa(sinks, q_swa, n_p, ck, cv, (0, 0), kv_swa, kv_swa, (0, 1), n_p,
                   bs, ts, ts, True)

    o_mem_p = _mem_attn(q_mem, mem_kv, bp, tp, min(tp, MEM_ATTN_ROWS))
    o_mem_s = _mem_attn_cache(q_mem, n_p, cache_mem_k, cache_mem_v, bs, ts)

    merged = _merge((o_gla_p, o_swa_p, o_mem_p), (o_gla_s, o_swa_s, o_mem_s), w_branch_b, gates, tm, WIDE_COLS)

    x2, hf, eid, wts = _outproj_router(merged, w_out_b, xp2, xs2, norm_ffn_g[0], w_router, b_router, tm)

    n_assign = TOP_K * n
    n_tiles = n_assign // MOE_TILE + N_EXPERTS
    eid_kmajor = jnp.concatenate([eid[:, k] for k in range(TOP_K)])
    tile_expert, tile_rows, next_expert, tok_slots, dst_slots = _moe_schedule(eid_kmajor, n, n_tiles)
    yk = _moe(hf, tile_expert, tile_rows, next_expert, tok_slots, dst_slots, w_up[0], w_down[0],
              n_assign + 2 * MOE_TILE)
    y_p, y_s = _combine(x2, yk, wts, n_p, tm)

    y_p = y_p.reshape(bp, tp, d)
    y_s = y_s.reshape(bs, ts, d)
    kv_p = jnp.stack([kv_swa[(b + 1) * tp - WINDOW:(b + 1) * tp] for b in range(bp)])
    kv_p = kv_p.reshape(bp, WINDOW, 2, SWA_KV_HEADS, SWA_HD)
    kv_s = kv_swa[n_p:].reshape(bs, ts, 2, SWA_KV_HEADS, SWA_HD)
    swk_s = jnp.concatenate([cache_swa_k[0], kv_s[:, :, 0]], axis=1)[:, ts:ts + keep_s]
    swv_s = jnp.concatenate([cache_swa_v[0], kv_s[:, :, 1]], axis=1)[:, ts:ts + keep_s]
    mk_p = mem_kv[:, :mem_w].reshape(bp, N_MEM, MEM_HEADS, MEM_HD)
    mv_p = mem_kv[:, mem_w:].reshape(bp, N_MEM, MEM_HEADS, MEM_HD)
    return (y_p, y_s, gla_state_p[None], kv_p[:, :, 0][None], kv_p[:, :, 1][None], mk_p[None], mv_p[None],
            gla_state_s[None], swk_s[None], swv_s[None])
```

```python
import functools

import jax
import jax.numpy as jnp
from jax import lax
from jax.experimental import pallas as pl
from jax.experimental.pallas import tpu as pltpu

F32 = jnp.float32
BF16 = jnp.bfloat16

D_MODEL = 2048
CHUNK = 64
EPS = 1e-6
PAST_LEN = 1024
GLA_HEADS = 4
GLA_DV = 512
GLA_DK = 256
GLA_LOWRANK = 16
GLA_NORMALIZER = 16.0
SWA_HD = 64
SWA_HEADS = 32
SWA_KV_HEADS = 4
SWA_GROUP = 8
WINDOW = 128
ROPE_DIM = 16
ROPE_THETA = 500000.0
N_MEM = 256
MEM_HEADS = 4
MEM_HD = 512
N_GROUPS = 8
EXPERTS_PER_GROUP = 8
N_EXPERTS = 64
TOP_K = 2
D_FF = 512

LANES = 128
VMEM_LIMIT = 56 * 1024 * 1024
ROW_TILE = 512
PROJ_ROWS = 1536
COL_TILE = 512
WIDE_COLS = 1024
MEM_ATTN_ROWS = 1024
SWA_STAGE_KV_HEADS = 4
ROUTER_SUB_ROWS = 256
MOE_TILE = 256
MOE_ROW_GROUP = 32
TOKEN_TILE_ROWS = D_MODEL // 2 // LANES
TOKEN_TILE_PITCH = 12
TOKEN_F32_ROWS = D_MODEL // LANES
TOKEN_F32_PITCH = 20
NEG_BIG = -1e30


def _params(n_axes):
    return pltpu.CompilerParams(dimension_semantics=("arbitrary",) * n_axes,
                                vmem_limit_bytes=VMEM_LIMIT)


def _norm_kernel(x_ref, g_ref, o_ref):
    x = x_ref[...]
    y = x * lax.rsqrt(jnp.mean(x * x, axis=-1, keepdims=True) + EPS)
    o_ref[...] = (y * g_ref[...]).astype(o_ref.dtype)


def _rms_norm_rows(x, g, out_dtype, tm):
    n, d = x.shape
    return pl.pallas_call(
        _norm_kernel,
        out_shape=jax.ShapeDtypeStruct((n, d), out_dtype),
        grid=(n // tm,),
        in_specs=[pl.BlockSpec((tm, d), lambda i: (i, 0)),
                  pl.BlockSpec((1, d), lambda i: (0, 0))],
        out_specs=pl.BlockSpec((tm, d), lambda i: (i, 0)),
        compiler_params=_params(1),
        name="rms_norm_rows",
    )(x, g.reshape(1, d))


def _norm2_kernel(xp_ref, xs_ref, g_ref, o_ref, *, p_tiles):
    def emit(x_ref):
        x = x_ref[...]
        y = x * lax.rsqrt(jnp.mean(x * x, axis=-1, keepdims=True) + EPS)
        o_ref[...] = (y * g_ref[...]).astype(o_ref.dtype)

    @pl.when(pl.program_id(0) < p_tiles)
    def _():
        emit(xp_ref)

    @pl.when(pl.program_id(0) >= p_tiles)
    def _():
        emit(xs_ref)


def _split_specs(p_tiles, tm, d):
    return (pl.BlockSpec((tm, d), lambda i: (jnp.minimum(i, p_tiles - 1), 0)),
            pl.BlockSpec((tm, d), lambda i: (jnp.maximum(i - p_tiles, 0), 0)))


def _rms_norm_two(xp, xs, g, tm):
    (n_p, d), n_s = xp.shape, xs.shape[0]
    p_tiles = n_p // tm
    return pl.pallas_call(
        functools.partial(_norm2_kernel, p_tiles=p_tiles),
        out_shape=jax.ShapeDtypeStruct((n_p + n_s, d), BF16),
        grid=((n_p + n_s) // tm,),
        in_specs=[*_split_specs(p_tiles, tm, d), pl.BlockSpec((1, d), lambda i: (0, 0))],
        out_specs=pl.BlockSpec((tm, d), lambda i: (i, 0)),
        compiler_params=_params(1),
        name="rms_norm_mix",
    )(xp, xs, g.reshape(1, d))


def _segment_rms(acc, seg_ref, inv_width):
    ss = jnp.dot((acc * acc).astype(BF16), seg_ref[...], preferred_element_type=F32)
    return ss * inv_width


def _rope(y, rows, cos_ref, sa_ref, sb_ref):
    width = y.shape[1]
    reps = width // LANES
    c = jnp.concatenate([cos_ref[rows, :]] * reps, axis=1)
    sa = jnp.concatenate([sa_ref[rows, :]] * reps, axis=1)
    sb = jnp.concatenate([sb_ref[rows, :]] * reps, axis=1)
    half = ROPE_DIM // 2
    return y * c + pltpu.roll(y, width - half, 1) * sa + pltpu.roll(y, half, 1) * sb


def _ep_plain(acc, rows):
    return acc


def _ep_lowrank(acc, rows):
    lane = lax.broadcasted_iota(jnp.int32, acc.shape, 1)
    return jnp.where(lane < GLA_LOWRANK, acc, 0.0)


def _sigmoid(x):
    return 0.5 * jnp.tanh(0.5 * x) + 0.5


def _ep_sigmoid(acc, rows, b_ref):
    return _sigmoid(acc + b_ref[...])


def _ep_qknorm_rope(acc, rows, g_ref, seg_ref, cos_ref, sa_ref, sb_ref, *, keep_from):
    ms = _segment_rms(acc, seg_ref, 1.0 / SWA_HD)
    y = acc * lax.rsqrt(ms + EPS) * g_ref[...]
    y = _rope(y, rows, cos_ref, sa_ref, sb_ref)
    if keep_from is not None:
        col = lax.broadcasted_iota(jnp.int32, y.shape, 1)
        y = jnp.where(col < keep_from, y, acc)
    return y


def _ep_headnorm(acc, rows, g_ref, *, norm_tiles):
    y = acc * lax.rsqrt(jnp.mean(acc * acc, axis=-1, keepdims=True) + EPS) * g_ref[...]
    if norm_tiles is not None:
        y = jnp.where(pl.program_id(0) < norm_tiles, y, acc)
    return y


def _wmm_kernel(a_ref, w_ref, *rest, w_is_transposed, sub_rows, epilogue):
    extras, o_ref, wbf = rest[:-2], rest[-2], rest[-1]

    @pl.when(pl.program_id(1) == 0)
    def _():
        wbf[...] = w_ref[...].astype(BF16)

    for r0 in range(0, a_ref.shape[0], sub_rows):
        rows = slice(r0, r0 + sub_rows)
        if w_is_transposed:
            acc = lax.dot_general(a_ref[rows, :], wbf[...], (((1,), (1,)), ((), ())), preferred_element_type=F32)
        else:
            acc = jnp.dot(a_ref[rows, :], wbf[...], preferred_element_type=F32)
        o_ref[rows, :] = epilogue(acc, rows, *extras).astype(o_ref.dtype)


def _wmatmul(epilogue, a, w, col0, n_cols, w_is_transposed, extras, extra_specs, out_dtype, tm, tn, name,
             sub_rows=None):
    m, k = a.shape
    sub_rows = tm if sub_rows is None else sub_rows
    assert m % tm == 0 and n_cols % tn == 0 and tm % sub_rows == 0
    if w_is_transposed:
        assert col0 % 8 == 0
        w_spec = pl.BlockSpec((pl.Element(tn), pl.Element(k)), lambda j, i: (pl.multiple_of(col0 + j * tn, 8), 0))
        w_tile = (tn, k)
    else:
        assert col0 % tn == 0
        w_spec = pl.BlockSpec((k, tn), lambda j, i: (0, col0 // tn + j))
        w_tile = (k, tn)
    kernel = functools.partial(_wmm_kernel, w_is_transposed=w_is_transposed, sub_rows=sub_rows, epilogue=epilogue)
    return pl.pallas_call(
        kernel,
        out_shape=jax.ShapeDtypeStruct((m, n_cols), out_dtype),
        grid=(n_cols // tn, m // tm),
        in_specs=[pl.BlockSpec((tm, k), lambda j, i: (i, 0)), w_spec] + list(extra_specs),
        out_specs=pl.BlockSpec((tm, tn), lambda j, i: (i, j)),
        scratch_shapes=[pltpu.VMEM(w_tile, BF16)],
        compiler_params=_params(2),
        name=name,
    )(a, w, *extras)


def _gla_kernel(q_ref, k_ref, v_ref, gg_ref, ga_ref, wa_ref, ba_ref, gn_ref, *rest, chunk, n_chunks, has_s0):
    s0_ref = rest[0] if has_s0 else None
    o_ref, sout_ref, s_scr, qt_scr, ku_scr, o_scr, dec_scr = rest[1:] if has_s0 else rest
    _gla_block(q_ref, k_ref, v_ref, gg_ref, ga_ref, wa_ref, ba_ref, gn_ref, s0_ref,
               o_ref, sout_ref, s_scr, qt_scr, ku_scr, o_scr, dec_scr, chunk=chunk, n_chunks=n_chunks)


def _gla_block(q_ref, k_ref, v_ref, gg_ref, ga_ref, wa_ref, ba_ref, gn_ref, s0_ref,
               o_ref, sout_ref, s_scr, qt_scr, ku_scr, o_scr, dec_scr, *, chunk, n_chunks):
    t = pl.program_id(1)
    tb = chunk * n_chunks
    heads = [(slice(h * GLA_DK, (h + 1) * GLA_DK), slice(h * GLA_DV, (h + 1) * GLA_DV)) for h in range(GLA_HEADS)]

    @pl.when(t == 0)
    def _():
        s_scr[...] = jnp.zeros_like(s_scr) if s0_ref is None else s0_ref[0]

    row = lax.broadcasted_iota(jnp.int32, (chunk, chunk), 0)
    col = lax.broadcasted_iota(jnp.int32, (chunk, chunk), 1)
    tril = (row >= col).astype(BF16)
    z = jnp.dot(ga_ref[...].astype(BF16), wa_ref[...], preferred_element_type=F32) + ba_ref[...]
    log_a = (jnp.minimum(z, 0.0) - jnp.log(1.0 + jnp.exp(-jnp.abs(z)))) * (1.0 / GLA_NORMALIZER)
    hi = log_a.astype(BF16)
    rest = log_a - hi.astype(F32)
    mid = rest.astype(BF16)
    lo = (rest - mid.astype(F32)).astype(BF16)
    b_parts, last_parts = [], []
    for ci in range(n_chunks):
        crows = slice(ci * chunk, (ci + 1) * chunk)
        b_c = (jnp.dot(tril, hi[crows], preferred_element_type=F32)
               + jnp.dot(tril, mid[crows], preferred_element_type=F32)
               + jnp.dot(tril, lo[crows], preferred_element_type=F32))
        b_last = b_c[chunk - 1:chunk, :]
        b_parts.append(b_c)
        last_parts.append(jnp.broadcast_to(b_last, b_c.shape))
        for h, (ks, _) in enumerate(heads):
            dec_scr[ci, h] = jnp.transpose(jnp.broadcast_to(jnp.exp(b_last[:, ks]), (LANES, GLA_DK)))
    b = jnp.concatenate(b_parts, axis=0)
    b_last_rows = jnp.concatenate(last_parts, axis=0)

    q = q_ref[...] * (GLA_DK ** -0.5)
    k = k_ref[...]
    q_t = (q * jnp.exp(b)).astype(BF16)
    k_t = (k * jnp.exp(-b)).astype(BF16)
    qt_scr[...] = q_t
    ku_scr[...] = (k * jnp.exp(b_last_rows - b)).astype(BF16)

    brow = lax.broadcasted_iota(jnp.int32, (tb, tb), 0)
    bcol = lax.broadcasted_iota(jnp.int32, (tb, tb), 1)
    mask = (bcol >= (brow & -chunk)) & (brow >= bcol)
    for ks, vs in heads:
        att = lax.dot_general(q_t[:, ks], k_t[:, ks], (((1,), (1,)), ((), ())), preferred_element_type=F32)
        att = jnp.where(mask, att, 0.0).astype(BF16)
        o_scr[:, vs] = jnp.dot(att, v_ref[:, vs].astype(BF16), preferred_element_type=F32)

    def one_chunk(ci, carry):
        rows = pl.ds(pl.multiple_of(ci * chunk, chunk), chunk)
        for h, (ks, vs) in enumerate(heads):
            s_old = s_scr[h]
            o_scr[rows, vs] += jnp.dot(qt_scr[rows, ks], s_old.astype(BF16), preferred_element_type=F32)
            decay = jnp.concatenate([dec_scr[ci, h]] * (GLA_DV // LANES), axis=1)
            s_scr[h] = decay * s_old + lax.dot_general(ku_scr[rows, ks], v_ref[rows, vs].astype(BF16),
                                                       (((0,), (0,)), ((), ())), preferred_element_type=F32)
        return carry

    lax.fori_loop(0, n_chunks, one_chunk, 0)

    for _, vs in heads:
        o = o_scr[:, vs]
        on = o * lax.rsqrt(jnp.mean(o * o, axis=-1, keepdims=True) + EPS) * gn_ref[...]
        gg = gg_ref[:, vs]
        o_ref[:, vs] = (on * (gg * _sigmoid(gg))).astype(o_ref.dtype)

    @pl.when(t == pl.num_programs(1) - 1)
    def _():
        sout_ref[0] = s_scr[...]


def _gla(qkvg, ga, wa, ba, gn, s0, batch, seq, row0, tb, chunk):
    nt = seq // tb
    base = row0 // tb
    qk_w = GLA_HEADS * GLA_DK
    v_w = GLA_HEADS * GLA_DV
    rows = lambda b, t: base + b * nt + t
    has_s0 = s0 is not None
    kernel = functools.partial(_gla_kernel, chunk=chunk, n_chunks=tb // chunk, has_s0=has_s0)
    state_spec = pl.BlockSpec((1, GLA_HEADS, GLA_DK, GLA_DV), lambda b, t: (b, 0, 0, 0))
    return pl.pallas_call(
        kernel,
        out_shape=(jax.ShapeDtypeStruct((batch * seq, v_w), BF16),
                   jax.ShapeDtypeStruct((batch, GLA_HEADS, GLA_DK, GLA_DV), F32)),
        grid=(batch, nt),
        in_specs=[pl.BlockSpec((tb, qk_w), lambda b, t: (rows(b, t), 0)),
                  pl.BlockSpec((tb, qk_w), lambda b, t: (rows(b, t), 1)),
                  pl.BlockSpec((tb, v_w), lambda b, t: (rows(b, t), 1)),
                  pl.BlockSpec((tb, v_w), lambda b, t: (rows(b, t), 2)),
                  pl.BlockSpec((tb, LANES), lambda b, t: (rows(b, t), 0)),
                  pl.BlockSpec((LANES, qk_w), lambda b, t: (0, 0)),
                  pl.BlockSpec((1, qk_w), lambda b, t: (0, 0)),
                  pl.BlockSpec((1, GLA_DV), lambda b, t: (0, 0))] + ([state_spec] if has_s0 else []),
        out_specs=(pl.BlockSpec((tb, v_w), lambda b, t: (b * nt + t, 0)),
                   pl.BlockSpec((1, GLA_HEADS, GLA_DK, GLA_DV), lambda b, t: (b, 0, 0, 0))),
        scratch_shapes=[pltpu.VMEM((GLA_HEADS, GLA_DK, GLA_DV), F32),
                        pltpu.VMEM((tb, qk_w), BF16),
                        pltpu.VMEM((tb, qk_w), BF16),
                        pltpu.VMEM((tb, v_w), F32),
                        pltpu.VMEM((tb // chunk, GLA_HEADS, GLA_DK, LANES), F32)],
        compiler_params=_params(2),
        name="gla_chunks",
    )(qkvg, qkvg, qkvg, qkvg, ga, wa, ba, gn, *([s0] if has_s0 else []))


def _swa_kernel(sink_ref, q_ref, kp_ref, vp_ref, ko_ref, vo_ref, o_ref, bias_scr, *, tq, prev_from_cache):
    i = pl.program_id(1)
    nk = WINDOW + tq
    k_all = jnp.concatenate([kp_ref[...], ko_ref[...]], axis=0)
    v_all = jnp.concatenate([vp_ref[...], vo_ref[...]], axis=0)
    qc = lax.broadcasted_iota(jnp.int32, (tq, nk), 0) // CHUNK + WINDOW // CHUNK
    kcol = lax.broadcasted_iota(jnp.int32, (tq, nk), 1)
    kc = kcol // CHUNK
    valid = (kc <= qc) & (kc >= qc - WINDOW // CHUNK)
    if not prev_from_cache:
        valid = valid & ((kcol >= WINDOW) | (i > 0))
    bias_scr[...] = jnp.where(valid, 0.0, NEG_BIG)
    lane = lax.broadcasted_iota(jnp.int32, (nk, LANES), 1)
    low = lane < SWA_HD
    low_q = lax.broadcasted_iota(jnp.int32, (tq, LANES), 1) < SWA_HD
    kms, vms = [], []
    for g in range(SWA_KV_HEADS):
        slab = slice((g // 2) * LANES, (g // 2 + 1) * LANES)
        k2 = k_all[:, slab]
        v2 = v_all[:, slab]
        k2r = pltpu.roll(k2, SWA_HD, 1)
        v2r = pltpu.roll(v2, SWA_HD, 1)
        if g % 2 == 0:
            k_lo, k_hi, v_lo, v_hi = k2, k2r, v2, v2r
        else:
            k_lo, k_hi, v_lo, v_hi = k2r, k2, v2r, v2
        zero = jnp.zeros_like(k2)
        one = jnp.ones_like(k2)
        kms.append((jnp.where(low, k_lo, zero).astype(BF16), jnp.where(low, zero, k_hi).astype(BF16)))
        vms.append((jnp.where(low, v_lo, one).astype(BF16), jnp.where(low, one, v_hi).astype(BF16)))

    for g0 in range(0, SWA_KV_HEADS, SWA_STAGE_KV_HEADS):
        heads = [(g, j, half) for g in range(g0, g0 + SWA_STAGE_KV_HEADS)
                 for j in range(SWA_GROUP // 2) for half in range(2)]
        sinks = [sink_ref[g * SWA_GROUP + 2 * j + half] for g, j, half in heads]
        scores = []
        for g, j, half in heads:
            qs = q_ref[:, (g * 4 + j) * LANES:(g * 4 + j + 1) * LANES]
            s = lax.dot_general(qs, kms[g][half], (((1,), (1,)), ((), ())), preferred_element_type=F32)
            scores.append(s + bias_scr[...])
        maxes = [jnp.maximum(jnp.max(s, axis=-1, keepdims=True), sk) for s, sk in zip(scores, sinks)]
        exps = [jnp.exp(s - m).astype(BF16) for s, m in zip(scores, maxes)]
        sink_terms = [jnp.exp(sk - m) for sk, m in zip(sinks, maxes)]
        for idx in range(0, len(heads), 2):
            g, j, _ = heads[idx]
            a_lo = jnp.dot(exps[idx], vms[g][0], preferred_element_type=F32)
            a_hi = jnp.dot(exps[idx + 1], vms[g][1], preferred_element_type=F32)
            num = jnp.where(low_q, a_lo, a_hi)
            den = pltpu.roll(jnp.where(low_q, a_hi, a_lo), SWA_HD, 1)
            den = den + jnp.where(low_q, sink_terms[idx], sink_terms[idx + 1])
            o_ref[:, (g * 4 + j) * LANES:(g * 4 + j + 1) * LANES] = (num / den).astype(o_ref.dtype)


def _swa(sinks, q, q_row0, k_prev, v_prev, prev_col, k_own, v_own, own_col, own_row0,
         batch, seq, tq, prev_from_cache):
    nt = seq // tq
    qb = q_row0 // tq
    ob = own_row0 // tq
    kv_w = SWA_KV_HEADS * SWA_HD
    if prev_from_cache:
        prev_map = lambda b, t, c: (b, c)
    else:
        per = seq // WINDOW
        prev_map = lambda b, t, c: (b * per + jnp.maximum(t * (tq // WINDOW) - 1, 0), c)
    kernel = functools.partial(_swa_kernel, tq=tq, prev_from_cache=prev_from_cache)
    return pl.pallas_call(
        kernel,
        out_shape=jax.ShapeDtypeStruct((batch * seq, SWA_HEADS * SWA_HD), BF16),
        grid=(batch, nt),
        in_specs=[pl.BlockSpec(memory_space=pltpu.SMEM),
                  pl.BlockSpec((tq, SWA_HEADS * SWA_HD), lambda b, t: (qb + b * nt + t, 0)),
                  pl.BlockSpec((WINDOW, kv_w), lambda b, t: prev_map(b, t, prev_col[0])),
                  pl.BlockSpec((WINDOW, kv_w), lambda b, t: prev_map(b, t, prev_col[1])),
                  pl.BlockSpec((tq, kv_w), lambda b, t: (ob + b * nt + t, own_col[0])),
                  pl.BlockSpec((tq, kv_w), lambda b, t: (ob + b * nt + t, own_col[1]))],
        out_specs=pl.BlockSpec((tq, SWA_HEADS * SWA_HD), lambda b, t: (b * nt + t, 0)),
        scratch_shapes=[pltpu.VMEM((tq, WINDOW + tq), F32)],
        compiler_params=_params(2),
        name="swa_band",
    )(sinks, q, k_prev, v_prev, k_own, v_own)


def _mem_attn_head(q, k, v):
    s = lax.dot_general(q, k.astype(BF16), (((1,), (1,)), ((), ())),
                        preferred_element_type=F32) * (MEM_HD ** -0.5)
    m = jnp.max(s, axis=-1, keepdims=True)
    e = jnp.exp(s - m)
    p = (e / jnp.sum(e, axis=-1, keepdims=True)).astype(BF16)
    return jnp.dot(p, v.astype(BF16), preferred_element_type=F32)


def _mem_attn_kernel(q_ref, kv_ref, o_ref, kv_bf):
    @pl.when(pl.program_id(1) == 0)
    def _():
        kv_bf[...] = kv_ref[...].astype(BF16)

    width = MEM_HEADS * MEM_HD
    cols = [slice(h * MEM_HD, (h + 1) * MEM_HD) for h in range(MEM_HEADS)]
    scores = [lax.dot_general(q_ref[:, c], kv_bf[:, c], (((1,), (1,)), ((), ())),
                              preferred_element_type=F32) * (MEM_HD ** -0.5) for c in cols]
    exps = [jnp.exp(s - jnp.max(s, axis=-1, keepdims=True)) for s in scores]
    probs = [(e / jnp.sum(e, axis=-1, keepdims=True)).astype(BF16) for e in exps]
    for c, p in zip(cols, probs):
        v = kv_bf[:, width + c.start:width + c.stop]
        o_ref[:, c] = jnp.dot(p, v, preferred_element_type=F32).astype(o_ref.dtype)


def _mem_attn_cache_kernel(q_ref, k_hbm, v_hbm, o_ref, kbuf, vbuf, sem):
    b = pl.program_id(0)
    slot = b % 2

    def copies(req, s):
        return ([pltpu.make_async_copy(k_hbm.at[0, req, :, h, :], kbuf.at[s, h], sem.at[s, 0])
                 for h in range(MEM_HEADS)]
                + [pltpu.make_async_copy(v_hbm.at[0, req, :, h, :], vbuf.at[s, h], sem.at[s, 1])
                   for h in range(MEM_HEADS)])

    @pl.when(b == 0)
    def _():
        for c in copies(0, 0):
            c.start()

    @pl.when(b + 1 < pl.num_programs(0))
    def _():
        for c in copies(b + 1, 1 - slot):
            c.start()

    for c in copies(b, slot):
        c.wait()
    for h in range(MEM_HEADS):
        cols = slice(h * MEM_HD, (h + 1) * MEM_HD)
        o_ref[:, cols] = _mem_attn_head(q_ref[:, cols], kbuf[slot, h], vbuf[slot, h]).astype(o_ref.dtype)


def _mem_attn_cache(q, q_row0, cache_k, cache_v, batch, seq):
    qb = q_row0 // seq
    width = MEM_HEADS * MEM_HD
    hbm = pl.BlockSpec(memory_space=pl.ANY)
    slabs = pltpu.VMEM((2, MEM_HEADS, N_MEM, MEM_HD), F32)
    return pl.pallas_call(
        _mem_attn_cache_kernel,
        out_shape=jax.ShapeDtypeStruct((batch * seq, width), BF16),
        grid=(batch,),
        in_specs=[pl.BlockSpec((seq, width), lambda b: (qb + b, 0)), hbm, hbm],
        out_specs=pl.BlockSpec((seq, width), lambda b: (b, 0)),
        scratch_shapes=[slabs, slabs, pltpu.SemaphoreType.DMA((2, 2))],
        compiler_params=_params(1),
        name="mem_attn_cache",
    )(q, cache_k, cache_v)


def _mem_attn(q, mem_kv, batch, seq, tq):
    nt = seq // tq
    width = MEM_HEADS * MEM_HD
    return pl.pallas_call(
        _mem_attn_kernel,
        out_shape=jax.ShapeDtypeStruct((batch * seq, width), BF16),
        grid=(batch, nt),
        in_specs=[pl.BlockSpec((tq, width), lambda b, t: (b * nt + t, 0)),
                  pl.BlockSpec((N_MEM, 2 * width), lambda b, t: (b, 0))],
        out_specs=pl.BlockSpec((tq, width), lambda b, t: (b * nt + t, 0)),
        scratch_shapes=[pltpu.VMEM((N_MEM, 2 * width), BF16)],
        compiler_params=_params(2),
        name="mem_attn",
    )(q, mem_kv)


def _merge_kernel(a0p, a0s, a1p, a1s, a2p, a2s, w_ref, g0_ref, g1_ref, g2_ref, o_ref, *, p_tiles):
    def emit(a0_ref, a1_ref, a2_ref):
        acc = g0_ref[...] * jnp.dot(a0_ref[...], w_ref[0], preferred_element_type=F32)
        acc = acc + g1_ref[...] * jnp.dot(a1_ref[...], w_ref[1], preferred_element_type=F32)
        acc = acc + g2_ref[...] * jnp.dot(a2_ref[...], w_ref[2], preferred_element_type=F32)
        o_ref[...] = acc.astype(o_ref.dtype)

    @pl.when(pl.program_id(1) < p_tiles)
    def _():
        emit(a0p, a1p, a2p)

    @pl.when(pl.program_id(1) >= p_tiles)
    def _():
        emit(a0s, a1s, a2s)


def _merge(branches_p, branches_s, w_branch, gates, tm, tn):
    n_p, d = branches_p[0].shape
    n = n_p + branches_s[0].shape[0]
    nj = d // tn
    p_tiles = n_p // tm
    ap = pl.BlockSpec((tm, d), lambda j, i: (jnp.minimum(i, p_tiles - 1), 0))
    asp = pl.BlockSpec((tm, d), lambda j, i: (jnp.maximum(i - p_tiles, 0), 0))
    operands = [a for pair in zip(branches_p, branches_s) for a in pair]
    return pl.pallas_call(
        functools.partial(_merge_kernel, p_tiles=p_tiles),
        out_shape=jax.ShapeDtypeStruct((n, d), BF16),
        grid=(nj, n // tm),
        in_specs=[ap, asp, ap, asp, ap, asp,
                  pl.BlockSpec((3, d, tn), lambda j, i: (0, 0, j), pipeline_mode=pl.Buffered(1)),
                  pl.BlockSpec((tm, tn), lambda j, i: (i, j)),
                  pl.BlockSpec((tm, tn), lambda j, i: (i, nj + j)),
                  pl.BlockSpec((tm, tn), lambda j, i: (i, 2 * nj + j))],
        out_specs=pl.BlockSpec((tm, tn), lambda j, i: (i, j)),
        compiler_params=_params(2),
        name="branch_merge",
    )(*operands, w_branch, gates, gates, gates)


def _outproj_router_kernel(m_ref, w_ref, xp_ref, xs_ref, g_ref, wr_ref, br_ref,
                           x2_ref, hf_ref, eid_ref, wt_ref, *, p_tiles):
    is_prompt = pl.program_id(0) < p_tiles
    tm, d = x2_ref.shape
    for r0 in range(0, tm, ROUTER_SUB_ROWS):
        rows = slice(r0, r0 + ROUTER_SUB_ROWS)
        acc = jnp.dot(m_ref[rows, :], w_ref[...], preferred_element_type=F32)
        x = jnp.where(is_prompt, xp_ref[rows, :], xs_ref[rows, :]) + acc
        x2_ref[rows, :] = x
        hf = x * lax.rsqrt(jnp.mean(x * x, axis=-1, keepdims=True) + EPS) * g_ref[...]
        hb = hf.astype(BF16)
        bits = pltpu.bitcast(hb.astype(F32), jnp.uint32)
        packed = bits[:, d // 2:] | (bits[:, :d // 2] >> 16)
        for s in range(TOKEN_TILE_ROWS):
            hf_ref[pl.ds(r0 * TOKEN_TILE_ROWS + s, ROUTER_SUB_ROWS, stride=TOKEN_TILE_ROWS), :] = (
                packed[:, s * LANES:(s + 1) * LANES])
        logits = jnp.dot(hb, wr_ref[...], preferred_element_type=F32) + br_ref[...]
        eid, wts = _route(logits)
        eid_ref[rows, :] = eid
        wt_ref[rows, :] = wts


def _route(logits):
    lane = lax.broadcasted_iota(jnp.int32, logits.shape, 1).astype(F32)
    big = 1e6
    is_g = lane < N_GROUPS
    lg = jnp.where(is_g, logits, NEG_BIG)
    mg = jnp.max(lg, axis=-1, keepdims=True)
    gsel = jnp.min(jnp.where(is_g & (lg == mg), lane, big), axis=-1, keepdims=True)
    g_w = 1.0 / jnp.sum(jnp.where(is_g, jnp.exp(lg - mg), 0.0), axis=-1, keepdims=True)
    e_lo = N_GROUPS + gsel * EXPERTS_PER_GROUP
    in_grp = (lane >= e_lo) & (lane < e_lo + EXPERTS_PER_GROUP)
    le = jnp.where(in_grp, logits, NEG_BIG)
    me = jnp.max(le, axis=-1, keepdims=True)
    ee = jnp.where(in_grp, jnp.exp(le - me), 0.0)
    pe = ee / jnp.sum(ee, axis=-1, keepdims=True)
    pe = jnp.where(in_grp, pe, -1.0)
    p1 = jnp.max(pe, axis=-1, keepdims=True)
    i1 = jnp.min(jnp.where(pe == p1, lane, big), axis=-1, keepdims=True)
    pe2 = jnp.where(lane == i1, -1.0, pe)
    p2 = jnp.max(pe2, axis=-1, keepdims=True)
    i2 = jnp.min(jnp.where(pe2 == p2, lane, big), axis=-1, keepdims=True)
    tot = p1 + p2
    w1 = g_w * p1 / tot
    w2 = g_w * p2 / tot
    eid = jnp.where(lane == 0.0, i1 - N_GROUPS, jnp.where(lane == 1.0, i2 - N_GROUPS, 0.0))
    return eid.astype(jnp.int32), jnp.where(lane == 0.0, w1, jnp.where(lane == 1.0, w2, 0.0))


def _outproj_router(merged, w_out, xp, xs, g, wr, br, tm):
    n, d = merged.shape
    p_tiles = xp.shape[0] // tm
    const = lambda shape: pl.BlockSpec(shape, lambda i: (0, 0), pipeline_mode=pl.Buffered(1))
    row = lambda width: pl.BlockSpec((tm, width), lambda i: (i, 0))
    return pl.pallas_call(
        functools.partial(_outproj_router_kernel, p_tiles=p_tiles),
        out_shape=(jax.ShapeDtypeStruct((n, d), F32),
                   jax.ShapeDtypeStruct((n * TOKEN_TILE_ROWS, LANES), jnp.uint32),
                   jax.ShapeDtypeStruct((n, LANES), jnp.int32),
                   jax.ShapeDtypeStruct((n, LANES), F32)),
        grid=(n // tm,),
        in_specs=[row(d), const((d, d)), *_split_specs(p_tiles, tm, d),
                  const((1, d)), const((d, LANES)), const((1, LANES))],
        out_specs=(row(d), pl.BlockSpec((tm * TOKEN_TILE_ROWS, LANES), lambda i: (i, 0)),
                   row(LANES), row(LANES)),
        compiler_params=_params(1),
        name="outproj_router",
    )(merged, w_out, xp, xs, g.reshape(1, d), wr, br)


def _moe_kernel(te_ref, nv_ref, nxt_ref, tok_ref, tok_next_ref, dst_ref, hf_hbm, wup_hbm, wdn_hbm, y_hbm,
                xg, yb, wup_f32, wdn_f32, wup_bf, wdn_bf, in_sem, out_sem, w_sem):
    t = pl.program_id(0)
    buf = t % 2
    rows_now = nv_ref[t]
    rows_next = nv_ref[t + 1]
    rows_prev = nv_ref[jnp.maximum(t - 1, 0)]
    valid = rows_now > 0
    valid_next = rows_next > 0
    new_expert = (t == 0) | (te_ref[t] != te_ref[jnp.maximum(t - 1, 0)])

    def weight_copies(e):
        return (pltpu.make_async_copy(wup_hbm.at[e], wup_f32, w_sem.at[0]),
                pltpu.make_async_copy(wdn_hbm.at[e], wdn_f32, w_sem.at[1]))

    in_rows, in_pitch = TOKEN_TILE_ROWS, TOKEN_TILE_PITCH
    out_rows, out_pitch = TOKEN_F32_ROWS, TOKEN_F32_PITCH

    def row_in(row0, r, b):
        return pltpu.make_async_copy(hf_hbm.at[pl.ds(pl.multiple_of(row0, in_rows), in_rows), :],
                                     xg.at[b, pl.ds(r * in_pitch, in_rows), :], in_sem.at[b])

    def row_out(row0, r, b):
        return pltpu.make_async_copy(yb.at[b, pl.ds(r * out_pitch, out_rows), :],
                                     y_hbm.at[pl.ds(pl.multiple_of(row0, out_rows), out_rows), :], out_sem.at[b])

    group = MOE_ROW_GROUP
    n_groups = MOE_TILE // group

    def group_in(b):
        return pltpu.make_async_copy(hf_hbm.at[pl.ds(0, group * in_rows), :],
                                     xg.at[b, pl.ds(0, group * in_rows), :], in_sem.at[b])

    def group_out(b):
        return pltpu.make_async_copy(yb.at[b, pl.ds(0, group * out_rows), :],
                                     y_hbm.at[pl.ds(0, group * out_rows), :], out_sem.at[b])

    def per_started_group(rows, fn):
        for g in range(n_groups):
            pl.when(rows > g * group)(functools.partial(fn, g))

    @pl.when(t == 0)
    def _():
        for c in weight_copies(te_ref[0]):
            c.start(priority=1)
        xg[...] = jnp.zeros_like(xg)

        def first(r, c):
            row_in(tok_ref[0, 0, r], r, 0).start()
            return c
        lax.fori_loop(0, ((rows_now + group - 1) // group) * group, first, 0)
        yb[...] = jnp.zeros_like(yb)
        n_real = y_hbm.shape[0] - 2 * MOE_TILE * out_rows
        for b in range(2):
            spare = pltpu.make_async_copy(
                yb.at[b, pl.ds(0, MOE_TILE * out_rows), :],
                y_hbm.at[pl.ds(n_real + b * MOE_TILE * out_rows, MOE_TILE * out_rows), :], out_sem.at[b])
            spare.start()
            spare.wait()

    @pl.when(valid & new_expert)
    def _():
        for c in weight_copies(te_ref[t]):
            c.wait()
        wup_bf[...] = wup_f32[...].astype(BF16)
        wdn_bf[...] = wdn_f32[...].astype(BF16)

        @pl.when(nxt_ref[t] >= 0)
        def _():
            for c in weight_copies(nxt_ref[t]):
                c.start(priority=1)

    def gather_next(g):
        for r in range(g * group, (g + 1) * group):
            row_in(tok_next_ref[0, 0, r], r, 1 - buf).start()

    def scatter_now(g):
        for r in range(g * group, (g + 1) * group):
            row_out(dst_ref[0, 0, r], r, buf).start(priority=r % 2)

    per_started_group(rows_next, gather_next)

    @pl.when(valid)
    def _():
        per_started_group(rows_now, lambda g: group_in(buf).wait())
        lo, hi = [], []
        for s in range(in_rows):
            word = xg[buf, pl.ds(s, MOE_TILE, stride=in_pitch), :]
            lo.append(pltpu.bitcast(word << 16, F32).astype(BF16))
            hi.append(pltpu.bitcast(word & jnp.uint32(0xFFFF0000), F32).astype(BF16))
        x = jnp.concatenate(lo + hi, axis=1)
        h1 = jnp.dot(x, wup_bf[...], preferred_element_type=F32)
        gate = h1[:, :D_FF]
        up = h1[:, D_FF:]
        act = (gate * _sigmoid(gate)) * up
        ye = jnp.dot(act.astype(BF16), wdn_bf[...], preferred_element_type=F32)
        for s in range(out_rows):
            yb[buf, pl.ds(s, MOE_TILE, stride=out_pitch), :] = ye[:, s * LANES:(s + 1) * LANES]
        per_started_group(rows_now, scatter_now)

        @pl.when(t > 0)
        def _():
            per_started_group(rows_prev, lambda g: group_out(1 - buf).wait())

        @pl.when(jnp.logical_not(valid_next))
        def _():
            per_started_group(rows_now, lambda g: group_out(buf).wait())


def _moe(hf, tile_expert, tile_rows, next_expert, tok_slots, dst_slots, w_up, w_down, out_tokens):
    n_tiles = tile_expert.shape[0]
    d = w_up.shape[1]
    slot_spec = lambda off: pl.BlockSpec((1, 1, MOE_TILE),
                                         lambda t, te, nv, ne: (jnp.minimum(t + off, n_tiles - 1), 0, 0),
                                         memory_space=pltpu.SMEM)
    hbm = pl.BlockSpec(memory_space=pl.ANY)
    grid_spec = pltpu.PrefetchScalarGridSpec(
        num_scalar_prefetch=3,
        grid=(n_tiles,),
        in_specs=[slot_spec(0), slot_spec(1), slot_spec(0), hbm, hbm, hbm],
        out_specs=hbm,
        scratch_shapes=[pltpu.VMEM((2, MOE_TILE * TOKEN_TILE_PITCH, LANES), jnp.uint32),
                        pltpu.VMEM((2, MOE_TILE * TOKEN_F32_PITCH, LANES), F32),
                        pltpu.VMEM((d, 2 * D_FF), F32),
                        pltpu.VMEM((D_FF, d), F32),
                        pltpu.VMEM((d, 2 * D_FF), BF16),
                        pltpu.VMEM((D_FF, d), BF16),
                        pltpu.SemaphoreType.DMA((2,)),
                        pltpu.SemaphoreType.DMA((2,)),
                        pltpu.SemaphoreType.DMA((2,))],
    )
    return pl.pallas_call(
        _moe_kernel,
        out_shape=jax.ShapeDtypeStruct((out_tokens * TOKEN_F32_ROWS, LANES), F32),
        grid_spec=grid_spec,
        compiler_params=_params(1),
        name="moe_experts",
    )(tile_expert, tile_rows, next_expert, tok_slots, tok_slots, dst_slots, hf, w_up, w_down)


def _combine_kernel(x_ref, y0_ref, y1_ref, w_ref, op_ref, os_ref, *, p_tiles):
    tm = x_ref.shape[0]
    w = w_ref[...]
    w0 = w[:, 0:1]
    w1 = w[:, 1:2]

    def emit(o_ref):
        for s in range(TOKEN_F32_ROWS):
            cols = slice(s * LANES, (s + 1) * LANES)
            rows = pl.ds(s, tm, stride=TOKEN_F32_ROWS)
            o_ref[:, cols] = x_ref[:, cols] + (y0_ref[rows, :] * w0 + y1_ref[rows, :] * w1)

    @pl.when(pl.program_id(0) < p_tiles)
    def _():
        emit(op_ref)

    @pl.when(pl.program_id(0) >= p_tiles)
    def _():
        emit(os_ref)


def _combine(x2, yk, wts, n_p, tm):
    n, d = x2.shape
    p_tiles = n_p // tm
    k1 = n // tm
    y_rows = tm * TOKEN_F32_ROWS
    return pl.pallas_call(
        functools.partial(_combine_kernel, p_tiles=p_tiles),
        out_shape=(jax.ShapeDtypeStruct((n_p, d), F32), jax.ShapeDtypeStruct((n - n_p, d), F32)),
        grid=(n // tm,),
        in_specs=[pl.BlockSpec((tm, d), lambda i: (i, 0)),
                  pl.BlockSpec((y_rows, LANES), lambda i: (i, 0)),
                  pl.BlockSpec((y_rows, LANES), lambda i: (k1 + i, 0)),
                  pl.BlockSpec((tm, LANES), lambda i: (i, 0))],
        out_specs=_split_specs(p_tiles, tm, d),
        compiler_params=_params(1),
        name="moe_combine",
    )(x2, yk, yk, wts)


def _rope_tables(pos):
    half = ROPE_DIM // 2
    inv = ROPE_THETA ** (-jnp.arange(half, dtype=F32) / half)
    ang = pos.astype(F32)[:, None] * inv[None, :]
    cos, sin = jnp.cos(ang), jnp.sin(ang)
    n = pos.shape[0]
    pad = jnp.zeros((n, SWA_HD - ROPE_DIM), F32)
    cos_h = jnp.concatenate([cos, cos, pad + 1.0], axis=1)
    sa_h = jnp.concatenate([-sin, jnp.zeros_like(sin), pad], axis=1)
    sb_h = jnp.concatenate([jnp.zeros_like(sin), sin, pad], axis=1)
    reps = LANES // SWA_HD
    return tuple(jnp.tile(a, (1, reps)) for a in (cos_h, sa_h, sb_h))


def _moe_schedule(eid, n_tok, n_tiles):
    a = eid.shape[0]
    order = jnp.argsort(eid, stable=True).astype(jnp.int32)
    counts = jnp.bincount(eid, length=N_EXPERTS).astype(jnp.int32)
    tiles_per = (counts + MOE_TILE - 1) // MOE_TILE
    tile_end = jnp.cumsum(tiles_per)
    tile_start = tile_end - tiles_per
    sorted_start = jnp.cumsum(counts) - counts
    tile_id = jnp.arange(n_tiles, dtype=jnp.int32)
    used = tile_id < tile_end[-1]
    te = jnp.minimum(jnp.sum(tile_end[None, :] <= tile_id[:, None], axis=1), N_EXPERTS - 1).astype(jnp.int32)
    last_used_e = te[jnp.maximum(tile_end[-1] - 1, 0)]
    te = jnp.where(used, te, last_used_e)
    row_in_expert = (tile_id - tile_start[te]) * MOE_TILE
    rows_valid = jnp.where(used, jnp.clip(counts[te] - row_in_expert, 0, MOE_TILE), 0).astype(jnp.int32)
    r = jnp.arange(MOE_TILE, dtype=jnp.int32)[None, :]
    src = sorted_start[te][:, None] + row_in_expert[:, None] + r
    real = r < rows_valid[:, None]
    assign = order[jnp.clip(src, 0, a - 1)]
    tok = jnp.where(real, assign % n_tok, 0) * TOKEN_TILE_ROWS
    spare = a + (tile_id[:, None] % 2) * MOE_TILE + r
    dst = jnp.where(real, assign, spare) * TOKEN_F32_ROWS
    tile_rows = jnp.concatenate([rows_valid, jnp.zeros((1,), jnp.int32)])
    e_id = jnp.arange(N_EXPERTS, dtype=jnp.int32)[None, :]
    later = (e_id > te[:, None]) & (counts[None, :] > 0)
    nxt = jnp.min(jnp.where(later, e_id, N_EXPERTS), axis=1)
    nxt = jnp.where(nxt < N_EXPERTS, nxt, -1).astype(jnp.int32)
    shape = (n_tiles, 1, MOE_TILE)
    return te, tile_rows, nxt, tok.reshape(shape).astype(jnp.int32), dst.reshape(shape).astype(jnp.int32)


def kernel(x_prompt, x_sample, state_gla, cache_swa_k, cache_swa_v, cache_mem_k, cache_mem_v,
           mem_prompt, norm_mix_g, w_in, w_a2, b_a2, gla_norm_g, swa_q_norm_g, swa_k_norm_g,
           swa_sinks, norm_mem_g, w_mem_kv, mem_q_norm_g, mem_k_norm_g, w_gate, b_gate,
           w_branch, w_out, norm_ffn_g, w_router_group, b_router_group, w_router_expert,
           b_router_expert, w_up, w_down):
    bp, tp, d = x_prompt.shape
    bs, ts, _ = x_sample.shape
    n_p, n_s = bp * tp, bs * ts
    n = n_p + n_s
    tm, tn = ROW_TILE, COL_TILE
    tp_rows = PROJ_ROWS if n % PROJ_ROWS == 0 else tm
    heavy_sub = tp_rows // 2
    norm_sub = tp_rows // 4
    assert d == D_MODEL and n_p % tm == 0 and n_s % tm == 0 and w_in.shape[0] == 1
    keep_s = cache_swa_k.shape[2]
    assert keep_s == WINDOW and tp % WINDOW == 0

    qk_w = GLA_HEADS * GLA_DK
    v_w = GLA_HEADS * GLA_DV
    c0 = 2 * qk_w + 2 * v_w
    sq_w = SWA_HEADS * SWA_HD
    kv_w = SWA_KV_HEADS * SWA_HD
    mem_w = MEM_HEADS * MEM_HD
    w_in_t = jnp.transpose(w_in[0])
    w_branch_b = w_branch[0].astype(BF16)
    w_out_b = w_out[0].astype(BF16)
    w_a2_b = jnp.pad(w_a2[0], ((0, LANES - GLA_LOWRANK), (0, 0))).astype(BF16)
    w_router = jnp.pad(jnp.concatenate([w_router_group[0], w_router_expert[0]], axis=1),
                       ((0, 0), (0, LANES - N_GROUPS - N_EXPERTS))).astype(BF16)
    b_router = jnp.pad(jnp.concatenate([b_router_group[0], b_router_expert[0]]),
                       (0, LANES - N_GROUPS - N_EXPERTS)).reshape(1, LANES)

    pos = jnp.concatenate([jnp.tile(jnp.arange(tp, dtype=jnp.int32), bp),
                           jnp.tile(PAST_LEN + jnp.arange(ts, dtype=jnp.int32), bs)])
    cos_t, sa_t, sb_t = _rope_tables(pos)
    seg_id = jnp.arange(tn, dtype=jnp.int32) // SWA_HD
    seg = (seg_id[:, None] == seg_id[None, :]).astype(BF16)
    rope_specs = [pl.BlockSpec((tp_rows, LANES), lambda j, i: (i, 0))] * 3
    row_vec = lambda width: pl.BlockSpec((1, width), lambda j, i: (0, 0))
    seg_spec = pl.BlockSpec((tn, tn), lambda j, i: (0, 0))

    xp2 = x_prompt.reshape(n_p, d)
    xs2 = x_sample.reshape(n_s, d)
    h = _rms_norm_two(xp2, xs2, norm_mix_g[0], tm)

    c1 = c0 + GLA_LOWRANK
    c2 = c1 + sq_w
    c3 = c2 + 2 * kv_w
    qkvg = _wmatmul(_ep_plain, h, w_in_t, 0, c0, True, [], [], F32, tp_rows, WIDE_COLS, "proj_gla",
                    sub_rows=heavy_sub)
    ga = _wmatmul(_ep_lowrank, h, w_in_t, c0, LANES, True, [], [], F32, tp_rows, LANES, "proj_gla_lowrank")
    q_gain = jnp.tile(swa_q_norm_g[0] * (SWA_HD ** -0.5), tn // SWA_HD).reshape(1, tn)
    q_swa = _wmatmul(functools.partial(_ep_qknorm_rope, keep_from=None), h, w_in_t, c1, sq_w, True,
                     [q_gain, seg, cos_t, sa_t, sb_t], [row_vec(tn), seg_spec] + rope_specs,
                     BF16, tp_rows, tn, "proj_swa_q", sub_rows=norm_sub)
    k_gain = jnp.tile(swa_k_norm_g[0], tn // SWA_HD).reshape(1, tn)
    kv_swa = _wmatmul(functools.partial(_ep_qknorm_rope, keep_from=kv_w), h, w_in_t, c2, 2 * kv_w, True,
                      [k_gain, seg, cos_t, sa_t, sb_t], [row_vec(tn), seg_spec] + rope_specs,
                      F32, tp_rows, tn, "proj_swa_kv", sub_rows=norm_sub)
    q_mem = _wmatmul(functools.partial(_ep_headnorm, norm_tiles=None), h, w_in_t, c3, mem_w, True,
                     [mem_q_norm_g[0].reshape(1, MEM_HD)], [row_vec(MEM_HD)],
                     BF16, tp_rows, MEM_HD, "proj_mem_q", sub_rows=norm_sub)
    gates = _wmatmul(_ep_sigmoid, h, w_gate[0], 0, 3 * d, False, [b_gate[0].reshape(1, -1)],
                     [pl.BlockSpec((1, WIDE_COLS), lambda j, i: (0, j))], F32, tp_rows, WIDE_COLS, "proj_gates",
                     sub_rows=heavy_sub)

    mem_rows = bp * N_MEM
    hm = _rms_norm_rows(mem_prompt.reshape(mem_rows, d), norm_mem_g[0], BF16, N_MEM)
    mem_kv = _wmatmul(functools.partial(_ep_headnorm, norm_tiles=MEM_HEADS), hm, w_mem_kv[0], 0, 2 * mem_w, False,
                      [mem_k_norm_g[0].reshape(1, MEM_HD)], [row_vec(MEM_HD)], F32, mem_rows, MEM_HD, "mem_kv")

    ba = b_a2[0].reshape(1, qk_w)
    gn = gla_norm_g[0].reshape(1, GLA_DV)
    o_gla_p, gla_state_p = _gla(qkvg, ga, w_a2_b, ba, gn, None, bp, tp, 0, 256, CHUNK)
    o_gla_s, gla_state_s = _gla(qkvg, ga, w_a2_b, ba, gn, state_gla[0], bs, ts, n_p, ts, min(CHUNK, ts))

    sinks = swa_sinks[0]
    o_swa_p = _swa(sinks, q_swa, 0, kv_swa, kv_swa, (0, 1), kv_swa, kv_swa, (0, 1), 0,
                   bp, tp, WINDOW, False)
    ck = cache_swa_k[0].reshape(bs * keep_s, kv_w)
    cv = cache_swa_v[0].reshape(bs * keep_s, kv_w)
    o_swa_s = _swa(sinks, q_swa, n_p, ck, cv, (0, 0), kv_swa, kv_swa, (0, 1), n_p,
                   bs, ts, ts, True)

    o_mem_p = _mem_attn(q_mem, mem_kv, bp, tp, min(tp, MEM_ATTN_ROWS))
    o_mem_s = _mem_attn_cache(q_mem, n_p, cache_mem_k, cache_mem_v, bs, ts)

    merged = _merge((o_gla_p, o_swa_p, o_mem_p), (o_gla_s, o_swa_s, o_mem_s), w_branch_b, gates, tm, WIDE_COLS)

    x2, hf, eid, wts = _outproj_router(merged, w_out_b, xp2, xs2, norm_ffn_g[0], w_router, b_router, tm)

    n_assign = TOP_K * n
    n_tiles = n_assign // MOE_TILE + N_EXPERTS
    eid_kmajor = jnp.concatenate([eid[:, k] for k in range(TOP_K)])
    tile_expert, tile_rows, next_expert, tok_slots, dst_slots = _moe_schedule(eid_kmajor, n, n_tiles)
    yk = _moe(hf, tile_expert, tile_rows, next_expert, tok_slots, dst_slots, w_up[0], w_down[0],
              n_assign + 2 * MOE_TILE)
    y_p, y_s = _combine(x2, yk, wts, n_p, tm)

    y_p = y_p.reshape(bp, tp, d)
    y_s = y_s.reshape(bs, ts, d)
    kv_p = jnp.stack([kv_swa[(b + 1) * tp - WINDOW:(b + 1) * tp] for b in range(bp)])
    kv_p = kv_p.reshape(bp, WINDOW, 2, SWA_KV_HEADS, SWA_HD)
    kv_s = kv_swa[n_p:].reshape(bs, ts, 2, SWA_KV_HEADS, SWA_HD)
    swk_s = jnp.concatenate([cache_swa_k[0], kv_s[:, :, 0]], axis=1)[:, ts:ts + keep_s]
    swv_s = jnp.concatenate([cache_swa_v[0], kv_s[:, :, 1]], axis=1)[:, ts:ts + keep_s]
    mk_p = mem_kv[:, :mem_w].reshape(bp, N_MEM, MEM_HEADS, MEM_HD)
    mv_p = mem_kv[:, mem_w:].reshape(bp, N_MEM, MEM_HEADS, MEM_HD)
    return (y_p, y_s, gla_state_p[None], kv_p[:, :, 0][None], kv_p[:, :, 1][None], mk_p[None], mv_p[None],
            gla_state_s[None], swk_s[None], swv_s[None])
```
